```python
import math, functools
import jax, jax.numpy as jnp
from jax import lax
import numpy as np

D_MODEL = 1024
BATCH = 4
SEQ = 4096
DEPTH = 1
DEC_BATCH = 32
DEC_SEQ = 4
PAST_LEN = 16384
PAGE_SIZE = 128

PLE_DIM = 256
EPS = 1e-6
A_HEADS = 4
A_DQK = 128
A_DV = 256
A_QK = A_HEADS * A_DQK
A_WIDTH = A_HEADS * A_DV
CONV_W = 4
CHUNK = 64
FORGET_BIAS = 3.0
B_HEADS = 8
B_KV = 2
B_HD = 128
B_WIDTH = B_HEADS * B_HD
IDX_HEADS = 8
IDX_DIM = 64
TOPK_MAX = 256
Q_BLOCK = 128

IN_SPLITS = (
    ('a_q', A_QK), ('a_k', A_QK), ('a_v', A_WIDTH), ('a_i', A_HEADS), ('a_f', A_HEADS),
    ('a_o', A_WIDTH), ('a_z', A_WIDTH),
    ('b_q', B_WIDTH), ('b_k', B_KV * B_HD), ('b_v', B_KV * B_HD), ('b_iq', IDX_HEADS * IDX_DIM),
    ('b_ik', IDX_DIM), ('b_iw', IDX_HEADS), ('b_z', B_WIDTH),
    ('g_a', D_MODEL), ('g_b', D_MODEL),
)
IN_COLS = sum(n for _, n in IN_SPLITS)

kernel_name = 'hybrid_mlstm_dsa_parallel_decode_step'


def rmsnorm(x, g):
    xf = x.astype(jnp.float32)
    y = xf * lax.rsqrt(jnp.mean(xf * xf, axis=-1, keepdims=True) + EPS)
    return (y * g.astype(jnp.float32)).astype(x.dtype)


def split_cols(h):
    names = [nm for nm, _ in IN_SPLITS]
    cuts = np.cumsum([n for _, n in IN_SPLITS])[:-1].tolist()
    return dict(zip(names, jnp.split(h, cuts, axis=-1)))


def causal_conv(u, buf, w, b):
    T = u.shape[1]
    up = jnp.concatenate([buf.astype(u.dtype), u], axis=1)
    y = b + w[0] * up[:, 0:T]
    for j in range(1, CONV_W):
        y = y + w[j] * up[:, j:j + T]
    return jax.nn.silu(y), up[:, up.shape[1] - (CONV_W - 1):]


def mlstm_chunkwise(q, k, v, i_pre, f_pre, C0, n0, m0, chunk):
    f32 = jnp.float32
    B, T, H, _ = q.shape
    nc = T // chunk

    def to_chunks(a):
        a = a.astype(f32).reshape((B, nc, chunk) + a.shape[2:])
        return jnp.moveaxis(jnp.moveaxis(a, 3, 2), 1, 0)

    xs = (to_chunks(q), to_chunks(k), to_chunks(v), to_chunks(i_pre),
          to_chunks(jax.nn.log_sigmoid(f_pre.astype(f32))))
    causal = jnp.tril(jnp.ones((chunk, chunk), dtype=bool))

    def step(carry, blk):
        C, n, m = carry
        qb, kb, vb, ib, lfb = blk
        b = jnp.cumsum(lfb, axis=-1)
        inter = b + m[..., None]
        logd = jnp.where(causal, b[..., :, None] - b[..., None, :] + ib[..., None, :], -jnp.inf)
        m_t = jnp.maximum(inter, jnp.max(logd, axis=-1))
        dmat = jnp.exp(logd - m_t[..., None])
        w_inter = jnp.exp(inter - m_t)
        s = jnp.einsum('bhtd,bhsd->bhts', qb, kb) * dmat
        num = jnp.einsum('bhts,bhsv->bhtv', s, vb) + w_inter[..., None] * jnp.einsum('bhvd,bhtd->bhtv', C, qb)
        den = jnp.sum(s, axis=-1) + w_inter * jnp.einsum('bhd,bhtd->bht', n, qb)
        h = num / jnp.maximum(jnp.abs(den), jnp.exp(-m_t))[..., None]
        m_new = m_t[..., -1]
        w_prev = jnp.exp(b[..., -1] + m - m_new)
        w_s = jnp.exp(b[..., -1:] - b + ib - m_new[..., None])
        C_new = w_prev[..., None, None] * C + jnp.einsum('bhs,bhsv,bhsd->bhvd', w_s, vb, kb)
        n_new = w_prev[..., None] * n + jnp.einsum('bhs,bhsd->bhd', w_s, kb)
        return (C_new, n_new, m_new), h

    (C, n, m), h = lax.scan(step, (C0.astype(f32), n0.astype(f32), m0.astype(f32)), xs)
    h = jnp.moveaxis(jnp.moveaxis(h, 0, 1), 2, 3).reshape(B, T, H, v.shape[-1])
    return h, C, n, m


def mlstm_branch(parts, conv_buf, C0, n0, m0, chunk, conv_w, conv_b, if_bias, norm_a):
    B, T, _ = parts['a_q'].shape
    qk, conv_new = causal_conv(jnp.concatenate([parts['a_q'], parts['a_k']], axis=-1), conv_buf, conv_w, conv_b)
    q = qk[..., :A_QK].reshape(B, T, A_HEADS, A_DQK) * (A_DQK ** -0.5)
    k = qk[..., A_QK:].reshape(B, T, A_HEADS, A_DQK)
    v = parts['a_v'].reshape(B, T, A_HEADS, A_DV)
    i_pre = parts['a_i'] + if_bias[:A_HEADS]
    f_pre = parts['a_f'] + if_bias[A_HEADS:]
    h, C, n, m = mlstm_chunkwise(q, k, v, i_pre, f_pre, C0, n0, m0, chunk)
    h = h * lax.rsqrt(jnp.mean(h * h, axis=-1, keepdims=True) + EPS)
    h = (h.reshape(B, T, A_WIDTH) * norm_a.astype(jnp.float32)).astype(parts['a_v'].dtype)
    y = jax.nn.sigmoid(parts['a_o']) * h * jax.nn.silu(parts['a_z'])
    return y, (C, n, m, conv_new)


def indexer_scores(iq, iw, ik, visible):
    r = jax.nn.relu(jnp.einsum('bthd,bsd->bths', iq.astype(jnp.float32), ik.astype(jnp.float32)) * (IDX_DIM ** -0.5))
    sc = jnp.einsum('bths,bth->bts', r, iw.astype(jnp.float32) * (IDX_HEADS ** -0.5))
    return jnp.where(visible[None], sc, -jnp.inf)


def sparse_attend(q, k_sel, v_sel, valid):
    s = jnp.einsum('btngd,btknd->btngk', q.astype(jnp.float32), k_sel.astype(jnp.float32)) * (B_HD ** -0.5)
    s = jnp.where(valid[:, :, None, None, :], s, -jnp.inf)
    p = jax.nn.softmax(s, axis=-1)
    return jnp.einsum('btngk,btknd->btngd', p, v_sel.astype(jnp.float32))


def dsa_prompt(q, k, v, iq, iw, ik):
    B, S = q.shape[:2]
    topk = min(TOPK_MAX, S // 4)
    nb = S // Q_BLOCK
    key_pos = jnp.arange(S)
    bidx = jnp.arange(B)[:, None, None]

    def blk(a):
        return jnp.moveaxis(a.reshape((B, nb, Q_BLOCK) + a.shape[2:]), 1, 0)

    def one(xs):
        qb, iqb, iwb, t0 = xs
        qpos = t0 + jnp.arange(Q_BLOCK)
        visible = key_pos[None, :] <= qpos[:, None]
        _, sel = lax.top_k(indexer_scores(iqb, iwb, ik, visible), topk)
        valid = sel <= qpos[None, :, None]
        return sparse_attend(qb, k[bidx, sel], v[bidx, sel], valid)

    o = lax.map(one, (blk(q), blk(iq), blk(iw), jnp.arange(nb) * Q_BLOCK))
    return jnp.moveaxis(o, 0, 1).reshape(B, S, B_WIDTH)


def dsa_sample(q, k, v, iq, iw, ik, cache_k, cache_v, cache_ik, page_table):
    B, T = q.shape[:2]
    L = PAST_LEN + T
    topk = min(TOPK_MAX, L // 4)
    ik_past = cache_ik[page_table].reshape(B, PAST_LEN, IDX_DIM).astype(ik.dtype)
    ik_all = jnp.concatenate([ik_past, ik], axis=1)
    qpos = PAST_LEN + jnp.arange(T)
    visible = jnp.arange(L)[None, :] <= qpos[:, None]
    _, sel = lax.top_k(indexer_scores(iq, iw, ik_all, visible), topk)
    valid = sel <= qpos[None, :, None]
    in_past = (sel < PAST_LEN)[..., None, None]
    bidx = jnp.arange(B)[:, None, None]
    ps = jnp.minimum(sel, PAST_LEN - 1)
    phys = page_table[bidx, ps // PAGE_SIZE]
    off = ps % PAGE_SIZE
    ns = jnp.clip(sel - PAST_LEN, 0, T - 1)
    ks = jnp.where(in_past, cache_k[phys, off].astype(k.dtype), k[bidx, ns])
    vs = jnp.where(in_past, cache_v[phys, off].astype(v.dtype), v[bidx, ns])
    return sparse_attend(q, ks, vs, valid).reshape(B, T, B_WIDTH)


def trunk_layer(x, pe, conv_buf, C0, n0, m0, chunk, attend_fn, w_in, conv_w, conv_b, if_bias, norm_a,
                w_a_proj, w_b_proj, w_out, norm_pre, norm_post, w_ple, w_ple_gate):
    B, T, _ = x.shape
    xn = rmsnorm(x, norm_pre)
    parts = split_cols(xn @ w_in)
    ya, (C, n, m, conv_new) = mlstm_branch(parts, conv_buf, C0, n0, m0, chunk, conv_w, conv_b, if_bias, norm_a)
    q = parts['b_q'].reshape(B, T, B_KV, B_HEADS // B_KV, B_HD)
    k = parts['b_k'].reshape(B, T, B_KV, B_HD)
    v = parts['b_v'].reshape(B, T, B_KV, B_HD)
    iq = parts['b_iq'].reshape(B, T, IDX_HEADS, IDX_DIM)
    ik = parts['b_ik']
    yb = attend_fn(q, k, v, iq, parts['b_iw'], ik).astype(x.dtype) * jax.nn.silu(parts['b_z'])
    merged = jax.nn.sigmoid(parts['g_a']) * (ya @ w_a_proj) + jax.nn.sigmoid(parts['g_b']) * (yb @ w_b_proj)
    x = x + rmsnorm(merged @ w_out, norm_post)
    x = x + (pe @ w_ple) * jax.nn.sigmoid(x @ w_ple_gate)
    return x, (k, v, ik, C, n, m, conv_new)


def setup_inputs(seed: int = 0) -> dict:
    key = jax.random.key(seed)
    ks = jax.random.split(key, 24)
    n_pages = PAST_LEN // PAGE_SIZE
    n_pool = (DEC_BATCH * n_pages * 5) // 4
    nrm = jax.random.normal
    f32 = jnp.float32
    page_table = jax.random.permutation(ks[0], n_pool)[:DEC_BATCH * n_pages].reshape(DEC_BATCH, n_pages).astype(jnp.int32)
    if_bias = jnp.concatenate([0.1 * nrm(ks[1], (DEPTH, A_HEADS), f32),
                               FORGET_BIAS + 0.1 * nrm(ks[2], (DEPTH, A_HEADS), f32)], axis=-1)
    return {
        'x_prompt': nrm(ks[3], (BATCH, SEQ, D_MODEL), f32),
        'x_sample': nrm(ks[4], (DEC_BATCH, DEC_SEQ, D_MODEL), f32),
        'cache_k': nrm(ks[5], (DEPTH, n_pool, PAGE_SIZE, B_KV, B_HD), f32),
        'cache_v': nrm(ks[6], (DEPTH, n_pool, PAGE_SIZE, B_KV, B_HD), f32),
        'cache_idx_k': nrm(ks[7], (DEPTH, n_pool, PAGE_SIZE, IDX_DIM), f32),
        'page_table': page_table,
        'state_C': 0.5 * nrm(ks[8], (DEPTH, DEC_BATCH, A_HEADS, A_DV, A_DQK), f32),
        'state_n': nrm(ks[9], (DEPTH, DEC_BATCH, A_HEADS, A_DQK), f32),
        'state_m': nrm(ks[10], (DEPTH, DEC_BATCH, A_HEADS), f32),
        'state_conv': nrm(ks[11], (DEPTH, DEC_BATCH, CONV_W - 1, 2 * A_QK), f32),
        'p_prompt': nrm(ks[12], (DEPTH, BATCH, SEQ, PLE_DIM), f32),
        'p_sample': nrm(ks[13], (DEPTH, DEC_BATCH, DEC_SEQ, PLE_DIM), f32),
        'w_in': nrm(ks[14], (DEPTH, D_MODEL, IN_COLS), f32) * D_MODEL ** -0.5,
        'conv_w': nrm(ks[15], (DEPTH, CONV_W, 2 * A_QK), f32) * CONV_W ** -0.5,
        'conv_b': 0.01 * nrm(ks[16], (DEPTH, 2 * A_QK), f32),
        'if_bias': if_bias,
        'norm_a': 1.0 + 0.01 * nrm(ks[17], (DEPTH, A_WIDTH), f32),
        'w_a_proj': nrm(ks[18], (DEPTH, A_WIDTH, D_MODEL), f32) * A_WIDTH ** -0.5,
        'w_b_proj': nrm(ks[19], (DEPTH, B_WIDTH, D_MODEL), f32) * B_WIDTH ** -0.5,
        'w_out': nrm(ks[20], (DEPTH, D_MODEL, D_MODEL), f32) * D_MODEL ** -0.5,
        'norm_pre': 1.0 + 0.01 * nrm(ks[21], (DEPTH, D_MODEL), f32),
        'norm_post': 1.0 + 0.01 * nrm(ks[22], (DEPTH, D_MODEL), f32),
        'w_ple': nrm(ks[23], (DEPTH, PLE_DIM, D_MODEL), f32) * PLE_DIM ** -0.5,
        'w_ple_gate': nrm(jax.random.fold_in(key, 99), (DEPTH, D_MODEL, D_MODEL), f32) * D_MODEL ** -0.5,
    }


def reference(x_prompt, x_sample, cache_k, cache_v, cache_idx_k, page_table, state_C, state_n, state_m, state_conv,
              p_prompt, p_sample, w_in, conv_w, conv_b, if_bias, norm_a, w_a_proj, w_b_proj, w_out,
              norm_pre, norm_post, w_ple, w_ple_gate):
    f32 = jnp.float32
    B, T = x_prompt.shape[:2]
    Ts = x_sample.shape[1]
    yp, ys = x_prompt, x_sample
    sp_list, ss_list = [], []
    for i in range(DEPTH):
        w = (w_in[i], conv_w[i], conv_b[i], if_bias[i], norm_a[i], w_a_proj[i], w_b_proj[i], w_out[i],
             norm_pre[i], norm_post[i], w_ple[i], w_ple_gate[i])
        zero_conv = jnp.zeros((B, CONV_W - 1, 2 * A_QK), x_prompt.dtype)
        zC = jnp.zeros((B, A_HEADS, A_DV, A_DQK), f32)
        zn = jnp.zeros((B, A_HEADS, A_DQK), f32)
        zm = jnp.zeros((B, A_HEADS), f32)
        yp, sp = trunk_layer(yp, p_prompt[i], zero_conv, zC, zn, zm, math.gcd(T, CHUNK), dsa_prompt, *w)
        attend_s = functools.partial(dsa_sample, cache_k=cache_k[i], cache_v=cache_v[i],
                                     cache_ik=cache_idx_k[i], page_table=page_table)
        ys, ss = trunk_layer(ys, p_sample[i], state_conv[i], state_C[i], state_n[i], state_m[i],
                             math.gcd(Ts, CHUNK), attend_s, *w)
        sp_list.append(sp)
        ss_list.append(ss)

    def stk(lst, j):
        return jnp.stack([s[j] for s in lst])

    return (yp, ys,
            stk(sp_list, 0), stk(sp_list, 1), stk(sp_list, 2), stk(sp_list, 3), stk(sp_list, 4), stk(sp_list, 5), stk(sp_list, 6),
            stk(ss_list, 0), stk(ss_list, 1), stk(ss_list, 2), stk(ss_list, 3), stk(ss_list, 4), stk(ss_list, 5), stk(ss_list, 6))
```

```python
import functools

import jax
import jax.numpy as jnp
from jax import lax
from jax.experimental import pallas as pl
from jax.experimental.pallas import tpu as pltpu

F32 = jnp.float32
BF16 = jnp.bfloat16
I32 = jnp.int32

EPS = 1e-6
A_HEADS = 4
A_DQK = 128
A_DV = 256
A_QK = A_HEADS * A_DQK
A_WIDTH = A_HEADS * A_DV
CONV_W = 4
B_HEADS = 8
B_KV = 2
B_HD = 128
B_GROUP = B_HEADS // B_KV
B_WIDTH = B_HEADS * B_HD
IDX_HEADS = 8
IDX_DIM = 64
TOPK_MAX = 256
PAGE = 128
LANES = 128
VMEM_LIMIT = 56 * 1024 * 1024

Q_SCALE = A_DQK ** -0.5
ATT_SCALE = B_HD ** -0.5
IDX_SCALE = IDX_DIM ** -0.5
IW_SCALE = IDX_HEADS ** -0.5
NEG_BIG = -1e30
INT_MIN = -(2 ** 31)
KEY_NEG_INF = INT_MIN + 0x007FFFFF

_SPLITS = (('a_q', A_QK), ('a_k', A_QK), ('a_v', A_WIDTH), ('a_i', A_HEADS), ('a_f', A_HEADS),
           ('a_o', A_WIDTH), ('a_z', A_WIDTH), ('b_q', B_WIDTH), ('b_k', B_KV * B_HD), ('b_v', B_KV * B_HD),
           ('b_iq', IDX_HEADS * IDX_DIM), ('b_ik', IDX_DIM), ('b_iw', IDX_HEADS), ('b_z', B_WIDTH),
           ('g_a', 1024), ('g_b', 1024))

P_QK = 0
P_AV = 1024
P_AO = 2048
P_AZ = 3072
P_BQ = 4096
P_BK = 5120
P_BV = 5376
P_IQ = 5632
P_MISC = 6144
P_BZ = 6272
P_GA = 7296
P_GB = 8320
P_COLS = 9344
MISC_IW = 8
MISC_IK = 64


def _sigmoid(x):
    return 1.0 / (1.0 + jnp.exp(-x))


def _silu(x):
    return x * _sigmoid(x)


def _dot(a, b):
    return jnp.dot(a, b, preferred_element_type=F32)


def _dot_nt(a, b):
    return lax.dot_general(a, b, (((1,), (1,)), ((), ())), preferred_element_type=F32)


def _transpose(x):
    r, c = x.shape
    rows = []
    for j in range(c // LANES):
        rows.append(jnp.concatenate([x[i * LANES:(i + 1) * LANES, j * LANES:(j + 1) * LANES].T
                                     for i in range(r // LANES)], axis=1))
    return jnp.concatenate(rows, axis=0)


def _pack_w_in(w_in, d_model):
    offs = {}
    c = 0
    for name, n in _SPLITS:
        offs[name] = (c, c + n)
        c += n

    def col(name):
        lo, hi = offs[name]
        return w_in[:, lo:hi]

    z = lambda n: jnp.zeros((d_model, n), w_in.dtype)
    misc = jnp.concatenate([col('a_i'), col('a_f'), col('b_iw'), z(MISC_IK - 16), col('b_ik')], axis=1)
    packed = jnp.concatenate([col('a_q'), col('a_k'), col('a_v'), col('a_o'), col('a_z'), col('b_q'), col('b_k'),
                              col('b_v'), col('b_iq'), misc, col('b_z'), col('g_a'), col('g_b')], axis=1)
    return packed.astype(BF16)


def _inproj_kernel(x_ref, g_ref, w_ref, cw_ref, cb_ref, mb_ref,
                   q_ref, k_ref, tail_ref, av_ref, ga_ref, bq_ref, kf_ref, vf_ref, kb_ref, vb_ref, iq_ref,
                   misc_ref, miscb_ref, zb_ref, sga_ref, sgb_ref, hist_ref, *, fuse_conv, tiles_per_seq):
    tm = x_ref.shape[0]
    x = x_ref[...]
    ms = jnp.mean(x * x, axis=-1, keepdims=True)
    xn = (x * lax.rsqrt(ms + EPS) * g_ref[...]).astype(BF16)

    def proj(lo, hi):
        return _dot(xn, w_ref[:, lo:hi])

    qk = proj(P_QK, P_QK + 2 * A_QK)
    tail_ref[0] = qk[tm - 8:tm, :]
    if fuse_conv:
        @pl.when(pl.program_id(0) % tiles_per_seq == 0)
        def _():
            hist_ref[0:8, :] = jnp.zeros((8, 2 * A_QK), F32)

        hist_ref[8:8 + tm, :] = qk
        cw = cw_ref[...]
        y = cb_ref[...] + cw[0:1, :] * hist_ref[5:5 + tm, :]
        y = y + cw[1:2, :] * hist_ref[6:6 + tm, :]
        y = y + cw[2:3, :] * hist_ref[7:7 + tm, :]
        y = y + cw[3:4, :] * qk
        act = _silu(y)
        q_ref[...] = (act[:, :A_QK] * Q_SCALE).astype(q_ref.dtype)
        k_ref[...] = act[:, A_QK:].astype(k_ref.dtype)
        hist_ref[0:8, :] = qk[tm - 8:tm, :]
    else:
        q_ref[...] = qk[:, :A_QK]
        k_ref[...] = qk[:, A_QK:]

    av_ref[...] = proj(P_AV, P_AV + A_WIDTH).astype(av_ref.dtype)
    ga_ref[...] = _sigmoid(proj(P_AO, P_AO + A_WIDTH)) * _silu(proj(P_AZ, P_AZ + A_WIDTH))
    bq_ref[...] = proj(P_BQ, P_BQ + B_WIDTH).astype(bq_ref.dtype)
    kf = proj(P_BK, P_BK + B_KV * B_HD)
    vf = proj(P_BV, P_BV + B_KV * B_HD)
    kf_ref[...] = kf
    vf_ref[...] = vf
    kb_ref[...] = kf.astype(BF16)
    vb_ref[...] = vf.astype(BF16)
    iq_ref[...] = proj(P_IQ, P_IQ + IDX_HEADS * IDX_DIM).astype(iq_ref.dtype)

    raw = proj(P_MISC, P_MISC + LANES) + mb_ref[...]
    lane = lax.broadcasted_iota(I32, raw.shape, 1)
    logsig = jnp.minimum(raw, 0.0) - jnp.log(1.0 + jnp.exp(-jnp.abs(raw)))
    misc = jnp.where((lane >= A_HEADS) & (lane < 2 * A_HEADS), logsig,
                     jnp.where((lane >= MISC_IW) & (lane < MISC_IW + IDX_HEADS), raw * IW_SCALE, raw))
    misc_ref[...] = misc
    miscb_ref[...] = jnp.where(lane >= MISC_IK, raw, 0.0).astype(BF16)

    zb_ref[...] = _silu(proj(P_BZ, P_BZ + B_WIDTH))
    sga_ref[...] = _sigmoid(proj(P_GA, P_GA + 1024))
    sgb_ref[...] = _sigmoid(proj(P_GB, P_GB + 1024))


def _inproj(x2d, norm_pre, w_packed, conv_w, conv_b, misc_bias, *, fuse_conv, seq_len, tm):
    m, d = x2d.shape
    nt = m // tm
    qk_dt = BF16 if fuse_conv else F32
    widths = [('q', A_QK, qk_dt), ('k', A_QK, qk_dt), ('tail', None, F32), ('av', A_WIDTH, BF16),
              ('ga', A_WIDTH, F32), ('bq', B_WIDTH, BF16), ('kf', B_KV * B_HD, F32), ('vf', B_KV * B_HD, F32),
              ('kb', B_KV * B_HD, BF16), ('vb', B_KV * B_HD, BF16), ('iq', IDX_HEADS * IDX_DIM, BF16),
              ('misc', LANES, F32), ('miscb', LANES, BF16), ('zb', B_WIDTH, F32), ('sga', d, F32), ('sgb', d, F32)]
    out_shape, out_specs = [], []
    for name, w, dt in widths:
        if name == 'tail':
            out_shape.append(jax.ShapeDtypeStruct((nt, 8, 2 * A_QK), dt))
            out_specs.append(pl.BlockSpec((1, 8, 2 * A_QK), lambda i: (i, 0, 0)))
        else:
            out_shape.append(jax.ShapeDtypeStruct((m, w), dt))
            out_specs.append(pl.BlockSpec((tm, w), lambda i: (i, 0)))
    const = lambda shape: pl.BlockSpec(shape, lambda i: (0,) * len(shape))
    outs = pl.pallas_call(
        functools.partial(_inproj_kernel, fuse_conv=fuse_conv, tiles_per_seq=max(seq_len // tm, 1)),
        grid=(nt,),
        in_specs=[pl.BlockSpec((tm, d), lambda i: (i, 0)), const((1, d)),
                  pl.BlockSpec((d, P_COLS), lambda i: (0, 0), pipeline_mode=pl.Buffered(1)),
                  const((CONV_W, 2 * A_QK)), const((1, 2 * A_QK)), const((1, LANES))],
        out_specs=out_specs,
        out_shape=out_shape,
        scratch_shapes=[pltpu.VMEM((8 + tm, 2 * A_QK), F32)],
        compiler_params=pltpu.CompilerParams(dimension_semantics=("arbitrary",), vmem_limit_bytes=VMEM_LIMIT),
        name="inproj_conv" if fuse_conv else "inproj",
    )(x2d, norm_pre, w_packed, conv_w, conv_b, misc_bias)
    return {name: o for (name, _, _), o in zip(widths, outs)}


def _outproj_kernel(ya_ref, yb_ref, sga_ref, sgb_ref, x_ref, pe_ref, wa_ref, wb_ref, wo_ref, wg_ref, wp_ref,
                    g_ref, o_ref):
    merged = sga_ref[...] * _dot(ya_ref[...], wa_ref[...]) + sgb_ref[...] * _dot(yb_ref[...], wb_ref[...])
    z = _dot(merged.astype(BF16), wo_ref[...])
    ms = jnp.mean(z * z, axis=-1, keepdims=True)
    x1 = x_ref[...] + z * lax.rsqrt(ms + EPS) * g_ref[...]
    gate = _sigmoid(_dot(x1.astype(BF16), wg_ref[...]))
    o_ref[...] = x1 + _dot(pe_ref[...].astype(BF16), wp_ref[...]) * gate


def _outproj(ya, yb, sga, sgb, x2d, pe2d, wa, wb, wo, wg, wp, norm_post, *, tm):
    m, d = x2d.shape
    pd = pe2d.shape[1]
    row = lambda w: pl.BlockSpec((tm, w), lambda i: (i, 0))
    const = lambda shape: pl.BlockSpec(shape, lambda i: (0,) * len(shape))
    return pl.pallas_call(
        _outproj_kernel,
        grid=(m // tm,),
        in_specs=[row(A_WIDTH), row(B_WIDTH), row(d), row(d), row(d), row(pd),
                  const((A_WIDTH, d)), const((B_WIDTH, d)), const((d, d)), const((d, d)), const((pd, d)),
                  const((1, d))],
        out_specs=row(d),
        out_shape=jax.ShapeDtypeStruct((m, d), F32),
        compiler_params=pltpu.CompilerParams(dimension_semantics=("arbitrary",), vmem_limit_bytes=VMEM_LIMIT),
        name="outproj",
    )(ya, yb, sga, sgb, x2d, pe2d, wa, wb, wo, wg, wp, norm_post)


def _cumsum_rows(x):
    n = x.shape[0]
    row = lax.broadcasted_iota(I32, x.shape, 0)
    s = 1
    while s < n:
        x = x + jnp.where(row >= s, pltpu.roll(x, s, axis=0), 0.0)
        s *= 2
    return x


def _lane_col(x, lane_iota, idx):
    return jnp.sum(jnp.where(lane_iota == idx, x, 0.0), axis=1, keepdims=True)


def _mlstm_prompt_kernel(q_ref, k_ref, v_ref, misc_ref, ga_ref, na_ref, ya_ref, c_out_ref, nm_out_ref,
                         c_ref, n_ref, m_ref):
    c_idx = pl.program_id(1)
    L = q_ref.shape[0]

    @pl.when(c_idx == 0)
    def _():
        c_ref[...] = jnp.zeros(c_ref.shape, F32)
        n_ref[...] = jnp.zeros(n_ref.shape, F32)
        m_ref[...] = jnp.zeros(m_ref.shape, F32)

    misc = misc_ref[...]
    lane = lax.broadcasted_iota(I32, misc.shape, 1)
    cs = _cumsum_rows(misc)
    b_al = pltpu.roll(cs, LANES - A_HEADS, axis=1)
    xa = jnp.where(lane < A_HEADS, misc - b_al, cs)
    xt = _transpose(xa)
    t_i = lax.broadcasted_iota(I32, (L, L), 0)
    s_i = lax.broadcasted_iota(I32, (L, L), 1)
    causal = s_i <= t_i

    for h in range(A_HEADS):
        a_col = _lane_col(xa, lane, h)
        b_col = _lane_col(xa, lane, A_HEADS + h)
        a_row = xt[h:h + 1, :]
        m_prev = m_ref[h:h + 1, 0:1]
        logit = jnp.where(causal, a_row, -jnp.inf)
        g_col = jnp.maximum(m_prev, jnp.max(logit, axis=1, keepdims=True))
        dmat = jnp.exp(logit - g_col)
        w_inter = jnp.exp(m_prev - g_col)
        qh = q_ref[:, h * A_DQK:(h + 1) * A_DQK]
        kh = k_ref[:, h * A_DQK:(h + 1) * A_DQK]
        vh = v_ref[:, h * A_DV:(h + 1) * A_DV]
        c_h = c_ref[h]
        n_h = n_ref[h:h + 1, :]
        s = _dot_nt(qh, kh) * dmat
        num = _dot(s.astype(BF16), vh) + w_inter * _dot_nt(qh, c_h.astype(BF16))
        den = jnp.sum(s, axis=1, keepdims=True) + w_inter * jnp.sum(qh.astype(F32) * n_h, axis=1, keepdims=True)
        m_t = b_col + g_col
        hh = num / jnp.maximum(jnp.abs(den), jnp.exp(-m_t))
        hn = hh * lax.rsqrt(jnp.mean(hh * hh, axis=1, keepdims=True) + EPS)
        sl = slice(h * A_DV, (h + 1) * A_DV)
        ya_ref[:, sl] = (ga_ref[:, sl] * (hn * na_ref[:, sl])).astype(ya_ref.dtype)

        g_last = g_col[L - 1:L, :]
        w_prev = jnp.exp(m_prev - g_last)
        w_s = jnp.exp(a_col - g_last)
        vw = vh.astype(F32) * w_s
        c_ref[h] = w_prev * c_h + _dot(_transpose(vw).astype(BF16), kh)
        n_ref[h:h + 1, :] = w_prev * n_h + jnp.sum(kh.astype(F32) * w_s, axis=0, keepdims=True)
        m_ref[h:h + 1, :] = jnp.broadcast_to(b_col[L - 1:L, :] + g_last, (1, LANES))

    @pl.when(c_idx == pl.num_programs(1) - 1)
    def _():
        c_out_ref[0] = c_ref[...]
        nm_out_ref[0, 0:A_HEADS, :] = n_ref[0:A_HEADS, :]
        nm_out_ref[0, A_HEADS:2 * A_HEADS, :] = m_ref[0:A_HEADS, :]


def _mlstm_prompt(q, k, v, misc, ga, norm_a, *, batch, seq_len, chunk):
    nc = seq_len // chunk
    row = lambda w: pl.BlockSpec((chunk, w), lambda b, c: (b * nc + c, 0))
    return pl.pallas_call(
        _mlstm_prompt_kernel,
        grid=(batch, nc),
        in_specs=[row(A_QK), row(A_QK), row(A_WIDTH), row(LANES), row(A_WIDTH),
                  pl.BlockSpec((1, A_WIDTH), lambda b, c: (0, 0))],
        out_specs=[row(A_WIDTH),
                   pl.BlockSpec((1, A_HEADS, A_DV, A_DQK), lambda b, c: (b, 0, 0, 0)),
                   pl.BlockSpec((1, 8, LANES), lambda b, c: (b, 0, 0))],
        out_shape=[jax.ShapeDtypeStruct((batch * seq_len, A_WIDTH), BF16),
                   jax.ShapeDtypeStruct((batch, A_HEADS, A_DV, A_DQK), F32),
                   jax.ShapeDtypeStruct((batch, 8, LANES), F32)],
        scratch_shapes=[pltpu.VMEM((A_HEADS, A_DV, A_DQK), F32), pltpu.VMEM((8, LANES), F32),
                        pltpu.VMEM((8, LANES), F32)],
        compiler_params=pltpu.CompilerParams(dimension_semantics=("arbitrary", "arbitrary"),
                                             vmem_limit_bytes=VMEM_LIMIT),
        name="mlstm_prompt",
    )(q, k, v, misc, ga, norm_a)


def _sortable_key(x):
    bits = pltpu.bitcast(x, I32)
    return jnp.where(bits < 0, bits ^ 0x7FFFFFFF, bits)


def _dsa_prompt_kernel(bq_ref, iq_ref, misc_ref, zb_ref, ikb_ref, kb_ref, vt_ref, yb_ref,
                       key_ref, acc_ref, m_ref, l_ref, *, topk, nq):
    j = pl.program_id(1)
    nkt = j + 1
    T = PAGE

    iq_t = _transpose(iq_ref[...].astype(F32))
    zeros_pad = jnp.zeros((LANES - IDX_DIM, T), F32)
    iq_pad = [jnp.concatenate([zeros_pad, iq_t[h * IDX_DIM:(h + 1) * IDX_DIM, :]], axis=0).astype(BF16)
              for h in range(IDX_HEADS)]
    misc_t = _transpose(misc_ref[...])
    q_t = _transpose(bq_ref[...].astype(F32))
    q_grp = [jnp.concatenate([q_t[(g * B_GROUP + hh) * B_HD:(g * B_GROUP + hh + 1) * B_HD, :]
                              for hh in range(B_GROUP)], axis=1).astype(BF16) for g in range(B_KV)]

    row_i = lax.broadcasted_iota(I32, (T, T), 0)
    lane_i = lax.broadcasted_iota(I32, (T, T), 1)

    def idx_body(c, carry):
        ikc = ikb_ref[0, c]
        acc = jnp.zeros((T, T), F32)
        for h in range(IDX_HEADS):
            r = jnp.maximum(_dot(ikc, iq_pad[h]) * IDX_SCALE, 0.0)
            acc = acc + r * misc_t[MISC_IW + h:MISC_IW + h + 1, :]
        vis = (c * T + row_i) <= (j * T + lane_i)
        key_ref[c] = _sortable_key(jnp.where(vis, acc, -jnp.inf))
        return carry

    lax.fori_loop(0, nkt, idx_body, 0)

    def count(pred):
        def body(c, cnt):
            m = pred(key_ref[c], c).astype(I32)
            return cnt + jnp.sum(m.reshape(T // 8, 8, T), axis=0)
        cnt8 = lax.fori_loop(0, nkt, body, jnp.zeros((8, T), I32))
        return jnp.sum(cnt8, axis=0, keepdims=True)

    def bit_body(i, u):
        bit = jnp.left_shift(jnp.int32(1), 31 - i)
        cand_u = u | bit
        cand_s = cand_u ^ INT_MIN
        cnt = count(lambda kt, c: kt >= cand_s)
        return jnp.where(cnt >= topk, cand_u, u)

    u = lax.fori_loop(0, 32, bit_body, jnp.zeros((1, T), I32))
    thr = u ^ INT_MIN
    n_gt = count(lambda kt, c: kt > thr)
    n_ge = count(lambda kt, c: kt >= thr)
    need = topk - n_gt

    def tie_search(_):
        nbits = (nq * T - 1).bit_length()
        def pbody(i, p):
            cand = p | jnp.left_shift(jnp.int32(1), nbits - 1 - i)
            cnt = count(lambda kt, c: (kt == thr) & ((c * T + row_i) < cand))
            return jnp.where(cnt < need, cand, p)
        return lax.fori_loop(0, nbits, pbody, jnp.zeros((1, T), I32))

    has_tie = jnp.max(jnp.where((n_ge > topk) & (thr > KEY_NEG_INF), 1, 0)) > 0
    p_lim = lax.cond(has_tie, tie_search, lambda _: jnp.full((1, T), 2 ** 30, I32), 0)

    acc_ref[...] = jnp.zeros(acc_ref.shape, F32)
    m_ref[...] = jnp.full(m_ref.shape, NEG_BIG, F32)
    l_ref[...] = jnp.zeros(l_ref.shape, F32)

    def att_body(c, carry):
        kt = key_ref[c]
        pos = c * T + row_i
        sel = ((kt > thr) | ((kt == thr) & (pos <= p_lim))) & (kt > KEY_NEG_INF)
        kc = kb_ref[0, c]
        vtc = vt_ref[0, c]
        for g in range(B_KV):
            s_g = _dot(kc[:, g * B_HD:(g + 1) * B_HD], q_grp[g]) * ATT_SCALE
            ps = []
            for hh in range(B_GROUP):
                h = g * B_GROUP + hh
                s = jnp.where(sel, s_g[:, hh * T:(hh + 1) * T], NEG_BIG)
                m_old = m_ref[h:h + 1, :]
                m_new = jnp.maximum(m_old, jnp.max(s, axis=0, keepdims=True))
                alpha = jnp.exp(m_old - m_new)
                p = jnp.exp(s - m_new)
                l_ref[h:h + 1, :] = alpha * l_ref[h:h + 1, :] + jnp.sum(p, axis=0, keepdims=True)
                m_ref[h:h + 1, :] = m_new
                acc_ref[h] = acc_ref[h] * alpha
                ps.append(p.astype(BF16))
            pv = _dot(vtc[g * B_HD:(g + 1) * B_HD, :], jnp.concatenate(ps, axis=1))
            for hh in range(B_GROUP):
                h = g * B_GROUP + hh
                acc_ref[h] = acc_ref[h] + pv[:, hh * T:(hh + 1) * T]
        return carry

    lax.fori_loop(0, nkt, att_body, 0)

    for h in range(B_HEADS):
        o_t = acc_ref[h] / l_ref[h:h + 1, :]
        sl = slice(h * B_HD, (h + 1) * B_HD)
        yb_ref[:, sl] = (_transpose(o_t) * zb_ref[:, sl]).astype(yb_ref.dtype)


def _dsa_prompt(bq, iq, misc, zb, ikb, kb, vt, *, batch, seq_len, topk):
    nq = seq_len // PAGE
    row = lambda w: pl.BlockSpec((PAGE, w), lambda b, j: (b * nq + j, 0))
    per_b = lambda a: pl.BlockSpec((1,) + a.shape[1:], lambda b, j: (b, 0, 0, 0))
    return pl.pallas_call(
        functools.partial(_dsa_prompt_kernel, topk=topk, nq=nq),
        grid=(batch, nq),
        in_specs=[row(B_WIDTH), row(IDX_HEADS * IDX_DIM), row(LANES), row(B_WIDTH),
                  per_b(ikb), per_b(kb), per_b(vt)],
        out_specs=row(B_WIDTH),
        out_shape=jax.ShapeDtypeStruct((batch * seq_len, B_WIDTH), BF16),
        scratch_shapes=[pltpu.VMEM((nq, PAGE, PAGE), I32), pltpu.VMEM((B_HEADS, B_HD, PAGE), F32),
                        pltpu.VMEM((B_HEADS, PAGE), F32), pltpu.VMEM((B_HEADS, PAGE), F32)],
        compiler_params=pltpu.CompilerParams(dimension_semantics=("arbitrary", "arbitrary"),
                                             vmem_limit_bytes=VMEM_LIMIT),
        name="dsa_prompt",
    )(bq, iq, misc, zb, ikb, kb, vt)


def _prompt_path(x, pe, w, w_packed, misc_bias):
    batch, seq_len, d = x.shape
    m = batch * seq_len
    x2d = x.reshape(m, d)
    a = _inproj(x2d, w['norm_pre'], w_packed, w['conv_w'], w['conv_b'], misc_bias,
                fuse_conv=True, seq_len=seq_len, tm=min(256, seq_len))
    chunk = min(128, seq_len)
    ya, c_out, nm_out = _mlstm_prompt(a['q'], a['k'], a['av'], a['misc'], a['ga'], w['norm_a'],
                                      batch=batch, seq_len=seq_len, chunk=chunk)
    nk = seq_len // PAGE
    ikb = a['miscb'].reshape(batch, nk, PAGE, LANES)
    kb = a['kb'].reshape(batch, nk, PAGE, B_KV * B_HD)
    vt = jnp.swapaxes(a['vb'].reshape(batch, nk, PAGE, B_KV * B_HD), 2, 3)
    topk = min(TOPK_MAX, seq_len // 4)
    yb = _dsa_prompt(a['bq'], a['iq'], a['misc'], a['zb'], ikb, kb, vt, batch=batch, seq_len=seq_len, topk=topk)
    y = _outproj(ya, yb, a['sga'], a['sgb'], x2d, pe.reshape(m, -1), w['wa'], w['wb'], w['wo'], w['wg'], w['wp'],
                 w['norm_post'], tm=min(256, m))
    tiles_per_seq = seq_len // min(256, seq_len)
    tail = a['tail'].reshape(batch, tiles_per_seq, 8, 2 * A_QK)[:, -1, 8 - (CONV_W - 1):, :]
    return dict(
        y=y.reshape(batch, seq_len, d),
        k=a['kf'].reshape(1, batch, seq_len, B_KV, B_HD),
        v=a['vf'].reshape(1, batch, seq_len, B_KV, B_HD),
        ik=a['misc'][:, MISC_IK:].reshape(1, batch, seq_len, IDX_DIM),
        C=c_out[None],
        n=nm_out[None, :, 0:A_HEADS, :],
        m=nm_out[None, :, A_HEADS:2 * A_HEADS, 0],
        conv=tail[None],
    )


def _mlstm_sample_kernel(q_ref, k_ref, conv0_ref, cw_ref, cb_ref, v_ref, misc_ref, ga_ref, na_ref, n0_ref, m0_ref,
                         c0_ref, ya_ref, c_out_ref, n_out_ref, m_out_ref,
                         qs_ref, ks_ref, cq_ref, wprev_ref, vwt_ref, numi_ref, den_ref, wint_ref, enm_ref, *, db, ds):
    b = pl.program_id(0)
    ms = ds * db
    rows = lambda t: slice(t * db, (t + 1) * db)

    @pl.when(b == 0)
    def _():
        cw = cw_ref[...]
        u = [conv0_ref[j] for j in range(CONV_W - 1)]
        u += [jnp.concatenate([q_ref[rows(t), :], k_ref[rows(t), :]], axis=1) for t in range(ds)]
        for t in range(ds):
            y = cb_ref[...] + cw[0:1, :] * u[t]
            for jj in range(1, CONV_W):
                y = y + cw[jj:jj + 1, :] * u[t + jj]
            act = _silu(y)
            qs_ref[rows(t), :] = act[:, :A_QK] * Q_SCALE
            ks_ref[rows(t), :] = act[:, A_QK:]
        cq_ref[...] = jnp.zeros(cq_ref.shape, F32)

        lane = lax.broadcasted_iota(I32, (db, LANES), 1)
        for h in range(A_HEADS):
            i_t = [_lane_col(misc_ref[rows(t), :], lane, h) for t in range(ds)]
            lf_t = [_lane_col(misc_ref[rows(t), :], lane, A_HEADS + h) for t in range(ds)]
            b_t = [lf_t[0]]
            for t in range(1, ds):
                b_t.append(b_t[-1] + lf_t[t])
            a_t = [i_t[t] - b_t[t] for t in range(ds)]
            m_prev = jnp.max(m0_ref[h], axis=1, keepdims=True)
            g_t = [jnp.maximum(m_prev, a_t[0])]
            for t in range(1, ds):
                g_t.append(jnp.maximum(g_t[-1], a_t[t]))
            hq = slice(h * A_DQK, (h + 1) * A_DQK)
            hv = slice(h * A_DV, (h + 1) * A_DV)
            qh = [qs_ref[rows(t), hq] for t in range(ds)]
            kh = [ks_ref[rows(t), hq] for t in range(ds)]
            vh = [v_ref[rows(t), hv].astype(F32) for t in range(ds)]
            n0 = n0_ref[h]
            for t in range(ds):
                num = jnp.zeros((db, A_DV), F32)
                den = jnp.zeros((db, 1), F32)
                for s in range(t + 1):
                    w_ts = jnp.sum(qh[t] * kh[s], axis=1, keepdims=True) * jnp.exp(a_t[s] - g_t[t])
                    num = num + w_ts * vh[s]
                    den = den + w_ts
                w_inter = jnp.exp(m_prev - g_t[t])
                den = den + w_inter * jnp.sum(qh[t] * n0, axis=1, keepdims=True)
                numi_ref[rows(t), hv] = num
                den_ref[h, rows(t), :] = jnp.broadcast_to(den, (db, LANES))
                wint_ref[h, rows(t), :] = jnp.broadcast_to(w_inter, (db, LANES))
                enm_ref[h, rows(t), :] = jnp.broadcast_to(jnp.exp(-(b_t[t] + g_t[t])), (db, LANES))
            g_last = g_t[ds - 1]
            w_prev = jnp.exp(m_prev - g_last)
            wprev_ref[h] = jnp.broadcast_to(w_prev, (db, LANES))
            n_new = w_prev * n0
            vw = []
            for s in range(ds):
                w_s = jnp.exp(a_t[s] - g_last)
                n_new = n_new + w_s * kh[s]
                vw.append(vh[s] * w_s)
            n_out_ref[h] = n_new
            m_out_ref[h] = jnp.broadcast_to(b_t[ds - 1] + g_last, (db, LANES))
            vw_all = jnp.concatenate(vw, axis=0)
            if ms < LANES:
                vw_all = jnp.concatenate([vw_all, jnp.zeros((LANES - ms, A_DV), F32)], axis=0)
            vwt_ref[h] = _transpose(vw_all)

    mcols = vwt_ref.shape[2]
    row_i = lax.broadcasted_iota(I32, (ms, A_DQK), 0)
    col_i = lax.broadcasted_iota(I32, (A_DV, mcols), 1)
    row_mine = row_i == b
    col_mine = col_i == b
    for t in range(1, ds):
        row_mine = row_mine | (row_i == b + t * db)
        col_mine = col_mine | (col_i == b + t * db)
    for h in range(A_HEADS):
        hq = slice(h * A_DQK, (h + 1) * A_DQK)
        c0 = c0_ref[0, h]
        q_mine = jnp.where(row_mine, qs_ref[:, hq], 0.0).astype(BF16)
        cq_ref[h] = cq_ref[h] + _dot_nt(q_mine, c0.astype(BF16))
        vw_mine = jnp.where(col_mine, vwt_ref[h], 0.0).astype(BF16)
        k_all = ks_ref[:, hq]
        if mcols > ms:
            k_all = jnp.concatenate([k_all, jnp.zeros((mcols - ms, A_DQK), F32)], axis=0)
        c_out_ref[0, h] = wprev_ref[h, pl.ds(b, 1), :] * c0 + _dot(vw_mine, k_all.astype(BF16))

    @pl.when(b == pl.num_programs(0) - 1)
    def _():
        for h in range(A_HEADS):
            hv = slice(h * A_DV, (h + 1) * A_DV)
            num = numi_ref[:, hv] + wint_ref[h][:, 0:1] * cq_ref[h]
            hh = num / jnp.maximum(jnp.abs(den_ref[h][:, 0:1]), enm_ref[h][:, 0:1])
            hn = hh * lax.rsqrt(jnp.mean(hh * hh, axis=1, keepdims=True) + EPS)
            ya_ref[:, hv] = (ga_ref[:, hv] * (hn * na_ref[:, hv])).astype(ya_ref.dtype)


def _mlstm_sample(q_t, k_t, conv0_t, conv_w, conv_b, v_t, misc_t, ga_t, norm_a, n0_t, m0_bc, c0, *, db, ds):
    ms = db * ds
    mcols = max(ms, LANES)
    full = lambda a: pl.BlockSpec(a.shape, lambda b: (0,) * a.ndim)
    cblk = pl.BlockSpec((1, A_HEADS, A_DV, A_DQK), lambda b: (b, 0, 0, 0))
    hshape = jax.ShapeDtypeStruct((A_HEADS, db, LANES), F32)
    ins = (q_t, k_t, conv0_t, conv_w, conv_b, v_t, misc_t, ga_t, norm_a, n0_t, m0_bc)
    return pl.pallas_call(
        functools.partial(_mlstm_sample_kernel, db=db, ds=ds),
        grid=(db,),
        in_specs=[full(a) for a in ins] + [cblk],
        out_specs=[pl.BlockSpec((ms, A_WIDTH), lambda b: (0, 0)), cblk,
                   pl.BlockSpec((A_HEADS, db, LANES), lambda b: (0, 0, 0)),
                   pl.BlockSpec((A_HEADS, db, LANES), lambda b: (0, 0, 0))],
        out_shape=[jax.ShapeDtypeStruct((ms, A_WIDTH), BF16), jax.ShapeDtypeStruct(c0.shape, F32), hshape, hshape],
        scratch_shapes=[pltpu.VMEM((ms, A_QK), F32), pltpu.VMEM((ms, A_QK), F32),
                        pltpu.VMEM((A_HEADS, ms, A_DV), F32), pltpu.VMEM((A_HEADS, db, LANES), F32),
                        pltpu.VMEM((A_HEADS, A_DV, mcols), F32), pltpu.VMEM((ms, A_WIDTH), F32),
                        pltpu.VMEM((A_HEADS, ms, LANES), F32), pltpu.VMEM((A_HEADS, ms, LANES), F32),
                        pltpu.VMEM((A_HEADS, ms, LANES), F32)],
        compiler_params=pltpu.CompilerParams(dimension_semantics=("arbitrary",), vmem_limit_bytes=VMEM_LIMIT),
        name="mlstm_sample",
    )(*ins, c0)


def _dsa_sample_select_kernel(pt_ref, iq_ref, iw_ref, iknew_ref, *refs, topk, pages_per_step, n_pages, ds):
    page_refs = refs[:pages_per_step]
    bias_ref, sc_ref = refs[pages_per_step:]
    g = pl.program_id(1)
    iq = iq_ref[0]
    iw = iw_ref[0]
    t_i = lax.broadcasted_iota(I32, (8, PAGE), 0)
    s_i = lax.broadcasted_iota(I32, (8, PAGE), 1)

    def scores(keys_bf16, visible):
        r = jnp.maximum(_dot_nt(iq, keys_bf16) * IDX_SCALE, 0.0) * iw
        sc = jnp.sum(r.reshape(8, IDX_HEADS, PAGE), axis=1)
        return jnp.where(visible, sc, -jnp.inf)

    for i in range(pages_per_step):
        sc_ref[g * pages_per_step + i] = scores(page_refs[i][0].astype(BF16), t_i < ds)

    @pl.when(g == pl.num_programs(1) - 1)
    def _():
        np1 = n_pages + 1
        sc_ref[n_pages] = scores(iknew_ref[0], (t_i < ds) & (s_i <= t_i))

        keys = _sortable_key(sc_ref[...])
        pos = (lax.broadcasted_iota(I32, keys.shape, 0) * PAGE + lax.broadcasted_iota(I32, keys.shape, 2))

        def count(mask):
            return jnp.sum(jnp.sum(mask.astype(I32), axis=0), axis=1, keepdims=True)

        def bit_body(i, u):
            cand_u = u | jnp.left_shift(jnp.int32(1), 31 - i)
            cnt = count(keys >= (cand_u ^ INT_MIN))
            return jnp.where(cnt >= topk, cand_u, u)

        thr = lax.fori_loop(0, 32, bit_body, jnp.zeros((8, 1), I32)) ^ INT_MIN
        need = topk - count(keys > thr)
        tied = keys == thr
        nbits = (np1 * PAGE - 1).bit_length()

        def pos_body(i, p):
            cand = p | jnp.left_shift(jnp.int32(1), nbits - 1 - i)
            cnt = count(tied & (pos < cand))
            return jnp.where(cnt < need, cand, p)

        p_lim = lax.fori_loop(0, nbits, pos_body, jnp.zeros((8, 1), I32))
        sel = ((keys > thr) | (tied & (pos <= p_lim))) & (keys > KEY_NEG_INF)
        bias_ref[0] = jnp.where(sel, 0.0, NEG_BIG)


def _dsa_sample_select(page_table, iq_s, iw_bc, ik_new, cache_ik, *, topk, ds, pages_per_step):
    db, n_pages = page_table.shape
    npg = n_pages // pages_per_step
    per_b = lambda a: pl.BlockSpec((1,) + a.shape[1:], lambda b, g, pt: (b,) + (0,) * (a.ndim - 1))
    page_specs = [pl.BlockSpec((1, PAGE, IDX_DIM), lambda b, g, pt, i=i: (pt[b, g * pages_per_step + i], 0, 0))
                  for i in range(pages_per_step)]
    out_sds = jax.ShapeDtypeStruct((db, n_pages + 1, 8, PAGE), F32)
    return pl.pallas_call(
        functools.partial(_dsa_sample_select_kernel, topk=topk, pages_per_step=pages_per_step, n_pages=n_pages,
                          ds=ds),
        grid_spec=pltpu.PrefetchScalarGridSpec(
            num_scalar_prefetch=1, grid=(db, npg),
            in_specs=[per_b(iq_s), per_b(iw_bc), per_b(ik_new)] + page_specs,
            out_specs=pl.BlockSpec((1, n_pages + 1, 8, PAGE), lambda b, g, pt: (b, 0, 0, 0)),
            scratch_shapes=[pltpu.VMEM((n_pages + 1, 8, PAGE), F32)]),
        out_shape=out_sds,
        compiler_params=pltpu.CompilerParams(dimension_semantics=("arbitrary", "arbitrary"),
                                             vmem_limit_bytes=VMEM_LIMIT),
        name="dsa_sample_select",
    )(page_table, iq_s, iw_bc, ik_new, *([cache_ik] * pages_per_step))


def _dsa_sample_attend_kernel(pt_ref, q_ref, bias_ref, knew_ref, vnew_ref, zb_ref, *refs, pages_per_step, n_pages,
                              ds):
    k_refs = refs[:pages_per_step]
    v_refs = refs[pages_per_step:2 * pages_per_step]
    o_ref, acc_ref, m_ref, l_ref = refs[2 * pages_per_step:]
    g = pl.program_id(1)
    rows_q = ds * B_HEADS
    q = q_ref[0]

    @pl.when(g == 0)
    def _():
        acc_ref[...] = jnp.zeros(acc_ref.shape, F32)
        m_ref[...] = jnp.full(m_ref.shape, NEG_BIG, F32)
        l_ref[...] = jnp.zeros(l_ref.shape, F32)

    def step(k_page, v_page, bias8):
        bias = jnp.concatenate([jnp.broadcast_to(bias8[t:t + 1, :], (B_HEADS, PAGE)) for t in range(ds)], axis=0)
        s = _dot_nt(q, k_page) * ATT_SCALE + bias
        m_old = m_ref[...]
        m_new = jnp.maximum(m_old, jnp.max(s, axis=1, keepdims=True))
        alpha = jnp.exp(m_old - m_new)
        p = jnp.exp(s - m_new)
        l_ref[...] = alpha * l_ref[...] + jnp.sum(p, axis=1, keepdims=True)
        m_ref[...] = m_new
        acc_ref[...] = alpha * acc_ref[...] + _dot(p.astype(BF16), v_page)

    for i in range(pages_per_step):
        step(k_refs[i][0].astype(BF16), v_refs[i][0].astype(BF16), bias_ref[0, g * pages_per_step + i])

    @pl.when(g == pl.num_programs(1) - 1)
    def _():
        step(knew_ref[0], vnew_ref[0], bias_ref[0, n_pages])
        o = acc_ref[...] / l_ref[...]
        head = lax.broadcasted_iota(I32, (rows_q, B_HD), 0) & (B_HEADS - 1)
        o_sel = o[:, 0:B_HD]
        for kv in range(1, B_KV):
            o_sel = jnp.where(head >= kv * B_GROUP, o[:, kv * B_HD:(kv + 1) * B_HD], o_sel)
        o_ref[0] = (o_sel * zb_ref[0]).astype(o_ref.dtype)


def _dsa_sample_attend(page_table, q_bd, bias, k_new, v_new, zb_s, cache_k, cache_v, *, ds, pages_per_step):
    db, n_pages = page_table.shape
    npg = n_pages // pages_per_step
    rows_q = ds * B_HEADS
    kvw = B_KV * B_HD
    per_b = lambda a: pl.BlockSpec((1,) + a.shape[1:], lambda b, g, pt: (b,) + (0,) * (a.ndim - 1))
    page_specs = [pl.BlockSpec((1, PAGE, kvw), lambda b, g, pt, i=i: (pt[b, g * pages_per_step + i], 0, 0))
                  for i in range(pages_per_step)]
    return pl.pallas_call(
        functools.partial(_dsa_sample_attend_kernel, pages_per_step=pages_per_step, n_pages=n_pages, ds=ds),
        grid_spec=pltpu.PrefetchScalarGridSpec(
            num_scalar_prefetch=1, grid=(db, npg),
            in_specs=[per_b(q_bd), per_b(bias), per_b(k_new), per_b(v_new), per_b(zb_s)] + page_specs + page_specs,
            out_specs=pl.BlockSpec((1, rows_q, B_HD), lambda b, g, pt: (b, 0, 0)),
            scratch_shapes=[pltpu.VMEM((rows_q, kvw), F32), pltpu.VMEM((rows_q, 1), F32),
                            pltpu.VMEM((rows_q, 1), F32)]),
        out_shape=jax.ShapeDtypeStruct((db, rows_q, B_HD), BF16),
        compiler_params=pltpu.CompilerParams(dimension_semantics=("arbitrary", "arbitrary"),
                                             vmem_limit_bytes=VMEM_LIMIT),
        name="dsa_sample_attend",
    )(page_table, q_bd, bias, k_new, v_new, zb_s, *([cache_k] * pages_per_step), *([cache_v] * pages_per_step))


def _sample_path(x, pe, cache_k, cache_v, cache_ik, page_table, c0, n0, m0, conv0, w, w_packed, misc_bias):
    db, ds, d = x.shape
    ms = db * ds
    assert ds >= CONV_W - 1 and ds <= 8
    n_pages = page_table.shape[1]
    a = _inproj(x.reshape(ms, d), w['norm_pre'], w_packed, w['conv_w'], w['conv_b'], misc_bias,
                fuse_conv=False, seq_len=ms, tm=ms)

    tmaj = lambda z: jnp.swapaxes(z.reshape(db, ds, -1), 0, 1).reshape(ms, -1)
    ya_t, c_new, n_new, m_new = _mlstm_sample(
        tmaj(a['q']), tmaj(a['k']), jnp.swapaxes(conv0, 0, 1), w['conv_w'], w['conv_b'], tmaj(a['av']),
        tmaj(a['misc']), tmaj(a['ga']), w['norm_a'], jnp.swapaxes(n0, 0, 1),
        jnp.broadcast_to(jnp.swapaxes(m0, 0, 1)[:, :, None], (A_HEADS, db, LANES)), c0, db=db, ds=ds)
    ya = jnp.swapaxes(ya_t.reshape(ds, db, -1), 0, 1).reshape(ms, -1)

    topk = min(TOPK_MAX, (n_pages * PAGE + ds) // 4)
    pad_rows = lambda z: jnp.pad(z.reshape(db, ds, -1), ((0, 0), (0, PAGE - ds), (0, 0))).astype(BF16)
    pad_tok = lambda z: jnp.pad(z.reshape(db, ds, -1), ((0, 0), (0, 8 - ds), (0, 0)))
    iq_s = pad_tok(a['iq']).reshape(db, 8 * IDX_HEADS, IDX_DIM)
    iw_bc = jnp.broadcast_to(pad_tok(a['misc'][:, MISC_IW:MISC_IW + IDX_HEADS]).reshape(db, 8 * IDX_HEADS, 1),
                             (db, 8 * IDX_HEADS, LANES))
    pps = 16 if n_pages % 16 == 0 else n_pages
    bias = _dsa_sample_select(page_table, iq_s, iw_bc, pad_rows(a['misc'][:, MISC_IK:]),
                              cache_ik.reshape(-1, PAGE, IDX_DIM), topk=topk, ds=ds, pages_per_step=pps)
    q5 = a['bq'].reshape(db, ds * B_HEADS, 1, B_HD)
    kv_of_row = (jnp.arange(ds * B_HEADS) % B_HEADS) // B_GROUP
    q_bd = jnp.where((kv_of_row[None, :, None, None] == jnp.arange(B_KV)[None, None, :, None]), q5,
                     jnp.zeros((), BF16)).reshape(db, ds * B_HEADS, B_KV * B_HD)
    pps = 8 if n_pages % 8 == 0 else n_pages
    attn = _dsa_sample_attend(page_table, q_bd, bias, pad_rows(a['kf']), pad_rows(a['vf']),
                              a['zb'].reshape(db, ds * B_HEADS, B_HD),
                              cache_k.reshape(-1, PAGE, B_KV * B_HD), cache_v.reshape(-1, PAGE, B_KV * B_HD),
                              ds=ds, pages_per_step=pps)
    yb = attn.reshape(ms, B_WIDTH)
    y = _outproj(ya, yb, a['sga'], a['sgb'], x.reshape(ms, d), pe.reshape(ms, -1), w['wa'], w['wb'], w['wo'],
                 w['wg'], w['wp'], w['norm_post'], tm=ms)
    qk_pre = jnp.concatenate([a['q'], a['k']], axis=1).reshape(db, ds, 2 * A_QK)
    return dict(
        y=y.reshape(db, ds, d),
        k=a['kf'].reshape(1, db, ds, B_KV, B_HD),
        v=a['vf'].reshape(1, db, ds, B_KV, B_HD),
        ik=a['misc'][:, MISC_IK:].reshape(1, db, ds, IDX_DIM),
        C=c_new[None],
        n=jnp.swapaxes(n_new, 0, 1)[None],
        m=jnp.swapaxes(m_new[:, :, 0], 0, 1)[None],
        conv=qk_pre[:, ds - (CONV_W - 1):, :][None],
    )


def _prep_weights(w_in, conv_w, conv_b, if_bias, norm_a, w_a_proj, w_b_proj, w_out, norm_pre, norm_post, w_ple,
                  w_ple_gate):
    d = w_in.shape[1]
    misc_bias = jnp.zeros((1, LANES), F32).at[0, :2 * A_HEADS].set(if_bias[0])
    w = dict(norm_pre=norm_pre[0][None], norm_post=norm_post[0][None], norm_a=norm_a[0][None],
             conv_w=conv_w[0], conv_b=conv_b[0][None],
             wa=w_a_proj[0].astype(BF16), wb=w_b_proj[0].astype(BF16), wo=w_out[0].astype(BF16),
             wg=w_ple_gate[0].astype(BF16), wp=w_ple[0].astype(BF16))
    return w, _pack_w_in(w_in[0], d), misc_bias


def kernel(x_prompt, x_sample, cache_k, cache_v, cache_idx_k, page_table, state_C, state_n, state_m, state_conv,
           p_prompt, p_sample, w_in, conv_w, conv_b, if_bias, norm_a, w_a_proj, w_b_proj, w_out, norm_pre,
           norm_post, w_ple, w_ple_gate):
    w, w_packed, misc_bias = _prep_weights(w_in, conv_w, conv_b, if_bias, norm_a, w_a_proj, w_b_proj, w_out,
                                           norm_pre, norm_post, w_ple, w_ple_gate)
    p = _prompt_path(x_prompt, p_prompt[0], w, w_packed, misc_bias)
    s = _sample_path(x_sample, p_sample[0], cache_k[0], cache_v[0], cache_idx_k[0], page_table, state_C[0],
                     state_n[0], state_m[0], state_conv[0], w, w_packed, misc_bias)
    names = ('k', 'v', 'ik', 'C', 'n', 'm', 'conv')
    return (p['y'], s['y']) + tuple(p[n] for n in names) + tuple(s[n] for n in names)
```

```python
import functools

import jax
import jax.numpy as jnp
from jax import lax
from jax.experimental import pallas as pl
from jax.experimental.pallas import tpu as pltpu

F32 = jnp.float32
BF16 = jnp.bfloat16
I32 = jnp.int32

EPS = 1e-6
A_HEADS = 4
A_DQK = 128
A_DV = 256
A_QK = A_HEADS * A_DQK
A_WIDTH = A_HEADS * A_DV
CONV_W = 4
B_HEADS = 8
B_KV = 2
B_HD = 128
B_GROUP = B_HEADS // B_KV
B_WIDTH = B_HEADS * B_HD
IDX_HEADS = 8
IDX_DIM = 64
TOPK_MAX = 256
PAGE = 128
LANES = 128
VMEM_LIMIT = 56 * 1024 * 1024

Q_SCALE = A_DQK ** -0.5
ATT_SCALE = B_HD ** -0.5
IDX_SCALE = IDX_DIM ** -0.5
IW_SCALE = IDX_HEADS ** -0.5
NEG_BIG = -1e30
INT_MIN = -(2 ** 31)
KEY_NEG_INF = INT_MIN + 0x007FFFFF

_SPLITS = (('a_q', A_QK), ('a_k', A_QK), ('a_v', A_WIDTH), ('a_i', A_HEADS), ('a_f', A_HEADS),
           ('a_o', A_WIDTH), ('a_z', A_WIDTH), ('b_q', B_WIDTH), ('b_k', B_KV * B_HD), ('b_v', B_KV * B_HD),
           ('b_iq', IDX_HEADS * IDX_DIM), ('b_ik', IDX_DIM), ('b_iw', IDX_HEADS), ('b_z', B_WIDTH),
           ('g_a', 1024), ('g_b', 1024))

P_QK = 0
P_AV = 1024
P_AO = 2048
P_AZ = 3072
P_BQ = 4096
P_BK = 5120
P_BV = 5376
P_IQ = 5632
P_MISC = 6144
P_BZ = 6272
P_GA = 7296
P_GB = 8320
P_COLS = 9344
MISC_IW = 8
MISC_IK = 64


def _sigmoid(x):
    return 1.0 / (1.0 + jnp.exp(-x))


def _silu(x):
    return x * _sigmoid(x)


def _dot(a, b):
    return jnp.dot(a, b, preferred_element_type=F32)


def _dot_nt(a, b):
    return lax.dot_general(a, b, (((1,), (1,)), ((), ())), preferred_element_type=F32)


def _transpose(x):
    r, c = x.shape
    rows = []
    for j in range(c // LANES):
        rows.append(jnp.concatenate([x[i * LANES:(i + 1) * LANES, j * LANES:(j + 1) * LANES].T
                                     for i in range(r // LANES)], axis=1))
    return jnp.concatenate(rows, axis=0)


def _pack_w_in(w_in, d_model):
    offs = {}
    c = 0
    for name, n in _SPLITS:
        offs[name] = (c, c + n)
        c += n

    def col(name):
        lo, hi = offs[name]
        return w_in[:, lo:hi]

    z = lambda n: jnp.zeros((d_model, n), w_in.dtype)
    misc = jnp.concatenate([col('a_i'), col('a_f'), col('b_iw'), z(MISC_IK - 16), col('b_ik')], axis=1)
    packed = jnp.concatenate([col('a_q'), col('a_k'), col('a_v'), col('a_o'), col('a_z'), col('b_q'), col('b_k'),
                              col('b_v'), col('b_iq'), misc, col('b_z'), col('g_a'), col('g_b')], axis=1)
    return packed.astype(BF16)


def _inproj_kernel(x_ref, g_ref, w_ref, cw_ref, cb_ref, mb_ref,
                   q_ref, k_ref, tail_ref, av_ref, ga_ref, bq_ref, kf_ref, vf_ref, kb_ref, vb_ref, iq_ref,
                   misc_ref, miscb_ref, zb_ref, sga_ref, sgb_ref, hist_ref, *, fuse_conv, tiles_per_seq):
    tm = x_ref.shape[0]
    x = x_ref[...]
    ms = jnp.mean(x * x, axis=-1, keepdims=True)
    xn = (x * lax.rsqrt(ms + EPS) * g_ref[...]).astype(BF16)

    def proj(lo, hi):
        return _dot(xn, w_ref[:, lo:hi])

    qk = proj(P_QK, P_QK + 2 * A_QK)
    tail_ref[0] = qk[tm - 8:tm, :]
    if fuse_conv:
        @pl.when(pl.program_id(0) % tiles_per_seq == 0)
        def _():
            hist_ref[0:8, :] = jnp.zeros((8, 2 * A_QK), F32)

        hist_ref[8:8 + tm, :] = qk
        cw = cw_ref[...]
        y = cb_ref[...] + cw[0:1, :] * hist_ref[5:5 + tm, :]
        y = y + cw[1:2, :] * hist_ref[6:6 + tm, :]
        y = y + cw[2:3, :] * hist_ref[7:7 + tm, :]
        y = y + cw[3:4, :] * qk
        act = _silu(y)
        q_ref[...] = (act[:, :A_QK] * Q_SCALE).astype(q_ref.dtype)
        k_ref[...] = act[:, A_QK:].astype(k_ref.dtype)
        hist_ref[0:8, :] = qk[tm - 8:tm, :]
    else:
        q_ref[...] = qk[:, :A_QK]
        k_ref[...] = qk[:, A_QK:]

    av_ref[...] = proj(P_AV, P_AV + A_WIDTH).astype(av_ref.dtype)
    ga_ref[...] = _sigmoid(proj(P_AO, P_AO + A_WIDTH)) * _silu(proj(P_AZ, P_AZ + A_WIDTH))
    bq_ref[...] = proj(P_BQ, P_BQ + B_WIDTH).astype(bq_ref.dtype)
    kf = proj(P_BK, P_BK + B_KV * B_HD)
    vf = proj(P_BV, P_BV + B_KV * B_HD)
    kf_ref[...] = kf
    vf_ref[...] = vf
    kb_ref[...] = kf.astype(BF16)
    vb_ref[...] = vf.astype(BF16)
    iq_ref[...] = proj(P_IQ, P_IQ + IDX_HEADS * IDX_DIM).astype(iq_ref.dtype)

    raw = proj(P_MISC, P_MISC + LANES) + mb_ref[...]
    lane = lax.broadcasted_iota(I32, raw.shape, 1)
    logsig = jnp.minimum(raw, 0.0) - jnp.log(1.0 + jnp.exp(-jnp.abs(raw)))
    misc = jnp.where((lane >= A_HEADS) & (lane < 2 * A_HEADS), logsig,
                     jnp.where((lane >= MISC_IW) & (lane < MISC_IW + IDX_HEADS), raw * IW_SCALE, raw))
    misc_ref[...] = misc
    miscb_ref[...] = jnp.where(lane >= MISC_IK, raw, 0.0).astype(BF16)

    zb_ref[...] = _silu(proj(P_BZ, P_BZ + B_WIDTH))
    sga_ref[...] = _sigmoid(proj(P_GA, P_GA + 1024))
    sgb_ref[...] = _sigmoid(proj(P_GB, P_GB + 1024))


def _inproj(x2d, norm_pre, w_packed, conv_w, conv_b, misc_bias, *, fuse_conv, seq_len, tm):
    m, d = x2d.shape
    nt = m // tm
    qk_dt = BF16 if fuse_conv else F32
    widths = [('q', A_QK, qk_dt), ('k', A_QK, qk_dt), ('tail', None, F32), ('av', A_WIDTH, BF16),
              ('ga', A_WIDTH, F32), ('bq', B_WIDTH, BF16), ('kf', B_KV * B_HD, F32), ('vf', B_KV * B_HD, F32),
              ('kb', B_KV * B_HD, BF16), ('vb', B_KV * B_HD, BF16), ('iq', IDX_HEADS * IDX_DIM, BF16),
              ('misc', LANES, F32), ('miscb', LANES, BF16), ('zb', B_WIDTH, F32), ('sga', d, F32), ('sgb', d, F32)]
    out_shape, out_specs = [], []
    for name, w, dt in widths:
        if name == 'tail':
            out_shape.append(jax.ShapeDtypeStruct((nt, 8, 2 * A_QK), dt))
            out_specs.append(pl.BlockSpec((1, 8, 2 * A_QK), lambda i: (i, 0, 0)))
        else:
            out_shape.append(jax.ShapeDtypeStruct((m, w), dt))
            out_specs.append(pl.BlockSpec((tm, w), lambda i: (i, 0)))
    const = lambda shape: pl.BlockSpec(shape, lambda i: (0,) * len(shape))
    outs = pl.pallas_call(
        functools.partial(_inproj_kernel, fuse_conv=fuse_conv, tiles_per_seq=max(seq_len // tm, 1)),
        grid=(nt,),
        in_specs=[pl.BlockSpec((tm, d), lambda i: (i, 0)), const((1, d)),
                  pl.BlockSpec((d, P_COLS), lambda i: (0, 0), pipeline_mode=pl.Buffered(1)),
                  const((CONV_W, 2 * A_QK)), const((1, 2 * A_QK)), const((1, LANES))],
        out_specs=out_specs,
        out_shape=out_shape,
        scratch_shapes=[pltpu.VMEM((8 + tm, 2 * A_QK), F32)],
        compiler_params=pltpu.CompilerParams(dimension_semantics=("arbitrary",), vmem_limit_bytes=VMEM_LIMIT),
        name="inproj_conv" if fuse_conv else "inproj",
    )(x2d, norm_pre, w_packed, conv_w, conv_b, misc_bias)
    return {name: o for (name, _, _), o in zip(widths, outs)}


def _outproj_kernel(ya_ref, yb_ref, sga_ref, sgb_ref, x_ref, pe_ref, wa_ref, wb_ref, wo_ref, wg_ref, wp_ref,
                    g_ref, o_ref):
    merged = sga_ref[...] * _dot(ya_ref[...], wa_ref[...]) + sgb_ref[...] * _dot(yb_ref[...], wb_ref[...])
    z = _dot(merged.astype(BF16), wo_ref[...])
    ms = jnp.mean(z * z, axis=-1, keepdims=True)
    x1 = x_ref[...] + z * lax.rsqrt(ms + EPS) * g_ref[...]
    gate = _sigmoid(_dot(x1.astype(BF16), wg_ref[...]))
    o_ref[...] = x1 + _dot(pe_ref[...].astype(BF16), wp_ref[...]) * gate


def _outproj(ya, yb, sga, sgb, x2d, pe2d, wa, wb, wo, wg, wp, norm_post, *, tm):
    m, d = x2d.shape
    pd = pe2d.shape[1]
    row = lambda w: pl.BlockSpec((tm, w), lambda i: (i, 0))
    const = lambda shape: pl.BlockSpec(shape, lambda i: (0,) * len(shape))
    return pl.pallas_call(
        _outproj_kernel,
        grid=(m // tm,),
        in_specs=[row(A_WIDTH), row(B_WIDTH), row(d), row(d), row(d), row(pd),
                  const((A_WIDTH, d)), const((B_WIDTH, d)), const((d, d)), const((d, d)), const((pd, d)),
                  const((1, d))],
        out_specs=row(d),
        out_shape=jax.ShapeDtypeStruct((m, d), F32),
        compiler_params=pltpu.CompilerParams(dimension_semantics=("arbitrary",), vmem_limit_bytes=VMEM_LIMIT),
        name="outproj",
    )(ya, yb, sga, sgb, x2d, pe2d, wa, wb, wo, wg, wp, norm_post)


def _cumsum_rows(x):
    n = x.shape[0]
    row = lax.broadcasted_iota(I32, x.shape, 0)
    s = 1
    while s < n:
        x = x + jnp.where(row >= s, pltpu.roll(x, s, axis=0), 0.0)
        s *= 2
    return x


def _lane_col(x, lane_iota, idx):
    return jnp.sum(jnp.where(lane_iota == idx, x, 0.0), axis=1, keepdims=True)


def _mlstm_prompt_kernel(q_ref, k_ref, v_ref, misc_ref, ga_ref, na_ref, ya_ref, c_out_ref, nm_out_ref,
                         c_ref, n_ref, m_ref):
    c_idx = pl.program_id(1)
    L = q_ref.shape[0]

    @pl.when(c_idx == 0)
    def _():
        c_ref[...] = jnp.zeros(c_ref.shape, F32)
        n_ref[...] = jnp.zeros(n_ref.shape, F32)
        m_ref[...] = jnp.zeros(m_ref.shape, F32)

    misc = misc_ref[...]
    lane = lax.broadcasted_iota(I32, misc.shape, 1)
    cs = _cumsum_rows(misc)
    b_al = pltpu.roll(cs, LANES - A_HEADS, axis=1)
    xa = jnp.where(lane < A_HEADS, misc - b_al, cs)
    xt = _transpose(xa)
    t_i = lax.broadcasted_iota(I32, (L, L), 0)
    s_i = lax.broadcasted_iota(I32, (L, L), 1)
    causal = s_i <= t_i

    for h in range(A_HEADS):
        a_col = _lane_col(xa, lane, h)
        b_col = _lane_col(xa, lane, A_HEADS + h)
        a_row = xt[h:h + 1, :]
        m_prev = m_ref[h:h + 1, 0:1]
        logit = jnp.where(causal, a_row, -jnp.inf)
        g_col = jnp.maximum(m_prev, jnp.max(logit, axis=1, keepdims=True))
        dmat = jnp.exp(logit - g_col)
        w_inter = jnp.exp(m_prev - g_col)
        qh = q_ref[:, h * A_DQK:(h + 1) * A_DQK]
        kh = k_ref[:, h * A_DQK:(h + 1) * A_DQK]
        vh = v_ref[:, h * A_DV:(h + 1) * A_DV]
        c_h = c_ref[h]
        n_h = n_ref[h:h + 1, :]
        s = _dot_nt(qh, kh) * dmat
        num = _dot(s.astype(BF16), vh) + w_inter * _dot_nt(qh, c_h.astype(BF16))
        den = jnp.sum(s, axis=1, keepdims=True) + w_inter * jnp.sum(qh.astype(F32) * n_h, axis=1, keepdims=True)
        m_t = b_col + g_col
        hh = num / jnp.maximum(jnp.abs(den), jnp.exp(-m_t))
        hn = hh * lax.rsqrt(jnp.mean(hh * hh, axis=1, keepdims=True) + EPS)
        sl = slice(h * A_DV, (h + 1) * A_DV)
        ya_ref[:, sl] = (ga_ref[:, sl] * (hn * na_ref[:, sl])).astype(ya_ref.dtype)

        g_last = g_col[L - 1:L, :]
        w_prev = jnp.exp(m_prev - g_last)
        w_s = jnp.exp(a_col - g_last)
        vw = vh.astype(F32) * w_s
        c_ref[h] = w_prev * c_h + _dot(_transpose(vw).astype(BF16), kh)
        n_ref[h:h + 1, :] = w_prev * n_h + jnp.sum(kh.astype(F32) * w_s, axis=0, keepdims=True)
        m_ref[h:h + 1, :] = jnp.broadcast_to(b_col[L - 1:L, :] + g_last, (1, LANES))

    @pl.when(c_idx == pl.num_programs(1) - 1)
    def _():
        c_out_ref[0] = c_ref[...]
        nm_out_ref[0, 0:A_HEADS, :] = n_ref[0:A_HEADS, :]
        nm_out_ref[0, A_HEADS:2 * A_HEADS, :] = m_ref[0:A_HEADS, :]


def _mlstm_prompt(q, k, v, misc, ga, norm_a, *, batch, seq_len, chunk):
    nc = seq_len // chunk
    row = lambda w: pl.BlockSpec((chunk, w), lambda b, c: (b * nc + c, 0))
    return pl.pallas_call(
        _mlstm_prompt_kernel,
        grid=(batch, nc),
        in_specs=[row(A_QK), row(A_QK), row(A_WIDTH), row(LANES), row(A_WIDTH),
                  pl.BlockSpec((1, A_WIDTH), lambda b, c: (0, 0))],
        out_specs=[row(A_WIDTH),
                   pl.BlockSpec((1, A_HEADS, A_DV, A_DQK), lambda b, c: (b, 0, 0, 0)),
                   pl.BlockSpec((1, 8, LANES), lambda b, c: (b, 0, 0))],
        out_shape=[jax.ShapeDtypeStruct((batch * seq_len, A_WIDTH), BF16),
                   jax.ShapeDtypeStruct((batch, A_HEADS, A_DV, A_DQK), F32),
                   jax.ShapeDtypeStruct((batch, 8, LANES), F32)],
        scratch_shapes=[pltpu.VMEM((A_HEADS, A_DV, A_DQK), F32), pltpu.VMEM((8, LANES), F32),
                        pltpu.VMEM((8, LANES), F32)],
        compiler_params=pltpu.CompilerParams(dimension_semantics=("arbitrary", "arbitrary"),
                                             vmem_limit_bytes=VMEM_LIMIT),
        name="mlstm_prompt",
    )(q, k, v, misc, ga, norm_a)


def _key_to_f32(key):
    return pltpu.bitcast(jnp.where(key < 0, key ^ 0x7FFFFFFF, key), F32)


def _radix_threshold(count_ge, shape, topk):
    def bit_body(i, u):
        cand_u = u | jnp.left_shift(jnp.int32(1), 31 - i)
        cnt = count_ge(_key_to_f32(cand_u ^ INT_MIN))
        return jnp.where(cnt >= topk, cand_u, u)

    u = lax.fori_loop(0, 32, bit_body, jnp.zeros(shape, I32))
    return _key_to_f32(jnp.maximum(u ^ INT_MIN, KEY_NEG_INF))


def _dsa_prompt_kernel(bq_ref, iq_ref, misc_ref, zb_ref, ikb_ref, kb_ref, vt_ref, yb_ref,
                       sc_ref, acc_ref, m_ref, l_ref, *, topk, nq):
    j = pl.program_id(1)
    T = PAGE
    KT = 2 * PAGE
    nkt = j // 2 + 1

    iq_t = _transpose(iq_ref[...].astype(F32))
    zeros_pad = jnp.zeros((LANES - IDX_DIM, T), F32)
    iq_pad = [jnp.concatenate([zeros_pad, iq_t[h * IDX_DIM:(h + 1) * IDX_DIM, :]], axis=0).astype(BF16)
              for h in range(IDX_HEADS)]
    misc_t = _transpose(misc_ref[...])
    q_t = _transpose(bq_ref[...].astype(F32))
    q_grp = [jnp.concatenate([q_t[(g * B_GROUP + hh) * B_HD:(g * B_GROUP + hh + 1) * B_HD, :]
                              for hh in range(B_GROUP)], axis=1).astype(BF16) for g in range(B_KV)]

    row_i = lax.broadcasted_iota(I32, (KT, T), 0)
    lane_i = lax.broadcasted_iota(I32, (KT, T), 1)

    def idx_body(c, carry):
        ikc = ikb_ref[0, c]
        acc = jnp.zeros((KT, T), F32)
        for h in range(IDX_HEADS):
            r = jnp.maximum(_dot(ikc, iq_pad[h]) * IDX_SCALE, 0.0)
            acc = acc + r * misc_t[MISC_IW + h:MISC_IW + h + 1, :]
        vis = (c * KT + row_i) <= (j * T + lane_i)
        sc_ref[c] = jnp.where(vis, acc, -jnp.inf)
        return carry

    lax.fori_loop(0, nkt, idx_body, 0)

    def count(pred):
        def body(c, cnt):
            m = pred(sc_ref[c], c).astype(I32)
            return cnt + jnp.sum(m.reshape(4, KT // 32, 8, T), axis=1)
        cnt4 = lax.fori_loop(0, nkt, body, jnp.zeros((4, 8, T), I32))
        return jnp.sum(jnp.sum(cnt4, axis=0), axis=0, keepdims=True)

    thr = _radix_threshold(lambda t: count(lambda sc, c: sc >= t), (1, T), topk)
    n_gt = count(lambda sc, c: sc > thr)
    n_ge = count(lambda sc, c: sc >= thr)
    need = topk - n_gt

    def tie_search(_):
        nbits = (nq * T - 1).bit_length()
        def pbody(i, p):
            cand = p | jnp.left_shift(jnp.int32(1), nbits - 1 - i)
            cnt = count(lambda sc, c: (sc == thr) & ((c * KT + row_i) < cand))
            return jnp.where(cnt < need, cand, p)
        return lax.fori_loop(0, nbits, pbody, jnp.zeros((1, T), I32))

    has_tie = jnp.max(jnp.where((n_ge > topk) & (thr > -jnp.inf), 1, 0)) > 0
    p_lim = lax.cond(has_tie, tie_search, lambda _: jnp.full((1, T), 2 ** 30, I32), 0)

    acc_ref[...] = jnp.zeros(acc_ref.shape, F32)
    m_ref[...] = jnp.full(m_ref.shape, NEG_BIG, F32)
    l_ref[...] = jnp.zeros(l_ref.shape, F32)

    def att_body(c, carry):
        sc = sc_ref[c]
        pos = c * KT + row_i
        sel = ((sc > thr) | ((sc == thr) & (pos <= p_lim))) & (sc > -jnp.inf)
        kc = kb_ref[0, c]
        vtc = vt_ref[0, c]
        for g in range(B_KV):
            s_g = _dot(kc[:, g * B_HD:(g + 1) * B_HD], q_grp[g]) * ATT_SCALE
            ps = []
            for hh in range(B_GROUP):
                h = g * B_GROUP + hh
                s = jnp.where(sel, s_g[:, hh * T:(hh + 1) * T], NEG_BIG)
                m_old = m_ref[h:h + 1, :]
                m_new = jnp.maximum(m_old, jnp.max(s, axis=0, keepdims=True))
                alpha = jnp.exp(m_old - m_new)
                p = jnp.exp(s - m_new)
                l_ref[h:h + 1, :] = alpha * l_ref[h:h + 1, :] + jnp.sum(p, axis=0, keepdims=True)
                m_ref[h:h + 1, :] = m_new
                acc_ref[h] = acc_ref[h] * alpha
                ps.append(p.astype(BF16))
            pv = _dot(vtc[g * B_HD:(g + 1) * B_HD, :], jnp.concatenate(ps, axis=1))
            for hh in range(B_GROUP):
                h = g * B_GROUP + hh
                acc_ref[h] = acc_ref[h] + pv[:, hh * T:(hh + 1) * T]
        return carry

    lax.fori_loop(0, nkt, att_body, 0)

    for h in range(B_HEADS):
        o_t = acc_ref[h] / l_ref[h:h + 1, :]
        sl = slice(h * B_HD, (h + 1) * B_HD)
        yb_ref[:, sl] = (_transpose(o_t) * zb_ref[:, sl]).astype(yb_ref.dtype)


def _dsa_prompt(bq, iq, misc, zb, ikb, kb, vt, *, batch, seq_len, topk):
    nq = seq_len // PAGE
    row = lambda w: pl.BlockSpec((PAGE, w), lambda b, j: (b * nq + j, 0))
    per_b = lambda a: pl.BlockSpec((1,) + a.shape[1:], lambda b, j: (b, 0, 0, 0))
    return pl.pallas_call(
        functools.partial(_dsa_prompt_kernel, topk=topk, nq=nq),
        grid=(batch, nq),
        in_specs=[row(B_WIDTH), row(IDX_HEADS * IDX_DIM), row(LANES), row(B_WIDTH),
                  per_b(ikb), per_b(kb), per_b(vt)],
        out_specs=row(B_WIDTH),
        out_shape=jax.ShapeDtypeStruct((batch * seq_len, B_WIDTH), BF16),
        scratch_shapes=[pltpu.VMEM((nq // 2, 2 * PAGE, PAGE), F32), pltpu.VMEM((B_HEADS, B_HD, PAGE), F32),
                        pltpu.VMEM((B_HEADS, PAGE), F32), pltpu.VMEM((B_HEADS, PAGE), F32)],
        compiler_params=pltpu.CompilerParams(dimension_semantics=("arbitrary", "arbitrary"),
                                             vmem_limit_bytes=VMEM_LIMIT),
        name="dsa_prompt",
    )(bq, iq, misc, zb, ikb, kb, vt)


def _prompt_path(x, pe, w, w_packed, misc_bias):
    batch, seq_len, d = x.shape
    m = batch * seq_len
    x2d = x.reshape(m, d)
    a = _inproj(x2d, w['norm_pre'], w_packed, w['conv_w'], w['conv_b'], misc_bias,
                fuse_conv=True, seq_len=seq_len, tm=min(256, seq_len))
    chunk = min(128, seq_len)
    ya, c_out, nm_out = _mlstm_prompt(a['q'], a['k'], a['av'], a['misc'], a['ga'], w['norm_a'],
                                      batch=batch, seq_len=seq_len, chunk=chunk)
    nk2 = seq_len // (2 * PAGE)
    ikb = a['miscb'].reshape(batch, nk2, 2 * PAGE, LANES)
    kb = a['kb'].reshape(batch, nk2, 2 * PAGE, B_KV * B_HD)
    vt = jnp.swapaxes(a['vb'].reshape(batch, nk2, 2 * PAGE, B_KV * B_HD), 2, 3)
    topk = min(TOPK_MAX, seq_len // 4)
    yb = _dsa_prompt(a['bq'], a['iq'], a['misc'], a['zb'], ikb, kb, vt, batch=batch, seq_len=seq_len, topk=topk)
    y = _outproj(ya, yb, a['sga'], a['sgb'], x2d, pe.reshape(m, -1), w['wa'], w['wb'], w['wo'], w['wg'], w['wp'],
                 w['norm_post'], tm=min(256, m))
    tiles_per_seq = seq_len // min(256, seq_len)
    tail = a['tail'].reshape(batch, tiles_per_seq, 8, 2 * A_QK)[:, -1, 8 - (CONV_W - 1):, :]
    return dict(
        y=y.reshape(batch, seq_len, d),
        k=a['kf'].reshape(1, batch, seq_len, B_KV, B_HD),
        v=a['vf'].reshape(1, batch, seq_len, B_KV, B_HD),
        ik=a['misc'][:, MISC_IK:].reshape(1, batch, seq_len, IDX_DIM),
        C=c_out[None],
        n=nm_out[None, :, 0:A_HEADS, :],
        m=nm_out[None, :, A_HEADS:2 * A_HEADS, 0],
        conv=tail[None],
    )


def _mlstm_sample_kernel(q_ref, k_ref, conv0_ref, cw_ref, cb_ref, v_ref, misc_ref, ga_ref, na_ref, n0_ref, m0_ref,
                         c0_ref, ya_ref, c_out_ref, n_out_ref, m_out_ref,
                         qs_ref, ks_ref, cq_ref, wprev_ref, vwt_ref, numi_ref, den_ref, wint_ref, enm_ref, *, db, ds):
    b = pl.program_id(0)
    ms = ds * db
    rows = lambda t: slice(t * db, (t + 1) * db)

    @pl.when(b == 0)
    def _():
        cw = cw_ref[...]
        u = [conv0_ref[j] for j in range(CONV_W - 1)]
        u += [jnp.concatenate([q_ref[rows(t), :], k_ref[rows(t), :]], axis=1) for t in range(ds)]
        for t in range(ds):
            y = cb_ref[...] + cw[0:1, :] * u[t]
            for jj in range(1, CONV_W):
                y = y + cw[jj:jj + 1, :] * u[t + jj]
            act = _silu(y)
            qs_ref[rows(t), :] = act[:, :A_QK] * Q_SCALE
            ks_ref[rows(t), :] = act[:, A_QK:]
        cq_ref[...] = jnp.zeros(cq_ref.shape, F32)

        lane = lax.broadcasted_iota(I32, (db, LANES), 1)
        for h in range(A_HEADS):
            i_t = [_lane_col(misc_ref[rows(t), :], lane, h) for t in range(ds)]
            lf_t = [_lane_col(misc_ref[rows(t), :], lane, A_HEADS + h) for t in range(ds)]
            b_t = [lf_t[0]]
            for t in range(1, ds):
                b_t.append(b_t[-1] + lf_t[t])
            a_t = [i_t[t] - b_t[t] for t in range(ds)]
            m_prev = jnp.max(m0_ref[h], axis=1, keepdims=True)
            g_t = [jnp.maximum(m_prev, a_t[0])]
            for t in range(1, ds):
                g_t.append(jnp.maximum(g_t[-1], a_t[t]))
            hq = slice(h * A_DQK, (h + 1) * A_DQK)
            hv = slice(h * A_DV, (h + 1) * A_DV)
            qh = [qs_ref[rows(t), hq] for t in range(ds)]
            kh = [ks_ref[rows(t), hq] for t in range(ds)]
            vh = [v_ref[rows(t), hv].astype(F32) for t in range(ds)]
            n0 = n0_ref[h]
            for t in range(ds):
                num = jnp.zeros((db, A_DV), F32)
                den = jnp.zeros((db, 1), F32)
                for s in range(t + 1):
                    w_ts = jnp.sum(qh[t] * kh[s], axis=1, keepdims=True) * jnp.exp(a_t[s] - g_t[t])
                    num = num + w_ts * vh[s]
                    den = den + w_ts
                w_inter = jnp.exp(m_prev - g_t[t])
                den = den + w_inter * jnp.sum(qh[t] * n0, axis=1, keepdims=True)
                numi_ref[rows(t), hv] = num
                den_ref[h, rows(t), :] = jnp.broadcast_to(den, (db, LANES))
                wint_ref[h, rows(t), :] = jnp.broadcast_to(w_inter, (db, LANES))
                enm_ref[h, rows(t), :] = jnp.broadcast_to(jnp.exp(-(b_t[t] + g_t[t])), (db, LANES))
            g_last = g_t[ds - 1]
            w_prev = jnp.exp(m_prev - g_last)
            wprev_ref[h] = jnp.broadcast_to(w_prev, (db, LANES))
            n_new = w_prev * n0
            vw = []
            for s in range(ds):
                w_s = jnp.exp(a_t[s] - g_last)
                n_new = n_new + w_s * kh[s]
                vw.append(vh[s] * w_s)
            n_out_ref[h] = n_new
            m_out_ref[h] = jnp.broadcast_to(b_t[ds - 1] + g_last, (db, LANES))
            vw_all = jnp.concatenate(vw, axis=0)
            if ms < LANES:
                vw_all = jnp.concatenate([vw_all, jnp.zeros((LANES - ms, A_DV), F32)], axis=0)
            vwt_ref[h] = _transpose(vw_all)

    mcols = vwt_ref.shape[2]
    row_i = lax.broadcasted_iota(I32, (ms, A_DQK), 0)
    col_i = lax.broadcasted_iota(I32, (A_DV, mcols), 1)
    row_mine = row_i == b
    col_mine = col_i == b
    for t in range(1, ds):
        row_mine = row_mine | (row_i == b + t * db)
        col_mine = col_mine | (col_i == b + t * db)
    for h in range(A_HEADS):
        hq = slice(h * A_DQK, (h + 1) * A_DQK)
        c0 = c0_ref[0, h]
        q_mine = jnp.where(row_mine, qs_ref[:, hq], 0.0).astype(BF16)
        cq_ref[h] = cq_ref[h] + _dot_nt(q_mine, c0.astype(BF16))
        vw_mine = jnp.where(col_mine, vwt_ref[h], 0.0).astype(BF16)
        k_all = ks_ref[:, hq]
        if mcols > ms:
            k_all = jnp.concatenate([k_all, jnp.zeros((mcols - ms, A_DQK), F32)], axis=0)
        c_out_ref[0, h] = wprev_ref[h, pl.ds(b, 1), :] * c0 + _dot(vw_mine, k_all.astype(BF16))

    @pl.when(b == pl.num_programs(0) - 1)
    def _():
        for h in range(A_HEADS):
            hv = slice(h * A_DV, (h + 1) * A_DV)
            num = numi_ref[:, hv] + wint_ref[h][:, 0:1] * cq_ref[h]
            hh = num / jnp.maximum(jnp.abs(den_ref[h][:, 0:1]), enm_ref[h][:, 0:1])
            hn = hh * lax.rsqrt(jnp.mean(hh * hh, axis=1, keepdims=True) + EPS)
            ya_ref[:, hv] = (ga_ref[:, hv] * (hn * na_ref[:, hv])).astype(ya_ref.dtype)


def _mlstm_sample(q_t, k_t, conv0_t, conv_w, conv_b, v_t, misc_t, ga_t, norm_a, n0_t, m0_bc, c0, *, db, ds):
    ms = db * ds
    mcols = max(ms, LANES)
    full = lambda a: pl.BlockSpec(a.shape, lambda b: (0,) * a.ndim)
    cblk = pl.BlockSpec((1, A_HEADS, A_DV, A_DQK), lambda b: (b, 0, 0, 0))
    hshape = jax.ShapeDtypeStruct((A_HEADS, db, LANES), F32)
    ins = (q_t, k_t, conv0_t, conv_w, conv_b, v_t, misc_t, ga_t, norm_a, n0_t, m0_bc)
    return pl.pallas_call(
        functools.partial(_mlstm_sample_kernel, db=db, ds=ds),
        grid=(db,),
        in_specs=[full(a) for a in ins] + [cblk],
        out_specs=[pl.BlockSpec((ms, A_WIDTH), lambda b: (0, 0)), cblk,
                   pl.BlockSpec((A_HEADS, db, LANES), lambda b: (0, 0, 0)),
                   pl.BlockSpec((A_HEADS, db, LANES), lambda b: (0, 0, 0))],
        out_shape=[jax.ShapeDtypeStruct((ms, A_WIDTH), BF16), jax.ShapeDtypeStruct(c0.shape, F32), hshape, hshape],
        scratch_shapes=[pltpu.VMEM((ms, A_QK), F32), pltpu.VMEM((ms, A_QK), F32),
                        pltpu.VMEM((A_HEADS, ms, A_DV), F32), pltpu.VMEM((A_HEADS, db, LANES), F32),
                        pltpu.VMEM((A_HEADS, A_DV, mcols), F32), pltpu.VMEM((ms, A_WIDTH), F32),
                        pltpu.VMEM((A_HEADS, ms, LANES), F32), pltpu.VMEM((A_HEADS, ms, LANES), F32),
                        pltpu.VMEM((A_HEADS, ms, LANES), F32)],
        compiler_params=pltpu.CompilerParams(dimension_semantics=("arbitrary",), vmem_limit_bytes=VMEM_LIMIT),
        name="mlstm_sample",
    )(*ins, c0)


def _dsa_sample_select_kernel(pt_ref, iq_ref, iw_ref, iknew_ref, *refs, topk, pages_per_step, n_pages, ds):
    page_refs = refs[:pages_per_step]
    bias_ref, sc_ref = refs[pages_per_step:]
    g = pl.program_id(1)
    iq = iq_ref[0]
    iw = iw_ref[0]

    def scores(keys_t, visible):
        n = keys_t.shape[1] // PAGE
        r = jnp.maximum(_dot(iq, keys_t) * IDX_SCALE, 0.0) * jnp.concatenate([iw] * n, axis=1)
        sc = jnp.sum(r.reshape(8, IDX_HEADS, n * PAGE), axis=1)
        return jnp.where(visible, sc, -jnp.inf)

    t_all = lax.broadcasted_iota(I32, (8, pages_per_step * PAGE), 0)
    sc_step = scores(jnp.concatenate([r[0] for r in page_refs], axis=1).astype(BF16), t_all < ds)
    for i in range(pages_per_step):
        sc_ref[g * pages_per_step + i] = sc_step[:, i * PAGE:(i + 1) * PAGE]

    @pl.when(g == pl.num_programs(1) - 1)
    def _():
        np1 = n_pages + 1
        t_i = lax.broadcasted_iota(I32, (8, PAGE), 0)
        s_i = lax.broadcasted_iota(I32, (8, PAGE), 1)
        sc_ref[n_pages] = scores(iknew_ref[0], (t_i < ds) & (s_i <= t_i))

        sc = sc_ref[...]
        pos = (lax.broadcasted_iota(I32, sc.shape, 0) * PAGE + lax.broadcasted_iota(I32, sc.shape, 2))
        quarter = -(-np1 // 4)

        def count(mask):
            m = mask.astype(I32)
            parts = [jnp.sum(m[a:min(a + quarter, np1)], axis=0) for a in range(0, np1, quarter)]
            return jnp.sum(sum(parts[1:], parts[0]), axis=1, keepdims=True)

        thr = _radix_threshold(lambda t: count(sc >= t), (8, 1), topk)
        need = topk - count(sc > thr)
        tied = sc == thr
        nbits = (np1 * PAGE - 1).bit_length()

        def tie_search(_):
            def pos_body(i, p):
                cand = p | jnp.left_shift(jnp.int32(1), nbits - 1 - i)
                cnt = count(tied & (pos < cand))
                return jnp.where(cnt < need, cand, p)
            return lax.fori_loop(0, nbits, pos_body, jnp.zeros((8, 1), I32))

        has_tie = jnp.max(jnp.where((count(sc >= thr) > topk) & (thr > -jnp.inf), 1, 0)) > 0
        p_lim = lax.cond(has_tie, tie_search, lambda _: jnp.full((8, 1), 2 ** 30, I32), 0)
        sel = ((sc > thr) | (tied & (pos <= p_lim))) & (sc > -jnp.inf)
        bias_ref[0] = jnp.where(sel, 0.0, NEG_BIG)


def _dsa_sample_select(page_table, iq_s, iw_bc, ik_new, cache_ik, *, topk, ds, pages_per_step):
    db, n_pages = page_table.shape
    npg = n_pages // pages_per_step
    per_b = lambda a: pl.BlockSpec((1,) + a.shape[1:], lambda b, g, pt: (b,) + (0,) * (a.ndim - 1))
    page_specs = [pl.BlockSpec((1, IDX_DIM, PAGE), lambda b, g, pt, i=i: (pt[b, g * pages_per_step + i], 0, 0))
                  for i in range(pages_per_step)]
    out_sds = jax.ShapeDtypeStruct((db, n_pages + 1, 8, PAGE), F32)
    return pl.pallas_call(
        functools.partial(_dsa_sample_select_kernel, topk=topk, pages_per_step=pages_per_step, n_pages=n_pages,
                          ds=ds),
        grid_spec=pltpu.PrefetchScalarGridSpec(
            num_scalar_prefetch=1, grid=(db, npg),
            in_specs=[per_b(iq_s), per_b(iw_bc), per_b(ik_new)] + page_specs,
            out_specs=pl.BlockSpec((1, n_pages + 1, 8, PAGE), lambda b, g, pt: (b, 0, 0, 0)),
            scratch_shapes=[pltpu.VMEM((n_pages + 1, 8, PAGE), F32)]),
        out_shape=out_sds,
        compiler_params=pltpu.CompilerParams(dimension_semantics=("arbitrary", "arbitrary"),
                                             vmem_limit_bytes=VMEM_LIMIT),
        name="dsa_sample_select",
    )(page_table, iq_s, iw_bc, ik_new, *([cache_ik] * pages_per_step))


def _dsa_sample_attend_kernel(pt_ref, q_ref, bias_ref, knew_ref, vnew_ref, zb_ref, *refs, pages_per_step, n_pages,
                              ds):
    k_refs = refs[:pages_per_step]
    v_refs = refs[pages_per_step:2 * pages_per_step]
    o_ref, acc_ref, m_ref, l_ref = refs[2 * pages_per_step:]
    g = pl.program_id(1)
    rows_q = ds * B_HEADS
    q = q_ref[0]

    @pl.when(g == 0)
    def _():
        acc_ref[...] = jnp.zeros(acc_ref.shape, F32)
        m_ref[...] = jnp.full(m_ref.shape, NEG_BIG, F32)
        l_ref[...] = jnp.zeros(l_ref.shape, F32)

    def step(k_keys, v_keys, bias8):
        n = k_keys.shape[0]
        bias = jnp.concatenate([jnp.broadcast_to(bias8[t:t + 1, :], (B_HEADS, n)) for t in range(ds)], axis=0)
        s = _dot_nt(q, k_keys) * ATT_SCALE + bias
        m_old = m_ref[...]
        m_new = jnp.maximum(m_old, jnp.max(s, axis=1, keepdims=True))
        alpha = jnp.exp(m_old - m_new)
        p = jnp.exp(s - m_new)
        l_ref[...] = alpha * l_ref[...] + jnp.sum(p, axis=1, keepdims=True)
        m_ref[...] = m_new
        acc_ref[...] = alpha * acc_ref[...] + _dot(p.astype(BF16), v_keys)

    def page_keys(ref):
        return jnp.concatenate([ref[0, pl.ds(kv, PAGE, stride=B_KV), :] for kv in range(B_KV)], axis=1).astype(BF16)

    step(jnp.concatenate([page_keys(r) for r in k_refs], axis=0),
         jnp.concatenate([page_keys(r) for r in v_refs], axis=0),
         jnp.concatenate([bias_ref[0, g * pages_per_step + i] for i in range(pages_per_step)], axis=1))

    @pl.when(g == pl.num_programs(1) - 1)
    def _():
        step(knew_ref[0], vnew_ref[0], bias_ref[0, n_pages])
        o = acc_ref[...] / l_ref[...]
        head = lax.broadcasted_iota(I32, (rows_q, B_HD), 0) & (B_HEADS - 1)
        o_sel = o[:, 0:B_HD]
        for kv in range(1, B_KV):
            o_sel = jnp.where(head >= kv * B_GROUP, o[:, kv * B_HD:(kv + 1) * B_HD], o_sel)
        o_ref[0] = (o_sel * zb_ref[0]).astype(o_ref.dtype)


def _dsa_sample_attend(page_table, q_bd, bias, k_new, v_new, zb_s, cache_k, cache_v, *, ds, pages_per_step):
    db, n_pages = page_table.shape
    npg = n_pages // pages_per_step
    rows_q = ds * B_HEADS
    kvw = B_KV * B_HD
    per_b = lambda a: pl.BlockSpec((1,) + a.shape[1:], lambda b, g, pt: (b,) + (0,) * (a.ndim - 1))
    page_specs = [pl.BlockSpec((1, PAGE * B_KV, B_HD), lambda b, g, pt, i=i: (pt[b, g * pages_per_step + i], 0, 0))
                  for i in range(pages_per_step)]
    return pl.pallas_call(
        functools.partial(_dsa_sample_attend_kernel, pages_per_step=pages_per_step, n_pages=n_pages, ds=ds),
        grid_spec=pltpu.PrefetchScalarGridSpec(
            num_scalar_prefetch=1, grid=(db, npg),
            in_specs=[per_b(q_bd), per_b(bias), per_b(k_new), per_b(v_new), per_b(zb_s)] + page_specs + page_specs,
            out_specs=pl.BlockSpec((1, rows_q, B_HD), lambda b, g, pt: (b, 0, 0)),
            scratch_shapes=[pltpu.VMEM((rows_q, kvw), F32), pltpu.VMEM((rows_q, 1), F32),
                            pltpu.VMEM((rows_q, 1), F32)]),
        out_shape=jax.ShapeDtypeStruct((db, rows_q, B_HD), BF16),
        compiler_params=pltpu.CompilerParams(dimension_semantics=("arbitrary", "arbitrary"),
                                             vmem_limit_bytes=VMEM_LIMIT),
        name="dsa_sample_attend",
    )(page_table, q_bd, bias, k_new, v_new, zb_s, *([cache_k] * pages_per_step), *([cache_v] * pages_per_step))


def _sample_path(x, pe, cache_k, cache_v, cache_ik, page_table, c0, n0, m0, conv0, w, w_packed, misc_bias):
    db, ds, d = x.shape
    ms = db * ds
    assert ds >= CONV_W - 1 and ds <= 8
    n_pages = page_table.shape[1]
    a = _inproj(x.reshape(ms, d), w['norm_pre'], w_packed, w['conv_w'], w['conv_b'], misc_bias,
                fuse_conv=False, seq_len=ms, tm=ms)

    tmaj = lambda z: jnp.swapaxes(z.reshape(db, ds, -1), 0, 1).reshape(ms, -1)
    ya_t, c_new, n_new, m_new = _mlstm_sample(
        tmaj(a['q']), tmaj(a['k']), jnp.swapaxes(conv0, 0, 1), w['conv_w'], w['conv_b'], tmaj(a['av']),
        tmaj(a['misc']), tmaj(a['ga']), w['norm_a'], jnp.swapaxes(n0, 0, 1),
        jnp.broadcast_to(jnp.swapaxes(m0, 0, 1)[:, :, None], (A_HEADS, db, LANES)), c0, db=db, ds=ds)
    ya = jnp.swapaxes(ya_t.reshape(ds, db, -1), 0, 1).reshape(ms, -1)

    topk = min(TOPK_MAX, (n_pages * PAGE + ds) // 4)
    pad_rows = lambda z: jnp.pad(z.reshape(db, ds, -1), ((0, 0), (0, PAGE - ds), (0, 0))).astype(BF16)
    pad_tok = lambda z: jnp.pad(z.reshape(db, ds, -1), ((0, 0), (0, 8 - ds), (0, 0)))
    iq_s = pad_tok(a['iq']).reshape(db, 8 * IDX_HEADS, IDX_DIM)
    iw_bc = jnp.broadcast_to(pad_tok(a['misc'][:, MISC_IW:MISC_IW + IDX_HEADS]).reshape(db, 8 * IDX_HEADS, 1),
                             (db, 8 * IDX_HEADS, LANES))
    pps = 16 if n_pages % 16 == 0 else n_pages
    bias = _dsa_sample_select(page_table, iq_s, iw_bc, jnp.swapaxes(pad_rows(a['misc'][:, MISC_IK:]), 1, 2),
                              jnp.swapaxes(cache_ik, 1, 2), topk=topk, ds=ds, pages_per_step=pps)
    q5 = a['bq'].reshape(db, ds * B_HEADS, 1, B_HD)
    kv_of_row = (jnp.arange(ds * B_HEADS) % B_HEADS) // B_GROUP
    q_bd = jnp.where((kv_of_row[None, :, None, None] == jnp.arange(B_KV)[None, None, :, None]), q5,
                     jnp.zeros((), BF16)).reshape(db, ds * B_HEADS, B_KV * B_HD)
    attn = _dsa_sample_attend(page_table, q_bd, bias, pad_rows(a['kf']), pad_rows(a['vf']),
                              a['zb'].reshape(db, ds * B_HEADS, B_HD),
                              cache_k.reshape(-1, PAGE * B_KV, B_HD), cache_v.reshape(-1, PAGE * B_KV, B_HD),
                              ds=ds, pages_per_step=pps)
    yb = attn.reshape(ms, B_WIDTH)
    y = _outproj(ya, yb, a['sga'], a['sgb'], x.reshape(ms, d), pe.reshape(ms, -1), w['wa'], w['wb'], w['wo'],
                 w['wg'], w['wp'], w['norm_post'], tm=ms)
    qk_pre = jnp.concatenate([a['q'], a['k']], axis=1).reshape(db, ds, 2 * A_QK)
    return dict(
        y=y.reshape(db, ds, d),
        k=a['kf'].reshape(1, db, ds, B_KV, B_HD),
        v=a['vf'].reshape(1, db, ds, B_KV, B_HD),
        ik=a['misc'][:, MISC_IK:].reshape(1, db, ds, IDX_DIM),
        C=c_new[None],
        n=jnp.swapaxes(n_new, 0, 1)[None],
        m=jnp.swapaxes(m_new[:, :, 0], 0, 1)[None],
        conv=qk_pre[:, ds - (CONV_W - 1):, :][None],
    )


def _prep_weights(w_in, conv_w, conv_b, if_bias, norm_a, w_a_proj, w_b_proj, w_out, norm_pre, norm_post, w_ple,
                  w_ple_gate):
    d = w_in.shape[1]
    misc_bias = jnp.zeros((1, LANES), F32).at[0, :2 * A_HEADS].set(if_bias[0])
    w = dict(norm_pre=norm_pre[0][None], norm_post=norm_post[0][None], norm_a=norm_a[0][None],
             conv_w=conv_w[0], conv_b=conv_b[0][None],
             wa=w_a_proj[0].astype(BF16), wb=w_b_proj[0].astype(BF16), wo=w_out[0].astype(BF16),
             wg=w_ple_gate[0].astype(BF16), wp=w_ple[0].astype(BF16))
    return w, _pack_w_in(w_in[0], d), misc_bias


def kernel(x_prompt, x_sample, cache_k, cache_v, cache_idx_k, page_table, state_C, state_n, state_m, state_conv,
           p_prompt, p_sample, w_in, conv_w, conv_b, if_bias, norm_a, w_a_proj, w_b_proj, w_out, norm_pre,
           norm_post, w_ple, w_ple_gate):
    w, w_packed, misc_bias = _prep_weights(w_in, conv_w, conv_b, if_bias, norm_a, w_a_proj, w_b_proj, w_out,
                                           norm_pre, norm_post, w_ple, w_ple_gate)
    p = _prompt_path(x_prompt, p_prompt[0], w, w_packed, misc_bias)
    s = _sample_path(x_sample, p_sample[0], cache_k[0], cache_v[0], cache_idx_k[0], page_table, state_C[0],
                     state_n[0], state_m[0], state_conv[0], w, w_packed, misc_bias)
    names = ('k', 'v', 'ik', 'C', 'n', 'm', 'conv')
    return (p['y'], s['y']) + tuple(p[n] for n in names) + tuple(s[n] for n in names)
```

```python
import functools

import jax
import jax.numpy as jnp
from jax import lax
from jax.experimental import pallas as pl
from jax.experimental.pallas import tpu as pltpu

F32 = jnp.float32
BF16 = jnp.bfloat16
I32 = jnp.int32

EPS = 1e-6
A_HEADS = 4
A_DQK = 128
A_DV = 256
A_QK = A_HEADS * A_DQK
A_WIDTH = A_HEADS * A_DV
CONV_W = 4
B_HEADS = 8
B_KV = 2
B_HD = 128
B_GROUP = B_HEADS // B_KV
B_WIDTH = B_HEADS * B_HD
IDX_HEADS = 8
IDX_DIM = 64
TOPK_MAX = 256
PAGE = 128
LANES = 128
VMEM_LIMIT = 56 * 1024 * 1024

Q_SCALE = A_DQK ** -0.5
ATT_SCALE = B_HD ** -0.5
IDX_SCALE = IDX_DIM ** -0.5
IW_SCALE = IDX_HEADS ** -0.5
C_EXP = ATT_SCALE * 1.4426950408889634
VT_ROWS = B_HD + 16
NEG_BIG = -1e30
INT_MIN = -(2 ** 31)
KEY_NEG_INF = INT_MIN + 0x007FFFFF

_SPLITS = (('a_q', A_QK), ('a_k', A_QK), ('a_v', A_WIDTH), ('a_i', A_HEADS), ('a_f', A_HEADS),
           ('a_o', A_WIDTH), ('a_z', A_WIDTH), ('b_q', B_WIDTH), ('b_k', B_KV * B_HD), ('b_v', B_KV * B_HD),
           ('b_iq', IDX_HEADS * IDX_DIM), ('b_ik', IDX_DIM), ('b_iw', IDX_HEADS), ('b_z', B_WIDTH),
           ('g_a', 1024), ('g_b', 1024))

P_QK = 0
P_AV = 1024
P_AO = 2048
P_AZ = 3072
P_BQ = 4096
P_BK = 5120
P_BV = 5376
P_IQ = 5632
P_MISC = 6144
P_BZ = 6272
P_GA = 7296
P_GB = 8320
P_COLS = 9344
MISC_IW = 8
MISC_IK = 64


def _sigmoid(x):
    return 1.0 / (1.0 + jnp.exp(-x))


def _silu(x):
    return x * _sigmoid(x)


def _dot(a, b):
    return jnp.dot(a, b, preferred_element_type=F32)


def _dot_nt(a, b):
    return lax.dot_general(a, b, (((1,), (1,)), ((), ())), preferred_element_type=F32)


def _transpose(x):
    r, c = x.shape
    rows = []
    for j in range(c // LANES):
        rows.append(jnp.concatenate([x[i * LANES:(i + 1) * LANES, j * LANES:(j + 1) * LANES].T
                                     for i in range(r // LANES)], axis=1))
    return jnp.concatenate(rows, axis=0)


def _pack_w_in(w_in, d_model):
    offs = {}
    c = 0
    for name, n in _SPLITS:
        offs[name] = (c, c + n)
        c += n

    def col(name):
        lo, hi = offs[name]
        return w_in[:, lo:hi]

    z = lambda n: jnp.zeros((d_model, n), w_in.dtype)
    misc = jnp.concatenate([col('a_i'), col('a_f'), col('b_iw'), z(MISC_IK - 16), col('b_ik')], axis=1)
    packed = jnp.concatenate([col('a_q'), col('a_k'), col('a_v'), col('a_o'), col('a_z'), col('b_q'), col('b_k'),
                              col('b_v'), col('b_iq'), misc, col('b_z'), col('g_a'), col('g_b')], axis=1)
    return packed.astype(BF16)


def _inproj_kernel(x_ref, g_ref, w_ref, cw_ref, cb_ref, mb_ref,
                   q_ref, k_ref, tail_ref, av_ref, ga_ref, bq_ref, kf_ref, vf_ref, kb_ref, vb_ref, iq_ref,
                   misc_ref, miscb_ref, zb_ref, sga_ref, sgb_ref, hist_ref, *, fuse_conv, tiles_per_seq):
    tm = x_ref.shape[0]
    x = x_ref[...]
    ms = jnp.mean(x * x, axis=-1, keepdims=True)
    xn = (x * lax.rsqrt(ms + EPS) * g_ref[...]).astype(BF16)

    def proj(lo, hi):
        return _dot(xn, w_ref[:, lo:hi])

    qk = proj(P_QK, P_QK + 2 * A_QK)
    tail_ref[0] = qk[tm - 8:tm, :]
    if fuse_conv:
        @pl.when(pl.program_id(0) % tiles_per_seq == 0)
        def _():
            hist_ref[0:8, :] = jnp.zeros((8, 2 * A_QK), F32)

        hist_ref[8:8 + tm, :] = qk
        cw = cw_ref[...]
        y = cb_ref[...] + cw[0:1, :] * hist_ref[5:5 + tm, :]
        y = y + cw[1:2, :] * hist_ref[6:6 + tm, :]
        y = y + cw[2:3, :] * hist_ref[7:7 + tm, :]
        y = y + cw[3:4, :] * qk
        act = _silu(y)
        q_ref[...] = (act[:, :A_QK] * Q_SCALE).astype(q_ref.dtype)
        k_ref[...] = act[:, A_QK:].astype(k_ref.dtype)
        hist_ref[0:8, :] = qk[tm - 8:tm, :]
    else:
        q_ref[...] = qk[:, :A_QK]
        k_ref[...] = qk[:, A_QK:]

    av_ref[...] = proj(P_AV, P_AV + A_WIDTH).astype(av_ref.dtype)
    ga_ref[...] = _sigmoid(proj(P_AO, P_AO + A_WIDTH)) * _silu(proj(P_AZ, P_AZ + A_WIDTH))
    bq_ref[...] = (proj(P_BQ, P_BQ + B_WIDTH) * C_EXP).astype(bq_ref.dtype)
    kf = proj(P_BK, P_BK + B_KV * B_HD)
    vf = proj(P_BV, P_BV + B_KV * B_HD)
    kf_ref[...] = kf
    vf_ref[...] = vf
    kb_ref[...] = kf.astype(BF16)
    vb_ref[...] = vf.astype(BF16)
    iq_ref[...] = proj(P_IQ, P_IQ + IDX_HEADS * IDX_DIM).astype(iq_ref.dtype)

    raw = proj(P_MISC, P_MISC + LANES) + mb_ref[...]
    lane = lax.broadcasted_iota(I32, raw.shape, 1)
    logsig = jnp.minimum(raw, 0.0) - jnp.log(1.0 + jnp.exp(-jnp.abs(raw)))
    misc = jnp.where((lane >= A_HEADS) & (lane < 2 * A_HEADS), logsig,
                     jnp.where((lane >= MISC_IW) & (lane < MISC_IW + IDX_HEADS), raw * IW_SCALE, raw))
    misc_ref[...] = misc
    miscb_ref[...] = jnp.where(lane >= MISC_IK, raw, 0.0).astype(BF16)

    zb_ref[...] = _silu(proj(P_BZ, P_BZ + B_WIDTH))
    sga_ref[...] = _sigmoid(proj(P_GA, P_GA + 1024))
    sgb_ref[...] = _sigmoid(proj(P_GB, P_GB + 1024))


def _inproj(x2d, norm_pre, w_packed, conv_w, conv_b, misc_bias, *, fuse_conv, seq_len, tm):
    m, d = x2d.shape
    nt = m // tm
    qk_dt = BF16 if fuse_conv else F32
    widths = [('q', A_QK, qk_dt), ('k', A_QK, qk_dt), ('tail', None, F32), ('av', A_WIDTH, BF16),
              ('ga', A_WIDTH, F32), ('bq', B_WIDTH, BF16), ('kf', B_KV * B_HD, F32), ('vf', B_KV * B_HD, F32),
              ('kb', B_KV * B_HD, BF16), ('vb', B_KV * B_HD, BF16), ('iq', IDX_HEADS * IDX_DIM, BF16),
              ('misc', LANES, F32), ('miscb', LANES, BF16), ('zb', B_WIDTH, F32), ('sga', d, F32), ('sgb', d, F32)]
    out_shape, out_specs = [], []
    for name, w, dt in widths:
        if name == 'tail':
            out_shape.append(jax.ShapeDtypeStruct((nt, 8, 2 * A_QK), dt))
            out_specs.append(pl.BlockSpec((1, 8, 2 * A_QK), lambda i: (i, 0, 0)))
        else:
            out_shape.append(jax.ShapeDtypeStruct((m, w), dt))
            out_specs.append(pl.BlockSpec((tm, w), lambda i: (i, 0)))
    const = lambda shape: pl.BlockSpec(shape, lambda i: (0,) * len(shape))
    outs = pl.pallas_call(
        functools.partial(_inproj_kernel, fuse_conv=fuse_conv, tiles_per_seq=max(seq_len // tm, 1)),
        grid=(nt,),
        in_specs=[pl.BlockSpec((tm, d), lambda i: (i, 0)), const((1, d)),
                  pl.BlockSpec((d, P_COLS), lambda i: (0, 0), pipeline_mode=pl.Buffered(1)),
                  const((CONV_W, 2 * A_QK)), const((1, 2 * A_QK)), const((1, LANES))],
        out_specs=out_specs,
        out_shape=out_shape,
        scratch_shapes=[pltpu.VMEM((8 + tm, 2 * A_QK), F32)],
        compiler_params=pltpu.CompilerParams(dimension_semantics=("arbitrary",), vmem_limit_bytes=VMEM_LIMIT),
        name="inproj_conv" if fuse_conv else "inproj",
    )(x2d, norm_pre, w_packed, conv_w, conv_b, misc_bias)
    return {name: o for (name, _, _), o in zip(widths, outs)}


def _outproj_kernel(ya_ref, yb_ref, sga_ref, sgb_ref, x_ref, pe_ref, wa_ref, wb_ref, wo_ref, wg_ref, wp_ref,
                    g_ref, o_ref):
    merged = sga_ref[...] * _dot(ya_ref[...], wa_ref[...]) + sgb_ref[...] * _dot(yb_ref[...], wb_ref[...])
    z = _dot(merged.astype(BF16), wo_ref[...])
    ms = jnp.mean(z * z, axis=-1, keepdims=True)
    x1 = x_ref[...] + z * lax.rsqrt(ms + EPS) * g_ref[...]
    gate = _sigmoid(_dot(x1.astype(BF16), wg_ref[...]))
    o_ref[...] = x1 + _dot(pe_ref[...].astype(BF16), wp_ref[...]) * gate


def _outproj(ya, yb, sga, sgb, x2d, pe2d, wa, wb, wo, wg, wp, norm_post, *, tm):
    m, d = x2d.shape
    pd = pe2d.shape[1]
    row = lambda w: pl.BlockSpec((tm, w), lambda i: (i, 0))
    const = lambda shape: pl.BlockSpec(shape, lambda i: (0,) * len(shape))
    return pl.pallas_call(
        _outproj_kernel,
        grid=(m // tm,),
        in_specs=[row(A_WIDTH), row(B_WIDTH), row(d), row(d), row(d), row(pd),
                  const((A_WIDTH, d)), const((B_WIDTH, d)), const((d, d)), const((d, d)), const((pd, d)),
                  const((1, d))],
        out_specs=row(d),
        out_shape=jax.ShapeDtypeStruct((m, d), F32),
        compiler_params=pltpu.CompilerParams(dimension_semantics=("arbitrary",), vmem_limit_bytes=VMEM_LIMIT),
        name="outproj",
    )(ya, yb, sga, sgb, x2d, pe2d, wa, wb, wo, wg, wp, norm_post)


def _cumsum_rows(x):
    n = x.shape[0]
    row = lax.broadcasted_iota(I32, x.shape, 0)
    s = 1
    while s < n:
        x = x + jnp.where(row >= s, pltpu.roll(x, s, axis=0), 0.0)
        s *= 2
    return x


def _lane_col(x, lane_iota, idx):
    return jnp.sum(jnp.where(lane_iota == idx, x, 0.0), axis=1, keepdims=True)


def _mlstm_prompt_kernel(q_ref, k_ref, v_ref, misc_ref, ga_ref, na_ref, ya_ref, c_out_ref, nm_out_ref,
                         c_ref, n_ref, m_ref):
    c_idx = pl.program_id(1)
    L = q_ref.shape[0]

    @pl.when(c_idx == 0)
    def _():
        c_ref[...] = jnp.zeros(c_ref.shape, F32)
        n_ref[...] = jnp.zeros(n_ref.shape, F32)
        m_ref[...] = jnp.zeros(m_ref.shape, F32)

    misc = misc_ref[...]
    lane = lax.broadcasted_iota(I32, misc.shape, 1)
    cs = _cumsum_rows(misc)
    b_al = pltpu.roll(cs, LANES - A_HEADS, axis=1)
    xa = jnp.where(lane < A_HEADS, misc - b_al, cs)
    xt = _transpose(xa)
    t_i = lax.broadcasted_iota(I32, (L, L), 0)
    s_i = lax.broadcasted_iota(I32, (L, L), 1)
    causal = s_i <= t_i

    for h in range(A_HEADS):
        a_col = _lane_col(xa, lane, h)
        b_col = _lane_col(xa, lane, A_HEADS + h)
        a_row = xt[h:h + 1, :]
        m_prev = m_ref[h:h + 1, 0:1]
        logit = jnp.where(causal, a_row, -jnp.inf)
        g_col = jnp.maximum(m_prev, jnp.max(logit, axis=1, keepdims=True))
        dmat = jnp.exp(logit - g_col)
        w_inter = jnp.exp(m_prev - g_col)
        qh = q_ref[:, h * A_DQK:(h + 1) * A_DQK]
        kh = k_ref[:, h * A_DQK:(h + 1) * A_DQK]
        vh = v_ref[:, h * A_DV:(h + 1) * A_DV]
        c_h = c_ref[h]
        n_h = n_ref[h:h + 1, :]
        s = _dot_nt(qh, kh) * dmat
        num = _dot(s.astype(BF16), vh) + w_inter * _dot_nt(qh, c_h.astype(BF16))
        den = jnp.sum(s, axis=1, keepdims=True) + w_inter * jnp.sum(qh.astype(F32) * n_h, axis=1, keepdims=True)
        m_t = b_col + g_col
        hh = num / jnp.maximum(jnp.abs(den), jnp.exp(-m_t))
        hn = hh * lax.rsqrt(jnp.mean(hh * hh, axis=1, keepdims=True) + EPS)
        sl = slice(h * A_DV, (h + 1) * A_DV)
        ya_ref[:, sl] = (ga_ref[:, sl] * (hn * na_ref[:, sl])).astype(ya_ref.dtype)

        g_last = g_col[L - 1:L, :]
        w_prev = jnp.exp(m_prev - g_last)
        w_s = jnp.exp(a_col - g_last)
        vw = vh.astype(F32) * w_s
        c_ref[h] = w_prev * c_h + _dot(_transpose(vw).astype(BF16), kh)
        n_ref[h:h + 1, :] = w_prev * n_h + jnp.sum(kh.astype(F32) * w_s, axis=0, keepdims=True)
        m_ref[h:h + 1, :] = jnp.broadcast_to(b_col[L - 1:L, :] + g_last, (1, LANES))

    @pl.when(c_idx == pl.num_programs(1) - 1)
    def _():
        c_out_ref[0] = c_ref[...]
        nm_out_ref[0, 0:A_HEADS, :] = n_ref[0:A_HEADS, :]
        nm_out_ref[0, A_HEADS:2 * A_HEADS, :] = m_ref[0:A_HEADS, :]


def _mlstm_prompt(q, k, v, misc, ga, norm_a, *, batch, seq_len, chunk):
    nc = seq_len // chunk
    row = lambda w: pl.BlockSpec((chunk, w), lambda b, c: (b * nc + c, 0))
    return pl.pallas_call(
        _mlstm_prompt_kernel,
        grid=(batch, nc),
        in_specs=[row(A_QK), row(A_QK), row(A_WIDTH), row(LANES), row(A_WIDTH),
                  pl.BlockSpec((1, A_WIDTH), lambda b, c: (0, 0))],
        out_specs=[row(A_WIDTH),
                   pl.BlockSpec((1, A_HEADS, A_DV, A_DQK), lambda b, c: (b, 0, 0, 0)),
                   pl.BlockSpec((1, 8, LANES), lambda b, c: (b, 0, 0))],
        out_shape=[jax.ShapeDtypeStruct((batch * seq_len, A_WIDTH), BF16),
                   jax.ShapeDtypeStruct((batch, A_HEADS, A_DV, A_DQK), F32),
                   jax.ShapeDtypeStruct((batch, 8, LANES), F32)],
        scratch_shapes=[pltpu.VMEM((A_HEADS, A_DV, A_DQK), F32), pltpu.VMEM((8, LANES), F32),
                        pltpu.VMEM((8, LANES), F32)],
        compiler_params=pltpu.CompilerParams(dimension_semantics=("arbitrary", "arbitrary"),
                                             vmem_limit_bytes=VMEM_LIMIT),
        name="mlstm_prompt",
    )(q, k, v, misc, ga, norm_a)


def _key_to_f32(key):
    return pltpu.bitcast(jnp.where(key < 0, key ^ 0x7FFFFFFF, key), F32)


def _radix_threshold(count_ge, shape, topk):
    def bit_body(i, u):
        cand_u = u | jnp.left_shift(jnp.int32(1), 31 - i)
        cnt = count_ge(_key_to_f32(cand_u ^ INT_MIN))
        return jnp.where(cnt >= topk, cand_u, u)

    u = lax.fori_loop(0, 32, bit_body, jnp.zeros(shape, I32))
    key = jnp.maximum(u ^ INT_MIN, KEY_NEG_INF)
    return _key_to_f32(key), _key_to_f32(key + 1)


def _refine_in_bin(count_ge, thr, thr_up, topk):
    lo, hi = thr, thr_up
    for _ in range(6):
        mid = lo + 0.5 * (hi - lo)
        ok = count_ge(mid) >= topk
        lo = jnp.where(ok, mid, lo)
        hi = jnp.where(ok, hi, mid)
    return lo


def _dsa_prompt_kernel(bq_ref, iq_ref, misc_ref, zb_ref, ikb_ref, kb_ref, vt_ref, yb_ref,
                       sc_ref, s_ref, s2_ref, acc_ref, m_ref, *, topk, nq):
    j = pl.program_id(1)
    T = PAGE
    KT = 2 * PAGE
    nkt = j // 2 + 1

    iq_t = _transpose(iq_ref[...].astype(F32))
    zeros_pad = jnp.zeros((LANES - IDX_DIM, T), F32)
    iq_pad = [jnp.concatenate([zeros_pad, iq_t[h * IDX_DIM:(h + 1) * IDX_DIM, :]], axis=0).astype(BF16)
              for h in range(IDX_HEADS)]
    misc_t = _transpose(misc_ref[...])
    q_t = _transpose(bq_ref[...].astype(F32))
    q_grp = [jnp.concatenate([q_t[(g * B_GROUP + hh) * B_HD:(g * B_GROUP + hh + 1) * B_HD, :]
                              for hh in range(B_GROUP)], axis=1).astype(BF16) for g in range(B_KV)]

    row_i = lax.broadcasted_iota(I32, (KT, T), 0)
    lane_i = lax.broadcasted_iota(I32, (KT, T), 1)

    def idx_body(c, carry):
        ikc = ikb_ref[0, c]
        acc = jnp.zeros((KT, T), F32)
        for h in range(IDX_HEADS):
            r = jnp.maximum(_dot(ikc, iq_pad[h]) * IDX_SCALE, 0.0)
            acc = acc + r * misc_t[MISC_IW + h:MISC_IW + h + 1, :]
        vis = (c * KT + row_i) <= (j * T + lane_i)
        sc_ref[c] = jnp.where(vis, acc, -jnp.inf)
        return carry

    lax.fori_loop(0, nkt, idx_body, 0)

    def count(pred):
        def body(c, cnt):
            m = pred(sc_ref[c], c).astype(I32)
            return cnt + jnp.sum(m.reshape(4, KT // 32, 8, T), axis=1)
        cnt4 = lax.fori_loop(0, nkt, body, jnp.zeros((4, 8, T), I32))
        return jnp.sum(jnp.sum(cnt4, axis=0), axis=0, keepdims=True)

    count_ge = lambda t: count(lambda sc, c: sc >= t)
    thr0, thr_up = _radix_threshold(count_ge, (1, T), topk)
    n_ge = count_ge(thr0)

    def tie_search(_):
        thr_t = _refine_in_bin(count_ge, thr0, thr_up, topk)
        need = topk - count(lambda sc, c: sc > thr_t)
        nbits = (nq * T - 1).bit_length()
        def pbody(i, p):
            cand = p | jnp.left_shift(jnp.int32(1), nbits - 1 - i)
            cnt = count(lambda sc, c: (sc == thr_t) & ((c * KT + row_i) < cand))
            return jnp.where(cnt < need, cand, p)
        return thr_t, lax.fori_loop(0, nbits, pbody, jnp.zeros((1, T), I32))

    has_tie = jnp.max(jnp.where((n_ge > topk) & (thr0 > -jnp.inf), 1, 0)) > 0
    thr, p_lim = lax.cond(has_tie, tie_search, lambda _: (thr0, jnp.full((1, T), 2 ** 30, I32)), 0)

    acc_ref[...] = jnp.zeros(acc_ref.shape, F32)
    m_ref[...] = jnp.full(m_ref.shape, NEG_BIG, F32)
    last_pair = nq // 2 - 1

    def qk_scores(c, dst_ref):
        kc = kb_ref[0, jnp.minimum(c, last_pair)]
        for g in range(B_KV):
            dst_ref[g] = _dot(kc[:, g * B_HD:(g + 1) * B_HD], q_grp[g])

    def softmax_pv(c, src_ref):
        sc = sc_ref[c]
        pos = c * KT + row_i
        sel = ((sc > thr) | ((sc == thr) & (pos <= p_lim))) & (sc > -jnp.inf)
        bias = jnp.where(sel, 0.0, NEG_BIG)
        vtc = vt_ref[0, c]
        for h0 in range(0, B_HEADS, 2):
            g = h0 // B_GROUP
            ps, alphas = [], []
            for h in (h0, h0 + 1):
                hh = h % B_GROUP
                s = src_ref[g, :, hh * T:(hh + 1) * T] + bias
                m_old = m_ref[h:h + 1, :]
                m_new = jnp.maximum(m_old, jnp.max(s, axis=0, keepdims=True))
                alphas.append(jnp.exp2(m_old - m_new))
                ps.append(jnp.exp2(s - m_new).astype(BF16))
                m_ref[h:h + 1, :] = m_new
            pv = _dot(vtc[g * VT_ROWS:(g + 1) * VT_ROWS, :], jnp.concatenate(ps, axis=1))
            for i, h in enumerate((h0, h0 + 1)):
                acc_ref[h] = acc_ref[h] * alphas[i] + pv[:, i * T:(i + 1) * T]

    qk_scores(0, s_ref)

    def att_body(i, carry):
        c0 = 2 * i
        qk_scores(c0 + 1, s2_ref)
        softmax_pv(c0, s_ref)

        @pl.when(c0 + 1 < nkt)
        def _():
            qk_scores(c0 + 2, s_ref)
            softmax_pv(c0 + 1, s2_ref)

        return carry

    lax.fori_loop(0, (nkt + 1) // 2, att_body, 0)

    for h in range(B_HEADS):
        o_t = acc_ref[h, 0:B_HD, :] / acc_ref[h, B_HD:B_HD + 1, :]
        sl = slice(h * B_HD, (h + 1) * B_HD)
        yb_ref[:, sl] = (_transpose(o_t) * zb_ref[:, sl]).astype(yb_ref.dtype)


def _dsa_prompt(bq, iq, misc, zb, ikb, kb, vt, *, batch, seq_len, topk):
    nq = seq_len // PAGE
    row = lambda w: pl.BlockSpec((PAGE, w), lambda b, j: (b * nq + j, 0))
    per_b = lambda a: pl.BlockSpec((1,) + a.shape[1:], lambda b, j: (b, 0, 0, 0))
    return pl.pallas_call(
        functools.partial(_dsa_prompt_kernel, topk=topk, nq=nq),
        grid=(batch, nq),
        in_specs=[row(B_WIDTH), row(IDX_HEADS * IDX_DIM), row(LANES), row(B_WIDTH),
                  per_b(ikb), per_b(kb), per_b(vt)],
        out_specs=row(B_WIDTH),
        out_shape=jax.ShapeDtypeStruct((batch * seq_len, B_WIDTH), BF16),
        scratch_shapes=[pltpu.VMEM((nq // 2, 2 * PAGE, PAGE), F32),
                        pltpu.VMEM((B_KV, 2 * PAGE, B_GROUP * PAGE), F32),
                        pltpu.VMEM((B_KV, 2 * PAGE, B_GROUP * PAGE), F32), pltpu.VMEM((B_HEADS, VT_ROWS, PAGE), F32),
                        pltpu.VMEM((B_HEADS, PAGE), F32)],
        compiler_params=pltpu.CompilerParams(dimension_semantics=("arbitrary", "arbitrary"),
                                             vmem_limit_bytes=VMEM_LIMIT),
        name="dsa_prompt",
    )(bq, iq, misc, zb, ikb, kb, vt)


def _prompt_path(x, pe, w, w_packed, misc_bias):
    batch, seq_len, d = x.shape
    m = batch * seq_len
    x2d = x.reshape(m, d)
    a = _inproj(x2d, w['norm_pre'], w_packed, w['conv_w'], w['conv_b'], misc_bias,
                fuse_conv=True, seq_len=seq_len, tm=min(256, seq_len))
    chunk = min(128, seq_len)
    ya, c_out, nm_out = _mlstm_prompt(a['q'], a['k'], a['av'], a['misc'], a['ga'], w['norm_a'],
                                      batch=batch, seq_len=seq_len, chunk=chunk)
    nk2 = seq_len // (2 * PAGE)
    ikb = a['miscb'].reshape(batch, nk2, 2 * PAGE, LANES)
    kb = a['kb'].reshape(batch, nk2, 2 * PAGE, B_KV * B_HD)
    vt = jnp.swapaxes(a['vb'].reshape(batch, nk2, 2 * PAGE, B_KV, B_HD), 2, 4)
    vt = jnp.concatenate([jnp.swapaxes(vt, 2, 3), jnp.ones((batch, nk2, B_KV, VT_ROWS - B_HD, 2 * PAGE), BF16)],
                         axis=3).reshape(batch, nk2, B_KV * VT_ROWS, 2 * PAGE)
    topk = min(TOPK_MAX, seq_len // 4)
    yb = _dsa_prompt(a['bq'], a['iq'], a['misc'], a['zb'], ikb, kb, vt, batch=batch, seq_len=seq_len, topk=topk)
    y = _outproj(ya, yb, a['sga'], a['sgb'], x2d, pe.reshape(m, -1), w['wa'], w['wb'], w['wo'], w['wg'], w['wp'],
                 w['norm_post'], tm=min(256, m))
    tiles_per_seq = seq_len // min(256, seq_len)
    tail = a['tail'].reshape(batch, tiles_per_seq, 8, 2 * A_QK)[:, -1, 8 - (CONV_W - 1):, :]
    return dict(
        y=y.reshape(batch, seq_len, d),
        k=a['kf'].reshape(1, batch, seq_len, B_KV, B_HD),
        v=a['vf'].reshape(1, batch, seq_len, B_KV, B_HD),
        ik=a['misc'][:, MISC_IK:].reshape(1, batch, seq_len, IDX_DIM),
        C=c_out[None],
        n=nm_out[None, :, 0:A_HEADS, :],
        m=nm_out[None, :, A_HEADS:2 * A_HEADS, 0],
        conv=tail[None],
    )


def _mlstm_sample_kernel(q_ref, k_ref, conv0_ref, cw_ref, cb_ref, v_ref, misc_ref, ga_ref, na_ref, n0_ref, m0_ref,
                         c0_ref, ya_ref, c_out_ref, n_out_ref, m_out_ref,
                         qs_ref, ks_ref, cq_ref, wprev_ref, vwt_ref, numi_ref, den_ref, wint_ref, enm_ref, *, db, ds):
    b = pl.program_id(0)
    ms = ds * db
    rows = lambda t: slice(t * db, (t + 1) * db)

    @pl.when(b == 0)
    def _():
        cw = cw_ref[...]
        u = [conv0_ref[j] for j in range(CONV_W - 1)]
        u += [jnp.concatenate([q_ref[rows(t), :], k_ref[rows(t), :]], axis=1) for t in range(ds)]
        for t in range(ds):
            y = cb_ref[...] + cw[0:1, :] * u[t]
            for jj in range(1, CONV_W):
                y = y + cw[jj:jj + 1, :] * u[t + jj]
            act = _silu(y)
            qs_ref[rows(t), :] = act[:, :A_QK] * Q_SCALE
            ks_ref[rows(t), :] = act[:, A_QK:]
        cq_ref[...] = jnp.zeros(cq_ref.shape, F32)

        lane = lax.broadcasted_iota(I32, (db, LANES), 1)
        for h in range(A_HEADS):
            i_t = [_lane_col(misc_ref[rows(t), :], lane, h) for t in range(ds)]
            lf_t = [_lane_col(misc_ref[rows(t), :], lane, A_HEADS + h) for t in range(ds)]
            b_t = [lf_t[0]]
            for t in range(1, ds):
                b_t.append(b_t[-1] + lf_t[t])
            a_t = [i_t[t] - b_t[t] for t in range(ds)]
            m_prev = jnp.max(m0_ref[h], axis=1, keepdims=True)
            g_t = [jnp.maximum(m_prev, a_t[0])]
            for t in range(1, ds):
                g_t.append(jnp.maximum(g_t[-1], a_t[t]))
            hq = slice(h * A_DQK, (h + 1) * A_DQK)
            hv = slice(h * A_DV, (h + 1) * A_DV)
            qh = [qs_ref[rows(t), hq] for t in range(ds)]
            kh = [ks_ref[rows(t), hq] for t in range(ds)]
            vh = [v_ref[rows(t), hv].astype(F32) for t in range(ds)]
            n0 = n0_ref[h]
            for t in range(ds):
                num = jnp.zeros((db, A_DV), F32)
                den = jnp.zeros((db, 1), F32)
                for s in range(t + 1):
                    w_ts = jnp.sum(qh[t] * kh[s], axis=1, keepdims=True) * jnp.exp(a_t[s] - g_t[t])
                    num = num + w_ts * vh[s]
                    den = den + w_ts
                w_inter = jnp.exp(m_prev - g_t[t])
                den = den + w_inter * jnp.sum(qh[t] * n0, axis=1, keepdims=True)
                numi_ref[rows(t), hv] = num
                den_ref[h, rows(t), :] = jnp.broadcast_to(den, (db, LANES))
                wint_ref[h, rows(t), :] = jnp.broadcast_to(w_inter, (db, LANES))
                enm_ref[h, rows(t), :] = jnp.broadcast_to(jnp.exp(-(b_t[t] + g_t[t])), (db, LANES))
            g_last = g_t[ds - 1]
            w_prev = jnp.exp(m_prev - g_last)
            wprev_ref[h] = jnp.broadcast_to(w_prev, (db, LANES))
            n_new = w_prev * n0
            vw = []
            for s in range(ds):
                w_s = jnp.exp(a_t[s] - g_last)
                n_new = n_new + w_s * kh[s]
                vw.append(vh[s] * w_s)
            n_out_ref[h] = n_new
            m_out_ref[h] = jnp.broadcast_to(b_t[ds - 1] + g_last, (db, LANES))
            vw_all = jnp.concatenate(vw, axis=0)
            if ms < LANES:
                vw_all = jnp.concatenate([vw_all, jnp.zeros((LANES - ms, A_DV), F32)], axis=0)
            vwt_ref[h] = _transpose(vw_all)

    mcols = vwt_ref.shape[2]
    row_i = lax.broadcasted_iota(I32, (ms, A_DQK), 0)
    col_i = lax.broadcasted_iota(I32, (A_DV, mcols), 1)
    row_mine = row_i == b
    col_mine = col_i == b
    for t in range(1, ds):
        row_mine = row_mine | (row_i == b + t * db)
        col_mine = col_mine | (col_i == b + t * db)
    for h in range(A_HEADS):
        hq = slice(h * A_DQK, (h + 1) * A_DQK)
        c0 = c0_ref[0, h]
        q_mine = jnp.where(row_mine, qs_ref[:, hq], 0.0).astype(BF16)
        cq_ref[h] = cq_ref[h] + _dot_nt(q_mine, c0.astype(BF16))
        vw_mine = jnp.where(col_mine, vwt_ref[h], 0.0).astype(BF16)
        k_all = ks_ref[:, hq]
        if mcols > ms:
            k_all = jnp.concatenate([k_all, jnp.zeros((mcols - ms, A_DQK), F32)], axis=0)
        c_out_ref[0, h] = wprev_ref[h, pl.ds(b, 1), :] * c0 + _dot(vw_mine, k_all.astype(BF16))

    @pl.when(b == pl.num_programs(0) - 1)
    def _():
        for h in range(A_HEADS):
            hv = slice(h * A_DV, (h + 1) * A_DV)
            num = numi_ref[:, hv] + wint_ref[h][:, 0:1] * cq_ref[h]
            hh = num / jnp.maximum(jnp.abs(den_ref[h][:, 0:1]), enm_ref[h][:, 0:1])
            hn = hh * lax.rsqrt(jnp.mean(hh * hh, axis=1, keepdims=True) + EPS)
            ya_ref[:, hv] = (ga_ref[:, hv] * (hn * na_ref[:, hv])).astype(ya_ref.dtype)


def _mlstm_sample(q_t, k_t, conv0_t, conv_w, conv_b, v_t, misc_t, ga_t, norm_a, n0_t, m0_bc, c0, *, db, ds):
    ms = db * ds
    mcols = max(ms, LANES)
    full = lambda a: pl.BlockSpec(a.shape, lambda b: (0,) * a.ndim)
    cblk = pl.BlockSpec((1, A_HEADS, A_DV, A_DQK), lambda b: (b, 0, 0, 0))
    hshape = jax.ShapeDtypeStruct((A_HEADS, db, LANES), F32)
    ins = (q_t, k_t, conv0_t, conv_w, conv_b, v_t, misc_t, ga_t, norm_a, n0_t, m0_bc)
    return pl.pallas_call(
        functools.partial(_mlstm_sample_kernel, db=db, ds=ds),
        grid=(db,),
        in_specs=[full(a) for a in ins] + [cblk],
        out_specs=[pl.BlockSpec((ms, A_WIDTH), lambda b: (0, 0)), cblk,
                   pl.BlockSpec((A_HEADS, db, LANES), lambda b: (0, 0, 0)),
                   pl.BlockSpec((A_HEADS, db, LANES), lambda b: (0, 0, 0))],
        out_shape=[jax.ShapeDtypeStruct((ms, A_WIDTH), BF16), jax.ShapeDtypeStruct(c0.shape, F32), hshape, hshape],
        scratch_shapes=[pltpu.VMEM((ms, A_QK), F32), pltpu.VMEM((ms, A_QK), F32),
                        pltpu.VMEM((A_HEADS, ms, A_DV), F32), pltpu.VMEM((A_HEADS, db, LANES), F32),
                        pltpu.VMEM((A_HEADS, A_DV, mcols), F32), pltpu.VMEM((ms, A_WIDTH), F32),
                        pltpu.VMEM((A_HEADS, ms, LANES), F32), pltpu.VMEM((A_HEADS, ms, LANES), F32),
                        pltpu.VMEM((A_HEADS, ms, LANES), F32)],
        compiler_params=pltpu.CompilerParams(dimension_semantics=("arbitrary",), vmem_limit_bytes=VMEM_LIMIT),
        name="mlstm_sample",
    )(*ins, c0)


def _dsa_sample_scores_kernel(pt_ref, iq_ref, iw_ref, iknew_ref, *refs, pages_per_step, n_pages, ds):
    page_refs = refs[:pages_per_step]
    sc_ref = refs[pages_per_step]
    g = pl.program_id(1)
    iq = iq_ref[0]
    iw = iw_ref[0]

    def scores(keys_t, visible):
        n = keys_t.shape[1] // PAGE
        r = jnp.maximum(_dot(iq, keys_t) * IDX_SCALE, 0.0) * jnp.concatenate([iw] * n, axis=1)
        sc = jnp.sum(r.reshape(8, IDX_HEADS, n * PAGE), axis=1)
        return jnp.where(visible, sc, -jnp.inf)

    t_all = lax.broadcasted_iota(I32, (8, pages_per_step * PAGE), 0)
    sc_step = scores(jnp.concatenate([r[0] for r in page_refs], axis=1).astype(BF16), t_all < ds)
    for i in range(pages_per_step):
        sc_ref[0, g * pages_per_step + i] = sc_step[:, i * PAGE:(i + 1) * PAGE]

    @pl.when(g == pl.num_programs(1) - 1)
    def _():
        t_i = lax.broadcasted_iota(I32, (8, PAGE), 0)
        s_i = lax.broadcasted_iota(I32, (8, PAGE), 1)
        sc_ref[0, n_pages] = scores(iknew_ref[0], (t_i < ds) & (s_i <= t_i))


def _dsa_sample_scores(page_table, iq_s, iw_bc, ik_new, cache_ik, *, ds, pages_per_step):
    db, n_pages = page_table.shape
    npg = n_pages // pages_per_step
    per_b = lambda a: pl.BlockSpec((1,) + a.shape[1:], lambda b, g, pt: (b,) + (0,) * (a.ndim - 1))
    page_specs = [pl.BlockSpec((1, IDX_DIM, PAGE), lambda b, g, pt, i=i: (pt[b, g * pages_per_step + i], 0, 0))
                  for i in range(pages_per_step)]
    return pl.pallas_call(
        functools.partial(_dsa_sample_scores_kernel, pages_per_step=pages_per_step, n_pages=n_pages, ds=ds),
        grid_spec=pltpu.PrefetchScalarGridSpec(
            num_scalar_prefetch=1, grid=(db, npg),
            in_specs=[per_b(iq_s), per_b(iw_bc), per_b(ik_new)] + page_specs,
            out_specs=pl.BlockSpec((1, n_pages + 1, 8, PAGE), lambda b, g, pt: (b, 0, 0, 0))),
        out_shape=jax.ShapeDtypeStruct((db, n_pages + 1, 8, PAGE), F32),
        compiler_params=pltpu.CompilerParams(dimension_semantics=("arbitrary", "arbitrary"),
                                             vmem_limit_bytes=VMEM_LIMIT),
        name="dsa_sample_scores",
    )(page_table, iq_s, iw_bc, ik_new, *([cache_ik] * pages_per_step))


def _dsa_sample_search_kernel(sc_ref, bias_ref, *, topk):
    sc = sc_ref[...]
    nb, np1 = sc.shape[0], sc.shape[1]
    pos = lax.broadcasted_iota(I32, sc.shape, 1) * PAGE + lax.broadcasted_iota(I32, sc.shape, 3)

    def count(mask):
        per_lane = jnp.sum(mask.astype(I32), axis=1, keepdims=True)
        return jnp.sum(per_lane, axis=3, keepdims=True)

    count_ge = lambda t: count(sc_ref[...] >= t)
    thr0, thr_up = _radix_threshold(count_ge, (nb, 1, 8, 1), topk)
    nbits = (np1 * PAGE - 1).bit_length()

    def tie_search(_):
        thr_t = _refine_in_bin(count_ge, thr0, thr_up, topk)
        need = topk - count(sc > thr_t)
        tied = sc == thr_t

        def pos_body(i, p):
            cand = p | jnp.left_shift(jnp.int32(1), nbits - 1 - i)
            cnt = count(tied & (pos < cand))
            return jnp.where(cnt < need, cand, p)
        return thr_t, lax.fori_loop(0, nbits, pos_body, jnp.zeros(thr0.shape, I32))

    has_tie = jnp.max(jnp.where((count_ge(thr0) > topk) & (thr0 > -jnp.inf), 1, 0)) > 0
    thr, p_lim = lax.cond(has_tie, tie_search, lambda _: (thr0, jnp.full(thr0.shape, 2 ** 30, I32)), 0)
    sel = ((sc > thr) | ((sc == thr) & (pos <= p_lim))) & (sc > -jnp.inf)
    bias_ref[...] = jnp.where(sel, 0.0, NEG_BIG)


def _dsa_sample_search(scores, *, topk):
    db = scores.shape[0]
    nb = 8 if db % 8 == 0 else 1
    blk = pl.BlockSpec((nb,) + scores.shape[1:], lambda i: (i, 0, 0, 0))
    return pl.pallas_call(
        functools.partial(_dsa_sample_search_kernel, topk=topk),
        grid=(db // nb,),
        in_specs=[blk],
        out_specs=blk,
        out_shape=jax.ShapeDtypeStruct(scores.shape, F32),
        compiler_params=pltpu.CompilerParams(dimension_semantics=("arbitrary",), vmem_limit_bytes=VMEM_LIMIT),
        name="dsa_sample_search",
    )(scores)


def _dsa_sample_attend_kernel(pt_ref, q_ref, bias_ref, knew_ref, vnew_ref, zb_ref, *refs, pages_per_step, n_pages,
                              ds):
    k_refs = refs[:pages_per_step]
    v_refs = refs[pages_per_step:2 * pages_per_step]
    o_ref, acc_ref, m_ref, l_ref = refs[2 * pages_per_step:]
    g = pl.program_id(1)
    rows_q = ds * B_HEADS
    q = q_ref[0]

    @pl.when(g == 0)
    def _():
        acc_ref[...] = jnp.zeros(acc_ref.shape, F32)
        m_ref[...] = jnp.full(m_ref.shape, NEG_BIG, F32)
        l_ref[...] = jnp.zeros(l_ref.shape, F32)

    def step(k_keys, v_keys, bias8):
        n = k_keys.shape[0]
        bias = jnp.concatenate([jnp.broadcast_to(bias8[t:t + 1, :], (B_HEADS, n)) for t in range(ds)], axis=0)
        s = _dot_nt(q, k_keys) + bias
        m_old = m_ref[...]
        m_new = jnp.maximum(m_old, jnp.max(s, axis=1, keepdims=True))
        alpha = jnp.exp2(m_old - m_new)
        p = jnp.exp2(s - m_new)
        l_ref[...] = alpha * l_ref[...] + jnp.sum(p, axis=1, keepdims=True)
        m_ref[...] = m_new
        acc_ref[...] = alpha * acc_ref[...] + _dot(p.astype(BF16), v_keys)

    def page_keys(ref):
        return jnp.concatenate([ref[0, pl.ds(kv, PAGE, stride=B_KV), :] for kv in range(B_KV)], axis=1).astype(BF16)

    step(jnp.concatenate([page_keys(r) for r in k_refs], axis=0),
         jnp.concatenate([page_keys(r) for r in v_refs], axis=0),
         jnp.concatenate([bias_ref[0, g * pages_per_step + i] for i in range(pages_per_step)], axis=1))

    @pl.when(g == pl.num_programs(1) - 1)
    def _():
        step(knew_ref[0], vnew_ref[0], bias_ref[0, n_pages])
        o = acc_ref[...] / l_ref[...]
        head = lax.broadcasted_iota(I32, (rows_q, B_HD), 0) & (B_HEADS - 1)
        o_sel = o[:, 0:B_HD]
        for kv in range(1, B_KV):
            o_sel = jnp.where(head >= kv * B_GROUP, o[:, kv * B_HD:(kv + 1) * B_HD], o_sel)
        o_ref[0] = (o_sel * zb_ref[0]).astype(o_ref.dtype)


def _dsa_sample_attend(page_table, q_bd, bias, k_new, v_new, zb_s, cache_k, cache_v, *, ds, pages_per_step):
    db, n_pages = page_table.shape
    npg = n_pages // pages_per_step
    rows_q = ds * B_HEADS
    kvw = B_KV * B_HD
    per_b = lambda a: pl.BlockSpec((1,) + a.shape[1:], lambda b, g, pt: (b,) + (0,) * (a.ndim - 1))
    page_specs = [pl.BlockSpec((1, PAGE * B_KV, B_HD), lambda b, g, pt, i=i: (pt[b, g * pages_per_step + i], 0, 0))
                  for i in range(pages_per_step)]
    return pl.pallas_call(
        functools.partial(_dsa_sample_attend_kernel, pages_per_step=pages_per_step, n_pages=n_pages, ds=ds),
        grid_spec=pltpu.PrefetchScalarGridSpec(
            num_scalar_prefetch=1, grid=(db, npg),
            in_specs=[per_b(q_bd), per_b(bias), per_b(k_new), per_b(v_new), per_b(zb_s)] + page_specs + page_specs,
            out_specs=pl.BlockSpec((1, rows_q, B_HD), lambda b, g, pt: (b, 0, 0)),
            scratch_shapes=[pltpu.VMEM((rows_q, kvw), F32), pltpu.VMEM((rows_q, 1), F32),
                            pltpu.VMEM((rows_q, 1), F32)]),
        out_shape=jax.ShapeDtypeStruct((db, rows_q, B_HD), BF16),
        compiler_params=pltpu.CompilerParams(dimension_semantics=("arbitrary", "arbitrary"),
                                             vmem_limit_bytes=VMEM_LIMIT),
        name="dsa_sample_attend",
    )(page_table, q_bd, bias, k_new, v_new, zb_s, *([cache_k] * pages_per_step), *([cache_v] * pages_per_step))


def _sample_path(x, pe, cache_k, cache_v, cache_ik, page_table, c0, n0, m0, conv0, w, w_packed, misc_bias):
    db, ds, d = x.shape
    ms = db * ds
    assert ds >= CONV_W - 1 and ds <= 8
    n_pages = page_table.shape[1]
    a = _inproj(x.reshape(ms, d), w['norm_pre'], w_packed, w['conv_w'], w['conv_b'], misc_bias,
                fuse_conv=False, seq_len=ms, tm=ms)

    tmaj = lambda z: jnp.swapaxes(z.reshape(db, ds, -1), 0, 1).reshape(ms, -1)
    ya_t, c_new, n_new, m_new = _mlstm_sample(
        tmaj(a['q']), tmaj(a['k']), jnp.swapaxes(conv0, 0, 1), w['conv_w'], w['conv_b'], tmaj(a['av']),
        tmaj(a['misc']), tmaj(a['ga']), w['norm_a'], jnp.swapaxes(n0, 0, 1),
        jnp.broadcast_to(jnp.swapaxes(m0, 0, 1)[:, :, None], (A_HEADS, db, LANES)), c0, db=db, ds=ds)
    ya = jnp.swapaxes(ya_t.reshape(ds, db, -1), 0, 1).reshape(ms, -1)

    topk = min(TOPK_MAX, (n_pages * PAGE + ds) // 4)
    pad_rows = lambda z: jnp.pad(z.reshape(db, ds, -1), ((0, 0), (0, PAGE - ds), (0, 0))).astype(BF16)
    pad_tok = lambda z: jnp.pad(z.reshape(db, ds, -1), ((0, 0), (0, 8 - ds), (0, 0)))
    iq_s = pad_tok(a['iq']).reshape(db, 8 * IDX_HEADS, IDX_DIM)
    iw_bc = jnp.broadcast_to(pad_tok(a['misc'][:, MISC_IW:MISC_IW + IDX_HEADS]).reshape(db, 8 * IDX_HEADS, 1),
                             (db, 8 * IDX_HEADS, LANES))
    pps = 16 if n_pages % 16 == 0 else n_pages
    scores = _dsa_sample_scores(page_table, iq_s, iw_bc, jnp.swapaxes(pad_rows(a['misc'][:, MISC_IK:]), 1, 2),
                                jnp.swapaxes(cache_ik, 1, 2), ds=ds, pages_per_step=pps)
    bias = _dsa_sample_search(scores, topk=topk)
    q5 = a['bq'].reshape(db, ds * B_HEADS, 1, B_HD)
    kv_of_row = (jnp.arange(ds * B_HEADS) % B_HEADS) // B_GROUP
    q_bd = jnp.where((kv_of_row[None, :, None, None] == jnp.arange(B_KV)[None, None, :, None]), q5,
                     jnp.zeros((), BF16)).reshape(db, ds * B_HEADS, B_KV * B_HD)
    attn = _dsa_sample_attend(page_table, q_bd, bias, pad_rows(a['kf']), pad_rows(a['vf']),
                              a['zb'].reshape(db, ds * B_HEADS, B_HD),
                              cache_k.reshape(-1, PAGE * B_KV, B_HD), cache_v.reshape(-1, PAGE * B_KV, B_HD),
                              ds=ds, pages_per_step=pps)
    yb = attn.reshape(ms, B_WIDTH)
    y = _outproj(ya, yb, a['sga'], a['sgb'], x.reshape(ms, d), pe.reshape(ms, -1), w['wa'], w['wb'], w['wo'],
                 w['wg'], w['wp'], w['norm_post'], tm=ms)
    qk_pre = jnp.concatenate([a['q'], a['k']], axis=1).reshape(db, ds, 2 * A_QK)
    return dict(
        y=y.reshape(db, ds, d),
        k=a['kf'].reshape(1, db, ds, B_KV, B_HD),
        v=a['vf'].reshape(1, db, ds, B_KV, B_HD),
        ik=a['misc'][:, MISC_IK:].reshape(1, db, ds, IDX_DIM),
        C=c_new[None],
        n=jnp.swapaxes(n_new, 0, 1)[None],
        m=jnp.swapaxes(m_new[:, :, 0], 0, 1)[None],
        conv=qk_pre[:, ds - (CONV_W - 1):, :][None],
    )


def _prep_weights(w_in, conv_w, conv_b, if_bias, norm_a, w_a_proj, w_b_proj, w_out, norm_pre, norm_post, w_ple,
                  w_ple_gate):
    d = w_in.shape[1]
    misc_bias = jnp.zeros((1, LANES), F32).at[0, :2 * A_HEADS].set(if_bias[0])
    w = dict(norm_pre=norm_pre[0][None], norm_post=norm_post[0][None], norm_a=norm_a[0][None],
             conv_w=conv_w[0], conv_b=conv_b[0][None],
             wa=w_a_proj[0].astype(BF16), wb=w_b_proj[0].astype(BF16), wo=w_out[0].astype(BF16),
             wg=w_ple_gate[0].astype(BF16), wp=w_ple[0].astype(BF16))
    return w, _pack_w_in(w_in[0], d), misc_bias


def kernel(x_prompt, x_sample, cache_k, cache_v, cache_idx_k, page_table, state_C, state_n, state_m, state_conv,
           p_prompt, p_sample, w_in, conv_w, conv_b, if_bias, norm_a, w_a_proj, w_b_proj, w_out, norm_pre,
           norm_post, w_ple, w_ple_gate):
    w, w_packed, misc_bias = _prep_weights(w_in, conv_w, conv_b, if_bias, norm_a, w_a_proj, w_b_proj, w_out,
                                           norm_pre, norm_post, w_ple, w_ple_gate)
    p = _prompt_path(x_prompt, p_prompt[0], w, w_packed, misc_bias)
    s = _sample_path(x_sample, p_sample[0], cache_k[0], cache_v[0], cache_idx_k[0], page_table, state_C[0],
                     state_n[0], state_m[0], state_conv[0], w, w_packed, misc_bias)
    names = ('k', 'v', 'ik', 'C', 'n', 'm', 'conv')
    return (p['y'], s['y']) + tuple(p[n] for n in names) + tuple(s[n] for n in names)
```

```python
import functools

import jax
import jax.numpy as jnp
from jax import lax
from jax.experimental import pallas as pl
from jax.experimental.pallas import tpu as pltpu

F32 = jnp.float32
BF16 = jnp.bfloat16
I32 = jnp.int32

EPS = 1e-6
A_HEADS = 4
A_DQK = 128
A_DV = 256
A_QK = A_HEADS * A_DQK
A_WIDTH = A_HEADS * A_DV
CONV_W = 4
B_HEADS = 8
B_KV = 2
B_HD = 128
B_GROUP = B_HEADS // B_KV
B_WIDTH = B_HEADS * B_HD
IDX_HEADS = 8
IDX_DIM = 64
TOPK_MAX = 256
PAGE = 128
LANES = 128
VMEM_LIMIT = 56 * 1024 * 1024

Q_SCALE = A_DQK ** -0.5
ATT_SCALE = B_HD ** -0.5
IDX_SCALE = IDX_DIM ** -0.5
IW_SCALE = IDX_HEADS ** -0.5
C_EXP = ATT_SCALE * 1.4426950408889634
VT_ROWS = B_HD + 16
NEG_BIG = -1e30
INT_MIN = -(2 ** 31)
KEY_NEG_INF = INT_MIN + 0x007FFFFF

_SPLITS = (('a_q', A_QK), ('a_k', A_QK), ('a_v', A_WIDTH), ('a_i', A_HEADS), ('a_f', A_HEADS),
           ('a_o', A_WIDTH), ('a_z', A_WIDTH), ('b_q', B_WIDTH), ('b_k', B_KV * B_HD), ('b_v', B_KV * B_HD),
           ('b_iq', IDX_HEADS * IDX_DIM), ('b_ik', IDX_DIM), ('b_iw', IDX_HEADS), ('b_z', B_WIDTH),
           ('g_a', 1024), ('g_b', 1024))

P_QK = 0
P_AV = 1024
P_AO = 2048
P_AZ = 3072
P_BQ = 4096
P_BK = 5120
P_BV = 5376
P_IQ = 5632
P_MISC = 6144
P_BZ = 6272
P_GA = 7296
P_GB = 8320
P_COLS = 9344
MISC_IW = 8
MISC_IK = 64


def _sigmoid(x):
    return 1.0 / (1.0 + jnp.exp(-x))


def _silu(x):
    return x * _sigmoid(x)


def _dot(a, b):
    return jnp.dot(a, b, preferred_element_type=F32)


def _dot_nt(a, b):
    return lax.dot_general(a, b, (((1,), (1,)), ((), ())), preferred_element_type=F32)


def _transpose(x):
    r, c = x.shape
    rows = []
    for j in range(c // LANES):
        rows.append(jnp.concatenate([x[i * LANES:(i + 1) * LANES, j * LANES:(j + 1) * LANES].T
                                     for i in range(r // LANES)], axis=1))
    return jnp.concatenate(rows, axis=0)


def _pack_w_in(w_in, d_model):
    offs = {}
    c = 0
    for name, n in _SPLITS:
        offs[name] = (c, c + n)
        c += n

    def col(name):
        lo, hi = offs[name]
        return w_in[:, lo:hi]

    z = lambda n: jnp.zeros((d_model, n), w_in.dtype)
    misc = jnp.concatenate([col('a_i'), col('a_f'), col('b_iw'), z(MISC_IK - 16), col('b_ik')], axis=1)
    packed = jnp.concatenate([col('a_q'), col('a_k'), col('a_v'), col('a_o'), col('a_z'), col('b_q'), col('b_k'),
                              col('b_v'), col('b_iq'), misc, col('b_z'), col('g_a'), col('g_b')], axis=1)
    return packed.astype(BF16)


def _inproj_kernel(x_ref, g_ref, w_ref, cw_ref, cb_ref, mb_ref,
                   q_ref, k_ref, tail_ref, av_ref, ga_ref, bq_ref, kf_ref, vf_ref, kb_ref, vb_ref, iq_ref,
                   misc_ref, miscb_ref, zb_ref, sga_ref, sgb_ref, hist_ref, *, fuse_conv, tiles_per_seq):
    tm = x_ref.shape[0]
    x = x_ref[...]
    ms = jnp.mean(x * x, axis=-1, keepdims=True)
    xn = (x * lax.rsqrt(ms + EPS) * g_ref[...]).astype(BF16)

    def proj(lo, hi):
        return _dot(xn, w_ref[:, lo:hi])

    qk = proj(P_QK, P_QK + 2 * A_QK)
    tail_ref[0] = qk[tm - 8:tm, :]
    if fuse_conv:
        @pl.when(pl.program_id(0) % tiles_per_seq == 0)
        def _():
            hist_ref[0:8, :] = jnp.zeros((8, 2 * A_QK), F32)

        hist_ref[8:8 + tm, :] = qk
        cw = cw_ref[...]
        y = cb_ref[...] + cw[0:1, :] * hist_ref[5:5 + tm, :]
        y = y + cw[1:2, :] * hist_ref[6:6 + tm, :]
        y = y + cw[2:3, :] * hist_ref[7:7 + tm, :]
        y = y + cw[3:4, :] * qk
        act = _silu(y)
        q_ref[...] = (act[:, :A_QK] * Q_SCALE).astype(q_ref.dtype)
        k_ref[...] = act[:, A_QK:].astype(k_ref.dtype)
        hist_ref[0:8, :] = qk[tm - 8:tm, :]
    else:
        q_ref[...] = qk[:, :A_QK]
        k_ref[...] = qk[:, A_QK:]

    av_ref[...] = proj(P_AV, P_AV + A_WIDTH).astype(av_ref.dtype)
    ga_ref[...] = _sigmoid(proj(P_AO, P_AO + A_WIDTH)) * _silu(proj(P_AZ, P_AZ + A_WIDTH))
    bq_ref[...] = (proj(P_BQ, P_BQ + B_WIDTH) * C_EXP).astype(bq_ref.dtype)
    kf = proj(P_BK, P_BK + B_KV * B_HD)
    vf = proj(P_BV, P_BV + B_KV * B_HD)
    kf_ref[...] = kf
    vf_ref[...] = vf
    kb_ref[...] = kf.astype(BF16)
    vb_ref[...] = vf.astype(BF16)
    iq_ref[...] = proj(P_IQ, P_IQ + IDX_HEADS * IDX_DIM).astype(iq_ref.dtype)

    raw = proj(P_MISC, P_MISC + LANES) + mb_ref[...]
    lane = lax.broadcasted_iota(I32, raw.shape, 1)
    logsig = jnp.minimum(raw, 0.0) - jnp.log(1.0 + jnp.exp(-jnp.abs(raw)))
    misc = jnp.where((lane >= A_HEADS) & (lane < 2 * A_HEADS), logsig,
                     jnp.where((lane >= MISC_IW) & (lane < MISC_IW + IDX_HEADS), raw * IW_SCALE, raw))
    misc_ref[...] = misc
    miscb_ref[...] = jnp.where(lane >= MISC_IK, raw, 0.0).astype(BF16)

    zb_ref[...] = _silu(proj(P_BZ, P_BZ + B_WIDTH))
    sga_ref[...] = _sigmoid(proj(P_GA, P_GA + 1024))
    sgb_ref[...] = _sigmoid(proj(P_GB, P_GB + 1024))


def _inproj(x2d, norm_pre, w_packed, conv_w, conv_b, misc_bias, *, fuse_conv, seq_len, tm):
    m, d = x2d.shape
    nt = m // tm
    qk_dt = BF16 if fuse_conv else F32
    widths = [('q', A_QK, qk_dt), ('k', A_QK, qk_dt), ('tail', None, F32), ('av', A_WIDTH, BF16),
              ('ga', A_WIDTH, F32), ('bq', B_WIDTH, BF16), ('kf', B_KV * B_HD, F32), ('vf', B_KV * B_HD, F32),
              ('kb', B_KV * B_HD, BF16), ('vb', B_KV * B_HD, BF16), ('iq', IDX_HEADS * IDX_DIM, BF16),
              ('misc', LANES, F32), ('miscb', LANES, BF16), ('zb', B_WIDTH, F32), ('sga', d, F32), ('sgb', d, F32)]
    out_shape, out_specs = [], []
    for name, w, dt in widths:
        if name == 'tail':
            out_shape.append(jax.ShapeDtypeStruct((nt, 8, 2 * A_QK), dt))
            out_specs.append(pl.BlockSpec((1, 8, 2 * A_QK), lambda i: (i, 0, 0)))
        else:
            out_shape.append(jax.ShapeDtypeStruct((m, w), dt))
            out_specs.append(pl.BlockSpec((tm, w), lambda i: (i, 0)))
    const = lambda shape: pl.BlockSpec(shape, lambda i: (0,) * len(shape))
    outs = pl.pallas_call(
        functools.partial(_inproj_kernel, fuse_conv=fuse_conv, tiles_per_seq=max(seq_len // tm, 1)),
        grid=(nt,),
        in_specs=[pl.BlockSpec((tm, d), lambda i: (i, 0)), const((1, d)),
                  pl.BlockSpec((d, P_COLS), lambda i: (0, 0), pipeline_mode=pl.Buffered(1)),
                  const((CONV_W, 2 * A_QK)), const((1, 2 * A_QK)), const((1, LANES))],
        out_specs=out_specs,
        out_shape=out_shape,
        scratch_shapes=[pltpu.VMEM((8 + tm, 2 * A_QK), F32)],
        compiler_params=pltpu.CompilerParams(dimension_semantics=("arbitrary",), vmem_limit_bytes=VMEM_LIMIT),
        name="inproj_conv" if fuse_conv else "inproj",
    )(x2d, norm_pre, w_packed, conv_w, conv_b, misc_bias)
    return {name: o for (name, _, _), o in zip(widths, outs)}


def _outproj_kernel(ya_ref, yb_ref, sga_ref, sgb_ref, x_ref, pe_ref, wa_ref, wb_ref, wo_ref, wg_ref, wp_ref,
                    g_ref, o_ref):
    merged = sga_ref[...] * _dot(ya_ref[...], wa_ref[...]) + sgb_ref[...] * _dot(yb_ref[...], wb_ref[...])
    z = _dot(merged.astype(BF16), wo_ref[...])
    ms = jnp.mean(z * z, axis=-1, keepdims=True)
    x1 = x_ref[...] + z * lax.rsqrt(ms + EPS) * g_ref[...]
    gate = _sigmoid(_dot(x1.astype(BF16), wg_ref[...]))
    o_ref[...] = x1 + _dot(pe_ref[...].astype(BF16), wp_ref[...]) * gate


def _outproj(ya, yb, sga, sgb, x2d, pe2d, wa, wb, wo, wg, wp, norm_post, *, tm):
    m, d = x2d.shape
    pd = pe2d.shape[1]
    row = lambda w: pl.BlockSpec((tm, w), lambda i: (i, 0))
    const = lambda shape: pl.BlockSpec(shape, lambda i: (0,) * len(shape))
    return pl.pallas_call(
        _outproj_kernel,
        grid=(m // tm,),
        in_specs=[row(A_WIDTH), row(B_WIDTH), row(d), row(d), row(d), row(pd),
                  const((A_WIDTH, d)), const((B_WIDTH, d)), const((d, d)), const((d, d)), const((pd, d)),
                  const((1, d))],
        out_specs=row(d),
        out_shape=jax.ShapeDtypeStruct((m, d), F32),
        compiler_params=pltpu.CompilerParams(dimension_semantics=("arbitrary",), vmem_limit_bytes=VMEM_LIMIT),
        name="outproj",
    )(ya, yb, sga, sgb, x2d, pe2d, wa, wb, wo, wg, wp, norm_post)


def _cumsum_rows(x):
    n = x.shape[0]
    row = lax.broadcasted_iota(I32, x.shape, 0)
    s = 1
    while s < n:
        x = x + jnp.where(row >= s, pltpu.roll(x, s, axis=0), 0.0)
        s *= 2
    return x


def _lane_col(x, lane_iota, idx):
    return jnp.sum(jnp.where(lane_iota == idx, x, 0.0), axis=1, keepdims=True)


def _mlstm_prompt_kernel(q_ref, k_ref, v_ref, misc_ref, ga_ref, na_ref, ya_ref, c_out_ref, nm_out_ref,
                         c_ref, n_ref, m_ref):
    c_idx = pl.program_id(1)
    nb, L = q_ref.shape[0], q_ref.shape[1]

    @pl.when(c_idx == 0)
    def _():
        c_ref[...] = jnp.zeros(c_ref.shape, F32)
        n_ref[...] = jnp.zeros(n_ref.shape, F32)
        m_ref[...] = jnp.zeros(m_ref.shape, F32)

    lane = lax.broadcasted_iota(I32, (L, LANES), 1)
    t_i = lax.broadcasted_iota(I32, (L, L), 0)
    s_i = lax.broadcasted_iota(I32, (L, L), 1)
    causal = s_i <= t_i

    for sq in range(nb):
        misc = misc_ref[sq]
        cs = _cumsum_rows(misc)
        b_al = pltpu.roll(cs, LANES - A_HEADS, axis=1)
        xa = jnp.where(lane < A_HEADS, misc - b_al, cs)
        xt = _transpose(xa)

        for h in range(A_HEADS):
            a_col = _lane_col(xa, lane, h)
            b_col = _lane_col(xa, lane, A_HEADS + h)
            a_row = xt[h:h + 1, :]
            m_prev = m_ref[sq, h:h + 1, 0:1]
            logit = jnp.where(causal, a_row, -jnp.inf)
            g_col = jnp.maximum(m_prev, jnp.max(logit, axis=1, keepdims=True))
            dmat = jnp.exp(logit - g_col)
            w_inter = jnp.exp(m_prev - g_col)
            qh = q_ref[sq, :, h * A_DQK:(h + 1) * A_DQK]
            kh = k_ref[sq, :, h * A_DQK:(h + 1) * A_DQK]
            vh = v_ref[sq, :, h * A_DV:(h + 1) * A_DV]
            c_h = c_ref[sq, h]
            n_h = n_ref[sq, h:h + 1, :]
            s = _dot_nt(qh, kh) * dmat
            num = _dot(s.astype(BF16), vh) + w_inter * _dot_nt(qh, c_h.astype(BF16))
            den = (jnp.sum(s, axis=1, keepdims=True)
                   + w_inter * jnp.sum(qh.astype(F32) * n_h, axis=1, keepdims=True))
            m_t = b_col + g_col
            hh = num / jnp.maximum(jnp.abs(den), jnp.exp(-m_t))
            hn = hh * lax.rsqrt(jnp.mean(hh * hh, axis=1, keepdims=True) + EPS)
            sl = slice(h * A_DV, (h + 1) * A_DV)
            ya_ref[sq, :, sl] = (ga_ref[sq, :, sl] * (hn * na_ref[:, sl])).astype(ya_ref.dtype)

            g_last = g_col[L - 1:L, :]
            w_prev = jnp.exp(m_prev - g_last)
            w_s = jnp.exp(a_col - g_last)
            vw = vh.astype(F32) * w_s
            c_ref[sq, h] = w_prev * c_h + _dot(_transpose(vw).astype(BF16), kh)
            n_ref[sq, h:h + 1, :] = w_prev * n_h + jnp.sum(kh.astype(F32) * w_s, axis=0, keepdims=True)
            m_ref[sq, h:h + 1, :] = jnp.broadcast_to(b_col[L - 1:L, :] + g_last, (1, LANES))

    @pl.when(c_idx == pl.num_programs(1) - 1)
    def _():
        c_out_ref[...] = c_ref[...]
        nm_out_ref[:, 0:A_HEADS, :] = n_ref[:, 0:A_HEADS, :]
        nm_out_ref[:, A_HEADS:2 * A_HEADS, :] = m_ref[:, 0:A_HEADS, :]


def _mlstm_prompt(q, k, v, misc, ga, norm_a, *, batch, seq_len, chunk):
    nc = seq_len // chunk
    nb = 1
    seq3 = lambda a: a.reshape(batch, seq_len, a.shape[-1])
    blk = lambda w: pl.BlockSpec((nb, chunk, w), lambda b, c: (b, c, 0))
    ya, c_out, nm_out = pl.pallas_call(
        _mlstm_prompt_kernel,
        grid=(batch // nb, nc),
        in_specs=[blk(A_QK), blk(A_QK), blk(A_WIDTH), blk(LANES), blk(A_WIDTH),
                  pl.BlockSpec((1, A_WIDTH), lambda b, c: (0, 0))],
        out_specs=[blk(A_WIDTH),
                   pl.BlockSpec((nb, A_HEADS, A_DV, A_DQK), lambda b, c: (b, 0, 0, 0)),
                   pl.BlockSpec((nb, 8, LANES), lambda b, c: (b, 0, 0))],
        out_shape=[jax.ShapeDtypeStruct((batch, seq_len, A_WIDTH), BF16),
                   jax.ShapeDtypeStruct((batch, A_HEADS, A_DV, A_DQK), F32),
                   jax.ShapeDtypeStruct((batch, 8, LANES), F32)],
        scratch_shapes=[pltpu.VMEM((nb, A_HEADS, A_DV, A_DQK), F32), pltpu.VMEM((nb, 8, LANES), F32),
                        pltpu.VMEM((nb, 8, LANES), F32)],
        compiler_params=pltpu.CompilerParams(dimension_semantics=("arbitrary", "arbitrary"),
                                             vmem_limit_bytes=VMEM_LIMIT),
        name="mlstm_prompt",
    )(seq3(q), seq3(k), seq3(v), seq3(misc), seq3(ga), norm_a)
    return ya.reshape(batch * seq_len, A_WIDTH), c_out, nm_out


def _key_to_f32(key):
    return pltpu.bitcast(jnp.where(key < 0, key ^ 0x7FFFFFFF, key), F32)


def _radix_threshold(count_ge, shape, topk):
    def bit_body(i, u):
        cand_u = u | jnp.left_shift(jnp.int32(1), 31 - i)
        cnt = count_ge(_key_to_f32(cand_u ^ INT_MIN))
        return jnp.where(cnt >= topk, cand_u, u)

    u = lax.fori_loop(0, 32, bit_body, jnp.zeros(shape, I32))
    key = jnp.maximum(u ^ INT_MIN, KEY_NEG_INF)
    return _key_to_f32(key), _key_to_f32(key + 1)


def _refine_in_bin(count_ge, thr, thr_up, topk):
    lo, hi = thr, thr_up
    for _ in range(6):
        mid = lo + 0.5 * (hi - lo)
        ok = count_ge(mid) >= topk
        lo = jnp.where(ok, mid, lo)
        hi = jnp.where(ok, hi, mid)
    return lo


def _dsa_prompt_kernel(bq_ref, iq_ref, misc_ref, zb_ref, ikb_ref, kb_ref, vt_ref, yb_ref,
                       sc_ref, s_ref, s2_ref, p_ref, p2_ref, al_ref, al2_ref, acc_ref, m_ref, *, topk, nq):
    j = pl.program_id(1)
    T = PAGE
    KT = 2 * PAGE
    nkt = j // 2 + 1

    iq_t = _transpose(iq_ref[...].astype(F32))
    zeros_pad = jnp.zeros((LANES - IDX_DIM, T), F32)
    iq_pad = [jnp.concatenate([zeros_pad, iq_t[h * IDX_DIM:(h + 1) * IDX_DIM, :]], axis=0).astype(BF16)
              for h in range(IDX_HEADS)]
    misc_t = _transpose(misc_ref[...])
    q_t = _transpose(bq_ref[...].astype(F32))
    q_grp = [jnp.concatenate([q_t[(g * B_GROUP + hh) * B_HD:(g * B_GROUP + hh + 1) * B_HD, :]
                              for hh in range(B_GROUP)], axis=1).astype(BF16) for g in range(B_KV)]

    row_i = lax.broadcasted_iota(I32, (KT, T), 0)
    lane_i = lax.broadcasted_iota(I32, (KT, T), 1)

    last_pair = nq // 2 - 1

    def idx_body(c, carry):
        ikc = ikb_ref[0, c]
        acc = jnp.zeros((KT, T), F32)
        for h in range(IDX_HEADS):
            r = jnp.maximum(_dot(ikc, iq_pad[h]) * IDX_SCALE, 0.0)
            acc = acc + r * misc_t[MISC_IW + h:MISC_IW + h + 1, :]
        vis = (c * KT + row_i) <= (j * T + lane_i)
        sc_ref[c] = jnp.where(vis, acc, -jnp.inf)
        return carry

    lax.fori_loop(0, nkt, idx_body, 0)

    def count(pred):
        def body(c, cnt):
            m = pred(sc_ref[c], c).reshape(KT // 32, 4, 8, T)
            for v in range(KT // 32):
                cnt = jnp.where(m[v], cnt + 1, cnt)
            return cnt
        cnt4 = lax.fori_loop(0, nkt, body, jnp.zeros((4, 8, T), I32))
        return jnp.sum(jnp.sum(cnt4, axis=0), axis=0, keepdims=True)

    count_ge = lambda t: count(lambda sc, c: sc >= t)
    thr0, thr_up = _radix_threshold(count_ge, (1, T), topk)
    n_ge = count_ge(thr0)

    def tie_search(_):
        thr_t = _refine_in_bin(count_ge, thr0, thr_up, topk)
        need = topk - count(lambda sc, c: sc > thr_t)
        nbits = (nq * T - 1).bit_length()
        def pbody(i, p):
            cand = p | jnp.left_shift(jnp.int32(1), nbits - 1 - i)
            cnt = count(lambda sc, c: (sc == thr_t) & ((c * KT + row_i) < cand))
            return jnp.where(cnt < need, cand, p)
        return thr_t, lax.fori_loop(0, nbits, pbody, jnp.zeros((1, T), I32))

    has_tie = jnp.max(jnp.where((n_ge > topk) & (thr0 > -jnp.inf), 1, 0)) > 0
    thr, p_lim = lax.cond(has_tie, tie_search, lambda _: (thr0, jnp.full((1, T), 2 ** 30, I32)), 0)

    acc_ref[...] = jnp.zeros(acc_ref.shape, F32)
    m_ref[...] = jnp.full(m_ref.shape, NEG_BIG, F32)

    def qk_scores(c, dst_ref):
        kc = kb_ref[0, jnp.minimum(c, last_pair)]
        for g in range(B_KV):
            dst_ref[g] = _dot(kc[:, g * B_HD:(g + 1) * B_HD], q_grp[g])

    def softmax(c, src_ref, p_dst, al_dst):
        sc = sc_ref[c]
        pos = c * KT + row_i
        sel = ((sc > thr) | ((sc == thr) & (pos <= p_lim))) & (sc > -jnp.inf)
        bias = jnp.where(sel, 0.0, NEG_BIG)
        for h in range(B_HEADS):
            g, hh = divmod(h, B_GROUP)
            s = src_ref[g, :, hh * T:(hh + 1) * T] + bias
            m_old = m_ref[h:h + 1, :]
            m_new = jnp.maximum(m_old, jnp.max(s, axis=0, keepdims=True))
            al_dst[h:h + 1, :] = jnp.exp2(m_old - m_new)
            p_dst[h // 2, :, (h % 2) * T:(h % 2 + 1) * T] = jnp.exp2(s - m_new).astype(BF16)
            m_ref[h:h + 1, :] = m_new

    def pv_update(c, p_src, al_src):
        vtc = vt_ref[0, c]
        for h0 in range(0, B_HEADS, 2):
            g = h0 // B_GROUP
            pv = _dot(vtc[g * VT_ROWS:(g + 1) * VT_ROWS, :], p_src[h0 // 2])
            for i, h in enumerate((h0, h0 + 1)):
                acc_ref[h] = acc_ref[h] * al_src[h:h + 1, :] + pv[:, i * T:(i + 1) * T]

    p2_ref[...] = jnp.zeros(p2_ref.shape, BF16)
    al2_ref[...] = jnp.ones(al2_ref.shape, F32)
    qk_scores(0, s_ref)

    def att_body(i, carry):
        c0 = 2 * i
        qk_scores(c0 + 1, s2_ref)
        softmax(c0, s_ref, p_ref, al_ref)
        pv_update(jnp.maximum(c0 - 1, 0), p2_ref, al2_ref)

        @pl.when(c0 + 1 < nkt)
        def _():
            qk_scores(c0 + 2, s_ref)
            softmax(c0 + 1, s2_ref, p2_ref, al2_ref)
            pv_update(c0, p_ref, al_ref)

        return carry

    lax.fori_loop(0, (nkt + 1) // 2, att_body, 0)

    @pl.when(nkt % 2 == 1)
    def _():
        pv_update(nkt - 1, p_ref, al_ref)

    @pl.when(nkt % 2 == 0)
    def _():
        pv_update(nkt - 1, p2_ref, al2_ref)

    for h in range(B_HEADS):
        o_t = acc_ref[h, 0:B_HD, :] / acc_ref[h, B_HD:B_HD + 1, :]
        sl = slice(h * B_HD, (h + 1) * B_HD)
        yb_ref[:, sl] = (_transpose(o_t) * zb_ref[:, sl]).astype(yb_ref.dtype)


def _dsa_prompt(bq, iq, misc, zb, ikb, kb, vt, *, batch, seq_len, topk):
    nq = seq_len // PAGE
    kt = 2 * PAGE
    scores = pltpu.VMEM((B_KV, kt, B_GROUP * PAGE), F32)
    weights = pltpu.VMEM((B_HEADS // 2, kt, 2 * PAGE), BF16)
    rescale = pltpu.VMEM((B_HEADS, PAGE), F32)
    row = lambda w: pl.BlockSpec((PAGE, w), lambda b, j: (b * nq + j, 0))
    per_b = lambda a: pl.BlockSpec((1,) + a.shape[1:], lambda b, j: (b, 0, 0, 0))
    return pl.pallas_call(
        functools.partial(_dsa_prompt_kernel, topk=topk, nq=nq),
        grid=(batch, nq),
        in_specs=[row(B_WIDTH), row(IDX_HEADS * IDX_DIM), row(LANES), row(B_WIDTH),
                  per_b(ikb), per_b(kb), per_b(vt)],
        out_specs=row(B_WIDTH),
        out_shape=jax.ShapeDtypeStruct((batch * seq_len, B_WIDTH), BF16),
        scratch_shapes=[pltpu.VMEM((nq // 2, kt, PAGE), F32), scores, scores, weights, weights, rescale, rescale,
                        pltpu.VMEM((B_HEADS, VT_ROWS, PAGE), F32),
                        pltpu.VMEM((B_HEADS, PAGE), F32)],
        compiler_params=pltpu.CompilerParams(dimension_semantics=("arbitrary", "arbitrary"),
                                             vmem_limit_bytes=VMEM_LIMIT),
        name="dsa_prompt",
    )(bq, iq, misc, zb, ikb, kb, vt)


def _prompt_path(x, pe, w, w_packed, misc_bias):
    batch, seq_len, d = x.shape
    m = batch * seq_len
    x2d = x.reshape(m, d)
    a = _inproj(x2d, w['norm_pre'], w_packed, w['conv_w'], w['conv_b'], misc_bias,
                fuse_conv=True, seq_len=seq_len, tm=min(256, seq_len))
    chunk = min(128, seq_len)
    ya, c_out, nm_out = _mlstm_prompt(a['q'], a['k'], a['av'], a['misc'], a['ga'], w['norm_a'],
                                      batch=batch, seq_len=seq_len, chunk=chunk)
    nk2 = seq_len // (2 * PAGE)
    ikb = a['miscb'].reshape(batch, nk2, 2 * PAGE, LANES)
    kb = a['kb'].reshape(batch, nk2, 2 * PAGE, B_KV * B_HD)
    vt = jnp.swapaxes(a['vb'].reshape(batch, nk2, 2 * PAGE, B_KV, B_HD), 2, 4)
    vt = jnp.concatenate([jnp.swapaxes(vt, 2, 3), jnp.ones((batch, nk2, B_KV, VT_ROWS - B_HD, 2 * PAGE), BF16)],
                         axis=3).reshape(batch, nk2, B_KV * VT_ROWS, 2 * PAGE)
    topk = min(TOPK_MAX, seq_len // 4)
    yb = _dsa_prompt(a['bq'], a['iq'], a['misc'], a['zb'], ikb, kb, vt, batch=batch, seq_len=seq_len, topk=topk)
    y = _outproj(ya, yb, a['sga'], a['sgb'], x2d, pe.reshape(m, -1), w['wa'], w['wb'], w['wo'], w['wg'], w['wp'],
                 w['norm_post'], tm=min(256, m))
    tiles_per_seq = seq_len // min(256, seq_len)
    tail = a['tail'].reshape(batch, tiles_per_seq, 8, 2 * A_QK)[:, -1, 8 - (CONV_W - 1):, :]
    return dict(
        y=y.reshape(batch, seq_len, d),
        k=a['kf'].reshape(1, batch, seq_len, B_KV, B_HD),
        v=a['vf'].reshape(1, batch, seq_len, B_KV, B_HD),
        ik=a['misc'][:, MISC_IK:].reshape(1, batch, seq_len, IDX_DIM),
        C=c_out[None],
        n=nm_out[None, :, 0:A_HEADS, :],
        m=nm_out[None, :, A_HEADS:2 * A_HEADS, 0],
        conv=tail[None],
    )


def _mlstm_sample_kernel(q_ref, k_ref, conv0_ref, cw_ref, cb_ref, v_ref, misc_ref, ga_ref, na_ref, n0_ref, m0_ref,
                         c0_ref, ya_ref, c_out_ref, n_out_ref, m_out_ref,
                         qs_ref, ks_ref, cq_ref, wprev_ref, vwt_ref, numi_ref, den_ref, wint_ref, enm_ref, *, db, ds):
    b = pl.program_id(0)
    ms = ds * db
    rows = lambda t: slice(t * db, (t + 1) * db)

    @pl.when(b == 0)
    def _():
        cw = cw_ref[...]
        u = [conv0_ref[j] for j in range(CONV_W - 1)]
        u += [jnp.concatenate([q_ref[rows(t), :], k_ref[rows(t), :]], axis=1) for t in range(ds)]
        for t in range(ds):
            y = cb_ref[...] + cw[0:1, :] * u[t]
            for jj in range(1, CONV_W):
                y = y + cw[jj:jj + 1, :] * u[t + jj]
            act = _silu(y)
            qs_ref[rows(t), :] = act[:, :A_QK] * Q_SCALE
            ks_ref[rows(t), :] = act[:, A_QK:]
        cq_ref[...] = jnp.zeros(cq_ref.shape, F32)

        lane = lax.broadcasted_iota(I32, (db, LANES), 1)
        for h in range(A_HEADS):
            i_t = [_lane_col(misc_ref[rows(t), :], lane, h) for t in range(ds)]
            lf_t = [_lane_col(misc_ref[rows(t), :], lane, A_HEADS + h) for t in range(ds)]
            b_t = [lf_t[0]]
            for t in range(1, ds):
                b_t.append(b_t[-1] + lf_t[t])
            a_t = [i_t[t] - b_t[t] for t in range(ds)]
            m_prev = jnp.max(m0_ref[h], axis=1, keepdims=True)
            g_t = [jnp.maximum(m_prev, a_t[0])]
            for t in range(1, ds):
                g_t.append(jnp.maximum(g_t[-1], a_t[t]))
            hq = slice(h * A_DQK, (h + 1) * A_DQK)
            hv = slice(h * A_DV, (h + 1) * A_DV)
            qh = [qs_ref[rows(t), hq] for t in range(ds)]
            kh = [ks_ref[rows(t), hq] for t in range(ds)]
            vh = [v_ref[rows(t), hv].astype(F32) for t in range(ds)]
            n0 = n0_ref[h]
            for t in range(ds):
                num = jnp.zeros((db, A_DV), F32)
                den = jnp.zeros((db, 1), F32)
                for s in range(t + 1):
                    w_ts = jnp.sum(qh[t] * kh[s], axis=1, keepdims=True) * jnp.exp(a_t[s] - g_t[t])
                    num = num + w_ts * vh[s]
                    den = den + w_ts
                w_inter = jnp.exp(m_prev - g_t[t])
                den = den + w_inter * jnp.sum(qh[t] * n0, axis=1, keepdims=True)
                numi_ref[rows(t), hv] = num
                den_ref[h, rows(t), :] = jnp.broadcast_to(den, (db, LANES))
                wint_ref[h, rows(t), :] = jnp.broadcast_to(w_inter, (db, LANES))
                enm_ref[h, rows(t), :] = jnp.broadcast_to(jnp.exp(-(b_t[t] + g_t[t])), (db, LANES))
            g_last = g_t[ds - 1]
            w_prev = jnp.exp(m_prev - g_last)
            wprev_ref[h] = jnp.broadcast_to(w_prev, (db, LANES))
            n_new = w_prev * n0
            vw = []
            for s in range(ds):
                w_s = jnp.exp(a_t[s] - g_last)
                n_new = n_new + w_s * kh[s]
                vw.append(vh[s] * w_s)
            n_out_ref[h] = n_new
            m_out_ref[h] = jnp.broadcast_to(b_t[ds - 1] + g_last, (db, LANES))
            vw_all = jnp.concatenate(vw, axis=0)
            if ms < LANES:
                vw_all = jnp.concatenate([vw_all, jnp.zeros((LANES - ms, A_DV), F32)], axis=0)
            vwt_ref[h] = _transpose(vw_all)

    mcols = vwt_ref.shape[2]
    row_i = lax.broadcasted_iota(I32, (ms, A_DQK), 0)
    col_i = lax.broadcasted_iota(I32, (A_DV, mcols), 1)
    row_mine = row_i == b
    col_mine = col_i == b
    for t in range(1, ds):
        row_mine = row_mine | (row_i == b + t * db)
        col_mine = col_mine | (col_i == b + t * db)
    for h in range(A_HEADS):
        hq = slice(h * A_DQK, (h + 1) * A_DQK)
        c0 = c0_ref[0, h]
        q_mine = jnp.where(row_mine, qs_ref[:, hq], 0.0).astype(BF16)
        cq_ref[h] = cq_ref[h] + _dot_nt(q_mine, c0.astype(BF16))
        vw_mine = jnp.where(col_mine, vwt_ref[h], 0.0).astype(BF16)
        k_all = ks_ref[:, hq]
        if mcols > ms:
            k_all = jnp.concatenate([k_all, jnp.zeros((mcols - ms, A_DQK), F32)], axis=0)
        c_out_ref[0, h] = wprev_ref[h, pl.ds(b, 1), :] * c0 + _dot(vw_mine, k_all.astype(BF16))

    @pl.when(b == pl.num_programs(0) - 1)
    def _():
        for h in range(A_HEADS):
            hv = slice(h * A_DV, (h + 1) * A_DV)
            num = numi_ref[:, hv] + wint_ref[h][:, 0:1] * cq_ref[h]
            hh = num / jnp.maximum(jnp.abs(den_ref[h][:, 0:1]), enm_ref[h][:, 0:1])
            hn = hh * lax.rsqrt(jnp.mean(hh * hh, axis=1, keepdims=True) + EPS)
            ya_ref[:, hv] = (ga_ref[:, hv] * (hn * na_ref[:, hv])).astype(ya_ref.dtype)


def _mlstm_sample(q_t, k_t, conv0_t, conv_w, conv_b, v_t, misc_t, ga_t, norm_a, n0_t, m0_bc, c0, *, db, ds):
    ms = db * ds
    mcols = max(ms, LANES)
    full = lambda a: pl.BlockSpec(a.shape, lambda b: (0,) * a.ndim)
    cblk = pl.BlockSpec((1, A_HEADS, A_DV, A_DQK), lambda b: (b, 0, 0, 0))
    hshape = jax.ShapeDtypeStruct((A_HEADS, db, LANES), F32)
    ins = (q_t, k_t, conv0_t, conv_w, conv_b, v_t, misc_t, ga_t, norm_a, n0_t, m0_bc)
    return pl.pallas_call(
        functools.partial(_mlstm_sample_kernel, db=db, ds=ds),
        grid=(db,),
        in_specs=[full(a) for a in ins] + [cblk],
        out_specs=[pl.BlockSpec((ms, A_WIDTH), lambda b: (0, 0)), cblk,
                   pl.BlockSpec((A_HEADS, db, LANES), lambda b: (0, 0, 0)),
                   pl.BlockSpec((A_HEADS, db, LANES), lambda b: (0, 0, 0))],
        out_shape=[jax.ShapeDtypeStruct((ms, A_WIDTH), BF16), jax.ShapeDtypeStruct(c0.shape, F32), hshape, hshape],
        scratch_shapes=[pltpu.VMEM((ms, A_QK), F32), pltpu.VMEM((ms, A_QK), F32),
                        pltpu.VMEM((A_HEADS, ms, A_DV), F32), pltpu.VMEM((A_HEADS, db, LANES), F32),
                        pltpu.VMEM((A_HEADS, A_DV, mcols), F32), pltpu.VMEM((ms, A_WIDTH), F32),
                        pltpu.VMEM((A_HEADS, ms, LANES), F32), pltpu.VMEM((A_HEADS, ms, LANES), F32),
                        pltpu.VMEM((A_HEADS, ms, LANES), F32)],
        compiler_params=pltpu.CompilerParams(dimension_semantics=("arbitrary",), vmem_limit_bytes=VMEM_LIMIT),
        name="mlstm_sample",
    )(*ins, c0)


def _page_ring(pt_ref, pages_per_step, streams):
    b, g = pl.program_id(0), pl.program_id(1)
    nb, ng = pl.num_programs(0), pl.num_programs(1)
    step = b * ng + g
    slot = lax.rem(step, 2)

    def copies(bb, gg, sl):
        return [pltpu.make_async_copy(hbm.at[pt_ref[bb, gg * pages_per_step + i]], buf.at[sl, i], sem.at[sl])
                for hbm, buf, sem in streams for i in range(pages_per_step)]

    @pl.when(step == 0)
    def _():
        for cp in copies(0, 0, 0):
            cp.start()

    @pl.when(step + 1 < nb * ng)
    def _():
        wrap = g + 1 == ng
        for cp in copies(jnp.where(wrap, b + 1, b), jnp.where(wrap, 0, g + 1), 1 - slot):
            cp.start()

    for cp in copies(b, g, slot):
        cp.wait()
    return slot


def _dsa_sample_scores_kernel(pt_ref, iq_ref, iw_ref, iknew_ref, cache_ref, sc_ref, page_buf, sem,
                              *, pages_per_step, n_pages, ds):
    g = pl.program_id(1)
    slot = _page_ring(pt_ref, pages_per_step, [(cache_ref, page_buf, sem)])
    page_refs = [page_buf.at[slot, i] for i in range(pages_per_step)]
    iq = iq_ref[0]
    iw = iw_ref[0]

    def scores(keys_t, visible):
        n = keys_t.shape[1] // PAGE
        r = jnp.maximum(_dot(iq, keys_t) * IDX_SCALE, 0.0) * jnp.concatenate([iw] * n, axis=1)
        sc = jnp.sum(r.reshape(8, IDX_HEADS, n * PAGE), axis=1)
        return jnp.where(visible, sc, -jnp.inf)

    t_all = lax.broadcasted_iota(I32, (8, pages_per_step * PAGE), 0)
    sc_step = scores(jnp.concatenate([r[...] for r in page_refs], axis=1).astype(BF16), t_all < ds)
    for i in range(pages_per_step):
        sc_ref[0, g * pages_per_step + i] = sc_step[:, i * PAGE:(i + 1) * PAGE]

    @pl.when(g == pl.num_programs(1) - 1)
    def _():
        t_i = lax.broadcasted_iota(I32, (8, PAGE), 0)
        s_i = lax.broadcasted_iota(I32, (8, PAGE), 1)
        sc_ref[0, n_pages] = scores(iknew_ref[0], (t_i < ds) & (s_i <= t_i))


def _dsa_sample_scores(page_table, iq_s, iw_bc, ik_new, cache_ik, *, ds, pages_per_step):
    db, n_pages = page_table.shape
    npg = n_pages // pages_per_step
    per_b = lambda a: pl.BlockSpec((1,) + a.shape[1:], lambda b, g, pt: (b,) + (0,) * (a.ndim - 1))
    return pl.pallas_call(
        functools.partial(_dsa_sample_scores_kernel, pages_per_step=pages_per_step, n_pages=n_pages, ds=ds),
        grid_spec=pltpu.PrefetchScalarGridSpec(
            num_scalar_prefetch=1, grid=(db, npg),
            in_specs=[per_b(iq_s), per_b(iw_bc), per_b(ik_new), pl.BlockSpec(memory_space=pl.ANY)],
            out_specs=pl.BlockSpec((1, n_pages + 1, 8, PAGE), lambda b, g, pt: (b, 0, 0, 0)),
            scratch_shapes=[pltpu.VMEM((2, pages_per_step, IDX_DIM, PAGE), F32), pltpu.SemaphoreType.DMA((2,))]),
        out_shape=jax.ShapeDtypeStruct((db, n_pages + 1, 8, PAGE), F32),
        compiler_params=pltpu.CompilerParams(dimension_semantics=("arbitrary", "arbitrary"),
                                             vmem_limit_bytes=VMEM_LIMIT),
        name="dsa_sample_scores",
    )(page_table, iq_s, iw_bc, ik_new, cache_ik)


def _dsa_sample_search_kernel(sc_ref, bias_ref, *, topk):
    sc = sc_ref[...]
    nb, np1 = sc.shape[0], sc.shape[1]
    pos = lax.broadcasted_iota(I32, sc.shape, 1) * PAGE + lax.broadcasted_iota(I32, sc.shape, 3)

    def count(mask):
        per_lane = jnp.sum(mask.astype(I32), axis=1, keepdims=True)
        return jnp.sum(per_lane, axis=3, keepdims=True)

    count_ge = lambda t: count(sc_ref[...] >= t)
    thr0, thr_up = _radix_threshold(count_ge, (nb, 1, 8, 1), topk)
    nbits = (np1 * PAGE - 1).bit_length()

    def tie_search(_):
        thr_t = _refine_in_bin(count_ge, thr0, thr_up, topk)
        need = topk - count(sc > thr_t)
        tied = sc == thr_t

        def pos_body(i, p):
            cand = p | jnp.left_shift(jnp.int32(1), nbits - 1 - i)
            cnt = count(tied & (pos < cand))
            return jnp.where(cnt < need, cand, p)
        return thr_t, lax.fori_loop(0, nbits, pos_body, jnp.zeros(thr0.shape, I32))

    has_tie = jnp.max(jnp.where((count_ge(thr0) > topk) & (thr0 > -jnp.inf), 1, 0)) > 0
    thr, p_lim = lax.cond(has_tie, tie_search, lambda _: (thr0, jnp.full(thr0.shape, 2 ** 30, I32)), 0)
    sel = ((sc > thr) | ((sc == thr) & (pos <= p_lim))) & (sc > -jnp.inf)
    bias_ref[...] = jnp.where(sel, 0.0, NEG_BIG)


def _dsa_sample_search(scores, *, topk):
    db = scores.shape[0]
    nb = 8 if db % 8 == 0 else 1
    blk = pl.BlockSpec((nb,) + scores.shape[1:], lambda i: (i, 0, 0, 0))
    return pl.pallas_call(
        functools.partial(_dsa_sample_search_kernel, topk=topk),
        grid=(db // nb,),
        in_specs=[blk],
        out_specs=blk,
        out_shape=jax.ShapeDtypeStruct(scores.shape, F32),
        compiler_params=pltpu.CompilerParams(dimension_semantics=("arbitrary",), vmem_limit_bytes=VMEM_LIMIT),
        name="dsa_sample_search",
    )(scores)


def _dsa_sample_attend_kernel(pt_ref, q_ref, bias_ref, knew_ref, vnew_ref, zb_ref, ck_ref, cv_ref, o_ref,
                              k_buf, v_buf, k_sem, v_sem, acc_ref, m_ref, l_ref, *, pages_per_step, n_pages, ds):
    g = pl.program_id(1)
    slot = _page_ring(pt_ref, pages_per_step, [(ck_ref, k_buf, k_sem), (cv_ref, v_buf, v_sem)])
    k_refs = [k_buf.at[slot, i] for i in range(pages_per_step)]
    v_refs = [v_buf.at[slot, i] for i in range(pages_per_step)]
    rows_q = ds * B_HEADS
    q = q_ref[0]

    @pl.when(g == 0)
    def _():
        acc_ref[...] = jnp.zeros(acc_ref.shape, F32)
        m_ref[...] = jnp.full(m_ref.shape, NEG_BIG, F32)
        l_ref[...] = jnp.zeros(l_ref.shape, F32)

    def step(k_keys, v_keys, bias8):
        n = k_keys.shape[0]
        bias = jnp.concatenate([jnp.broadcast_to(bias8[t:t + 1, :], (B_HEADS, n)) for t in range(ds)], axis=0)
        s = _dot_nt(q, k_keys) + bias
        m_old = m_ref[...]
        m_new = jnp.maximum(m_old, jnp.max(s, axis=1, keepdims=True))
        alpha = jnp.exp2(m_old - m_new)
        p = jnp.exp2(s - m_new)
        l_ref[...] = alpha * l_ref[...] + jnp.sum(p, axis=1, keepdims=True)
        m_ref[...] = m_new
        acc_ref[...] = alpha * acc_ref[...] + _dot(p.astype(BF16), v_keys)

    def page_keys(ref):
        return jnp.concatenate([ref[pl.ds(kv, PAGE, stride=B_KV), :] for kv in range(B_KV)], axis=1).astype(BF16)

    step(jnp.concatenate([page_keys(r) for r in k_refs], axis=0),
         jnp.concatenate([page_keys(r) for r in v_refs], axis=0),
         jnp.concatenate([bias_ref[0, g * pages_per_step + i] for i in range(pages_per_step)], axis=1))

    @pl.when(g == pl.num_programs(1) - 1)
    def _():
        step(knew_ref[0], vnew_ref[0], bias_ref[0, n_pages])
        o = acc_ref[...] / l_ref[...]
        head = lax.broadcasted_iota(I32, (rows_q, B_HD), 0) & (B_HEADS - 1)
        o_sel = o[:, 0:B_HD]
        for kv in range(1, B_KV):
            o_sel = jnp.where(head >= kv * B_GROUP, o[:, kv * B_HD:(kv + 1) * B_HD], o_sel)
        o_ref[0] = (o_sel * zb_ref[0]).astype(o_ref.dtype)


def _dsa_sample_attend(page_table, q_bd, bias, k_new, v_new, zb_s, cache_k, cache_v, *, ds, pages_per_step):
    db, n_pages = page_table.shape
    npg = n_pages // pages_per_step
    rows_q = ds * B_HEADS
    kvw = B_KV * B_HD
    per_b = lambda a: pl.BlockSpec((1,) + a.shape[1:], lambda b, g, pt: (b,) + (0,) * (a.ndim - 1))
    any_space = pl.BlockSpec(memory_space=pl.ANY)
    page_buf = pltpu.VMEM((2, pages_per_step, PAGE * B_KV, B_HD), F32)
    return pl.pallas_call(
        functools.partial(_dsa_sample_attend_kernel, pages_per_step=pages_per_step, n_pages=n_pages, ds=ds),
        grid_spec=pltpu.PrefetchScalarGridSpec(
            num_scalar_prefetch=1, grid=(db, npg),
            in_specs=[per_b(q_bd), per_b(bias), per_b(k_new), per_b(v_new), per_b(zb_s), any_space, any_space],
            out_specs=pl.BlockSpec((1, rows_q, B_HD), lambda b, g, pt: (b, 0, 0)),
            scratch_shapes=[page_buf, page_buf, pltpu.SemaphoreType.DMA((2,)), pltpu.SemaphoreType.DMA((2,)),
                            pltpu.VMEM((rows_q, kvw), F32), pltpu.VMEM((rows_q, 1), F32),
                            pltpu.VMEM((rows_q, 1), F32)]),
        out_shape=jax.ShapeDtypeStruct((db, rows_q, B_HD), BF16),
        compiler_params=pltpu.CompilerParams(dimension_semantics=("arbitrary", "arbitrary"),
                                             vmem_limit_bytes=VMEM_LIMIT),
        name="dsa_sample_attend",
    )(page_table, q_bd, bias, k_new, v_new, zb_s, cache_k, cache_v)


def _sample_path(x, pe, cache_k, cache_v, cache_ik, page_table, c0, n0, m0, conv0, w, w_packed, misc_bias):
    db, ds, d = x.shape
    ms = db * ds
    assert ds >= CONV_W - 1 and ds <= 8
    n_pages = page_table.shape[1]
    a = _inproj(x.reshape(ms, d), w['norm_pre'], w_packed, w['conv_w'], w['conv_b'], misc_bias,
                fuse_conv=False, seq_len=ms, tm=ms)

    tmaj = lambda z: jnp.swapaxes(z.reshape(db, ds, -1), 0, 1).reshape(ms, -1)
    ya_t, c_new, n_new, m_new = _mlstm_sample(
        tmaj(a['q']), tmaj(a['k']), jnp.swapaxes(conv0, 0, 1), w['conv_w'], w['conv_b'], tmaj(a['av']),
        tmaj(a['misc']), tmaj(a['ga']), w['norm_a'], jnp.swapaxes(n0, 0, 1),
        jnp.broadcast_to(jnp.swapaxes(m0, 0, 1)[:, :, None], (A_HEADS, db, LANES)), c0, db=db, ds=ds)
    ya = jnp.swapaxes(ya_t.reshape(ds, db, -1), 0, 1).reshape(ms, -1)

    topk = min(TOPK_MAX, (n_pages * PAGE + ds) // 4)
    pad_rows = lambda z: jnp.pad(z.reshape(db, ds, -1), ((0, 0), (0, PAGE - ds), (0, 0))).astype(BF16)
    pad_tok = lambda z: jnp.pad(z.reshape(db, ds, -1), ((0, 0), (0, 8 - ds), (0, 0)))
    iq_s = pad_tok(a['iq']).reshape(db, 8 * IDX_HEADS, IDX_DIM)
    iw_bc = jnp.broadcast_to(pad_tok(a['misc'][:, MISC_IW:MISC_IW + IDX_HEADS]).reshape(db, 8 * IDX_HEADS, 1),
                             (db, 8 * IDX_HEADS, LANES))
    pps = 16 if n_pages % 16 == 0 else n_pages
    scores = _dsa_sample_scores(page_table, iq_s, iw_bc, jnp.swapaxes(pad_rows(a['misc'][:, MISC_IK:]), 1, 2),
                                jnp.swapaxes(cache_ik, 1, 2), ds=ds, pages_per_step=pps)
    bias = _dsa_sample_search(scores, topk=topk)
    q5 = a['bq'].reshape(db, ds * B_HEADS, 1, B_HD)
    kv_of_row = (jnp.arange(ds * B_HEADS) % B_HEADS) // B_GROUP
    q_bd = jnp.where((kv_of_row[None, :, None, None] == jnp.arange(B_KV)[None, None, :, None]), q5,
                     jnp.zeros((), BF16)).reshape(db, ds * B_HEADS, B_KV * B_HD)
    attn = _dsa_sample_attend(page_table, q_bd, bias, pad_rows(a['kf']), pad_rows(a['vf']),
                              a['zb'].reshape(db, ds * B_HEADS, B_HD),
                              cache_k.reshape(-1, PAGE * B_KV, B_HD), cache_v.reshape(-1, PAGE * B_KV, B_HD),
                              ds=ds, pages_per_step=pps)
    yb = attn.reshape(ms, B_WIDTH)
    y = _outproj(ya, yb, a['sga'], a['sgb'], x.reshape(ms, d), pe.reshape(ms, -1), w['wa'], w['wb'], w['wo'],
                 w['wg'], w['wp'], w['norm_post'], tm=ms)
    qk_pre = jnp.concatenate([a['q'], a['k']], axis=1).reshape(db, ds, 2 * A_QK)
    return dict(
        y=y.reshape(db, ds, d),
        k=a['kf'].reshape(1, db, ds, B_KV, B_HD),
        v=a['vf'].reshape(1, db, ds, B_KV, B_HD),
        ik=a['misc'][:, MISC_IK:].reshape(1, db, ds, IDX_DIM),
        C=c_new[None],
        n=jnp.swapaxes(n_new, 0, 1)[None],
        m=jnp.swapaxes(m_new[:, :, 0], 0, 1)[None],
        conv=qk_pre[:, ds - (CONV_W - 1):, :][None],
    )


def _prep_weights(w_in, conv_w, conv_b, if_bias, norm_a, w_a_proj, w_b_proj, w_out, norm_pre, norm_post, w_ple,
                  w_ple_gate):
    d = w_in.shape[1]
    misc_bias = jnp.zeros((1, LANES), F32).at[0, :2 * A_HEADS].set(if_bias[0])
    w = dict(norm_pre=norm_pre[0][None], norm_post=norm_post[0][None], norm_a=norm_a[0][None],
             conv_w=conv_w[0], conv_b=conv_b[0][None],
             wa=w_a_proj[0].astype(BF16), wb=w_b_proj[0].astype(BF16), wo=w_out[0].astype(BF16),
             wg=w_ple_gate[0].astype(BF16), wp=w_ple[0].astype(BF16))
    return w, _pack_w_in(w_in[0], d), misc_bias


def kernel(x_prompt, x_sample, cache_k, cache_v, cache_idx_k, page_table, state_C, state_n, state_m, state_conv,
           p_prompt, p_sample, w_in, conv_w, conv_b, if_bias, norm_a, w_a_proj, w_b_proj, w_out, norm_pre,
           norm_post, w_ple, w_ple_gate):
    w, w_packed, misc_bias = _prep_weights(w_in, conv_w, conv_b, if_bias, norm_a, w_a_proj, w_b_proj, w_out,
                                           norm_pre, norm_post, w_ple, w_ple_gate)
    p = _prompt_path(x_prompt, p_prompt[0], w, w_packed, misc_bias)
    s = _sample_path(x_sample, p_sample[0], cache_k[0], cache_v[0], cache_idx_k[0], page_table, state_C[0],
                     state_n[0], state_m[0], state_conv[0], w, w_packed, misc_bias)
    names = ('k', 'v', 'ik', 'C', 'n', 'm', 'conv')
    return (p['y'], s['y']) + tuple(p[n] for n in names) + tuple(s[n] for n in names)
```

```python
import functools

import jax
import jax.numpy as jnp
from jax import lax
from jax.experimental import pallas as pl
from jax.experimental.pallas import tpu as pltpu

F32 = jnp.float32
BF16 = jnp.bfloat16
I32 = jnp.int32

EPS = 1e-6
A_HEADS = 4
A_DQK = 128
A_DV = 256
A_QK = A_HEADS * A_DQK
A_WIDTH = A_HEADS * A_DV
CONV_W = 4
B_HEADS = 8
B_KV = 2
B_HD = 128
B_GROUP = B_HEADS // B_KV
B_WIDTH = B_HEADS * B_HD
IDX_HEADS = 8
IDX_DIM = 64
TOPK_MAX = 256
PAGE = 128
LANES = 128
VMEM_LIMIT = 56 * 1024 * 1024

Q_SCALE = A_DQK ** -0.5
ATT_SCALE = B_HD ** -0.5
IDX_SCALE = IDX_DIM ** -0.5
IW_SCALE = IDX_HEADS ** -0.5
C_EXP = ATT_SCALE * 1.4426950408889634
VT_ROWS = B_HD + 16
NEG_BIG = -1e30
INT_MIN = -(2 ** 31)
KEY_NEG_INF = INT_MIN + 0x007FFFFF

_SPLITS = (('a_q', A_QK), ('a_k', A_QK), ('a_v', A_WIDTH), ('a_i', A_HEADS), ('a_f', A_HEADS),
           ('a_o', A_WIDTH), ('a_z', A_WIDTH), ('b_q', B_WIDTH), ('b_k', B_KV * B_HD), ('b_v', B_KV * B_HD),
           ('b_iq', IDX_HEADS * IDX_DIM), ('b_ik', IDX_DIM), ('b_iw', IDX_HEADS), ('b_z', B_WIDTH),
           ('g_a', 1024), ('g_b', 1024))

P_QK = 0
P_AV = 1024
P_AO = 2048
P_AZ = 3072
P_BQ = 4096
P_BK = 5120
P_BV = 5376
P_IQ = 5632
P_MISC = 6144
P_BZ = 6272
P_GA = 7296
P_GB = 8320
P_COLS = 9344
MISC_IW = 8
MISC_IK = 64


def _sigmoid(x):
    return 1.0 / (1.0 + jnp.exp(-x))


def _silu(x):
    return x * _sigmoid(x)


def _dot(a, b):
    return jnp.dot(a, b, preferred_element_type=F32)


def _dot_nt(a, b):
    return lax.dot_general(a, b, (((1,), (1,)), ((), ())), preferred_element_type=F32)


def _transpose(x):
    r, c = x.shape
    rows = []
    for j in range(c // LANES):
        rows.append(jnp.concatenate([x[i * LANES:(i + 1) * LANES, j * LANES:(j + 1) * LANES].T
                                     for i in range(r // LANES)], axis=1))
    return jnp.concatenate(rows, axis=0)


def _pack_w_in(w_in, d_model):
    offs = {}
    c = 0
    for name, n in _SPLITS:
        offs[name] = (c, c + n)
        c += n

    def col(name):
        lo, hi = offs[name]
        return w_in[:, lo:hi]

    z = lambda n: jnp.zeros((d_model, n), w_in.dtype)
    misc = jnp.concatenate([col('a_i'), col('a_f'), col('b_iw'), z(MISC_IK - 16), col('b_ik')], axis=1)
    packed = jnp.concatenate([col('a_q'), col('a_k'), col('a_v'), col('a_o'), col('a_z'), col('b_q'), col('b_k'),
                              col('b_v'), col('b_iq'), misc, col('b_z'), col('g_a'), col('g_b')], axis=1)
    return packed.astype(BF16)


def _inproj_kernel(x_ref, g_ref, w_ref, cw_ref, cb_ref, mb_ref, *refs, names, prompt, tiles_per_seq):
    o = dict(zip(names, refs))
    hist_ref = refs[len(names)]
    tm = x_ref.shape[0]
    x = x_ref[...]
    ms = jnp.mean(x * x, axis=-1, keepdims=True)
    xn = (x * lax.rsqrt(ms + EPS) * g_ref[...]).astype(BF16)

    def proj(lo, hi):
        return _dot(xn, w_ref[:, lo:hi])

    qk = proj(P_QK, P_QK + 2 * A_QK)
    if prompt:
        o['tail'][0] = qk[tm - 8:tm, :]

        @pl.when(pl.program_id(0) % tiles_per_seq == 0)
        def _():
            hist_ref[0:8, :] = jnp.zeros((8, 2 * A_QK), F32)

        hist_ref[8:8 + tm, :] = qk
        cw = cw_ref[...]
        y = cb_ref[...] + cw[0:1, :] * hist_ref[5:5 + tm, :]
        y = y + cw[1:2, :] * hist_ref[6:6 + tm, :]
        y = y + cw[2:3, :] * hist_ref[7:7 + tm, :]
        y = y + cw[3:4, :] * qk
        act = _silu(y)
        o['q'][...] = (act[:, :A_QK] * Q_SCALE).astype(BF16)
        o['k'][...] = act[:, A_QK:].astype(BF16)
        hist_ref[0:8, :] = qk[tm - 8:tm, :]
    else:
        o['q'][...] = qk[:, :A_QK]
        o['k'][...] = qk[:, A_QK:]

    o['av'][...] = proj(P_AV, P_AV + A_WIDTH).astype(BF16)
    o['ga'][...] = (_sigmoid(proj(P_AO, P_AO + A_WIDTH)) * _silu(proj(P_AZ, P_AZ + A_WIDTH))).astype(BF16)
    bq = proj(P_BQ, P_BQ + B_WIDTH) * C_EXP
    iq = proj(P_IQ, P_IQ + IDX_HEADS * IDX_DIM)
    kf = proj(P_BK, P_BK + B_KV * B_HD)
    vf = proj(P_BV, P_BV + B_KV * B_HD)
    o['kf'][...] = kf
    o['vf'][...] = vf
    if prompt:
        for r in range(tm // PAGE):
            rows = slice(r * PAGE, (r + 1) * PAGE)
            o['bq'][r] = _transpose(bq[rows, :]).astype(BF16)
            o['iq'][r] = _transpose(iq[rows, :]).astype(BF16)
        ones = jnp.ones((VT_ROWS - B_HD, 2 * PAGE), BF16)
        for p in range(tm // (2 * PAGE)):
            rows = slice(p * 2 * PAGE, (p + 1) * 2 * PAGE)
            parts = []
            for g in range(B_KV):
                parts += [_transpose(vf[rows, g * B_HD:(g + 1) * B_HD]).astype(BF16), ones]
            o['vt'][p] = jnp.concatenate(parts, axis=0)
        o['kb'][...] = kf.astype(BF16)
    else:
        o['bq'][...] = bq.astype(BF16)
        o['iq'][...] = iq.astype(BF16)

    raw = proj(P_MISC, P_MISC + LANES) + mb_ref[...]
    lane = lax.broadcasted_iota(I32, raw.shape, 1)
    logsig = jnp.minimum(raw, 0.0) - jnp.log(1.0 + jnp.exp(-jnp.abs(raw)))
    o['misc'][...] = jnp.where((lane >= A_HEADS) & (lane < 2 * A_HEADS), logsig,
                               jnp.where((lane >= MISC_IW) & (lane < MISC_IW + IDX_HEADS), raw * IW_SCALE, raw))
    if prompt:
        o['miscb'][...] = jnp.where(lane >= MISC_IK, raw, 0.0).astype(BF16)

    o['zb'][...] = _silu(proj(P_BZ, P_BZ + B_WIDTH)).astype(BF16)
    o['sga'][...] = _sigmoid(proj(P_GA, P_GA + 1024)).astype(BF16)
    o['sgb'][...] = _sigmoid(proj(P_GB, P_GB + 1024)).astype(BF16)


def _inproj(x2d, norm_pre, w_packed, conv_w, conv_b, misc_bias, *, prompt, seq_len, tm):
    m, d = x2d.shape
    nt = m // tm
    rows = lambda w, dt: (jax.ShapeDtypeStruct((m, w), dt), pl.BlockSpec((tm, w), lambda i: (i, 0)))
    tiles = lambda n, r, c, dt: (jax.ShapeDtypeStruct((m // n, r, c), dt),
                                 pl.BlockSpec((tm // n, r, c), lambda i: (i, 0, 0)))
    outs = dict(q=rows(A_QK, BF16 if prompt else F32), k=rows(A_QK, BF16 if prompt else F32), av=rows(A_WIDTH, BF16),
                ga=rows(A_WIDTH, BF16), kf=rows(B_KV * B_HD, F32), vf=rows(B_KV * B_HD, F32), misc=rows(LANES, F32),
                zb=rows(B_WIDTH, BF16), sga=rows(d, BF16), sgb=rows(d, BF16))
    if prompt:
        outs.update(tail=tiles(tm, 8, 2 * A_QK, F32), bq=tiles(PAGE, B_WIDTH, PAGE, BF16),
                    iq=tiles(PAGE, IDX_HEADS * IDX_DIM, PAGE, BF16),
                    vt=tiles(2 * PAGE, B_KV * VT_ROWS, 2 * PAGE, BF16), kb=rows(B_KV * B_HD, BF16),
                    miscb=rows(LANES, BF16))
    else:
        outs.update(bq=rows(B_WIDTH, BF16), iq=rows(IDX_HEADS * IDX_DIM, BF16))
    names = tuple(outs)
    const = lambda shape: pl.BlockSpec(shape, lambda i: (0,) * len(shape))
    res = pl.pallas_call(
        functools.partial(_inproj_kernel, names=names, prompt=prompt, tiles_per_seq=max(seq_len // tm, 1)),
        grid=(nt,),
        in_specs=[pl.BlockSpec((tm, d), lambda i: (i, 0)), const((1, d)),
                  pl.BlockSpec((d, P_COLS), lambda i: (0, 0), pipeline_mode=pl.Buffered(1)),
                  const((CONV_W, 2 * A_QK)), const((1, 2 * A_QK)), const((1, LANES))],
        out_specs=[outs[n][1] for n in names],
        out_shape=[outs[n][0] for n in names],
        scratch_shapes=[pltpu.VMEM((8 + tm, 2 * A_QK), F32)],
        compiler_params=pltpu.CompilerParams(dimension_semantics=("arbitrary",), vmem_limit_bytes=VMEM_LIMIT),
        name="inproj_prompt" if prompt else "inproj_decode",
    )(x2d, norm_pre, w_packed, conv_w, conv_b, misc_bias)
    return dict(zip(names, res))


def _outproj_kernel(ya_ref, yb_ref, sga_ref, sgb_ref, x_ref, pe_ref, wa_ref, wb_ref, wo_ref, wg_ref, wp_ref,
                    g_ref, o_ref):
    merged = sga_ref[...] * _dot(ya_ref[...], wa_ref[...]) + sgb_ref[...] * _dot(yb_ref[...], wb_ref[...])
    z = _dot(merged.astype(BF16), wo_ref[...])
    ms = jnp.mean(z * z, axis=-1, keepdims=True)
    x1 = x_ref[...] + z * lax.rsqrt(ms + EPS) * g_ref[...]
    gate = _sigmoid(_dot(x1.astype(BF16), wg_ref[...]))
    o_ref[...] = x1 + _dot(pe_ref[...].astype(BF16), wp_ref[...]) * gate


def _outproj(ya, yb, sga, sgb, x2d, pe2d, wa, wb, wo, wg, wp, norm_post, *, tm):
    m, d = x2d.shape
    pd = pe2d.shape[1]
    row = lambda w: pl.BlockSpec((tm, w), lambda i: (i, 0))
    const = lambda shape: pl.BlockSpec(shape, lambda i: (0,) * len(shape))
    return pl.pallas_call(
        _outproj_kernel,
        grid=(m // tm,),
        in_specs=[row(A_WIDTH), row(B_WIDTH), row(d), row(d), row(d), row(pd),
                  const((A_WIDTH, d)), const((B_WIDTH, d)), const((d, d)), const((d, d)), const((pd, d)),
                  const((1, d))],
        out_specs=row(d),
        out_shape=jax.ShapeDtypeStruct((m, d), F32),
        compiler_params=pltpu.CompilerParams(dimension_semantics=("arbitrary",), vmem_limit_bytes=VMEM_LIMIT),
        name="outproj",
    )(ya, yb, sga, sgb, x2d, pe2d, wa, wb, wo, wg, wp, norm_post)


def _cumsum_rows(x):
    n = x.shape[0]
    row = lax.broadcasted_iota(I32, x.shape, 0)
    s = 1
    while s < n:
        x = x + jnp.where(row >= s, pltpu.roll(x, s, axis=0), 0.0)
        s *= 2
    return x


def _lane_col(x, lane_iota, idx):
    return jnp.sum(jnp.where(lane_iota == idx, x, 0.0), axis=1, keepdims=True)


def _mlstm_prompt_kernel(q_ref, k_ref, v_ref, misc_ref, ga_ref, na_ref, ya_ref, c_out_ref, nm_out_ref,
                         c_ref, n_ref, m_ref):
    c_idx = pl.program_id(1)
    nb, L = q_ref.shape[0], q_ref.shape[1]

    @pl.when(c_idx == 0)
    def _():
        c_ref[...] = jnp.zeros(c_ref.shape, F32)
        n_ref[...] = jnp.zeros(n_ref.shape, F32)
        m_ref[...] = jnp.zeros(m_ref.shape, F32)

    lane = lax.broadcasted_iota(I32, (L, LANES), 1)
    t_i = lax.broadcasted_iota(I32, (L, L), 0)
    s_i = lax.broadcasted_iota(I32, (L, L), 1)
    causal = s_i <= t_i

    for sq in range(nb):
        misc = misc_ref[sq]
        cs = _cumsum_rows(misc)
        b_al = pltpu.roll(cs, LANES - A_HEADS, axis=1)
        xa = jnp.where(lane < A_HEADS, misc - b_al, cs)
        xt = _transpose(xa)

        for h in range(A_HEADS):
            a_col = _lane_col(xa, lane, h)
            b_col = _lane_col(xa, lane, A_HEADS + h)
            a_row = xt[h:h + 1, :]
            m_prev = m_ref[sq, h:h + 1, 0:1]
            logit = jnp.where(causal, a_row, -jnp.inf)
            g_col = jnp.maximum(m_prev, jnp.max(logit, axis=1, keepdims=True))
            dmat = jnp.exp(logit - g_col)
            w_inter = jnp.exp(m_prev - g_col)
            qh = q_ref[sq, :, h * A_DQK:(h + 1) * A_DQK]
            kh = k_ref[sq, :, h * A_DQK:(h + 1) * A_DQK]
            vh = v_ref[sq, :, h * A_DV:(h + 1) * A_DV]
            c_h = c_ref[sq, h]
            n_h = n_ref[sq, h:h + 1, :]
            s = _dot_nt(qh, kh) * dmat
            num = _dot(s.astype(BF16), vh) + w_inter * _dot_nt(qh, c_h.astype(BF16))
            den = (jnp.sum(s, axis=1, keepdims=True)
                   + w_inter * jnp.sum(qh.astype(F32) * n_h, axis=1, keepdims=True))
            m_t = b_col + g_col
            hh = num / jnp.maximum(jnp.abs(den), jnp.exp(-m_t))
            hn = hh * lax.rsqrt(jnp.mean(hh * hh, axis=1, keepdims=True) + EPS)
            sl = slice(h * A_DV, (h + 1) * A_DV)
            ya_ref[sq, :, sl] = (ga_ref[sq, :, sl] * (hn * na_ref[:, sl])).astype(ya_ref.dtype)

            g_last = g_col[L - 1:L, :]
            w_prev = jnp.exp(m_prev - g_last)
            w_s = jnp.exp(a_col - g_last)
            vw = vh.astype(F32) * w_s
            c_ref[sq, h] = w_prev * c_h + _dot(_transpose(vw).astype(BF16), kh)
            n_ref[sq, h:h + 1, :] = w_prev * n_h + jnp.sum(kh.astype(F32) * w_s, axis=0, keepdims=True)
            m_ref[sq, h:h + 1, :] = jnp.broadcast_to(b_col[L - 1:L, :] + g_last, (1, LANES))

    @pl.when(c_idx == pl.num_programs(1) - 1)
    def _():
        c_out_ref[...] = c_ref[...]
        nm_out_ref[:, 0:A_HEADS, :] = n_ref[:, 0:A_HEADS, :]
        nm_out_ref[:, A_HEADS:2 * A_HEADS, :] = m_ref[:, 0:A_HEADS, :]


def _mlstm_prompt(q, k, v, misc, ga, norm_a, *, batch, seq_len, chunk):
    nc = seq_len // chunk
    nb = 1
    seq3 = lambda a: a.reshape(batch, seq_len, a.shape[-1])
    blk = lambda w: pl.BlockSpec((nb, chunk, w), lambda b, c: (b, c, 0))
    ya, c_out, nm_out = pl.pallas_call(
        _mlstm_prompt_kernel,
        grid=(batch // nb, nc),
        in_specs=[blk(A_QK), blk(A_QK), blk(A_WIDTH), blk(LANES), blk(A_WIDTH),
                  pl.BlockSpec((1, A_WIDTH), lambda b, c: (0, 0))],
        out_specs=[blk(A_WIDTH),
                   pl.BlockSpec((nb, A_HEADS, A_DV, A_DQK), lambda b, c: (b, 0, 0, 0)),
                   pl.BlockSpec((nb, 8, LANES), lambda b, c: (b, 0, 0))],
        out_shape=[jax.ShapeDtypeStruct((batch, seq_len, A_WIDTH), BF16),
                   jax.ShapeDtypeStruct((batch, A_HEADS, A_DV, A_DQK), F32),
                   jax.ShapeDtypeStruct((batch, 8, LANES), F32)],
        scratch_shapes=[pltpu.VMEM((nb, A_HEADS, A_DV, A_DQK), F32), pltpu.VMEM((nb, 8, LANES), F32),
                        pltpu.VMEM((nb, 8, LANES), F32)],
        compiler_params=pltpu.CompilerParams(dimension_semantics=("arbitrary", "arbitrary"),
                                             vmem_limit_bytes=VMEM_LIMIT),
        name="mlstm_prompt",
    )(seq3(q), seq3(k), seq3(v), seq3(misc), seq3(ga), norm_a)
    return ya.reshape(batch * seq_len, A_WIDTH), c_out, nm_out


def _key_to_f32(key):
    return pltpu.bitcast(jnp.where(key < 0, key ^ 0x7FFFFFFF, key), F32)


def _radix_threshold(count_ge, shape, topk):
    def bit_body(i, u):
        cand_u = u | jnp.left_shift(jnp.int32(1), 31 - i)
        cnt = count_ge(_key_to_f32(cand_u ^ INT_MIN))
        return jnp.where(cnt >= topk, cand_u, u)

    u = lax.fori_loop(0, 32, bit_body, jnp.zeros(shape, I32))
    key = jnp.maximum(u ^ INT_MIN, KEY_NEG_INF)
    return _key_to_f32(key), _key_to_f32(key + 1)


def _refine_in_bin(count_ge, thr, thr_up, topk):
    lo, hi = thr, thr_up
    for _ in range(6):
        mid = lo + 0.5 * (hi - lo)
        ok = count_ge(mid) >= topk
        lo = jnp.where(ok, mid, lo)
        hi = jnp.where(ok, hi, mid)
    return lo


def _dsa_prompt_kernel(bq_ref, iq_ref, misc_ref, zb_ref, ikb_ref, kb_ref, vt_ref, yb_ref,
                       sc_ref, s_ref, s2_ref, p_ref, p2_ref, al_ref, al2_ref, acc_ref, m_ref, *, topk, nq):
    j = pl.program_id(1)
    T = PAGE
    KT = 2 * PAGE
    nkt = j // 2 + 1

    iq_t = iq_ref[0]
    zeros_pad = jnp.zeros((LANES - IDX_DIM, T), BF16)
    iq_pad = [jnp.concatenate([zeros_pad, iq_t[h * IDX_DIM:(h + 1) * IDX_DIM, :]], axis=0)
              for h in range(IDX_HEADS)]
    misc_t = _transpose(misc_ref[...])
    q_t = bq_ref[0]
    q_grp = [jnp.concatenate([q_t[(g * B_GROUP + hh) * B_HD:(g * B_GROUP + hh + 1) * B_HD, :]
                              for hh in range(B_GROUP)], axis=1) for g in range(B_KV)]

    row_i = lax.broadcasted_iota(I32, (KT, T), 0)
    lane_i = lax.broadcasted_iota(I32, (KT, T), 1)

    last_pair = nq // 2 - 1

    def idx_body(c, carry):
        ikc = ikb_ref[0, c]
        acc = jnp.zeros((KT, T), F32)
        for h in range(IDX_HEADS):
            r = jnp.maximum(_dot(ikc, iq_pad[h]) * IDX_SCALE, 0.0)
            acc = acc + r * misc_t[MISC_IW + h:MISC_IW + h + 1, :]
        vis = (c * KT + row_i) <= (j * T + lane_i)
        sc_ref[c] = jnp.where(vis, acc, -jnp.inf)
        return carry

    lax.fori_loop(0, nkt, idx_body, 0)

    def count(pred):
        def body(c, cnt):
            m = pred(sc_ref[c], c).reshape(KT // 32, 4, 8, T)
            for v in range(KT // 32):
                cnt = jnp.where(m[v], cnt + 1, cnt)
            return cnt
        cnt4 = lax.fori_loop(0, nkt, body, jnp.zeros((4, 8, T), I32))
        return jnp.sum(jnp.sum(cnt4, axis=0), axis=0, keepdims=True)

    count_ge = lambda t: count(lambda sc, c: sc >= t)
    thr0, thr_up = _radix_threshold(count_ge, (1, T), topk)
    n_ge = count_ge(thr0)

    def tie_search(_):
        thr_t = _refine_in_bin(count_ge, thr0, thr_up, topk)
        need = topk - count(lambda sc, c: sc > thr_t)
        nbits = (nq * T - 1).bit_length()
        def pbody(i, p):
            cand = p | jnp.left_shift(jnp.int32(1), nbits - 1 - i)
            cnt = count(lambda sc, c: (sc == thr_t) & ((c * KT + row_i) < cand))
            return jnp.where(cnt < need, cand, p)
        return thr_t, lax.fori_loop(0, nbits, pbody, jnp.zeros((1, T), I32))

    has_tie = jnp.max(jnp.where((n_ge > topk) & (thr0 > -jnp.inf), 1, 0)) > 0
    thr, p_lim = lax.cond(has_tie, tie_search, lambda _: (thr0, jnp.full((1, T), 2 ** 30, I32)), 0)

    acc_ref[...] = jnp.zeros(acc_ref.shape, F32)
    m_ref[...] = jnp.full(m_ref.shape, NEG_BIG, F32)

    def qk_scores(c, dst_ref):
        kc = kb_ref[0, jnp.minimum(c, last_pair)]
        for g in range(B_KV):
            dst_ref[g] = _dot(kc[:, g * B_HD:(g + 1) * B_HD], q_grp[g])

    def softmax(c, src_ref, p_dst, al_dst):
        sc = sc_ref[c]
        pos = c * KT + row_i
        sel = ((sc > thr) | ((sc == thr) & (pos <= p_lim))) & (sc > -jnp.inf)
        bias = jnp.where(sel, 0.0, NEG_BIG)
        for h in range(B_HEADS):
            g, hh = divmod(h, B_GROUP)
            s = src_ref[g, :, hh * T:(hh + 1) * T] + bias
            m_old = m_ref[h:h + 1, :]
            m_new = jnp.maximum(m_old, jnp.max(s, axis=0, keepdims=True))
            al_dst[h:h + 1, :] = jnp.exp2(m_old - m_new)
            p_dst[h // 2, :, (h % 2) * T:(h % 2 + 1) * T] = jnp.exp2(s - m_new).astype(BF16)
            m_ref[h:h + 1, :] = m_new

    def pv_update(c, p_src, al_src):
        vtc = vt_ref[0, c]
        for h0 in range(0, B_HEADS, 2):
            g = h0 // B_GROUP
            pv = _dot(vtc[g * VT_ROWS:(g + 1) * VT_ROWS, :], p_src[h0 // 2])
            for i, h in enumerate((h0, h0 + 1)):
                acc_ref[h] = acc_ref[h] * al_src[h:h + 1, :] + pv[:, i * T:(i + 1) * T]

    p2_ref[...] = jnp.zeros(p2_ref.shape, BF16)
    al2_ref[...] = jnp.ones(al2_ref.shape, F32)
    qk_scores(0, s_ref)

    def att_body(i, carry):
        c0 = 2 * i
        qk_scores(c0 + 1, s2_ref)
        softmax(c0, s_ref, p_ref, al_ref)
        pv_update(jnp.maximum(c0 - 1, 0), p2_ref, al2_ref)

        @pl.when(c0 + 1 < nkt)
        def _():
            qk_scores(c0 + 2, s_ref)
            softmax(c0 + 1, s2_ref, p2_ref, al2_ref)
            pv_update(c0, p_ref, al_ref)

        return carry

    lax.fori_loop(0, (nkt + 1) // 2, att_body, 0)

    @pl.when(nkt % 2 == 1)
    def _():
        pv_update(nkt - 1, p_ref, al_ref)

    @pl.when(nkt % 2 == 0)
    def _():
        pv_update(nkt - 1, p2_ref, al2_ref)

    for h in range(B_HEADS):
        o_t = acc_ref[h, 0:B_HD, :] / acc_ref[h, B_HD:B_HD + 1, :]
        sl = slice(h * B_HD, (h + 1) * B_HD)
        yb_ref[:, sl] = (_transpose(o_t) * zb_ref[:, sl]).astype(yb_ref.dtype)


def _dsa_prompt(bq, iq, misc, zb, ikb, kb, vt, *, batch, seq_len, topk):
    nq = seq_len // PAGE
    kt = 2 * PAGE
    scores = pltpu.VMEM((B_KV, kt, B_GROUP * PAGE), F32)
    weights = pltpu.VMEM((B_HEADS // 2, kt, 2 * PAGE), BF16)
    rescale = pltpu.VMEM((B_HEADS, PAGE), F32)
    row = lambda w: pl.BlockSpec((PAGE, w), lambda b, j: (b * nq + j, 0))
    tile_t = lambda r: pl.BlockSpec((1, r, PAGE), lambda b, j: (b * nq + j, 0, 0))
    per_b = lambda a: pl.BlockSpec((1,) + a.shape[1:], lambda b, j: (b, 0, 0, 0))
    return pl.pallas_call(
        functools.partial(_dsa_prompt_kernel, topk=topk, nq=nq),
        grid=(batch, nq),
        in_specs=[tile_t(B_WIDTH), tile_t(IDX_HEADS * IDX_DIM), row(LANES), row(B_WIDTH),
                  per_b(ikb), per_b(kb), per_b(vt)],
        out_specs=row(B_WIDTH),
        out_shape=jax.ShapeDtypeStruct((batch * seq_len, B_WIDTH), BF16),
        scratch_shapes=[pltpu.VMEM((nq // 2, kt, PAGE), F32), scores, scores, weights, weights, rescale, rescale,
                        pltpu.VMEM((B_HEADS, VT_ROWS, PAGE), F32),
                        pltpu.VMEM((B_HEADS, PAGE), F32)],
        compiler_params=pltpu.CompilerParams(dimension_semantics=("arbitrary", "arbitrary"),
                                             vmem_limit_bytes=VMEM_LIMIT),
        name="dsa_prompt",
    )(bq, iq, misc, zb, ikb, kb, vt)


def _prompt_path(x, pe, w, w_packed, misc_bias):
    batch, seq_len, d = x.shape
    m = batch * seq_len
    x2d = x.reshape(m, d)
    tm = min(256, seq_len)
    a = _inproj(x2d, w['norm_pre'], w_packed, w['conv_w'], w['conv_b'], misc_bias,
                prompt=True, seq_len=seq_len, tm=tm)
    chunk = min(256, seq_len)
    ya, c_out, nm_out = _mlstm_prompt(a['q'], a['k'], a['av'], a['misc'], a['ga'], w['norm_a'],
                                      batch=batch, seq_len=seq_len, chunk=chunk)
    nk2 = seq_len // (2 * PAGE)
    ikb = a['miscb'].reshape(batch, nk2, 2 * PAGE, LANES)
    kb = a['kb'].reshape(batch, nk2, 2 * PAGE, B_KV * B_HD)
    vt = a['vt'].reshape(batch, nk2, B_KV * VT_ROWS, 2 * PAGE)
    topk = min(TOPK_MAX, seq_len // 4)
    yb = _dsa_prompt(a['bq'], a['iq'], a['misc'], a['zb'], ikb, kb, vt, batch=batch, seq_len=seq_len, topk=topk)
    y = _outproj(ya, yb, a['sga'], a['sgb'], x2d, pe.reshape(m, -1), w['wa'], w['wb'], w['wo'], w['wg'], w['wp'],
                 w['norm_post'], tm=min(256, m))
    tiles_per_seq = seq_len // tm
    tail = a['tail'].reshape(batch, tiles_per_seq, 8, 2 * A_QK)[:, -1, 8 - (CONV_W - 1):, :]
    return dict(
        y=y.reshape(batch, seq_len, d),
        k=a['kf'].reshape(1, batch, seq_len, B_KV, B_HD),
        v=a['vf'].reshape(1, batch, seq_len, B_KV, B_HD),
        ik=a['misc'][:, MISC_IK:].reshape(1, batch, seq_len, IDX_DIM),
        C=c_out[None],
        n=nm_out[None, :, 0:A_HEADS, :],
        m=nm_out[None, :, A_HEADS:2 * A_HEADS, 0],
        conv=tail[None],
    )


def _mlstm_sample_kernel(q_ref, k_ref, conv0_ref, cw_ref, cb_ref, v_ref, misc_ref, ga_ref, na_ref, n0_ref, m0_ref,
                         c0_ref, ya_ref, c_out_ref, n_out_ref, m_out_ref,
                         qs_ref, ks_ref, cq_ref, wprev_ref, vwt_ref, numi_ref, den_ref, wint_ref, enm_ref, *, db, ds):
    b = pl.program_id(0)
    ms = ds * db
    rows = lambda t: slice(t * db, (t + 1) * db)

    @pl.when(b == 0)
    def _():
        cw = cw_ref[...]
        u = [conv0_ref[j] for j in range(CONV_W - 1)]
        u += [jnp.concatenate([q_ref[rows(t), :], k_ref[rows(t), :]], axis=1) for t in range(ds)]
        for t in range(ds):
            y = cb_ref[...] + cw[0:1, :] * u[t]
            for jj in range(1, CONV_W):
                y = y + cw[jj:jj + 1, :] * u[t + jj]
            act = _silu(y)
            qs_ref[rows(t), :] = act[:, :A_QK] * Q_SCALE
            ks_ref[rows(t), :] = act[:, A_QK:]
        cq_ref[...] = jnp.zeros(cq_ref.shape, F32)

        lane = lax.broadcasted_iota(I32, (db, LANES), 1)
        for h in range(A_HEADS):
            i_t = [_lane_col(misc_ref[rows(t), :], lane, h) for t in range(ds)]
            lf_t = [_lane_col(misc_ref[rows(t), :], lane, A_HEADS + h) for t in range(ds)]
            b_t = [lf_t[0]]
            for t in range(1, ds):
                b_t.append(b_t[-1] + lf_t[t])
            a_t = [i_t[t] - b_t[t] for t in range(ds)]
            m_prev = jnp.max(m0_ref[h], axis=1, keepdims=True)
            g_t = [jnp.maximum(m_prev, a_t[0])]
            for t in range(1, ds):
                g_t.append(jnp.maximum(g_t[-1], a_t[t]))
            hq = slice(h * A_DQK, (h + 1) * A_DQK)
            hv = slice(h * A_DV, (h + 1) * A_DV)
            qh = [qs_ref[rows(t), hq] for t in range(ds)]
            kh = [ks_ref[rows(t), hq] for t in range(ds)]
            vh = [v_ref[rows(t), hv].astype(F32) for t in range(ds)]
            n0 = n0_ref[h]
            for t in range(ds):
                num = jnp.zeros((db, A_DV), F32)
                den = jnp.zeros((db, 1), F32)
                for s in range(t + 1):
                    w_ts = jnp.sum(qh[t] * kh[s], axis=1, keepdims=True) * jnp.exp(a_t[s] - g_t[t])
                    num = num + w_ts * vh[s]
                    den = den + w_ts
                w_inter = jnp.exp(m_prev - g_t[t])
                den = den + w_inter * jnp.sum(qh[t] * n0, axis=1, keepdims=True)
                numi_ref[rows(t), hv] = num
                den_ref[h, rows(t), :] = jnp.broadcast_to(den, (db, LANES))
                wint_ref[h, rows(t), :] = jnp.broadcast_to(w_inter, (db, LANES))
                enm_ref[h, rows(t), :] = jnp.broadcast_to(jnp.exp(-(b_t[t] + g_t[t])), (db, LANES))
            g_last = g_t[ds - 1]
            w_prev = jnp.exp(m_prev - g_last)
            wprev_ref[h] = jnp.broadcast_to(w_prev, (db, LANES))
            n_new = w_prev * n0
            vw = []
            for s in range(ds):
                w_s = jnp.exp(a_t[s] - g_last)
                n_new = n_new + w_s * kh[s]
                vw.append(vh[s] * w_s)
            n_out_ref[h] = n_new
            m_out_ref[h] = jnp.broadcast_to(b_t[ds - 1] + g_last, (db, LANES))
            vw_all = jnp.concatenate(vw, axis=0)
            if ms < LANES:
                vw_all = jnp.concatenate([vw_all, jnp.zeros((LANES - ms, A_DV), F32)], axis=0)
            vwt_ref[h] = _transpose(vw_all)

    mcols = vwt_ref.shape[2]
    row_i = lax.broadcasted_iota(I32, (ms, A_DQK), 0)
    col_i = lax.broadcasted_iota(I32, (A_DV, mcols), 1)
    row_mine = row_i == b
    col_mine = col_i == b
    for t in range(1, ds):
        row_mine = row_mine | (row_i == b + t * db)
        col_mine = col_mine | (col_i == b + t * db)
    for h in range(A_HEADS):
        hq = slice(h * A_DQK, (h + 1) * A_DQK)
        c0 = c0_ref[0, h]
        q_mine = jnp.where(row_mine, qs_ref[:, hq], 0.0).astype(BF16)
        cq_ref[h] = cq_ref[h] + _dot_nt(q_mine, c0.astype(BF16))
        vw_mine = jnp.where(col_mine, vwt_ref[h], 0.0).astype(BF16)
        k_all = ks_ref[:, hq]
        if mcols > ms:
            k_all = jnp.concatenate([k_all, jnp.zeros((mcols - ms, A_DQK), F32)], axis=0)
        c_out_ref[0, h] = wprev_ref[h, pl.ds(b, 1), :] * c0 + _dot(vw_mine, k_all.astype(BF16))

    @pl.when(b == pl.num_programs(0) - 1)
    def _():
        for h in range(A_HEADS):
            hv = slice(h * A_DV, (h + 1) * A_DV)
            num = numi_ref[:, hv] + wint_ref[h][:, 0:1] * cq_ref[h]
            hh = num / jnp.maximum(jnp.abs(den_ref[h][:, 0:1]), enm_ref[h][:, 0:1])
            hn = hh * lax.rsqrt(jnp.mean(hh * hh, axis=1, keepdims=True) + EPS)
            ya_ref[:, hv] = (ga_ref[:, hv] * (hn * na_ref[:, hv])).astype(ya_ref.dtype)


def _mlstm_sample(q_t, k_t, conv0_t, conv_w, conv_b, v_t, misc_t, ga_t, norm_a, n0_t, m0_bc, c0, *, db, ds):
    ms = db * ds
    mcols = max(ms, LANES)
    full = lambda a: pl.BlockSpec(a.shape, lambda b: (0,) * a.ndim)
    cblk = pl.BlockSpec((1, A_HEADS, A_DV, A_DQK), lambda b: (b, 0, 0, 0))
    hshape = jax.ShapeDtypeStruct((A_HEADS, db, LANES), F32)
    ins = (q_t, k_t, conv0_t, conv_w, conv_b, v_t, misc_t, ga_t, norm_a, n0_t, m0_bc)
    return pl.pallas_call(
        functools.partial(_mlstm_sample_kernel, db=db, ds=ds),
        grid=(db,),
        in_specs=[full(a) for a in ins] + [cblk],
        out_specs=[pl.BlockSpec((ms, A_WIDTH), lambda b: (0, 0)), cblk,
                   pl.BlockSpec((A_HEADS, db, LANES), lambda b: (0, 0, 0)),
                   pl.BlockSpec((A_HEADS, db, LANES), lambda b: (0, 0, 0))],
        out_shape=[jax.ShapeDtypeStruct((ms, A_WIDTH), BF16), jax.ShapeDtypeStruct(c0.shape, F32), hshape, hshape],
        scratch_shapes=[pltpu.VMEM((ms, A_QK), F32), pltpu.VMEM((ms, A_QK), F32),
                        pltpu.VMEM((A_HEADS, ms, A_DV), F32), pltpu.VMEM((A_HEADS, db, LANES), F32),
                        pltpu.VMEM((A_HEADS, A_DV, mcols), F32), pltpu.VMEM((ms, A_WIDTH), F32),
                        pltpu.VMEM((A_HEADS, ms, LANES), F32), pltpu.VMEM((A_HEADS, ms, LANES), F32),
                        pltpu.VMEM((A_HEADS, ms, LANES), F32)],
        compiler_params=pltpu.CompilerParams(dimension_semantics=("arbitrary",), vmem_limit_bytes=VMEM_LIMIT),
        name="mlstm_sample",
    )(*ins, c0)


def _page_ring(pt_ref, pages_per_step, streams):
    b, g = pl.program_id(0), pl.program_id(1)
    nb, ng = pl.num_programs(0), pl.num_programs(1)
    step = b * ng + g
    slot = lax.rem(step, 2)

    def copies(bb, gg, sl):
        return [pltpu.make_async_copy(hbm.at[pt_ref[bb, gg * pages_per_step + i]], buf.at[sl, i], sem.at[sl])
                for hbm, buf, sem in streams for i in range(pages_per_step)]

    def start_all(cps):
        for n, cp in enumerate(cps):
            cp.start(priority=n % 2)

    @pl.when(step == 0)
    def _():
        start_all(copies(0, 0, 0))

    @pl.when(step + 1 < nb * ng)
    def _():
        wrap = g + 1 == ng
        start_all(copies(jnp.where(wrap, b + 1, b), jnp.where(wrap, 0, g + 1), 1 - slot))

    for cp in copies(b, g, slot):
        cp.wait()
    return slot


def _dsa_sample_scores_kernel(pt_ref, iq_ref, iw_ref, iknew_ref, cache_ref, sc_ref, page_buf, sem,
                              *, pages_per_step, n_pages, ds):
    g = pl.program_id(1)
    slot = _page_ring(pt_ref, pages_per_step, [(cache_ref, page_buf, sem)])
    page_refs = [page_buf.at[slot, i] for i in range(pages_per_step)]
    iq = iq_ref[0]
    iw = iw_ref[0]

    def scores(keys_t, visible):
        n = keys_t.shape[1] // PAGE
        r = jnp.maximum(_dot(iq, keys_t) * IDX_SCALE, 0.0) * jnp.concatenate([iw] * n, axis=1)
        sc = jnp.sum(r.reshape(8, IDX_HEADS, n * PAGE), axis=1)
        return jnp.where(visible, sc, -jnp.inf)

    t_all = lax.broadcasted_iota(I32, (8, pages_per_step * PAGE), 0)
    sc_step = scores(jnp.concatenate([r[...] for r in page_refs], axis=1).astype(BF16), t_all < ds)
    for i in range(pages_per_step):
        sc_ref[0, g * pages_per_step + i] = sc_step[:, i * PAGE:(i + 1) * PAGE]

    @pl.when(g == pl.num_programs(1) - 1)
    def _():
        t_i = lax.broadcasted_iota(I32, (8, PAGE), 0)
        s_i = lax.broadcasted_iota(I32, (8, PAGE), 1)
        sc_ref[0, n_pages] = scores(iknew_ref[0], (t_i < ds) & (s_i <= t_i))


def _dsa_sample_scores(page_table, iq_s, iw_bc, ik_new, cache_ik, *, ds, pages_per_step):
    db, n_pages = page_table.shape
    npg = n_pages // pages_per_step
    per_b = lambda a: pl.BlockSpec((1,) + a.shape[1:], lambda b, g, pt: (b,) + (0,) * (a.ndim - 1))
    return pl.pallas_call(
        functools.partial(_dsa_sample_scores_kernel, pages_per_step=pages_per_step, n_pages=n_pages, ds=ds),
        grid_spec=pltpu.PrefetchScalarGridSpec(
            num_scalar_prefetch=1, grid=(db, npg),
            in_specs=[per_b(iq_s), per_b(iw_bc), per_b(ik_new), pl.BlockSpec(memory_space=pl.ANY)],
            out_specs=pl.BlockSpec((1, n_pages + 1, 8, PAGE), lambda b, g, pt: (b, 0, 0, 0)),
            scratch_shapes=[pltpu.VMEM((2, pages_per_step, IDX_DIM, PAGE), F32), pltpu.SemaphoreType.DMA((2,))]),
        out_shape=jax.ShapeDtypeStruct((db, n_pages + 1, 8, PAGE), F32),
        compiler_params=pltpu.CompilerParams(dimension_semantics=("arbitrary", "arbitrary"),
                                             vmem_limit_bytes=VMEM_LIMIT),
        name="dsa_sample_scores",
    )(page_table, iq_s, iw_bc, ik_new, cache_ik)


def _dsa_sample_search_kernel(sc_ref, bias_ref, *, topk):
    sc = sc_ref[...]
    nb, np1 = sc.shape[0], sc.shape[1]
    pos = lax.broadcasted_iota(I32, sc.shape, 1) * PAGE + lax.broadcasted_iota(I32, sc.shape, 3)

    def count(mask):
        per_lane = jnp.sum(mask.astype(I32), axis=1, keepdims=True)
        return jnp.sum(per_lane, axis=3, keepdims=True)

    count_ge = lambda t: count(sc_ref[...] >= t)
    thr0, thr_up = _radix_threshold(count_ge, (nb, 1, 8, 1), topk)
    nbits = (np1 * PAGE - 1).bit_length()

    def tie_search(_):
        thr_t = _refine_in_bin(count_ge, thr0, thr_up, topk)
        need = topk - count(sc > thr_t)
        tied = sc == thr_t

        def pos_body(i, p):
            cand = p | jnp.left_shift(jnp.int32(1), nbits - 1 - i)
            cnt = count(tied & (pos < cand))
            return jnp.where(cnt < need, cand, p)
        return thr_t, lax.fori_loop(0, nbits, pos_body, jnp.zeros(thr0.shape, I32))

    has_tie = jnp.max(jnp.where((count_ge(thr0) > topk) & (thr0 > -jnp.inf), 1, 0)) > 0
    thr, p_lim = lax.cond(has_tie, tie_search, lambda _: (thr0, jnp.full(thr0.shape, 2 ** 30, I32)), 0)
    sel = ((sc > thr) | ((sc == thr) & (pos <= p_lim))) & (sc > -jnp.inf)
    bias_ref[...] = jnp.where(sel, 0.0, NEG_BIG)


def _dsa_sample_search(scores, *, topk):
    db = scores.shape[0]
    nb = 8 if db % 8 == 0 else 1
    blk = pl.BlockSpec((nb,) + scores.shape[1:], lambda i: (i, 0, 0, 0))
    return pl.pallas_call(
        functools.partial(_dsa_sample_search_kernel, topk=topk),
        grid=(db // nb,),
        in_specs=[blk],
        out_specs=blk,
        out_shape=jax.ShapeDtypeStruct(scores.shape, F32),
        compiler_params=pltpu.CompilerParams(dimension_semantics=("arbitrary",), vmem_limit_bytes=VMEM_LIMIT),
        name="dsa_sample_search",
    )(scores)


def _dsa_sample_attend_kernel(pt_ref, q_ref, bias_ref, knew_ref, vnew_ref, zb_ref, ck_ref, cv_ref, o_ref,
                              k_buf, v_buf, k_sem, v_sem, acc_ref, m_ref, l_ref, *, pages_per_step, n_pages, ds):
    g = pl.program_id(1)
    slot = _page_ring(pt_ref, pages_per_step, [(ck_ref, k_buf, k_sem), (cv_ref, v_buf, v_sem)])
    k_refs = [k_buf.at[slot, i] for i in range(pages_per_step)]
    v_refs = [v_buf.at[slot, i] for i in range(pages_per_step)]
    rows_q = ds * B_HEADS
    q = q_ref[0]

    @pl.when(g == 0)
    def _():
        acc_ref[...] = jnp.zeros(acc_ref.shape, F32)
        m_ref[...] = jnp.full(m_ref.shape, NEG_BIG, F32)
        l_ref[...] = jnp.zeros(l_ref.shape, F32)

    def step(k_keys, v_keys, bias8):
        n = k_keys.shape[0]
        bias = jnp.concatenate([jnp.broadcast_to(bias8[t:t + 1, :], (B_HEADS, n)) for t in range(ds)], axis=0)
        s = _dot_nt(q, k_keys) + bias
        m_old = m_ref[...]
        m_new = jnp.maximum(m_old, jnp.max(s, axis=1, keepdims=True))
        alpha = jnp.exp2(m_old - m_new)
        p = jnp.exp2(s - m_new)
        l_ref[...] = alpha * l_ref[...] + jnp.sum(p, axis=1, keepdims=True)
        m_ref[...] = m_new
        acc_ref[...] = alpha * acc_ref[...] + _dot(p.astype(BF16), v_keys)

    def page_keys(ref):
        return jnp.concatenate([ref[pl.ds(kv, PAGE, stride=B_KV), :] for kv in range(B_KV)], axis=1).astype(BF16)

    step(jnp.concatenate([page_keys(r) for r in k_refs], axis=0),
         jnp.concatenate([page_keys(r) for r in v_refs], axis=0),
         jnp.concatenate([bias_ref[0, g * pages_per_step + i] for i in range(pages_per_step)], axis=1))

    @pl.when(g == pl.num_programs(1) - 1)
    def _():
        step(knew_ref[0], vnew_ref[0], bias_ref[0, n_pages])
        o = acc_ref[...] / l_ref[...]
        head = lax.broadcasted_iota(I32, (rows_q, B_HD), 0) & (B_HEADS - 1)
        o_sel = o[:, 0:B_HD]
        for kv in range(1, B_KV):
            o_sel = jnp.where(head >= kv * B_GROUP, o[:, kv * B_HD:(kv + 1) * B_HD], o_sel)
        o_ref[0] = (o_sel * zb_ref[0]).astype(o_ref.dtype)


def _dsa_sample_attend(page_table, q_bd, bias, k_new, v_new, zb_s, cache_k, cache_v, *, ds, pages_per_step):
    db, n_pages = page_table.shape
    npg = n_pages // pages_per_step
    rows_q = ds * B_HEADS
    kvw = B_KV * B_HD
    per_b = lambda a: pl.BlockSpec((1,) + a.shape[1:], lambda b, g, pt: (b,) + (0,) * (a.ndim - 1))
    any_space = pl.BlockSpec(memory_space=pl.ANY)
    page_buf = pltpu.VMEM((2, pages_per_step, PAGE * B_KV, B_HD), F32)
    return pl.pallas_call(
        functools.partial(_dsa_sample_attend_kernel, pages_per_step=pages_per_step, n_pages=n_pages, ds=ds),
        grid_spec=pltpu.PrefetchScalarGridSpec(
            num_scalar_prefetch=1, grid=(db, npg),
            in_specs=[per_b(q_bd), per_b(bias), per_b(k_new), per_b(v_new), per_b(zb_s), any_space, any_space],
            out_specs=pl.BlockSpec((1, rows_q, B_HD), lambda b, g, pt: (b, 0, 0)),
            scratch_shapes=[page_buf, page_buf, pltpu.SemaphoreType.DMA((2,)), pltpu.SemaphoreType.DMA((2,)),
                            pltpu.VMEM((rows_q, kvw), F32), pltpu.VMEM((rows_q, 1), F32),
                            pltpu.VMEM((rows_q, 1), F32)]),
        out_shape=jax.ShapeDtypeStruct((db, rows_q, B_HD), BF16),
        compiler_params=pltpu.CompilerParams(dimension_semantics=("arbitrary", "arbitrary"),
                                             vmem_limit_bytes=VMEM_LIMIT),
        name="dsa_sample_attend",
    )(page_table, q_bd, bias, k_new, v_new, zb_s, cache_k, cache_v)


def _sample_path(x, pe, cache_k, cache_v, cache_ik, page_table, c0, n0, m0, conv0, w, w_packed, misc_bias):
    db, ds, d = x.shape
    ms = db * ds
    assert ds >= CONV_W - 1 and ds <= 8
    n_pages = page_table.shape[1]
    a = _inproj(x.reshape(ms, d), w['norm_pre'], w_packed, w['conv_w'], w['conv_b'], misc_bias,
                prompt=False, seq_len=ms, tm=ms)

    tmaj = lambda z: jnp.swapaxes(z.reshape(db, ds, -1), 0, 1).reshape(ms, -1)
    ya_t, c_new, n_new, m_new = _mlstm_sample(
        tmaj(a['q']), tmaj(a['k']), jnp.swapaxes(conv0, 0, 1), w['conv_w'], w['conv_b'], tmaj(a['av']),
        tmaj(a['misc']), tmaj(a['ga']), w['norm_a'], jnp.swapaxes(n0, 0, 1),
        jnp.broadcast_to(jnp.swapaxes(m0, 0, 1)[:, :, None], (A_HEADS, db, LANES)), c0, db=db, ds=ds)
    ya = jnp.swapaxes(ya_t.reshape(ds, db, -1), 0, 1).reshape(ms, -1)

    topk = min(TOPK_MAX, (n_pages * PAGE + ds) // 4)
    pad_rows = lambda z: jnp.pad(z.reshape(db, ds, -1), ((0, 0), (0, PAGE - ds), (0, 0))).astype(BF16)
    pad_tok = lambda z: jnp.pad(z.reshape(db, ds, -1), ((0, 0), (0, 8 - ds), (0, 0)))
    iq_s = pad_tok(a['iq']).reshape(db, 8 * IDX_HEADS, IDX_DIM)
    iw_bc = jnp.broadcast_to(pad_tok(a['misc'][:, MISC_IW:MISC_IW + IDX_HEADS]).reshape(db, 8 * IDX_HEADS, 1),
                             (db, 8 * IDX_HEADS, LANES))
    pps = 16 if n_pages % 16 == 0 else n_pages
    scores = _dsa_sample_scores(page_table, iq_s, iw_bc, jnp.swapaxes(pad_rows(a['misc'][:, MISC_IK:]), 1, 2),
                                jnp.swapaxes(cache_ik, 1, 2), ds=ds, pages_per_step=pps)
    bias = _dsa_sample_search(scores, topk=topk)
    q5 = a['bq'].reshape(db, ds * B_HEADS, 1, B_HD)
    kv_of_row = (jnp.arange(ds * B_HEADS) % B_HEADS) // B_GROUP
    q_bd = jnp.where((kv_of_row[None, :, None, None] == jnp.arange(B_KV)[None, None, :, None]), q5,
                     jnp.zeros((), BF16)).reshape(db, ds * B_HEADS, B_KV * B_HD)
    attn = _dsa_sample_attend(page_table, q_bd, bias, pad_rows(a['kf']), pad_rows(a['vf']),
                              a['zb'].reshape(db, ds * B_HEADS, B_HD),
                              cache_k.reshape(-1, PAGE * B_KV, B_HD), cache_v.reshape(-1, PAGE * B_KV, B_HD),
                              ds=ds, pages_per_step=pps)
    yb = attn.reshape(ms, B_WIDTH)
    y = _outproj(ya, yb, a['sga'], a['sgb'], x.reshape(ms, d), pe.reshape(ms, -1), w['wa'], w['wb'], w['wo'],
                 w['wg'], w['wp'], w['norm_post'], tm=ms)
    qk_pre = jnp.concatenate([a['q'], a['k']], axis=1).reshape(db, ds, 2 * A_QK)
    return dict(
        y=y.reshape(db, ds, d),
        k=a['kf'].reshape(1, db, ds, B_KV, B_HD),
        v=a['vf'].reshape(1, db, ds, B_KV, B_HD),
        ik=a['misc'][:, MISC_IK:].reshape(1, db, ds, IDX_DIM),
        C=c_new[None],
        n=jnp.swapaxes(n_new, 0, 1)[None],
        m=jnp.swapaxes(m_new[:, :, 0], 0, 1)[None],
        conv=qk_pre[:, ds - (CONV_W - 1):, :][None],
    )


def _prep_weights(w_in, conv_w, conv_b, if_bias, norm_a, w_a_proj, w_b_proj, w_out, norm_pre, norm_post, w_ple,
                  w_ple_gate):
    d = w_in.shape[1]
    misc_bias = jnp.zeros((1, LANES), F32).at[0, :2 * A_HEADS].set(if_bias[0])
    w = dict(norm_pre=norm_pre[0][None], norm_post=norm_post[0][None], norm_a=norm_a[0][None],
             conv_w=conv_w[0], conv_b=conv_b[0][None],
             wa=w_a_proj[0].astype(BF16), wb=w_b_proj[0].astype(BF16), wo=w_out[0].astype(BF16),
             wg=w_ple_gate[0].astype(BF16), wp=w_ple[0].astype(BF16))
    return w, _pack_w_in(w_in[0], d), misc_bias


def kernel(x_prompt, x_sample, cache_k, cache_v, cache_idx_k, page_table, state_C, state_n, state_m, state_conv,
           p_prompt, p_sample, w_in, conv_w, conv_b, if_bias, norm_a, w_a_proj, w_b_proj, w_out, norm_pre,
           norm_post, w_ple, w_ple_gate):
    w, w_packed, misc_bias = _prep_weights(w_in, conv_w, conv_b, if_bias, norm_a, w_a_proj, w_b_proj, w_out,
                                           norm_pre, norm_post, w_ple, w_ple_gate)
    p = _prompt_path(x_prompt, p_prompt[0], w, w_packed, misc_bias)
    s = _sample_path(x_sample, p_sample[0], cache_k[0], cache_v[0], cache_idx_k[0], page_table, state_C[0],
                     state_n[0], state_m[0], state_conv[0], w, w_packed, misc_bias)
    names = ('k', 'v', 'ik', 'C', 'n', 'm', 'conv')
    return (p['y'], s['y']) + tuple(p[n] for n in names) + tuple(s[n] for n in names)
```

```python
import functools

import jax
import jax.numpy as jnp
from jax import lax
from jax.experimental import pallas as pl
from jax.experimental.pallas import tpu as pltpu

F32 = jnp.float32
BF16 = jnp.bfloat16
I32 = jnp.int32

EPS = 1e-6
A_HEADS = 4
A_DQK = 128
A_DV = 256
A_QK = A_HEADS * A_DQK
A_WIDTH = A_HEADS * A_DV
CONV_W = 4
B_HEADS = 8
B_KV = 2
B_HD = 128
B_GROUP = B_HEADS // B_KV
B_WIDTH = B_HEADS * B_HD
IDX_HEADS = 8
IDX_DIM = 64
TOPK_MAX = 256
PAGE = 128
LANES = 128
VMEM_LIMIT = 56 * 1024 * 1024

Q_SCALE = A_DQK ** -0.5
ATT_SCALE = B_HD ** -0.5
IDX_SCALE = IDX_DIM ** -0.5
IW_SCALE = IDX_HEADS ** -0.5
C_EXP = ATT_SCALE * 1.4426950408889634
VT_ROWS = B_HD + 16
NEG_BIG = -1e30
INT_MIN = -(2 ** 31)
KEY_NEG_INF = INT_MIN + 0x007FFFFF

_SPLITS = (('a_q', A_QK), ('a_k', A_QK), ('a_v', A_WIDTH), ('a_i', A_HEADS), ('a_f', A_HEADS),
           ('a_o', A_WIDTH), ('a_z', A_WIDTH), ('b_q', B_WIDTH), ('b_k', B_KV * B_HD), ('b_v', B_KV * B_HD),
           ('b_iq', IDX_HEADS * IDX_DIM), ('b_ik', IDX_DIM), ('b_iw', IDX_HEADS), ('b_z', B_WIDTH),
           ('g_a', 1024), ('g_b', 1024))

P_QK = 0
P_AV = 1024
P_AO = 2048
P_AZ = 3072
P_BQ = 4096
P_BK = 5120
P_BV = 5376
P_IQ = 5632
P_MISC = 6144
P_BZ = 6272
P_GA = 7296
P_GB = 8320
P_COLS = 9344
MISC_IW = 8
MISC_IK = 64


def _sigmoid(x):
    return 1.0 / (1.0 + jnp.exp(-x))


def _silu(x):
    return x * _sigmoid(x)


def _dot(a, b):
    return jnp.dot(a, b, preferred_element_type=F32)


def _dot_nt(a, b):
    return lax.dot_general(a, b, (((1,), (1,)), ((), ())), preferred_element_type=F32)


def _transpose(x):
    r, c = x.shape
    rows = []
    for j in range(c // LANES):
        rows.append(jnp.concatenate([x[i * LANES:(i + 1) * LANES, j * LANES:(j + 1) * LANES].T
                                     for i in range(r // LANES)], axis=1))
    return jnp.concatenate(rows, axis=0)


def _pack_w_in(w_in, d_model):
    offs = {}
    c = 0
    for name, n in _SPLITS:
        offs[name] = (c, c + n)
        c += n

    def col(name):
        lo, hi = offs[name]
        return w_in[:, lo:hi]

    z = lambda n: jnp.zeros((d_model, n), w_in.dtype)
    misc = jnp.concatenate([col('a_i'), col('a_f'), col('b_iw'), z(MISC_IK - 16), col('b_ik')], axis=1)
    packed = jnp.concatenate([col('a_q'), col('a_k'), col('a_v'), col('a_o'), col('a_z'), col('b_q'), col('b_k'),
                              col('b_v'), col('b_iq'), misc, col('b_z'), col('g_a'), col('g_b')], axis=1)
    return packed.astype(BF16)


def _inproj_kernel(x_ref, g_ref, w_ref, cw_ref, cb_ref, mb_ref, *refs, names, prompt, tiles_per_seq):
    o = dict(zip(names, refs))
    hist_ref = refs[len(names)]
    tm = x_ref.shape[0]
    x = x_ref[...]
    ms = jnp.mean(x * x, axis=-1, keepdims=True)
    xn = (x * lax.rsqrt(ms + EPS) * g_ref[...]).astype(BF16)

    def proj(lo, hi):
        return _dot(xn, w_ref[:, lo:hi])

    qk = proj(P_QK, P_QK + 2 * A_QK)
    if prompt:
        o['tail'][0] = qk[tm - 8:tm, :]

        @pl.when(pl.program_id(0) % tiles_per_seq == 0)
        def _():
            hist_ref[0:8, :] = jnp.zeros((8, 2 * A_QK), F32)

        hist_ref[8:8 + tm, :] = qk
        cw = cw_ref[...]
        y = cb_ref[...] + cw[0:1, :] * hist_ref[5:5 + tm, :]
        y = y + cw[1:2, :] * hist_ref[6:6 + tm, :]
        y = y + cw[2:3, :] * hist_ref[7:7 + tm, :]
        y = y + cw[3:4, :] * qk
        act = _silu(y)
        o['q'][...] = (act[:, :A_QK] * Q_SCALE).astype(BF16)
        o['k'][...] = act[:, A_QK:].astype(BF16)
        hist_ref[0:8, :] = qk[tm - 8:tm, :]
    else:
        o['q'][...] = qk[:, :A_QK]
        o['k'][...] = qk[:, A_QK:]

    o['av'][...] = proj(P_AV, P_AV + A_WIDTH).astype(BF16)
    o['ga'][...] = (_sigmoid(proj(P_AO, P_AO + A_WIDTH)) * _silu(proj(P_AZ, P_AZ + A_WIDTH))).astype(BF16)
    bq = proj(P_BQ, P_BQ + B_WIDTH) * C_EXP
    iq = proj(P_IQ, P_IQ + IDX_HEADS * IDX_DIM)
    kf = proj(P_BK, P_BK + B_KV * B_HD)
    vf = proj(P_BV, P_BV + B_KV * B_HD)
    for kv in range(B_KV):
        o['kf'][pl.ds(kv, tm, stride=B_KV), :] = kf[:, kv * B_HD:(kv + 1) * B_HD]
        o['vf'][pl.ds(kv, tm, stride=B_KV), :] = vf[:, kv * B_HD:(kv + 1) * B_HD]
    if prompt:
        for r in range(tm // PAGE):
            rows = slice(r * PAGE, (r + 1) * PAGE)
            o['bq'][r] = _transpose(bq[rows, :]).astype(BF16)
            o['iq'][r] = _transpose(iq[rows, :]).astype(BF16)
        ones = jnp.ones((VT_ROWS - B_HD, 2 * PAGE), BF16)
        for p in range(tm // (2 * PAGE)):
            rows = slice(p * 2 * PAGE, (p + 1) * 2 * PAGE)
            parts = []
            for g in range(B_KV):
                parts += [_transpose(vf[rows, g * B_HD:(g + 1) * B_HD]).astype(BF16), ones]
            o['vt'][p] = jnp.concatenate(parts, axis=0)
        o['kb'][...] = kf.astype(BF16)
    else:
        o['bq'][...] = bq.astype(BF16)
        o['iq'][...] = iq.astype(BF16)

    raw = proj(P_MISC, P_MISC + LANES) + mb_ref[...]
    lane = lax.broadcasted_iota(I32, raw.shape, 1)
    logsig = jnp.minimum(raw, 0.0) - jnp.log(1.0 + jnp.exp(-jnp.abs(raw)))
    o['misc'][...] = jnp.where((lane >= A_HEADS) & (lane < 2 * A_HEADS), logsig,
                               jnp.where((lane >= MISC_IW) & (lane < MISC_IW + IDX_HEADS),
                                         raw * (IW_SCALE * IDX_SCALE), raw))
    if prompt:
        o['miscb'][...] = jnp.where(lane >= MISC_IK, raw, 0.0).astype(BF16)

    o['zb'][...] = _silu(proj(P_BZ, P_BZ + B_WIDTH)).astype(BF16)
    o['sga'][...] = _sigmoid(proj(P_GA, P_GA + 1024)).astype(BF16)
    o['sgb'][...] = _sigmoid(proj(P_GB, P_GB + 1024)).astype(BF16)


def _inproj(x2d, norm_pre, w_packed, conv_w, conv_b, misc_bias, *, prompt, seq_len, tm):
    m, d = x2d.shape
    nt = m // tm
    rows = lambda w, dt: (jax.ShapeDtypeStruct((m, w), dt), pl.BlockSpec((tm, w), lambda i: (i, 0)))
    tiles = lambda n, r, c, dt: (jax.ShapeDtypeStruct((m // n, r, c), dt),
                                 pl.BlockSpec((tm // n, r, c), lambda i: (i, 0, 0)))
    kv_rows = (jax.ShapeDtypeStruct((B_KV * m, B_HD), F32), pl.BlockSpec((B_KV * tm, B_HD), lambda i: (i, 0)))
    outs = dict(q=rows(A_QK, BF16 if prompt else F32), k=rows(A_QK, BF16 if prompt else F32), av=rows(A_WIDTH, BF16),
                ga=rows(A_WIDTH, BF16), kf=kv_rows, vf=kv_rows, misc=rows(LANES, F32),
                zb=rows(B_WIDTH, BF16), sga=rows(d, BF16), sgb=rows(d, BF16))
    if prompt:
        outs.update(tail=tiles(tm, 8, 2 * A_QK, F32), bq=tiles(PAGE, B_WIDTH, PAGE, BF16),
                    iq=tiles(PAGE, IDX_HEADS * IDX_DIM, PAGE, BF16),
                    vt=tiles(2 * PAGE, B_KV * VT_ROWS, 2 * PAGE, BF16), kb=rows(B_KV * B_HD, BF16),
                    miscb=rows(LANES, BF16))
    else:
        outs.update(bq=rows(B_WIDTH, BF16), iq=rows(IDX_HEADS * IDX_DIM, BF16))
    names = tuple(outs)
    const = lambda shape: pl.BlockSpec(shape, lambda i: (0,) * len(shape))
    res = pl.pallas_call(
        functools.partial(_inproj_kernel, names=names, prompt=prompt, tiles_per_seq=max(seq_len // tm, 1)),
        grid=(nt,),
        in_specs=[pl.BlockSpec((tm, d), lambda i: (i, 0)), const((1, d)),
                  pl.BlockSpec((d, P_COLS), lambda i: (0, 0), pipeline_mode=pl.Buffered(1)),
                  const((CONV_W, 2 * A_QK)), const((1, 2 * A_QK)), const((1, LANES))],
        out_specs=[outs[n][1] for n in names],
        out_shape=[outs[n][0] for n in names],
        scratch_shapes=[pltpu.VMEM((8 + tm, 2 * A_QK), F32)],
        compiler_params=pltpu.CompilerParams(dimension_semantics=("arbitrary",), vmem_limit_bytes=VMEM_LIMIT),
        name="inproj_prompt" if prompt else "inproj_decode",
    )(x2d, norm_pre, w_packed, conv_w, conv_b, misc_bias)
    return dict(zip(names, res))


def _outproj_kernel(ya_ref, yb_ref, sga_ref, sgb_ref, x_ref, pe_ref, wa_ref, wb_ref, wo_ref, wg_ref, wp_ref,
                    g_ref, o_ref):
    merged = sga_ref[...] * _dot(ya_ref[...], wa_ref[...]) + sgb_ref[...] * _dot(yb_ref[...], wb_ref[...])
    z = _dot(merged.astype(BF16), wo_ref[...])
    ms = jnp.mean(z * z, axis=-1, keepdims=True)
    x1 = x_ref[...] + z * lax.rsqrt(ms + EPS) * g_ref[...]
    gate = _sigmoid(_dot(x1.astype(BF16), wg_ref[...]))
    o_ref[...] = x1 + _dot(pe_ref[...].astype(BF16), wp_ref[...]) * gate


def _outproj(ya, yb, sga, sgb, x2d, pe2d, wa, wb, wo, wg, wp, norm_post, *, tm):
    m, d = x2d.shape
    pd = pe2d.shape[1]
    row = lambda w: pl.BlockSpec((tm, w), lambda i: (i, 0))
    const = lambda shape: pl.BlockSpec(shape, lambda i: (0,) * len(shape))
    return pl.pallas_call(
        _outproj_kernel,
        grid=(m // tm,),
        in_specs=[row(A_WIDTH), row(B_WIDTH), row(d), row(d), row(d), row(pd),
                  const((A_WIDTH, d)), const((B_WIDTH, d)), const((d, d)), const((d, d)), const((pd, d)),
                  const((1, d))],
        out_specs=row(d),
        out_shape=jax.ShapeDtypeStruct((m, d), F32),
        compiler_params=pltpu.CompilerParams(dimension_semantics=("arbitrary",), vmem_limit_bytes=VMEM_LIMIT),
        name="outproj",
    )(ya, yb, sga, sgb, x2d, pe2d, wa, wb, wo, wg, wp, norm_post)


def _cumsum_rows(x):
    n = x.shape[0]
    row = lax.broadcasted_iota(I32, x.shape, 0)
    s = 1
    while s < n:
        x = x + jnp.where(row >= s, pltpu.roll(x, s, axis=0), 0.0)
        s *= 2
    return x


def _lane_col(x, lane_iota, idx):
    return jnp.sum(jnp.where(lane_iota == idx, x, 0.0), axis=1, keepdims=True)


def _mlstm_prompt_kernel(q_ref, k_ref, v_ref, misc_ref, ga_ref, na_ref, ya_ref, c_out_ref, nm_out_ref,
                         c_ref, n_ref, m_ref):
    c_idx = pl.program_id(1)
    nb, L = q_ref.shape[0], q_ref.shape[1]

    @pl.when(c_idx == 0)
    def _():
        c_ref[...] = jnp.zeros(c_ref.shape, F32)
        n_ref[...] = jnp.zeros(n_ref.shape, F32)
        m_ref[...] = jnp.zeros(m_ref.shape, F32)

    lane = lax.broadcasted_iota(I32, (L, LANES), 1)
    t_i = lax.broadcasted_iota(I32, (L, L), 0)
    s_i = lax.broadcasted_iota(I32, (L, L), 1)
    causal = s_i <= t_i

    for sq in range(nb):
        misc = misc_ref[sq]
        cs = _cumsum_rows(misc)
        b_al = pltpu.roll(cs, LANES - A_HEADS, axis=1)
        xa = jnp.where(lane < A_HEADS, misc - b_al, cs)
        xt = _transpose(xa)

        for h in range(A_HEADS):
            a_col = _lane_col(xa, lane, h)
            b_col = _lane_col(xa, lane, A_HEADS + h)
            a_row = xt[h:h + 1, :]
            m_prev = m_ref[sq, h:h + 1, 0:1]
            logit = jnp.where(causal, a_row, -jnp.inf)
            g_col = jnp.maximum(m_prev, jnp.max(logit, axis=1, keepdims=True))
            dmat = jnp.exp(logit - g_col)
            w_inter = jnp.exp(m_prev - g_col)
            qh = q_ref[sq, :, h * A_DQK:(h + 1) * A_DQK]
            kh = k_ref[sq, :, h * A_DQK:(h + 1) * A_DQK]
            vh = v_ref[sq, :, h * A_DV:(h + 1) * A_DV]
            c_h = c_ref[sq, h]
            n_h = n_ref[sq, h:h + 1, :]
            s = _dot_nt(qh, kh) * dmat
            num = _dot(s.astype(BF16), vh) + w_inter * _dot_nt(qh, c_h.astype(BF16))
            den = (jnp.sum(s, axis=1, keepdims=True)
                   + w_inter * jnp.sum(qh.astype(F32) * n_h, axis=1, keepdims=True))
            m_t = b_col + g_col
            hh = num / jnp.maximum(jnp.abs(den), jnp.exp(-m_t))
            hn = hh * lax.rsqrt(jnp.mean(hh * hh, axis=1, keepdims=True) + EPS)
            sl = slice(h * A_DV, (h + 1) * A_DV)
            ya_ref[sq, :, sl] = (ga_ref[sq, :, sl] * (hn * na_ref[:, sl])).astype(ya_ref.dtype)

            g_last = g_col[L - 1:L, :]
            w_prev = jnp.exp(m_prev - g_last)
            w_s = jnp.exp(a_col - g_last)
            vw = vh.astype(F32) * w_s
            c_ref[sq, h] = w_prev * c_h + _dot(_transpose(vw).astype(BF16), kh)
            n_ref[sq, h:h + 1, :] = w_prev * n_h + jnp.sum(kh.astype(F32) * w_s, axis=0, keepdims=True)
            m_ref[sq, h:h + 1, :] = jnp.broadcast_to(b_col[L - 1:L, :] + g_last, (1, LANES))

    @pl.when(c_idx == pl.num_programs(1) - 1)
    def _():
        c_out_ref[...] = c_ref[...]
        nm_out_ref[:, 0:A_HEADS, :] = n_ref[:, 0:A_HEADS, :]
        nm_out_ref[:, A_HEADS:2 * A_HEADS, :] = m_ref[:, 0:A_HEADS, :]


def _mlstm_prompt(q, k, v, misc, ga, norm_a, *, batch, seq_len, chunk):
    nc = seq_len // chunk
    nb = 1
    seq3 = lambda a: a.reshape(batch, seq_len, a.shape[-1])
    blk = lambda w: pl.BlockSpec((nb, chunk, w), lambda b, c: (b, c, 0))
    ya, c_out, nm_out = pl.pallas_call(
        _mlstm_prompt_kernel,
        grid=(batch // nb, nc),
        in_specs=[blk(A_QK), blk(A_QK), blk(A_WIDTH), blk(LANES), blk(A_WIDTH),
                  pl.BlockSpec((1, A_WIDTH), lambda b, c: (0, 0))],
        out_specs=[blk(A_WIDTH),
                   pl.BlockSpec((nb, A_HEADS, A_DV, A_DQK), lambda b, c: (b, 0, 0, 0)),
                   pl.BlockSpec((nb, 8, LANES), lambda b, c: (b, 0, 0))],
        out_shape=[jax.ShapeDtypeStruct((batch, seq_len, A_WIDTH), BF16),
                   jax.ShapeDtypeStruct((batch, A_HEADS, A_DV, A_DQK), F32),
                   jax.ShapeDtypeStruct((batch, 8, LANES), F32)],
        scratch_shapes=[pltpu.VMEM((nb, A_HEADS, A_DV, A_DQK), F32), pltpu.VMEM((nb, 8, LANES), F32),
                        pltpu.VMEM((nb, 8, LANES), F32)],
        compiler_params=pltpu.CompilerParams(dimension_semantics=("arbitrary", "arbitrary"),
                                             vmem_limit_bytes=VMEM_LIMIT),
        name="mlstm_prompt",
    )(seq3(q), seq3(k), seq3(v), seq3(misc), seq3(ga), norm_a)
    return ya.reshape(batch * seq_len, A_WIDTH), c_out, nm_out


def _key_to_f32(key):
    return pltpu.bitcast(jnp.where(key < 0, key ^ 0x7FFFFFFF, key), F32)


def _radix_threshold(count_ge, shape, topk):
    def bit_body(i, u):
        cand_u = u | jnp.left_shift(jnp.int32(1), 31 - i)
        cnt = count_ge(_key_to_f32(cand_u ^ INT_MIN))
        return jnp.where(cnt >= topk, cand_u, u)

    u = lax.fori_loop(0, 32, bit_body, jnp.zeros(shape, I32))
    key = jnp.maximum(u ^ INT_MIN, KEY_NEG_INF)
    return _key_to_f32(key), _key_to_f32(key + 1)


def _refine_in_bin(count_ge, thr, thr_up, topk):
    lo, hi = thr, thr_up
    for _ in range(6):
        mid = lo + 0.5 * (hi - lo)
        ok = count_ge(mid) >= topk
        lo = jnp.where(ok, mid, lo)
        hi = jnp.where(ok, hi, mid)
    return lo


def _dsa_prompt_kernel(bq_ref, iq_ref, misc_ref, zb_ref, ikb_ref, kb_ref, vt_ref, yb_ref,
                       sc_ref, s_ref, s2_ref, p_ref, p2_ref, al_ref, al2_ref, acc_ref, m_ref, *, topk, nq):
    j = pl.program_id(1)
    T = PAGE
    KT = 2 * PAGE
    nkt = j // 2 + 1

    iq_t = iq_ref[0]
    zeros_pad = jnp.zeros((LANES - IDX_DIM, T), BF16)
    iq_pad = [jnp.concatenate([zeros_pad, iq_t[h * IDX_DIM:(h + 1) * IDX_DIM, :]], axis=0)
              for h in range(IDX_HEADS)]
    misc_t = _transpose(misc_ref[...])
    q_t = bq_ref[0]
    q_grp = [jnp.concatenate([q_t[(g * B_GROUP + hh) * B_HD:(g * B_GROUP + hh + 1) * B_HD, :]
                              for hh in range(B_GROUP)], axis=1) for g in range(B_KV)]

    row_i = lax.broadcasted_iota(I32, (KT, T), 0)
    lane_i = lax.broadcasted_iota(I32, (KT, T), 1)

    last_pair = nq // 2 - 1

    def idx_body(c, carry):
        ikc = ikb_ref[0, c]
        acc = jnp.zeros((KT, T), F32)
        for h in range(IDX_HEADS):
            r = jnp.maximum(_dot(ikc, iq_pad[h]), 0.0)
            acc = acc + r * misc_t[MISC_IW + h:MISC_IW + h + 1, :]
        vis = (c * KT + row_i) <= (j * T + lane_i)
        sc_ref[c] = jnp.where(vis, acc, -jnp.inf)
        return carry

    lax.fori_loop(0, nkt, idx_body, 0)

    def count(pred):
        def body(c, cnt):
            m = pred(sc_ref[c], c).reshape(KT // 32, 4, 8, T)
            for v in range(KT // 32):
                cnt = jnp.where(m[v], cnt + 1, cnt)
            return cnt
        cnt4 = lax.fori_loop(0, nkt, body, jnp.zeros((4, 8, T), I32))
        return jnp.sum(jnp.sum(cnt4, axis=0), axis=0, keepdims=True)

    count_ge = lambda t: count(lambda sc, c: sc >= t)
    thr0, thr_up = _radix_threshold(count_ge, (1, T), topk)
    n_ge = count_ge(thr0)

    def tie_search(_):
        thr_t = _refine_in_bin(count_ge, thr0, thr_up, topk)
        need = topk - count(lambda sc, c: sc > thr_t)
        nbits = (nq * T - 1).bit_length()
        def pbody(i, p):
            cand = p | jnp.left_shift(jnp.int32(1), nbits - 1 - i)
            cnt = count(lambda sc, c: (sc == thr_t) & ((c * KT + row_i) < cand))
            return jnp.where(cnt < need, cand, p)
        return thr_t, lax.fori_loop(0, nbits, pbody, jnp.zeros((1, T), I32))

    has_tie = jnp.max(jnp.where((n_ge > topk) & (thr0 > -jnp.inf), 1, 0)) > 0
    thr, p_lim = lax.cond(has_tie, tie_search, lambda _: (thr0, jnp.full((1, T), 2 ** 30, I32)), 0)

    acc_ref[...] = jnp.zeros(acc_ref.shape, F32)
    m_ref[...] = jnp.full(m_ref.shape, NEG_BIG, F32)

    def qk_scores(c, dst_ref):
        kc = kb_ref[0, jnp.minimum(c, last_pair)]
        for g in range(B_KV):
            dst_ref[g] = _dot(kc[:, g * B_HD:(g + 1) * B_HD], q_grp[g])

    def softmax(c, src_ref, p_dst, al_dst):
        sc = sc_ref[c]
        pos = c * KT + row_i
        sel = ((sc > thr) | ((sc == thr) & (pos <= p_lim))) & (sc > -jnp.inf)
        bias = jnp.where(sel, 0.0, NEG_BIG)
        for h in range(B_HEADS):
            g, hh = divmod(h, B_GROUP)
            s = src_ref[g, :, hh * T:(hh + 1) * T] + bias
            m_old = m_ref[h:h + 1, :]
            m_new = jnp.maximum(m_old, jnp.max(s, axis=0, keepdims=True))
            al_dst[h:h + 1, :] = jnp.exp2(m_old - m_new)
            p_dst[h // 2, :, (h % 2) * T:(h % 2 + 1) * T] = jnp.exp2(s - m_new).astype(BF16)
            m_ref[h:h + 1, :] = m_new

    def pv_update(c, p_src, al_src):
        vtc = vt_ref[0, c]
        for h0 in range(0, B_HEADS, 2):
            g = h0 // B_GROUP
            pv = _dot(vtc[g * VT_ROWS:(g + 1) * VT_ROWS, :], p_src[h0 // 2])
            for i, h in enumerate((h0, h0 + 1)):
                acc_ref[h] = acc_ref[h] * al_src[h:h + 1, :] + pv[:, i * T:(i + 1) * T]

    p2_ref[...] = jnp.zeros(p2_ref.shape, BF16)
    al2_ref[...] = jnp.ones(al2_ref.shape, F32)
    qk_scores(0, s_ref)

    def att_body(i, carry):
        c0 = 2 * i
        qk_scores(c0 + 1, s2_ref)
        softmax(c0, s_ref, p_ref, al_ref)
        pv_update(jnp.maximum(c0 - 1, 0), p2_ref, al2_ref)

        @pl.when(c0 + 1 < nkt)
        def _():
            qk_scores(c0 + 2, s_ref)
            softmax(c0 + 1, s2_ref, p2_ref, al2_ref)
            pv_update(c0, p_ref, al_ref)

        return carry

    lax.fori_loop(0, (nkt + 1) // 2, att_body, 0)

    @pl.when(nkt % 2 == 1)
    def _():
        pv_update(nkt - 1, p_ref, al_ref)

    @pl.when(nkt % 2 == 0)
    def _():
        pv_update(nkt - 1, p2_ref, al2_ref)

    for h in range(B_HEADS):
        o_t = acc_ref[h, 0:B_HD, :] / acc_ref[h, B_HD:B_HD + 1, :]
        sl = slice(h * B_HD, (h + 1) * B_HD)
        yb_ref[:, sl] = (_transpose(o_t) * zb_ref[:, sl]).astype(yb_ref.dtype)


def _dsa_prompt(bq, iq, misc, zb, ikb, kb, vt, *, batch, seq_len, topk):
    nq = seq_len // PAGE
    kt = 2 * PAGE
    scores = pltpu.VMEM((B_KV, kt, B_GROUP * PAGE), F32)
    weights = pltpu.VMEM((B_HEADS // 2, kt, 2 * PAGE), BF16)
    rescale = pltpu.VMEM((B_HEADS, PAGE), F32)
    row = lambda w: pl.BlockSpec((PAGE, w), lambda b, j: (b * nq + j, 0))
    tile_t = lambda r: pl.BlockSpec((1, r, PAGE), lambda b, j: (b * nq + j, 0, 0))
    per_b = lambda a: pl.BlockSpec((1,) + a.shape[1:], lambda b, j: (b, 0, 0, 0))
    return pl.pallas_call(
        functools.partial(_dsa_prompt_kernel, topk=topk, nq=nq),
        grid=(batch, nq),
        in_specs=[tile_t(B_WIDTH), tile_t(IDX_HEADS * IDX_DIM), row(LANES), row(B_WIDTH),
                  per_b(ikb), per_b(kb), per_b(vt)],
        out_specs=row(B_WIDTH),
        out_shape=jax.ShapeDtypeStruct((batch * seq_len, B_WIDTH), BF16),
        scratch_shapes=[pltpu.VMEM((nq // 2, kt, PAGE), F32), scores, scores, weights, weights, rescale, rescale,
                        pltpu.VMEM((B_HEADS, VT_ROWS, PAGE), F32),
                        pltpu.VMEM((B_HEADS, PAGE), F32)],
        compiler_params=pltpu.CompilerParams(dimension_semantics=("arbitrary", "arbitrary"),
                                             vmem_limit_bytes=VMEM_LIMIT),
        name="dsa_prompt",
    )(bq, iq, misc, zb, ikb, kb, vt)


def _prompt_path(x, pe, w, w_packed, misc_bias):
    batch, seq_len, d = x.shape
    m = batch * seq_len
    x2d = x.reshape(m, d)
    tm = min(256, seq_len)
    a = _inproj(x2d, w['norm_pre'], w_packed, w['conv_w'], w['conv_b'], misc_bias,
                prompt=True, seq_len=seq_len, tm=tm)
    chunk = min(256, seq_len)
    ya, c_out, nm_out = _mlstm_prompt(a['q'], a['k'], a['av'], a['misc'], a['ga'], w['norm_a'],
                                      batch=batch, seq_len=seq_len, chunk=chunk)
    nk2 = seq_len // (2 * PAGE)
    ikb = a['miscb'].reshape(batch, nk2, 2 * PAGE, LANES)
    kb = a['kb'].reshape(batch, nk2, 2 * PAGE, B_KV * B_HD)
    vt = a['vt'].reshape(batch, nk2, B_KV * VT_ROWS, 2 * PAGE)
    topk = min(TOPK_MAX, seq_len // 4)
    yb = _dsa_prompt(a['bq'], a['iq'], a['misc'], a['zb'], ikb, kb, vt, batch=batch, seq_len=seq_len, topk=topk)
    y = _outproj(ya, yb, a['sga'], a['sgb'], x2d, pe.reshape(m, -1), w['wa'], w['wb'], w['wo'], w['wg'], w['wp'],
                 w['norm_post'], tm=min(256, m))
    tiles_per_seq = seq_len // tm
    tail = a['tail'].reshape(batch, tiles_per_seq, 8, 2 * A_QK)[:, -1, 8 - (CONV_W - 1):, :]
    return dict(
        y=y.reshape(batch, seq_len, d),
        k=a['kf'].reshape(1, batch, seq_len, B_KV, B_HD),
        v=a['vf'].reshape(1, batch, seq_len, B_KV, B_HD),
        ik=a['misc'][:, MISC_IK:].reshape(1, batch, seq_len, IDX_DIM),
        C=c_out[None],
        n=nm_out[None, :, 0:A_HEADS, :],
        m=nm_out[None, :, A_HEADS:2 * A_HEADS, 0],
        conv=tail[None],
    )


def _mlstm_sample_kernel(q_ref, k_ref, conv0_ref, cw_ref, cb_ref, v_ref, misc_ref, ga_ref, na_ref, n0_ref, m0_ref,
                         c0_ref, ya_ref, c_out_ref, n_out_ref, m_out_ref,
                         qs_ref, ks_ref, cq_ref, wprev_ref, vwt_ref, numi_ref, den_ref, wint_ref, enm_ref, *, db, ds):
    b = pl.program_id(0)
    ms = ds * db
    rows = lambda t: slice(t * db, (t + 1) * db)

    @pl.when(b == 0)
    def _():
        cw = cw_ref[...]
        u = [conv0_ref[j] for j in range(CONV_W - 1)]
        u += [jnp.concatenate([q_ref[rows(t), :], k_ref[rows(t), :]], axis=1) for t in range(ds)]
        for t in range(ds):
            y = cb_ref[...] + cw[0:1, :] * u[t]
            for jj in range(1, CONV_W):
                y = y + cw[jj:jj + 1, :] * u[t + jj]
            act = _silu(y)
            qs_ref[rows(t), :] = act[:, :A_QK] * Q_SCALE
            ks_ref[rows(t), :] = act[:, A_QK:]
        cq_ref[...] = jnp.zeros(cq_ref.shape, F32)

        lane = lax.broadcasted_iota(I32, (db, LANES), 1)
        for h in range(A_HEADS):
            i_t = [_lane_col(misc_ref[rows(t), :], lane, h) for t in range(ds)]
            lf_t = [_lane_col(misc_ref[rows(t), :], lane, A_HEADS + h) for t in range(ds)]
            b_t = [lf_t[0]]
            for t in range(1, ds):
                b_t.append(b_t[-1] + lf_t[t])
            a_t = [i_t[t] - b_t[t] for t in range(ds)]
            m_prev = jnp.max(m0_ref[h], axis=1, keepdims=True)
            g_t = [jnp.maximum(m_prev, a_t[0])]
            for t in range(1, ds):
                g_t.append(jnp.maximum(g_t[-1], a_t[t]))
            hq = slice(h * A_DQK, (h + 1) * A_DQK)
            hv = slice(h * A_DV, (h + 1) * A_DV)
            qh = [qs_ref[rows(t), hq] for t in range(ds)]
            kh = [ks_ref[rows(t), hq] for t in range(ds)]
            vh = [v_ref[rows(t), hv].astype(F32) for t in range(ds)]
            n0 = n0_ref[h]
            for t in range(ds):
                num = jnp.zeros((db, A_DV), F32)
                den = jnp.zeros((db, 1), F32)
                for s in range(t + 1):
                    w_ts = jnp.sum(qh[t] * kh[s], axis=1, keepdims=True) * jnp.exp(a_t[s] - g_t[t])
                    num = num + w_ts * vh[s]
                    den = den + w_ts
                w_inter = jnp.exp(m_prev - g_t[t])
                den = den + w_inter * jnp.sum(qh[t] * n0, axis=1, keepdims=True)
                numi_ref[rows(t), hv] = num
                den_ref[h, rows(t), :] = jnp.broadcast_to(den, (db, LANES))
                wint_ref[h, rows(t), :] = jnp.broadcast_to(w_inter, (db, LANES))
                enm_ref[h, rows(t), :] = jnp.broadcast_to(jnp.exp(-(b_t[t] + g_t[t])), (db, LANES))
            g_last = g_t[ds - 1]
            w_prev = jnp.exp(m_prev - g_last)
            wprev_ref[h] = jnp.broadcast_to(w_prev, (db, LANES))
            n_new = w_prev * n0
            vw = []
            for s in range(ds):
                w_s = jnp.exp(a_t[s] - g_last)
                n_new = n_new + w_s * kh[s]
                vw.append(vh[s] * w_s)
            n_out_ref[h] = n_new
            m_out_ref[h] = jnp.broadcast_to(b_t[ds - 1] + g_last, (db, LANES))
            vw_all = jnp.concatenate(vw, axis=0)
            if ms < LANES:
                vw_all = jnp.concatenate([vw_all, jnp.zeros((LANES - ms, A_DV), F32)], axis=0)
            vwt_ref[h] = _transpose(vw_all)

    mcols = vwt_ref.shape[2]
    row_i = lax.broadcasted_iota(I32, (ms, A_DQK), 0)
    col_i = lax.broadcasted_iota(I32, (A_DV, mcols), 1)
    row_mine = row_i == b
    col_mine = col_i == b
    for t in range(1, ds):
        row_mine = row_mine | (row_i == b + t * db)
        col_mine = col_mine | (col_i == b + t * db)
    for h in range(A_HEADS):
        hq = slice(h * A_DQK, (h + 1) * A_DQK)
        c0 = c0_ref[0, h]
        q_mine = jnp.where(row_mine, qs_ref[:, hq], 0.0).astype(BF16)
        cq_ref[h] = cq_ref[h] + _dot_nt(q_mine, c0.astype(BF16))
        vw_mine = jnp.where(col_mine, vwt_ref[h], 0.0).astype(BF16)
        k_all = ks_ref[:, hq]
        if mcols > ms:
            k_all = jnp.concatenate([k_all, jnp.zeros((mcols - ms, A_DQK), F32)], axis=0)
        c_out_ref[0, h] = wprev_ref[h, pl.ds(b, 1), :] * c0 + _dot(vw_mine, k_all.astype(BF16))

    @pl.when(b == pl.num_programs(0) - 1)
    def _():
        for h in range(A_HEADS):
            hv = slice(h * A_DV, (h + 1) * A_DV)
            num = numi_ref[:, hv] + wint_ref[h][:, 0:1] * cq_ref[h]
            hh = num / jnp.maximum(jnp.abs(den_ref[h][:, 0:1]), enm_ref[h][:, 0:1])
            hn = hh * lax.rsqrt(jnp.mean(hh * hh, axis=1, keepdims=True) + EPS)
            ya_ref[:, hv] = (ga_ref[:, hv] * (hn * na_ref[:, hv])).astype(ya_ref.dtype)


def _mlstm_sample(q_t, k_t, conv0_t, conv_w, conv_b, v_t, misc_t, ga_t, norm_a, n0_t, m0_bc, c0, *, db, ds):
    ms = db * ds
    mcols = max(ms, LANES)
    full = lambda a: pl.BlockSpec(a.shape, lambda b: (0,) * a.ndim)
    cblk = pl.BlockSpec((1, A_HEADS, A_DV, A_DQK), lambda b: (b, 0, 0, 0))
    hshape = jax.ShapeDtypeStruct((A_HEADS, db, LANES), F32)
    ins = (q_t, k_t, conv0_t, conv_w, conv_b, v_t, misc_t, ga_t, norm_a, n0_t, m0_bc)
    return pl.pallas_call(
        functools.partial(_mlstm_sample_kernel, db=db, ds=ds),
        grid=(db,),
        in_specs=[full(a) for a in ins] + [cblk],
        out_specs=[pl.BlockSpec((ms, A_WIDTH), lambda b: (0, 0)), cblk,
                   pl.BlockSpec((A_HEADS, db, LANES), lambda b: (0, 0, 0)),
                   pl.BlockSpec((A_HEADS, db, LANES), lambda b: (0, 0, 0))],
        out_shape=[jax.ShapeDtypeStruct((ms, A_WIDTH), BF16), jax.ShapeDtypeStruct(c0.shape, F32), hshape, hshape],
        scratch_shapes=[pltpu.VMEM((ms, A_QK), F32), pltpu.VMEM((ms, A_QK), F32),
                        pltpu.VMEM((A_HEADS, ms, A_DV), F32), pltpu.VMEM((A_HEADS, db, LANES), F32),
                        pltpu.VMEM((A_HEADS, A_DV, mcols), F32), pltpu.VMEM((ms, A_WIDTH), F32),
                        pltpu.VMEM((A_HEADS, ms, LANES), F32), pltpu.VMEM((A_HEADS, ms, LANES), F32),
                        pltpu.VMEM((A_HEADS, ms, LANES), F32)],
        compiler_params=pltpu.CompilerParams(dimension_semantics=("arbitrary",), vmem_limit_bytes=VMEM_LIMIT),
        name="mlstm_sample",
    )(*ins, c0)


def _page_ring(pt_ref, pages_per_step, streams):
    b, g = pl.program_id(0), pl.program_id(1)
    nb, ng = pl.num_programs(0), pl.num_programs(1)
    step = b * ng + g
    slot = lax.rem(step, 2)

    def copies(bb, gg, sl):
        return [pltpu.make_async_copy(hbm.at[pt_ref[bb, gg * pages_per_step + i]], buf.at[sl, i], sem.at[sl])
                for hbm, buf, sem in streams for i in range(pages_per_step)]

    def start_all(cps):
        for n, cp in enumerate(cps):
            cp.start(priority=n % 2)

    @pl.when(step == 0)
    def _():
        start_all(copies(0, 0, 0))

    @pl.when(step + 1 < nb * ng)
    def _():
        wrap = g + 1 == ng
        start_all(copies(jnp.where(wrap, b + 1, b), jnp.where(wrap, 0, g + 1), 1 - slot))

    for cp in copies(b, g, slot):
        cp.wait()
    return slot


def _dsa_sample_scores_kernel(pt_ref, iq_ref, iw_ref, iknew_ref, cache_ref, sc_ref, page_buf, sem,
                              *, pages_per_step, n_pages, ds):
    g = pl.program_id(1)
    slot = _page_ring(pt_ref, pages_per_step, [(cache_ref, page_buf, sem)])
    page_refs = [page_buf.at[slot, i] for i in range(pages_per_step)]
    iq = iq_ref[0]
    iw = iw_ref[0]

    def scores(keys_t, visible):
        n = keys_t.shape[1] // PAGE
        r = jnp.maximum(_dot(iq, keys_t), 0.0) * jnp.concatenate([iw] * n, axis=1)
        sc = jnp.sum(r.reshape(8, IDX_HEADS, n * PAGE), axis=1)
        return jnp.where(visible, sc, -jnp.inf)

    t_all = lax.broadcasted_iota(I32, (8, pages_per_step * PAGE), 0)
    sc_step = scores(jnp.concatenate([r[...] for r in page_refs], axis=1).astype(BF16), t_all < ds)
    for i in range(pages_per_step):
        sc_ref[0, g * pages_per_step + i] = sc_step[:, i * PAGE:(i + 1) * PAGE]

    @pl.when(g == pl.num_programs(1) - 1)
    def _():
        t_i = lax.broadcasted_iota(I32, (8, PAGE), 0)
        s_i = lax.broadcasted_iota(I32, (8, PAGE), 1)
        sc_ref[0, n_pages] = scores(iknew_ref[0], (t_i < ds) & (s_i <= t_i))


def _dsa_sample_scores(page_table, iq_s, iw_bc, ik_new, cache_ik, *, ds, pages_per_step):
    db, n_pages = page_table.shape
    npg = n_pages // pages_per_step
    per_b = lambda a: pl.BlockSpec((1,) + a.shape[1:], lambda b, g, pt: (b,) + (0,) * (a.ndim - 1))
    return pl.pallas_call(
        functools.partial(_dsa_sample_scores_kernel, pages_per_step=pages_per_step, n_pages=n_pages, ds=ds),
        grid_spec=pltpu.PrefetchScalarGridSpec(
            num_scalar_prefetch=1, grid=(db, npg),
            in_specs=[per_b(iq_s), per_b(iw_bc), per_b(ik_new), pl.BlockSpec(memory_space=pl.ANY)],
            out_specs=pl.BlockSpec((1, n_pages + 1, 8, PAGE), lambda b, g, pt: (b, 0, 0, 0)),
            scratch_shapes=[pltpu.VMEM((2, pages_per_step, IDX_DIM, PAGE), F32), pltpu.SemaphoreType.DMA((2,))]),
        out_shape=jax.ShapeDtypeStruct((db, n_pages + 1, 8, PAGE), F32),
        compiler_params=pltpu.CompilerParams(dimension_semantics=("arbitrary", "arbitrary"),
                                             vmem_limit_bytes=VMEM_LIMIT),
        name="dsa_sample_scores",
    )(page_table, iq_s, iw_bc, ik_new, cache_ik)


def _dsa_sample_search_kernel(sc_ref, bias_ref, *, topk):
    sc = sc_ref[...]
    nb, np1 = sc.shape[0], sc.shape[1]
    pos = lax.broadcasted_iota(I32, sc.shape, 1) * PAGE + lax.broadcasted_iota(I32, sc.shape, 3)

    def count(mask):
        per_lane = jnp.sum(mask.astype(I32), axis=1, keepdims=True)
        return jnp.sum(per_lane, axis=3, keepdims=True)

    count_ge = lambda t: count(sc_ref[...] >= t)
    thr0, thr_up = _radix_threshold(count_ge, (nb, 1, 8, 1), topk)
    nbits = (np1 * PAGE - 1).bit_length()

    def tie_search(_):
        thr_t = _refine_in_bin(count_ge, thr0, thr_up, topk)
        need = topk - count(sc > thr_t)
        tied = sc == thr_t

        def pos_body(i, p):
            cand = p | jnp.left_shift(jnp.int32(1), nbits - 1 - i)
            cnt = count(tied & (pos < cand))
            return jnp.where(cnt < need, cand, p)
        return thr_t, lax.fori_loop(0, nbits, pos_body, jnp.zeros(thr0.shape, I32))

    has_tie = jnp.max(jnp.where((count_ge(thr0) > topk) & (thr0 > -jnp.inf), 1, 0)) > 0
    thr, p_lim = lax.cond(has_tie, tie_search, lambda _: (thr0, jnp.full(thr0.shape, 2 ** 30, I32)), 0)
    sel = ((sc > thr) | ((sc == thr) & (pos <= p_lim))) & (sc > -jnp.inf)
    bias_ref[...] = jnp.where(sel, 0.0, NEG_BIG)


def _dsa_sample_search(scores, *, topk):
    db = scores.shape[0]
    nb = 8 if db % 8 == 0 else 1
    blk = pl.BlockSpec((nb,) + scores.shape[1:], lambda i: (i, 0, 0, 0))
    return pl.pallas_call(
        functools.partial(_dsa_sample_search_kernel, topk=topk),
        grid=(db // nb,),
        in_specs=[blk],
        out_specs=blk,
        out_shape=jax.ShapeDtypeStruct(scores.shape, F32),
        compiler_params=pltpu.CompilerParams(dimension_semantics=("arbitrary",), vmem_limit_bytes=VMEM_LIMIT),
        name="dsa_sample_search",
    )(scores)


def _dsa_sample_attend_kernel(pt_ref, q_ref, bias_ref, knew_ref, vnew_ref, zb_ref, ck_ref, cv_ref, o_ref,
                              k_buf, v_buf, k_sem, v_sem, acc_ref, m_ref, l_ref, *, pages_per_step, n_pages, ds):
    g = pl.program_id(1)
    slot = _page_ring(pt_ref, pages_per_step, [(ck_ref, k_buf, k_sem), (cv_ref, v_buf, v_sem)])
    k_refs = [k_buf.at[slot, i] for i in range(pages_per_step)]
    v_refs = [v_buf.at[slot, i] for i in range(pages_per_step)]
    rows_q = ds * B_HEADS
    q = q_ref[0]

    @pl.when(g == 0)
    def _():
        acc_ref[...] = jnp.zeros(acc_ref.shape, F32)
        m_ref[...] = jnp.full(m_ref.shape, NEG_BIG, F32)
        l_ref[...] = jnp.zeros(l_ref.shape, F32)

    def step(k_keys, v_keys, bias8):
        n = k_keys.shape[0]
        bias = jnp.concatenate([jnp.broadcast_to(bias8[t:t + 1, :], (B_HEADS, n)) for t in range(ds)], axis=0)
        s = _dot_nt(q, k_keys) + bias
        m_old = m_ref[...]
        m_new = jnp.maximum(m_old, jnp.max(s, axis=1, keepdims=True))
        alpha = jnp.exp2(m_old - m_new)
        p = jnp.exp2(s - m_new)
        l_ref[...] = alpha * l_ref[...] + jnp.sum(p, axis=1, keepdims=True)
        m_ref[...] = m_new
        acc_ref[...] = alpha * acc_ref[...] + _dot(p.astype(BF16), v_keys)

    def page_keys(ref):
        return jnp.concatenate([ref[pl.ds(kv, PAGE, stride=B_KV), :] for kv in range(B_KV)], axis=1).astype(BF16)

    step(jnp.concatenate([page_keys(r) for r in k_refs], axis=0),
         jnp.concatenate([page_keys(r) for r in v_refs], axis=0),
         jnp.concatenate([bias_ref[0, g * pages_per_step + i] for i in range(pages_per_step)], axis=1))

    @pl.when(g == pl.num_programs(1) - 1)
    def _():
        step(knew_ref[0], vnew_ref[0], bias_ref[0, n_pages])
        o = acc_ref[...] / l_ref[...]
        head = lax.broadcasted_iota(I32, (rows_q, B_HD), 0) & (B_HEADS - 1)
        o_sel = o[:, 0:B_HD]
        for kv in range(1, B_KV):
            o_sel = jnp.where(head >= kv * B_GROUP, o[:, kv * B_HD:(kv + 1) * B_HD], o_sel)
        o_ref[0] = (o_sel * zb_ref[0]).astype(o_ref.dtype)


def _dsa_sample_attend(page_table, q_bd, bias, k_new, v_new, zb_s, cache_k, cache_v, *, ds, pages_per_step):
    db, n_pages = page_table.shape
    npg = n_pages // pages_per_step
    rows_q = ds * B_HEADS
    kvw = B_KV * B_HD
    per_b = lambda a: pl.BlockSpec((1,) + a.shape[1:], lambda b, g, pt: (b,) + (0,) * (a.ndim - 1))
    any_space = pl.BlockSpec(memory_space=pl.ANY)
    page_buf = pltpu.VMEM((2, pages_per_step, PAGE * B_KV, B_HD), F32)
    return pl.pallas_call(
        functools.partial(_dsa_sample_attend_kernel, pages_per_step=pages_per_step, n_pages=n_pages, ds=ds),
        grid_spec=pltpu.PrefetchScalarGridSpec(
            num_scalar_prefetch=1, grid=(db, npg),
            in_specs=[per_b(q_bd), per_b(bias), per_b(k_new), per_b(v_new), per_b(zb_s), any_space, any_space],
            out_specs=pl.BlockSpec((1, rows_q, B_HD), lambda b, g, pt: (b, 0, 0)),
            scratch_shapes=[page_buf, page_buf, pltpu.SemaphoreType.DMA((2,)), pltpu.SemaphoreType.DMA((2,)),
                            pltpu.VMEM((rows_q, kvw), F32), pltpu.VMEM((rows_q, 1), F32),
                            pltpu.VMEM((rows_q, 1), F32)]),
        out_shape=jax.ShapeDtypeStruct((db, rows_q, B_HD), BF16),
        compiler_params=pltpu.CompilerParams(dimension_semantics=("arbitrary", "arbitrary"),
                                             vmem_limit_bytes=VMEM_LIMIT),
        name="dsa_sample_attend",
    )(page_table, q_bd, bias, k_new, v_new, zb_s, cache_k, cache_v)


def _sample_path(x, pe, cache_k, cache_v, cache_ik, page_table, c0, n0, m0, conv0, w, w_packed, misc_bias):
    db, ds, d = x.shape
    ms = db * ds
    assert ds >= CONV_W - 1 and ds <= 8
    n_pages = page_table.shape[1]
    a = _inproj(x.reshape(ms, d), w['norm_pre'], w_packed, w['conv_w'], w['conv_b'], misc_bias,
                prompt=False, seq_len=ms, tm=ms)

    tmaj = lambda z: jnp.swapaxes(z.reshape(db, ds, -1), 0, 1).reshape(ms, -1)
    ya_t, c_new, n_new, m_new = _mlstm_sample(
        tmaj(a['q']), tmaj(a['k']), jnp.swapaxes(conv0, 0, 1), w['conv_w'], w['conv_b'], tmaj(a['av']),
        tmaj(a['misc']), tmaj(a['ga']), w['norm_a'], jnp.swapaxes(n0, 0, 1),
        jnp.broadcast_to(jnp.swapaxes(m0, 0, 1)[:, :, None], (A_HEADS, db, LANES)), c0, db=db, ds=ds)
    ya = jnp.swapaxes(ya_t.reshape(ds, db, -1), 0, 1).reshape(ms, -1)

    topk = min(TOPK_MAX, (n_pages * PAGE + ds) // 4)
    pad_rows = lambda z: jnp.pad(z.reshape(db, ds, -1), ((0, 0), (0, PAGE - ds), (0, 0))).astype(BF16)
    pad_tok = lambda z: jnp.pad(z.reshape(db, ds, -1), ((0, 0), (0, 8 - ds), (0, 0)))
    iq_s = pad_tok(a['iq']).reshape(db, 8 * IDX_HEADS, IDX_DIM)
    iw_bc = jnp.broadcast_to(pad_tok(a['misc'][:, MISC_IW:MISC_IW + IDX_HEADS]).reshape(db, 8 * IDX_HEADS, 1),
                             (db, 8 * IDX_HEADS, LANES))
    largest_divisor = lambda cap: max(p for p in range(1, cap + 1) if n_pages % p == 0)
    pps_scores, pps_attend = largest_divisor(64), largest_divisor(32)
    scores = _dsa_sample_scores(page_table, iq_s, iw_bc, jnp.swapaxes(pad_rows(a['misc'][:, MISC_IK:]), 1, 2),
                                jnp.swapaxes(cache_ik, 1, 2), ds=ds, pages_per_step=pps_scores)
    bias = _dsa_sample_search(scores, topk=topk)
    q5 = a['bq'].reshape(db, ds * B_HEADS, 1, B_HD)
    kv_of_row = (jnp.arange(ds * B_HEADS) % B_HEADS) // B_GROUP
    q_bd = jnp.where((kv_of_row[None, :, None, None] == jnp.arange(B_KV)[None, None, :, None]), q5,
                     jnp.zeros((), BF16)).reshape(db, ds * B_HEADS, B_KV * B_HD)
    attn = _dsa_sample_attend(page_table, q_bd, bias, pad_rows(a['kf']), pad_rows(a['vf']),
                              a['zb'].reshape(db, ds * B_HEADS, B_HD),
                              cache_k.reshape(-1, PAGE * B_KV, B_HD), cache_v.reshape(-1, PAGE * B_KV, B_HD),
                              ds=ds, pages_per_step=pps_attend)
    yb = attn.reshape(ms, B_WIDTH)
    y = _outproj(ya, yb, a['sga'], a['sgb'], x.reshape(ms, d), pe.reshape(ms, -1), w['wa'], w['wb'], w['wo'],
                 w['wg'], w['wp'], w['norm_post'], tm=ms)
    qk_pre = jnp.concatenate([a['q'], a['k']], axis=1).reshape(db, ds, 2 * A_QK)
    return dict(
        y=y.reshape(db, ds, d),
        k=a['kf'].reshape(1, db, ds, B_KV, B_HD),
        v=a['vf'].reshape(1, db, ds, B_KV, B_HD),
        ik=a['misc'][:, MISC_IK:].reshape(1, db, ds, IDX_DIM),
        C=c_new[None],
        n=jnp.swapaxes(n_new, 0, 1)[None],
        m=jnp.swapaxes(m_new[:, :, 0], 0, 1)[None],
        conv=qk_pre[:, ds - (CONV_W - 1):, :][None],
    )


def _prep_weights(w_in, conv_w, conv_b, if_bias, norm_a, w_a_proj, w_b_proj, w_out, norm_pre, norm_post, w_ple,
                  w_ple_gate):
    d = w_in.shape[1]
    misc_bias = jnp.zeros((1, LANES), F32).at[0, :2 * A_HEADS].set(if_bias[0])
    w = dict(norm_pre=norm_pre[0][None], norm_post=norm_post[0][None], norm_a=norm_a[0][None],
             conv_w=conv_w[0], conv_b=conv_b[0][None],
             wa=w_a_proj[0].astype(BF16), wb=w_b_proj[0].astype(BF16), wo=w_out[0].astype(BF16),
             wg=w_ple_gate[0].astype(BF16), wp=w_ple[0].astype(BF16))
    return w, _pack_w_in(w_in[0], d), misc_bias


def kernel(x_prompt, x_sample, cache_k, cache_v, cache_idx_k, page_table, state_C, state_n, state_m, state_conv,
           p_prompt, p_sample, w_in, conv_w, conv_b, if_bias, norm_a, w_a_proj, w_b_proj, w_out, norm_pre,
           norm_post, w_ple, w_ple_gate):
    w, w_packed, misc_bias = _prep_weights(w_in, conv_w, conv_b, if_bias, norm_a, w_a_proj, w_b_proj, w_out,
                                           norm_pre, norm_post, w_ple, w_ple_gate)
    p = _prompt_path(x_prompt, p_prompt[0], w, w_packed, misc_bias)
    s = _sample_path(x_sample, p_sample[0], cache_k[0], cache_v[0], cache_idx_k[0], page_table, state_C[0],
                     state_n[0], state_m[0], state_conv[0], w, w_packed, misc_bias)
    names = ('k', 'v', 'ik', 'C', 'n', 'm', 'conv')
    return (p['y'], s['y']) + tuple(p[n] for n in names) + tuple(s[n] for n in names)
```

```python
import functools

import jax
import jax.numpy as jnp
from jax import lax
from jax.experimental import pallas as pl
from jax.experimental.pallas import tpu as pltpu

F32 = jnp.float32
BF16 = jnp.bfloat16
I32 = jnp.int32

EPS = 1e-6
A_HEADS = 4
A_DQK = 128
A_DV = 256
A_QK = A_HEADS * A_DQK
A_WIDTH = A_HEADS * A_DV
CONV_W = 4
B_HEADS = 8
B_KV = 2
B_HD = 128
B_GROUP = B_HEADS // B_KV
B_WIDTH = B_HEADS * B_HD
IDX_HEADS = 8
IDX_DIM = 64
TOPK_MAX = 256
PAGE = 128
LANES = 128
VMEM_LIMIT = 56 * 1024 * 1024

Q_SCALE = A_DQK ** -0.5
ATT_SCALE = B_HD ** -0.5
IDX_SCALE = IDX_DIM ** -0.5
IW_SCALE = IDX_HEADS ** -0.5
C_EXP = ATT_SCALE * 1.4426950408889634
VT_ROWS = B_HD + 16
NEG_BIG = -1e30
INT_MIN = -(2 ** 31)
KEY_NEG_INF = INT_MIN + 0x007FFFFF

_SPLITS = (('a_q', A_QK), ('a_k', A_QK), ('a_v', A_WIDTH), ('a_i', A_HEADS), ('a_f', A_HEADS),
           ('a_o', A_WIDTH), ('a_z', A_WIDTH), ('b_q', B_WIDTH), ('b_k', B_KV * B_HD), ('b_v', B_KV * B_HD),
           ('b_iq', IDX_HEADS * IDX_DIM), ('b_ik', IDX_DIM), ('b_iw', IDX_HEADS), ('b_z', B_WIDTH),
           ('g_a', 1024), ('g_b', 1024))

P_QK = 0
P_AV = 1024
P_AO = 2048
P_AZ = 3072
P_BQ = 4096
P_BK = 5120
P_BV = 5376
P_IQ = 5632
P_MISC = 6144
P_BZ = 6272
P_GA = 7296
P_GB = 8320
P_COLS = 9344
MISC_IW = 8
MISC_IK = 64


def _sigmoid(x):
    return 1.0 / (1.0 + jnp.exp(-x))


def _silu(x):
    return x * _sigmoid(x)


def _dot(a, b):
    return jnp.dot(a, b, preferred_element_type=F32)


def _dot_nt(a, b):
    return lax.dot_general(a, b, (((1,), (1,)), ((), ())), preferred_element_type=F32)


def _transpose(x):
    r, c = x.shape
    rows = []
    for j in range(c // LANES):
        rows.append(jnp.concatenate([x[i * LANES:(i + 1) * LANES, j * LANES:(j + 1) * LANES].T
                                     for i in range(r // LANES)], axis=1))
    return jnp.concatenate(rows, axis=0)


def _pack_w_in(w_in, d_model):
    offs = {}
    c = 0
    for name, n in _SPLITS:
        offs[name] = (c, c + n)
        c += n

    def col(name):
        lo, hi = offs[name]
        return w_in[:, lo:hi]

    z = lambda n: jnp.zeros((d_model, n), w_in.dtype)
    misc = jnp.concatenate([col('a_i'), col('a_f'), col('b_iw'), z(MISC_IK - 16), col('b_ik')], axis=1)
    segs = (w_in[:, offs['a_q'][0]:offs['a_v'][1]], w_in[:, offs['a_o'][0]:offs['b_iq'][1]], misc,
            w_in[:, offs['b_z'][0]:offs['g_b'][1]])
    assert [s.shape[1] for s in segs] == [P_AO - P_QK, P_MISC - P_AO, P_BZ - P_MISC, P_COLS - P_BZ]
    return tuple(s.astype(BF16) for s in segs)


def _inproj_kernel(x_ref, g_ref, w0_ref, w1_ref, w2_ref, w3_ref, cw_ref, cb_ref, mb_ref, *refs, names, prompt,
                   tiles_per_seq):
    o = dict(zip(names, refs))
    hist_ref = refs[len(names)]
    tm = x_ref.shape[0]
    x = x_ref[...]
    ms = jnp.mean(x * x, axis=-1, keepdims=True)
    xn = (x * lax.rsqrt(ms + EPS) * g_ref[...]).astype(BF16)

    segments = ((P_QK, w0_ref), (P_AO, w1_ref), (P_MISC, w2_ref), (P_BZ, w3_ref))

    def proj(lo, hi):
        start, ref = [s for s in segments if s[0] <= lo][-1]
        return _dot(xn, ref[:, lo - start:hi - start])

    qk = proj(P_QK, P_QK + 2 * A_QK)
    if prompt:
        o['tail'][0] = qk[tm - 8:tm, :]

        @pl.when(pl.program_id(0) % tiles_per_seq == 0)
        def _():
            hist_ref[0:8, :] = jnp.zeros((8, 2 * A_QK), F32)

        hist_ref[8:8 + tm, :] = qk
        cw = cw_ref[...]
        y = cb_ref[...] + cw[0:1, :] * hist_ref[5:5 + tm, :]
        y = y + cw[1:2, :] * hist_ref[6:6 + tm, :]
        y = y + cw[2:3, :] * hist_ref[7:7 + tm, :]
        y = y + cw[3:4, :] * qk
        act = _silu(y)
        o['q'][...] = (act[:, :A_QK] * Q_SCALE).astype(BF16)
        o['k'][...] = act[:, A_QK:].astype(BF16)
        hist_ref[0:8, :] = qk[tm - 8:tm, :]
    else:
        o['q'][...] = qk[:, :A_QK]
        o['k'][...] = qk[:, A_QK:]

    o['av'][...] = proj(P_AV, P_AV + A_WIDTH).astype(BF16)
    o['ga'][...] = (_sigmoid(proj(P_AO, P_AO + A_WIDTH)) * _silu(proj(P_AZ, P_AZ + A_WIDTH))).astype(BF16)
    bq = proj(P_BQ, P_BQ + B_WIDTH) * C_EXP
    iq = proj(P_IQ, P_IQ + IDX_HEADS * IDX_DIM)
    kf = proj(P_BK, P_BK + B_KV * B_HD)
    vf = proj(P_BV, P_BV + B_KV * B_HD)
    for kv in range(B_KV):
        o['kf'][pl.ds(kv, tm, stride=B_KV), :] = kf[:, kv * B_HD:(kv + 1) * B_HD]
        o['vf'][pl.ds(kv, tm, stride=B_KV), :] = vf[:, kv * B_HD:(kv + 1) * B_HD]
    if prompt:
        for r in range(tm // PAGE):
            rows = slice(r * PAGE, (r + 1) * PAGE)
            o['bq'][r] = _transpose(bq[rows, :]).astype(BF16)
            o['iq'][r] = _transpose(iq[rows, :]).astype(BF16)
        ones = jnp.ones((VT_ROWS - B_HD, 2 * PAGE), BF16)
        for p in range(tm // (2 * PAGE)):
            rows = slice(p * 2 * PAGE, (p + 1) * 2 * PAGE)
            parts = []
            for g in range(B_KV):
                parts += [_transpose(vf[rows, g * B_HD:(g + 1) * B_HD]).astype(BF16), ones]
            o['vt'][p] = jnp.concatenate(parts, axis=0)
        o['kb'][...] = kf.astype(BF16)
    else:
        o['bq'][...] = bq.astype(BF16)
        o['iq'][...] = iq.astype(BF16)

    raw = proj(P_MISC, P_MISC + LANES) + mb_ref[...]
    lane = lax.broadcasted_iota(I32, raw.shape, 1)
    logsig = jnp.minimum(raw, 0.0) - jnp.log(1.0 + jnp.exp(-jnp.abs(raw)))
    o['misc'][...] = jnp.where((lane >= A_HEADS) & (lane < 2 * A_HEADS), logsig,
                               jnp.where((lane >= MISC_IW) & (lane < MISC_IW + IDX_HEADS),
                                         raw * (IW_SCALE * IDX_SCALE), raw))
    if prompt:
        o['miscb'][...] = jnp.where(lane >= MISC_IK, raw, 0.0).astype(BF16)

    o['zb'][...] = _silu(proj(P_BZ, P_BZ + B_WIDTH)).astype(BF16)
    o['sga'][...] = _sigmoid(proj(P_GA, P_GA + 1024)).astype(BF16)
    o['sgb'][...] = _sigmoid(proj(P_GB, P_GB + 1024)).astype(BF16)


def _inproj(x2d, norm_pre, w_packed, conv_w, conv_b, misc_bias, *, prompt, seq_len, tm):
    m, d = x2d.shape
    nt = m // tm
    rows = lambda w, dt: (jax.ShapeDtypeStruct((m, w), dt), pl.BlockSpec((tm, w), lambda i: (i, 0)))
    tiles = lambda n, r, c, dt: (jax.ShapeDtypeStruct((m // n, r, c), dt),
                                 pl.BlockSpec((tm // n, r, c), lambda i: (i, 0, 0)))
    kv_rows = (jax.ShapeDtypeStruct((B_KV * m, B_HD), F32), pl.BlockSpec((B_KV * tm, B_HD), lambda i: (i, 0)))
    outs = dict(q=rows(A_QK, BF16 if prompt else F32), k=rows(A_QK, BF16 if prompt else F32), av=rows(A_WIDTH, BF16),
                ga=rows(A_WIDTH, BF16), kf=kv_rows, vf=kv_rows, misc=rows(LANES, F32),
                zb=rows(B_WIDTH, BF16), sga=rows(d, BF16), sgb=rows(d, BF16))
    if prompt:
        outs.update(tail=tiles(tm, 8, 2 * A_QK, F32), bq=tiles(PAGE, B_WIDTH, PAGE, BF16),
                    iq=tiles(PAGE, IDX_HEADS * IDX_DIM, PAGE, BF16),
                    vt=tiles(2 * PAGE, B_KV * VT_ROWS, 2 * PAGE, BF16), kb=rows(B_KV * B_HD, BF16),
                    miscb=rows(LANES, BF16))
    else:
        outs.update(bq=rows(B_WIDTH, BF16), iq=rows(IDX_HEADS * IDX_DIM, BF16))
    names = tuple(outs)
    const = lambda shape: pl.BlockSpec(shape, lambda i: (0,) * len(shape))
    res = pl.pallas_call(
        functools.partial(_inproj_kernel, names=names, prompt=prompt, tiles_per_seq=max(seq_len // tm, 1)),
        grid=(nt,),
        in_specs=[pl.BlockSpec((tm, d), lambda i: (i, 0)), const((1, d))]
        + [pl.BlockSpec(seg.shape, lambda i: (0, 0), pipeline_mode=pl.Buffered(1)) for seg in w_packed]
        + [const((CONV_W, 2 * A_QK)), const((1, 2 * A_QK)), const((1, LANES))],
        out_specs=[outs[n][1] for n in names],
        out_shape=[outs[n][0] for n in names],
        scratch_shapes=[pltpu.VMEM((8 + tm, 2 * A_QK), F32)],
        compiler_params=pltpu.CompilerParams(dimension_semantics=("arbitrary",), vmem_limit_bytes=VMEM_LIMIT),
        name="inproj_prompt" if prompt else "inproj_decode",
    )(x2d, norm_pre, *w_packed, conv_w, conv_b, misc_bias)
    return dict(zip(names, res))


def _outproj_kernel(ya_ref, yb_ref, sga_ref, sgb_ref, x_ref, pe_ref, wa_ref, wb_ref, wo_ref, wg_ref, wp_ref,
                    g_ref, o_ref):
    merged = sga_ref[...] * _dot(ya_ref[...], wa_ref[...]) + sgb_ref[...] * _dot(yb_ref[...], wb_ref[...])
    z = _dot(merged.astype(BF16), wo_ref[...])
    ms = jnp.mean(z * z, axis=-1, keepdims=True)
    x1 = x_ref[...] + z * lax.rsqrt(ms + EPS) * g_ref[...]
    gate = _sigmoid(_dot(x1.astype(BF16), wg_ref[...]))
    o_ref[...] = x1 + _dot(pe_ref[...].astype(BF16), wp_ref[...]) * gate


def _outproj(ya, yb, sga, sgb, x2d, pe2d, wa, wb, wo, wg, wp, norm_post, *, tm):
    m, d = x2d.shape
    pd = pe2d.shape[1]
    row = lambda w: pl.BlockSpec((tm, w), lambda i: (i, 0))
    const = lambda shape: pl.BlockSpec(shape, lambda i: (0,) * len(shape))
    return pl.pallas_call(
        _outproj_kernel,
        grid=(m // tm,),
        in_specs=[row(A_WIDTH), row(B_WIDTH), row(d), row(d), row(d), row(pd),
                  const((A_WIDTH, d)), const((B_WIDTH, d)), const((d, d)), const((d, d)), const((pd, d)),
                  const((1, d))],
        out_specs=row(d),
        out_shape=jax.ShapeDtypeStruct((m, d), F32),
        compiler_params=pltpu.CompilerParams(dimension_semantics=("arbitrary",), vmem_limit_bytes=VMEM_LIMIT),
        name="outproj",
    )(ya, yb, sga, sgb, x2d, pe2d, wa, wb, wo, wg, wp, norm_post)


def _cumsum_rows(x):
    n = x.shape[0]
    row = lax.broadcasted_iota(I32, x.shape, 0)
    s = 1
    while s < n:
        x = x + jnp.where(row >= s, pltpu.roll(x, s, axis=0), 0.0)
        s *= 2
    return x


def _lane_col(x, lane_iota, idx):
    return jnp.sum(jnp.where(lane_iota == idx, x, 0.0), axis=1, keepdims=True)


def _mlstm_prompt_kernel(q_ref, k_ref, v_ref, misc_ref, ga_ref, na_ref, ya_ref, c_out_ref, nm_out_ref,
                         c_ref, n_ref, m_ref):
    c_idx = pl.program_id(1)
    nb, L = q_ref.shape[0], q_ref.shape[1]

    @pl.when(c_idx == 0)
    def _():
        c_ref[...] = jnp.zeros(c_ref.shape, F32)
        n_ref[...] = jnp.zeros(n_ref.shape, F32)
        m_ref[...] = jnp.zeros(m_ref.shape, F32)

    lane = lax.broadcasted_iota(I32, (L, LANES), 1)
    t_i = lax.broadcasted_iota(I32, (L, L), 0)
    s_i = lax.broadcasted_iota(I32, (L, L), 1)
    causal = s_i <= t_i

    for sq in range(nb):
        misc = misc_ref[sq]
        cs = _cumsum_rows(misc)
        b_al = pltpu.roll(cs, LANES - A_HEADS, axis=1)
        xa = jnp.where(lane < A_HEADS, misc - b_al, cs)
        xt = _transpose(xa)

        for h in range(A_HEADS):
            a_col = _lane_col(xa, lane, h)
            b_col = _lane_col(xa, lane, A_HEADS + h)
            a_row = xt[h:h + 1, :]
            m_prev = m_ref[sq, h:h + 1, 0:1]
            logit = jnp.where(causal, a_row, -jnp.inf)
            g_col = jnp.maximum(m_prev, jnp.max(logit, axis=1, keepdims=True))
            dmat = jnp.exp(logit - g_col)
            w_inter = jnp.exp(m_prev - g_col)
            qh = q_ref[sq, :, h * A_DQK:(h + 1) * A_DQK]
            kh = k_ref[sq, :, h * A_DQK:(h + 1) * A_DQK]
            vh = v_ref[sq, :, h * A_DV:(h + 1) * A_DV]
            c_h = c_ref[sq, h]
            n_h = n_ref[sq, h:h + 1, :]
            s = _dot_nt(qh, kh) * dmat
            num = _dot(s.astype(BF16), vh) + w_inter * _dot_nt(qh, c_h.astype(BF16))
            den = (jnp.sum(s, axis=1, keepdims=True)
                   + w_inter * jnp.sum(qh.astype(F32) * n_h, axis=1, keepdims=True))
            m_t = b_col + g_col
            hh = num / jnp.maximum(jnp.abs(den), jnp.exp(-m_t))
            hn = hh * lax.rsqrt(jnp.mean(hh * hh, axis=1, keepdims=True) + EPS)
            sl = slice(h * A_DV, (h + 1) * A_DV)
            ya_ref[sq, :, sl] = (ga_ref[sq, :, sl] * (hn * na_ref[:, sl])).astype(ya_ref.dtype)

            g_last = g_col[L - 1:L, :]
            w_prev = jnp.exp(m_prev - g_last)
            w_s = jnp.exp(a_col - g_last)
            vw = vh.astype(F32) * w_s
            c_ref[sq, h] = w_prev * c_h + _dot(_transpose(vw).astype(BF16), kh)
            n_ref[sq, h:h + 1, :] = w_prev * n_h + jnp.sum(kh.astype(F32) * w_s, axis=0, keepdims=True)
            m_ref[sq, h:h + 1, :] = jnp.broadcast_to(b_col[L - 1:L, :] + g_last, (1, LANES))

    @pl.when(c_idx == pl.num_programs(1) - 1)
    def _():
        c_out_ref[...] = c_ref[...]
        nm_out_ref[:, 0:A_HEADS, :] = n_ref[:, 0:A_HEADS, :]
        nm_out_ref[:, A_HEADS:2 * A_HEADS, :] = m_ref[:, 0:A_HEADS, :]


def _mlstm_prompt(q, k, v, misc, ga, norm_a, *, batch, seq_len, chunk):
    nc = seq_len // chunk
    nb = 1
    seq3 = lambda a: a.reshape(batch, seq_len, a.shape[-1])
    blk = lambda w: pl.BlockSpec((nb, chunk, w), lambda b, c: (b, c, 0))
    ya, c_out, nm_out = pl.pallas_call(
        _mlstm_prompt_kernel,
        grid=(batch // nb, nc),
        in_specs=[blk(A_QK), blk(A_QK), blk(A_WIDTH), blk(LANES), blk(A_WIDTH),
                  pl.BlockSpec((1, A_WIDTH), lambda b, c: (0, 0))],
        out_specs=[blk(A_WIDTH),
                   pl.BlockSpec((nb, A_HEADS, A_DV, A_DQK), lambda b, c: (b, 0, 0, 0)),
                   pl.BlockSpec((nb, 8, LANES), lambda b, c: (b, 0, 0))],
        out_shape=[jax.ShapeDtypeStruct((batch, seq_len, A_WIDTH), BF16),
                   jax.ShapeDtypeStruct((batch, A_HEADS, A_DV, A_DQK), F32),
                   jax.ShapeDtypeStruct((batch, 8, LANES), F32)],
        scratch_shapes=[pltpu.VMEM((nb, A_HEADS, A_DV, A_DQK), F32), pltpu.VMEM((nb, 8, LANES), F32),
                        pltpu.VMEM((nb, 8, LANES), F32)],
        compiler_params=pltpu.CompilerParams(dimension_semantics=("arbitrary", "arbitrary"),
                                             vmem_limit_bytes=VMEM_LIMIT),
        name="mlstm_prompt",
    )(seq3(q), seq3(k), seq3(v), seq3(misc), seq3(ga), norm_a)
    return ya.reshape(batch * seq_len, A_WIDTH), c_out, nm_out


def _key_to_f32(key):
    return pltpu.bitcast(jnp.where(key < 0, key ^ 0x7FFFFFFF, key), F32)


def _radix_threshold(count_ge, shape, topk):
    def bit_body(i, carry):
        u, n_u = carry
        cand_u = u | jnp.left_shift(jnp.int32(1), 31 - i)
        cnt = count_ge(_key_to_f32(cand_u ^ INT_MIN))
        ok = cnt >= topk
        return jnp.where(ok, cand_u, u), jnp.where(ok, cnt, n_u)

    u, n_u = lax.fori_loop(0, 32, bit_body, (jnp.zeros(shape, I32), jnp.zeros(shape, I32)))
    key = jnp.maximum(u ^ INT_MIN, KEY_NEG_INF)
    return _key_to_f32(key), _key_to_f32(key + 1), n_u


def _refine_in_bin(count_ge, thr, thr_up, topk):
    lo, hi = thr, thr_up
    for _ in range(6):
        mid = lo + 0.5 * (hi - lo)
        ok = count_ge(mid) >= topk
        lo = jnp.where(ok, mid, lo)
        hi = jnp.where(ok, hi, mid)
    return lo


def _dsa_prompt_kernel(bq_ref, iq_ref, misc_ref, zb_ref, ikb_ref, kb_ref, vt_ref, yb_ref,
                       sc_ref, s_ref, s2_ref, p_ref, p2_ref, al_ref, al2_ref, acc_ref, m_ref, *, topk, nq):
    j = pl.program_id(1)
    T = PAGE
    KT = 2 * PAGE
    nkt = j // 2 + 1

    iq_t = iq_ref[0]
    zeros_pad = jnp.zeros((LANES - IDX_DIM, T), BF16)
    iq_pad = [jnp.concatenate([zeros_pad, iq_t[h * IDX_DIM:(h + 1) * IDX_DIM, :]], axis=0)
              for h in range(IDX_HEADS)]
    IDX_GROUP = 2
    iq_wide = [jnp.concatenate(iq_pad[h:h + IDX_GROUP], axis=1) for h in range(0, IDX_HEADS, IDX_GROUP)]
    misc_t = _transpose(misc_ref[...])
    q_t = bq_ref[0]
    q_grp = [jnp.concatenate([q_t[(g * B_GROUP + hh) * B_HD:(g * B_GROUP + hh + 1) * B_HD, :]
                              for hh in range(B_GROUP)], axis=1) for g in range(B_KV)]

    row_i = lax.broadcasted_iota(I32, (KT, T), 0)
    lane_i = lax.broadcasted_iota(I32, (KT, T), 1)

    last_pair = nq // 2 - 1

    def idx_body(c, carry):
        ikc = ikb_ref[0, c]
        acc = jnp.zeros((KT, T), F32)
        for h0 in range(0, IDX_HEADS, IDX_GROUP):
            prod = _dot(ikc, iq_wide[h0 // IDX_GROUP])
            for i in range(IDX_GROUP):
                r = jnp.maximum(prod[:, i * T:(i + 1) * T], 0.0)
                acc = acc + r * misc_t[MISC_IW + h0 + i:MISC_IW + h0 + i + 1, :]
        vis = (c * KT + row_i) <= (j * T + lane_i)
        sc_ref[c] = jnp.where(vis, acc, -jnp.inf)
        return carry

    lax.fori_loop(0, nkt, idx_body, 0)

    def count(pred):
        def body(c, cnt):
            m = pred(sc_ref[c], c).reshape(KT // 32, 4, 8, T)
            for v in range(KT // 32):
                cnt = jnp.where(m[v], cnt + 1, cnt)
            return cnt
        cnt4 = lax.fori_loop(0, nkt, body, jnp.zeros((4, 8, T), I32))
        return jnp.sum(jnp.sum(cnt4, axis=0), axis=0, keepdims=True)

    count_ge = lambda t: count(lambda sc, c: sc >= t)
    thr0, thr_up, n_ge = _radix_threshold(count_ge, (1, T), topk)

    def tie_search(_):
        thr_t = _refine_in_bin(count_ge, thr0, thr_up, topk)
        need = topk - count(lambda sc, c: sc > thr_t)
        nbits = (nq * T - 1).bit_length()
        def pbody(i, p):
            cand = p | jnp.left_shift(jnp.int32(1), nbits - 1 - i)
            cnt = count(lambda sc, c: (sc == thr_t) & ((c * KT + row_i) < cand))
            return jnp.where(cnt < need, cand, p)
        return thr_t, lax.fori_loop(0, nbits, pbody, jnp.zeros((1, T), I32))

    has_tie = jnp.max(jnp.where((n_ge > topk) & (thr0 > -jnp.inf), 1, 0)) > 0
    thr, p_lim = lax.cond(has_tie, tie_search, lambda _: (thr0, jnp.full((1, T), 2 ** 30, I32)), 0)

    acc_ref[...] = jnp.zeros(acc_ref.shape, F32)
    m_ref[...] = jnp.full(m_ref.shape, NEG_BIG, F32)

    def qk_scores(c, dst_ref):
        kc = kb_ref[0, jnp.minimum(c, last_pair)]
        for g in range(B_KV):
            dst_ref[g] = _dot(kc[:, g * B_HD:(g + 1) * B_HD], q_grp[g])

    def softmax(c, src_ref, p_dst, al_dst):
        sc = sc_ref[c]
        pos = c * KT + row_i
        sel = ((sc > thr) | ((sc == thr) & (pos <= p_lim))) & (sc > -jnp.inf)
        bias = jnp.where(sel, 0.0, NEG_BIG)
        for h in range(B_HEADS):
            g, hh = divmod(h, B_GROUP)
            s = src_ref[g, :, hh * T:(hh + 1) * T] + bias
            m_old = m_ref[h:h + 1, :]
            m_new = jnp.maximum(m_old, jnp.max(s, axis=0, keepdims=True))
            al_dst[h:h + 1, :] = jnp.exp2(m_old - m_new)
            p_dst[h // 2, :, (h % 2) * T:(h % 2 + 1) * T] = jnp.exp2(s - m_new).astype(BF16)
            m_ref[h:h + 1, :] = m_new

    def pv_update(c, p_src, al_src):
        vtc = vt_ref[0, c]
        for h0 in range(0, B_HEADS, 2):
            g = h0 // B_GROUP
            pv = _dot(vtc[g * VT_ROWS:(g + 1) * VT_ROWS, :], p_src[h0 // 2])
            for i, h in enumerate((h0, h0 + 1)):
                acc_ref[h] = acc_ref[h] * al_src[h:h + 1, :] + pv[:, i * T:(i + 1) * T]

    p2_ref[...] = jnp.zeros(p2_ref.shape, BF16)
    al2_ref[...] = jnp.ones(al2_ref.shape, F32)
    qk_scores(0, s_ref)

    def att_body(i, carry):
        c0 = 2 * i
        qk_scores(c0 + 1, s2_ref)
        softmax(c0, s_ref, p_ref, al_ref)
        pv_update(jnp.maximum(c0 - 1, 0), p2_ref, al2_ref)

        @pl.when(c0 + 1 < nkt)
        def _():
            qk_scores(c0 + 2, s_ref)
            softmax(c0 + 1, s2_ref, p2_ref, al2_ref)
            pv_update(c0, p_ref, al_ref)

        return carry

    lax.fori_loop(0, (nkt + 1) // 2, att_body, 0)

    @pl.when(nkt % 2 == 1)
    def _():
        pv_update(nkt - 1, p_ref, al_ref)

    @pl.when(nkt % 2 == 0)
    def _():
        pv_update(nkt - 1, p2_ref, al2_ref)

    for h in range(B_HEADS):
        o_t = acc_ref[h, 0:B_HD, :] / acc_ref[h, B_HD:B_HD + 1, :]
        sl = slice(h * B_HD, (h + 1) * B_HD)
        yb_ref[:, sl] = (_transpose(o_t) * zb_ref[:, sl]).astype(yb_ref.dtype)


def _dsa_prompt(bq, iq, misc, zb, ikb, kb, vt, *, batch, seq_len, topk):
    nq = seq_len // PAGE
    kt = 2 * PAGE
    scores = pltpu.VMEM((B_KV, kt, B_GROUP * PAGE), F32)
    weights = pltpu.VMEM((B_HEADS // 2, kt, 2 * PAGE), BF16)
    rescale = pltpu.VMEM((B_HEADS, PAGE), F32)
    row = lambda w: pl.BlockSpec((PAGE, w), lambda b, j: (b * nq + j, 0))
    tile_t = lambda r: pl.BlockSpec((1, r, PAGE), lambda b, j: (b * nq + j, 0, 0))
    per_b = lambda a: pl.BlockSpec((1,) + a.shape[1:], lambda b, j: (b, 0, 0, 0))
    return pl.pallas_call(
        functools.partial(_dsa_prompt_kernel, topk=topk, nq=nq),
        grid=(batch, nq),
        in_specs=[tile_t(B_WIDTH), tile_t(IDX_HEADS * IDX_DIM), row(LANES), row(B_WIDTH),
                  per_b(ikb), per_b(kb), per_b(vt)],
        out_specs=row(B_WIDTH),
        out_shape=jax.ShapeDtypeStruct((batch * seq_len, B_WIDTH), BF16),
        scratch_shapes=[pltpu.VMEM((nq // 2, kt, PAGE), F32), scores, scores, weights, weights, rescale, rescale,
                        pltpu.VMEM((B_HEADS, VT_ROWS, PAGE), F32),
                        pltpu.VMEM((B_HEADS, PAGE), F32)],
        compiler_params=pltpu.CompilerParams(dimension_semantics=("arbitrary", "arbitrary"),
                                             vmem_limit_bytes=VMEM_LIMIT),
        name="dsa_prompt",
    )(bq, iq, misc, zb, ikb, kb, vt)


def _prompt_path(x, pe, w, w_packed, misc_bias):
    batch, seq_len, d = x.shape
    m = batch * seq_len
    x2d = x.reshape(m, d)
    tm = min(256, seq_len)
    a = _inproj(x2d, w['norm_pre'], w_packed, w['conv_w'], w['conv_b'], misc_bias,
                prompt=True, seq_len=seq_len, tm=tm)
    chunk = min(256, seq_len)
    ya, c_out, nm_out = _mlstm_prompt(a['q'], a['k'], a['av'], a['misc'], a['ga'], w['norm_a'],
                                      batch=batch, seq_len=seq_len, chunk=chunk)
    nk2 = seq_len // (2 * PAGE)
    ikb = a['miscb'].reshape(batch, nk2, 2 * PAGE, LANES)
    kb = a['kb'].reshape(batch, nk2, 2 * PAGE, B_KV * B_HD)
    vt = a['vt'].reshape(batch, nk2, B_KV * VT_ROWS, 2 * PAGE)
    topk = min(TOPK_MAX, seq_len // 4)
    yb = _dsa_prompt(a['bq'], a['iq'], a['misc'], a['zb'], ikb, kb, vt, batch=batch, seq_len=seq_len, topk=topk)
    y = _outproj(ya, yb, a['sga'], a['sgb'], x2d, pe.reshape(m, -1), w['wa'], w['wb'], w['wo'], w['wg'], w['wp'],
                 w['norm_post'], tm=min(256, m))
    tiles_per_seq = seq_len // tm
    tail = a['tail'].reshape(batch, tiles_per_seq, 8, 2 * A_QK)[:, -1, 8 - (CONV_W - 1):, :]
    return dict(
        y=y.reshape(batch, seq_len, d),
        k=a['kf'].reshape(1, batch, seq_len, B_KV, B_HD),
        v=a['vf'].reshape(1, batch, seq_len, B_KV, B_HD),
        ik=a['misc'][:, MISC_IK:].reshape(1, batch, seq_len, IDX_DIM),
        C=c_out[None],
        n=nm_out[None, :, 0:A_HEADS, :],
        m=nm_out[None, :, A_HEADS:2 * A_HEADS, 0],
        conv=tail[None],
    )


def _mlstm_sample_kernel(q_ref, k_ref, conv0_ref, cw_ref, cb_ref, v_ref, misc_ref, ga_ref, na_ref, n0_ref, m0_ref,
                         c0_ref, ya_ref, c_out_ref, n_out_ref, m_out_ref,
                         qs_ref, ks_ref, cq_ref, wprev_ref, vwt_ref, numi_ref, den_ref, wint_ref, enm_ref, *, db, ds):
    b = pl.program_id(0)
    ms = ds * db
    rows = lambda t: slice(t * db, (t + 1) * db)

    @pl.when(b == 0)
    def _():
        cw = cw_ref[...]
        u = [conv0_ref[j] for j in range(CONV_W - 1)]
        u += [jnp.concatenate([q_ref[rows(t), :], k_ref[rows(t), :]], axis=1) for t in range(ds)]
        for t in range(ds):
            y = cb_ref[...] + cw[0:1, :] * u[t]
            for jj in range(1, CONV_W):
                y = y + cw[jj:jj + 1, :] * u[t + jj]
            act = _silu(y)
            qs_ref[rows(t), :] = act[:, :A_QK] * Q_SCALE
            ks_ref[rows(t), :] = act[:, A_QK:]
        cq_ref[...] = jnp.zeros(cq_ref.shape, F32)

        lane = lax.broadcasted_iota(I32, (db, LANES), 1)
        for h in range(A_HEADS):
            i_t = [_lane_col(misc_ref[rows(t), :], lane, h) for t in range(ds)]
            lf_t = [_lane_col(misc_ref[rows(t), :], lane, A_HEADS + h) for t in range(ds)]
            b_t = [lf_t[0]]
            for t in range(1, ds):
                b_t.append(b_t[-1] + lf_t[t])
            a_t = [i_t[t] - b_t[t] for t in range(ds)]
            m_prev = jnp.max(m0_ref[h], axis=1, keepdims=True)
            g_t = [jnp.maximum(m_prev, a_t[0])]
            for t in range(1, ds):
                g_t.append(jnp.maximum(g_t[-1], a_t[t]))
            hq = slice(h * A_DQK, (h + 1) * A_DQK)
            hv = slice(h * A_DV, (h + 1) * A_DV)
            qh = [qs_ref[rows(t), hq] for t in range(ds)]
            kh = [ks_ref[rows(t), hq] for t in range(ds)]
            vh = [v_ref[rows(t), hv].astype(F32) for t in range(ds)]
            n0 = n0_ref[h]
            for t in range(ds):
                num = jnp.zeros((db, A_DV), F32)
                den = jnp.zeros((db, 1), F32)
                for s in range(t + 1):
                    w_ts = jnp.sum(qh[t] * kh[s], axis=1, keepdims=True) * jnp.exp(a_t[s] - g_t[t])
                    num = num + w_ts * vh[s]
                    den = den + w_ts
                w_inter = jnp.exp(m_prev - g_t[t])
                den = den + w_inter * jnp.sum(qh[t] * n0, axis=1, keepdims=True)
                numi_ref[rows(t), hv] = num
                den_ref[h, rows(t), :] = jnp.broadcast_to(den, (db, LANES))
                wint_ref[h, rows(t), :] = jnp.broadcast_to(w_inter, (db, LANES))
                enm_ref[h, rows(t), :] = jnp.broadcast_to(jnp.exp(-(b_t[t] + g_t[t])), (db, LANES))
            g_last = g_t[ds - 1]
            w_prev = jnp.exp(m_prev - g_last)
            wprev_ref[h] = jnp.broadcast_to(w_prev, (db, LANES))
            n_new = w_prev * n0
            vw = []
            for s in range(ds):
                w_s = jnp.exp(a_t[s] - g_last)
                n_new = n_new + w_s * kh[s]
                vw.append(vh[s] * w_s)
            n_out_ref[h] = n_new
            m_out_ref[h] = jnp.broadcast_to(b_t[ds - 1] + g_last, (db, LANES))
            vw_all = jnp.concatenate(vw, axis=0)
            if ms < LANES:
                vw_all = jnp.concatenate([vw_all, jnp.zeros((LANES - ms, A_DV), F32)], axis=0)
            vwt_ref[h] = _transpose(vw_all)

    mcols = vwt_ref.shape[2]
    row_i = lax.broadcasted_iota(I32, (ms, A_DQK), 0)
    col_i = lax.broadcasted_iota(I32, (A_DV, mcols), 1)
    row_mine = row_i == b
    col_mine = col_i == b
    for t in range(1, ds):
        row_mine = row_mine | (row_i == b + t * db)
        col_mine = col_mine | (col_i == b + t * db)
    for h in range(A_HEADS):
        hq = slice(h * A_DQK, (h + 1) * A_DQK)
        c0 = c0_ref[0, h]
        q_mine = jnp.where(row_mine, qs_ref[:, hq], 0.0).astype(BF16)
        cq_ref[h] = cq_ref[h] + _dot_nt(q_mine, c0.astype(BF16))
        vw_mine = jnp.where(col_mine, vwt_ref[h], 0.0).astype(BF16)
        k_all = ks_ref[:, hq]
        if mcols > ms:
            k_all = jnp.concatenate([k_all, jnp.zeros((mcols - ms, A_DQK), F32)], axis=0)
        c_out_ref[0, h] = wprev_ref[h, pl.ds(b, 1), :] * c0 + _dot(vw_mine, k_all.astype(BF16))

    @pl.when(b == pl.num_programs(0) - 1)
    def _():
        for h in range(A_HEADS):
            hv = slice(h * A_DV, (h + 1) * A_DV)
            num = numi_ref[:, hv] + wint_ref[h][:, 0:1] * cq_ref[h]
            hh = num / jnp.maximum(jnp.abs(den_ref[h][:, 0:1]), enm_ref[h][:, 0:1])
            hn = hh * lax.rsqrt(jnp.mean(hh * hh, axis=1, keepdims=True) + EPS)
            ya_ref[:, hv] = (ga_ref[:, hv] * (hn * na_ref[:, hv])).astype(ya_ref.dtype)


def _mlstm_sample(q_t, k_t, conv0_t, conv_w, conv_b, v_t, misc_t, ga_t, norm_a, n0_t, m0_bc, c0, *, db, ds):
    ms = db * ds
    mcols = max(ms, LANES)
    full = lambda a: pl.BlockSpec(a.shape, lambda b: (0,) * a.ndim)
    cblk = pl.BlockSpec((1, A_HEADS, A_DV, A_DQK), lambda b: (b, 0, 0, 0))
    hshape = jax.ShapeDtypeStruct((A_HEADS, db, LANES), F32)
    ins = (q_t, k_t, conv0_t, conv_w, conv_b, v_t, misc_t, ga_t, norm_a, n0_t, m0_bc)
    return pl.pallas_call(
        functools.partial(_mlstm_sample_kernel, db=db, ds=ds),
        grid=(db,),
        in_specs=[full(a) for a in ins] + [cblk],
        out_specs=[pl.BlockSpec((ms, A_WIDTH), lambda b: (0, 0)), cblk,
                   pl.BlockSpec((A_HEADS, db, LANES), lambda b: (0, 0, 0)),
                   pl.BlockSpec((A_HEADS, db, LANES), lambda b: (0, 0, 0))],
        out_shape=[jax.ShapeDtypeStruct((ms, A_WIDTH), BF16), jax.ShapeDtypeStruct(c0.shape, F32), hshape, hshape],
        scratch_shapes=[pltpu.VMEM((ms, A_QK), F32), pltpu.VMEM((ms, A_QK), F32),
                        pltpu.VMEM((A_HEADS, ms, A_DV), F32), pltpu.VMEM((A_HEADS, db, LANES), F32),
                        pltpu.VMEM((A_HEADS, A_DV, mcols), F32), pltpu.VMEM((ms, A_WIDTH), F32),
                        pltpu.VMEM((A_HEADS, ms, LANES), F32), pltpu.VMEM((A_HEADS, ms, LANES), F32),
                        pltpu.VMEM((A_HEADS, ms, LANES), F32)],
        compiler_params=pltpu.CompilerParams(dimension_semantics=("arbitrary",), vmem_limit_bytes=VMEM_LIMIT),
        name="mlstm_sample",
    )(*ins, c0)


def _page_ring(pt_ref, pages_per_step, streams):
    b, g = pl.program_id(0), pl.program_id(1)
    nb, ng = pl.num_programs(0), pl.num_programs(1)
    step = b * ng + g
    slot = lax.rem(step, 2)

    def copies(bb, gg, sl):
        return [pltpu.make_async_copy(hbm.at[pt_ref[bb, gg * pages_per_step + i]], buf.at[sl, i], sem.at[sl])
                for hbm, buf, sem in streams for i in range(pages_per_step)]

    def start_all(cps):
        for n, cp in enumerate(cps):
            cp.start(priority=n % 2)

    @pl.when(step == 0)
    def _():
        start_all(copies(0, 0, 0))

    @pl.when(step + 1 < nb * ng)
    def _():
        wrap = g + 1 == ng
        start_all(copies(jnp.where(wrap, b + 1, b), jnp.where(wrap, 0, g + 1), 1 - slot))

    for cp in copies(b, g, slot):
        cp.wait()
    return slot


def _dsa_sample_scores_kernel(pt_ref, iq_ref, iw_ref, iknew_ref, cache_ref, sc_ref, page_buf, sem,
                              *, pages_per_step, n_pages, ds):
    g = pl.program_id(1)
    slot = _page_ring(pt_ref, pages_per_step, [(cache_ref, page_buf, sem)])
    page_refs = [page_buf.at[slot, i] for i in range(pages_per_step)]
    iq = iq_ref[0]
    iw = iw_ref[0]

    def scores(keys_t, visible):
        n = keys_t.shape[1] // PAGE
        r = jnp.maximum(_dot(iq, keys_t), 0.0) * jnp.concatenate([iw] * n, axis=1)
        sc = jnp.sum(r.reshape(8, IDX_HEADS, n * PAGE), axis=1)
        return jnp.where(visible, sc, -jnp.inf)

    t_all = lax.broadcasted_iota(I32, (8, pages_per_step * PAGE), 0)
    sc_step = scores(jnp.concatenate([r[...] for r in page_refs], axis=1).astype(BF16), t_all < ds)
    for i in range(pages_per_step):
        sc_ref[0, g * pages_per_step + i] = sc_step[:, i * PAGE:(i + 1) * PAGE]

    @pl.when(g == pl.num_programs(1) - 1)
    def _():
        t_i = lax.broadcasted_iota(I32, (8, PAGE), 0)
        s_i = lax.broadcasted_iota(I32, (8, PAGE), 1)
        sc_ref[0, n_pages] = scores(iknew_ref[0], (t_i < ds) & (s_i <= t_i))


def _dsa_sample_scores(page_table, iq_s, iw_bc, ik_new, cache_ik, *, ds, pages_per_step):
    db, n_pages = page_table.shape
    npg = n_pages // pages_per_step
    per_b = lambda a: pl.BlockSpec((1,) + a.shape[1:], lambda b, g, pt: (b,) + (0,) * (a.ndim - 1))
    return pl.pallas_call(
        functools.partial(_dsa_sample_scores_kernel, pages_per_step=pages_per_step, n_pages=n_pages, ds=ds),
        grid_spec=pltpu.PrefetchScalarGridSpec(
            num_scalar_prefetch=1, grid=(db, npg),
            in_specs=[per_b(iq_s), per_b(iw_bc), per_b(ik_new), pl.BlockSpec(memory_space=pl.ANY)],
            out_specs=pl.BlockSpec((1, n_pages + 1, 8, PAGE), lambda b, g, pt: (b, 0, 0, 0)),
            scratch_shapes=[pltpu.VMEM((2, pages_per_step, IDX_DIM, PAGE), F32), pltpu.SemaphoreType.DMA((2,))]),
        out_shape=jax.ShapeDtypeStruct((db, n_pages + 1, 8, PAGE), F32),
        compiler_params=pltpu.CompilerParams(dimension_semantics=("arbitrary", "arbitrary"),
                                             vmem_limit_bytes=VMEM_LIMIT),
        name="dsa_sample_scores",
    )(page_table, iq_s, iw_bc, ik_new, cache_ik)


def _dsa_sample_search_kernel(sc_ref, bias_ref, *, topk):
    sc = sc_ref[...]
    nb, np1 = sc.shape[0], sc.shape[1]
    pos = lax.broadcasted_iota(I32, sc.shape, 1) * PAGE + lax.broadcasted_iota(I32, sc.shape, 3)

    def count(mask):
        per_lane = jnp.sum(mask.astype(I32), axis=1, keepdims=True)
        return jnp.sum(per_lane, axis=3, keepdims=True)

    count_ge = lambda t: count(sc_ref[...] >= t)
    thr0, thr_up, n_ge = _radix_threshold(count_ge, (nb, 1, 8, 1), topk)
    nbits = (np1 * PAGE - 1).bit_length()

    def tie_search(_):
        thr_t = _refine_in_bin(count_ge, thr0, thr_up, topk)
        need = topk - count(sc > thr_t)
        tied = sc == thr_t

        def pos_body(i, p):
            cand = p | jnp.left_shift(jnp.int32(1), nbits - 1 - i)
            cnt = count(tied & (pos < cand))
            return jnp.where(cnt < need, cand, p)
        return thr_t, lax.fori_loop(0, nbits, pos_body, jnp.zeros(thr0.shape, I32))

    has_tie = jnp.max(jnp.where((n_ge > topk) & (thr0 > -jnp.inf), 1, 0)) > 0
    thr, p_lim = lax.cond(has_tie, tie_search, lambda _: (thr0, jnp.full(thr0.shape, 2 ** 30, I32)), 0)
    sel = ((sc > thr) | ((sc == thr) & (pos <= p_lim))) & (sc > -jnp.inf)
    bias_ref[...] = jnp.where(sel, 0.0, NEG_BIG)


def _dsa_sample_search(scores, *, topk):
    db = scores.shape[0]
    nb = 8 if db % 8 == 0 else 1
    blk = pl.BlockSpec((nb,) + scores.shape[1:], lambda i: (i, 0, 0, 0))
    return pl.pallas_call(
        functools.partial(_dsa_sample_search_kernel, topk=topk),
        grid=(db // nb,),
        in_specs=[blk],
        out_specs=blk,
        out_shape=jax.ShapeDtypeStruct(scores.shape, F32),
        compiler_params=pltpu.CompilerParams(dimension_semantics=("arbitrary",), vmem_limit_bytes=VMEM_LIMIT),
        name="dsa_sample_search",
    )(scores)


def _dsa_sample_attend_kernel(pt_ref, q_ref, bias_ref, knew_ref, vnew_ref, zb_ref, ck_ref, cv_ref, o_ref,
                              k_buf, v_buf, k_sem, v_sem, acc_ref, m_ref, l_ref, *, pages_per_step, n_pages, ds):
    g = pl.program_id(1)
    slot = _page_ring(pt_ref, pages_per_step, [(ck_ref, k_buf, k_sem), (cv_ref, v_buf, v_sem)])
    k_refs = [k_buf.at[slot, i] for i in range(pages_per_step)]
    v_refs = [v_buf.at[slot, i] for i in range(pages_per_step)]
    rows_q = ds * B_HEADS
    q = q_ref[0]

    @pl.when(g == 0)
    def _():
        acc_ref[...] = jnp.zeros(acc_ref.shape, F32)
        m_ref[...] = jnp.full(m_ref.shape, NEG_BIG, F32)
        l_ref[...] = jnp.zeros(l_ref.shape, F32)

    def step(k_keys, v_keys, bias8):
        n = k_keys.shape[0]
        bias = jnp.concatenate([jnp.broadcast_to(bias8[t:t + 1, :], (B_HEADS, n)) for t in range(ds)], axis=0)
        s = _dot_nt(q, k_keys) + bias
        m_old = m_ref[...]
        m_new = jnp.maximum(m_old, jnp.max(s, axis=1, keepdims=True))
        alpha = jnp.exp2(m_old - m_new)
        p = jnp.exp2(s - m_new)
        l_ref[...] = alpha * l_ref[...] + jnp.sum(p, axis=1, keepdims=True)
        m_ref[...] = m_new
        acc_ref[...] = alpha * acc_ref[...] + _dot(p.astype(BF16), v_keys)

    def page_keys(ref):
        return jnp.concatenate([ref[pl.ds(kv, PAGE, stride=B_KV), :] for kv in range(B_KV)], axis=1).astype(BF16)

    step(jnp.concatenate([page_keys(r) for r in k_refs], axis=0),
         jnp.concatenate([page_keys(r) for r in v_refs], axis=0),
         jnp.concatenate([bias_ref[0, g * pages_per_step + i] for i in range(pages_per_step)], axis=1))

    @pl.when(g == pl.num_programs(1) - 1)
    def _():
        step(knew_ref[0], vnew_ref[0], bias_ref[0, n_pages])
        o = acc_ref[...] / l_ref[...]
        head = lax.broadcasted_iota(I32, (rows_q, B_HD), 0) & (B_HEADS - 1)
        o_sel = o[:, 0:B_HD]
        for kv in range(1, B_KV):
            o_sel = jnp.where(head >= kv * B_GROUP, o[:, kv * B_HD:(kv + 1) * B_HD], o_sel)
        o_ref[0] = (o_sel * zb_ref[0]).astype(o_ref.dtype)


def _dsa_sample_attend(page_table, q_bd, bias, k_new, v_new, zb_s, cache_k, cache_v, *, ds, pages_per_step):
    db, n_pages = page_table.shape
    npg = n_pages // pages_per_step
    rows_q = ds * B_HEADS
    kvw = B_KV * B_HD
    per_b = lambda a: pl.BlockSpec((1,) + a.shape[1:], lambda b, g, pt: (b,) + (0,) * (a.ndim - 1))
    any_space = pl.BlockSpec(memory_space=pl.ANY)
    page_buf = pltpu.VMEM((2, pages_per_step, PAGE * B_KV, B_HD), F32)
    return pl.pallas_call(
        functools.partial(_dsa_sample_attend_kernel, pages_per_step=pages_per_step, n_pages=n_pages, ds=ds),
        grid_spec=pltpu.PrefetchScalarGridSpec(
            num_scalar_prefetch=1, grid=(db, npg),
            in_specs=[per_b(q_bd), per_b(bias), per_b(k_new), per_b(v_new), per_b(zb_s), any_space, any_space],
            out_specs=pl.BlockSpec((1, rows_q, B_HD), lambda b, g, pt: (b, 0, 0)),
            scratch_shapes=[page_buf, page_buf, pltpu.SemaphoreType.DMA((2,)), pltpu.SemaphoreType.DMA((2,)),
                            pltpu.VMEM((rows_q, kvw), F32), pltpu.VMEM((rows_q, 1), F32),
                            pltpu.VMEM((rows_q, 1), F32)]),
        out_shape=jax.ShapeDtypeStruct((db, rows_q, B_HD), BF16),
        compiler_params=pltpu.CompilerParams(dimension_semantics=("arbitrary", "arbitrary"),
                                             vmem_limit_bytes=VMEM_LIMIT),
        name="dsa_sample_attend",
    )(page_table, q_bd, bias, k_new, v_new, zb_s, cache_k, cache_v)


def _sample_path(x, pe, cache_k, cache_v, cache_ik, page_table, c0, n0, m0, conv0, w, w_packed, misc_bias):
    db, ds, d = x.shape
    ms = db * ds
    assert ds >= CONV_W - 1 and ds <= 8
    n_pages = page_table.shape[1]
    a = _inproj(x.reshape(ms, d), w['norm_pre'], w_packed, w['conv_w'], w['conv_b'], misc_bias,
                prompt=False, seq_len=ms, tm=ms)

    tmaj = lambda z: jnp.swapaxes(z.reshape(db, ds, -1), 0, 1).reshape(ms, -1)
    ya_t, c_new, n_new, m_new = _mlstm_sample(
        tmaj(a['q']), tmaj(a['k']), jnp.swapaxes(conv0, 0, 1), w['conv_w'], w['conv_b'], tmaj(a['av']),
        tmaj(a['misc']), tmaj(a['ga']), w['norm_a'], jnp.swapaxes(n0, 0, 1),
        jnp.broadcast_to(jnp.swapaxes(m0, 0, 1)[:, :, None], (A_HEADS, db, LANES)), c0, db=db, ds=ds)
    ya = jnp.swapaxes(ya_t.reshape(ds, db, -1), 0, 1).reshape(ms, -1)

    topk = min(TOPK_MAX, (n_pages * PAGE + ds) // 4)
    pad_rows = lambda z: jnp.pad(z.reshape(db, ds, -1), ((0, 0), (0, PAGE - ds), (0, 0))).astype(BF16)
    pad_tok = lambda z: jnp.pad(z.reshape(db, ds, -1), ((0, 0), (0, 8 - ds), (0, 0)))
    iq_s = pad_tok(a['iq']).reshape(db, 8 * IDX_HEADS, IDX_DIM)
    iw_bc = jnp.broadcast_to(pad_tok(a['misc'][:, MISC_IW:MISC_IW + IDX_HEADS]).reshape(db, 8 * IDX_HEADS, 1),
                             (db, 8 * IDX_HEADS, LANES))
    largest_divisor = lambda cap: max(p for p in range(1, cap + 1) if n_pages % p == 0)
    pps_scores, pps_attend = largest_divisor(64), largest_divisor(64)
    scores = _dsa_sample_scores(page_table, iq_s, iw_bc, jnp.swapaxes(pad_rows(a['misc'][:, MISC_IK:]), 1, 2),
                                jnp.swapaxes(cache_ik, 1, 2), ds=ds, pages_per_step=pps_scores)
    bias = _dsa_sample_search(scores, topk=topk)
    q5 = a['bq'].reshape(db, ds * B_HEADS, 1, B_HD)
    kv_of_row = (jnp.arange(ds * B_HEADS) % B_HEADS) // B_GROUP
    q_bd = jnp.where((kv_of_row[None, :, None, None] == jnp.arange(B_KV)[None, None, :, None]), q5,
                     jnp.zeros((), BF16)).reshape(db, ds * B_HEADS, B_KV * B_HD)
    attn = _dsa_sample_attend(page_table, q_bd, bias, pad_rows(a['kf']), pad_rows(a['vf']),
                              a['zb'].reshape(db, ds * B_HEADS, B_HD),
                              cache_k.reshape(-1, PAGE * B_KV, B_HD), cache_v.reshape(-1, PAGE * B_KV, B_HD),
                              ds=ds, pages_per_step=pps_attend)
    yb = attn.reshape(ms, B_WIDTH)
    y = _outproj(ya, yb, a['sga'], a['sgb'], x.reshape(ms, d), pe.reshape(ms, -1), w['wa'], w['wb'], w['wo'],
                 w['wg'], w['wp'], w['norm_post'], tm=ms)
    qk_pre = jnp.concatenate([a['q'], a['k']], axis=1).reshape(db, ds, 2 * A_QK)
    return dict(
        y=y.reshape(db, ds, d),
        k=a['kf'].reshape(1, db, ds, B_KV, B_HD),
        v=a['vf'].reshape(1, db, ds, B_KV, B_HD),
        ik=a['misc'][:, MISC_IK:].reshape(1, db, ds, IDX_DIM),
        C=c_new[None],
        n=jnp.swapaxes(n_new, 0, 1)[None],
        m=jnp.swapaxes(m_new[:, :, 0], 0, 1)[None],
        conv=qk_pre[:, ds - (CONV_W - 1):, :][None],
    )


def _prep_weights(w_in, conv_w, conv_b, if_bias, norm_a, w_a_proj, w_b_proj, w_out, norm_pre, norm_post, w_ple,
                  w_ple_gate):
    d = w_in.shape[1]
    misc_bias = jnp.zeros((1, LANES), F32).at[0, :2 * A_HEADS].set(if_bias[0])
    w = dict(norm_pre=norm_pre[0][None], norm_post=norm_post[0][None], norm_a=norm_a[0][None],
             conv_w=conv_w[0], conv_b=conv_b[0][None],
             wa=w_a_proj[0].astype(BF16), wb=w_b_proj[0].astype(BF16), wo=w_out[0].astype(BF16),
             wg=w_ple_gate[0].astype(BF16), wp=w_ple[0].astype(BF16))
    return w, _pack_w_in(w_in[0], d), misc_bias


def kernel(x_prompt, x_sample, cache_k, cache_v, cache_idx_k, page_table, state_C, state_n, state_m, state_conv,
           p_prompt, p_sample, w_in, conv_w, conv_b, if_bias, norm_a, w_a_proj, w_b_proj, w_out, norm_pre,
           norm_post, w_ple, w_ple_gate):
    w, w_packed, misc_bias = _prep_weights(w_in, conv_w, conv_b, if_bias, norm_a, w_a_proj, w_b_proj, w_out,
                                           norm_pre, norm_post, w_ple, w_ple_gate)
    p = _prompt_path(x_prompt, p_prompt[0], w, w_packed, misc_bias)
    s = _sample_path(x_sample, p_sample[0], cache_k[0], cache_v[0], cache_idx_k[0], page_table, state_C[0],
                     state_n[0], state_m[0], state_conv[0], w, w_packed, misc_bias)
    names = ('k', 'v', 'ik', 'C', 'n', 'm', 'conv')
    return (p['y'], s['y']) + tuple(p[n] for n in names) + tuple(s[n] for n in names)
```

```python
import functools

import jax
import jax.numpy as jnp
from jax import lax
from jax.experimental import pallas as pl
from jax.experimental.pallas import tpu as pltpu

F32 = jnp.float32
BF16 = jnp.bfloat16
I32 = jnp.int32

EPS = 1e-6
A_HEADS = 4
A_DQK = 128
A_DV = 256
A_QK = A_HEADS * A_DQK
A_WIDTH = A_HEADS * A_DV
CONV_W = 4
B_HEADS = 8
B_KV = 2
B_HD = 128
B_GROUP = B_HEADS // B_KV
B_WIDTH = B_HEADS * B_HD
IDX_HEADS = 8
IDX_DIM = 64
TOPK_MAX = 256
PAGE = 128
LANES = 128
VMEM_LIMIT = 56 * 1024 * 1024

Q_SCALE = A_DQK ** -0.5
ATT_SCALE = B_HD ** -0.5
IDX_SCALE = IDX_DIM ** -0.5
IW_SCALE = IDX_HEADS ** -0.5
C_EXP = ATT_SCALE * 1.4426950408889634
VT_ROWS = B_HD + 16
NEG_BIG = -1e30
INT_MIN = -(2 ** 31)
KEY_NEG_INF = INT_MIN + 0x007FFFFF

_SPLITS = (('a_q', A_QK), ('a_k', A_QK), ('a_v', A_WIDTH), ('a_i', A_HEADS), ('a_f', A_HEADS),
           ('a_o', A_WIDTH), ('a_z', A_WIDTH), ('b_q', B_WIDTH), ('b_k', B_KV * B_HD), ('b_v', B_KV * B_HD),
           ('b_iq', IDX_HEADS * IDX_DIM), ('b_ik', IDX_DIM), ('b_iw', IDX_HEADS), ('b_z', B_WIDTH),
           ('g_a', 1024), ('g_b', 1024))

P_QK = 0
P_AV = 1024
P_AO = 2048
P_AZ = 3072
P_BQ = 4096
P_BK = 5120
P_BV = 5376
P_IQ = 5632
P_MISC = 6144
P_BZ = 6272
P_GA = 7296
P_GB = 8320
P_COLS = 9344
MISC_IW = 8
MISC_IK = 64


def _sigmoid(x):
    return 1.0 / (1.0 + jnp.exp(-x))


def _silu(x):
    return x * _sigmoid(x)


def _dot(a, b):
    return jnp.dot(a, b, preferred_element_type=F32)


def _dot_nt(a, b):
    return lax.dot_general(a, b, (((1,), (1,)), ((), ())), preferred_element_type=F32)


def _transpose(x):
    r, c = x.shape
    rows = []
    for j in range(c // LANES):
        rows.append(jnp.concatenate([x[i * LANES:(i + 1) * LANES, j * LANES:(j + 1) * LANES].T
                                     for i in range(r // LANES)], axis=1))
    return jnp.concatenate(rows, axis=0)


def _pack_w_in(w_in, d_model):
    offs = {}
    c = 0
    for name, n in _SPLITS:
        offs[name] = (c, c + n)
        c += n

    def col(name):
        lo, hi = offs[name]
        return w_in[:, lo:hi]

    z = lambda n: jnp.zeros((d_model, n), w_in.dtype)
    misc = jnp.concatenate([col('a_i'), col('a_f'), col('b_iw'), z(MISC_IK - 16), col('b_ik')], axis=1)
    segs = (w_in[:, offs['a_q'][0]:offs['a_v'][1]], w_in[:, offs['a_o'][0]:offs['b_iq'][1]], misc,
            w_in[:, offs['b_z'][0]:offs['g_b'][1]])
    assert [s.shape[1] for s in segs] == [P_AO - P_QK, P_MISC - P_AO, P_BZ - P_MISC, P_COLS - P_BZ]
    return tuple(s.astype(BF16) for s in segs)


def _inproj_kernel(x_ref, g_ref, w0_ref, w1_ref, w2_ref, w3_ref, cw_ref, cb_ref, mb_ref, *refs, names, prompt,
                   tiles_per_seq):
    o = dict(zip(names, refs))
    hist_ref = refs[len(names)]
    tm = x_ref.shape[0]
    x = x_ref[...]
    ms = jnp.mean(x * x, axis=-1, keepdims=True)
    xn = (x * lax.rsqrt(ms + EPS) * g_ref[...]).astype(BF16)

    segments = ((P_QK, w0_ref), (P_AO, w1_ref), (P_MISC, w2_ref), (P_BZ, w3_ref))

    def proj(lo, hi):
        start, ref = [s for s in segments if s[0] <= lo][-1]
        return _dot(xn, ref[:, lo - start:hi - start])

    qk = proj(P_QK, P_QK + 2 * A_QK)
    if prompt:
        o['tail'][0] = qk[tm - 8:tm, :]

        @pl.when(pl.program_id(0) % tiles_per_seq == 0)
        def _():
            hist_ref[0:8, :] = jnp.zeros((8, 2 * A_QK), F32)

        hist_ref[8:8 + tm, :] = qk
        cw = cw_ref[...]
        y = cb_ref[...] + cw[0:1, :] * hist_ref[5:5 + tm, :]
        y = y + cw[1:2, :] * hist_ref[6:6 + tm, :]
        y = y + cw[2:3, :] * hist_ref[7:7 + tm, :]
        y = y + cw[3:4, :] * qk
        act = _silu(y)
        o['q'][...] = (act[:, :A_QK] * Q_SCALE).astype(BF16)
        o['k'][...] = act[:, A_QK:].astype(BF16)
        for p in range(tm // (2 * PAGE)):
            o['kt'][p] = _transpose(act[p * 2 * PAGE:(p + 1) * 2 * PAGE, A_QK:]).astype(BF16)
        hist_ref[0:8, :] = qk[tm - 8:tm, :]
    else:
        o['q'][...] = qk[:, :A_QK]
        o['k'][...] = qk[:, A_QK:]

    o['av'][...] = proj(P_AV, P_AV + A_WIDTH).astype(BF16)
    o['ga'][...] = (_sigmoid(proj(P_AO, P_AO + A_WIDTH)) * _silu(proj(P_AZ, P_AZ + A_WIDTH))).astype(BF16)
    bq = proj(P_BQ, P_BQ + B_WIDTH) * C_EXP
    iq = proj(P_IQ, P_IQ + IDX_HEADS * IDX_DIM)
    kf = proj(P_BK, P_BK + B_KV * B_HD)
    vf = proj(P_BV, P_BV + B_KV * B_HD)
    for kv in range(B_KV):
        o['kf'][pl.ds(kv, tm, stride=B_KV), :] = kf[:, kv * B_HD:(kv + 1) * B_HD]
        o['vf'][pl.ds(kv, tm, stride=B_KV), :] = vf[:, kv * B_HD:(kv + 1) * B_HD]
    if prompt:
        for r in range(tm // PAGE):
            rows = slice(r * PAGE, (r + 1) * PAGE)
            o['bq'][r] = _transpose(bq[rows, :]).astype(BF16)
            o['iq'][r] = _transpose(iq[rows, :]).astype(BF16)
        ones = jnp.ones((VT_ROWS - B_HD, 2 * PAGE), BF16)
        for p in range(tm // (2 * PAGE)):
            rows = slice(p * 2 * PAGE, (p + 1) * 2 * PAGE)
            parts = []
            for g in range(B_KV):
                parts += [_transpose(vf[rows, g * B_HD:(g + 1) * B_HD]).astype(BF16), ones]
            o['vt'][p] = jnp.concatenate(parts, axis=0)
        o['kb'][...] = kf.astype(BF16)
    else:
        o['bq'][...] = bq.astype(BF16)
        o['iq'][...] = iq.astype(BF16)

    raw = proj(P_MISC, P_MISC + LANES) + mb_ref[...]
    lane = lax.broadcasted_iota(I32, raw.shape, 1)
    logsig = jnp.minimum(raw, 0.0) - jnp.log(1.0 + jnp.exp(-jnp.abs(raw)))
    o['misc'][...] = jnp.where((lane >= A_HEADS) & (lane < 2 * A_HEADS), logsig,
                               jnp.where((lane >= MISC_IW) & (lane < MISC_IW + IDX_HEADS),
                                         raw * (IW_SCALE * IDX_SCALE), raw))
    if prompt:
        o['miscb'][...] = jnp.where(lane >= MISC_IK, raw, 0.0).astype(BF16)

    o['zb'][...] = _silu(proj(P_BZ, P_BZ + B_WIDTH)).astype(BF16)
    o['sga'][...] = _sigmoid(proj(P_GA, P_GA + 1024)).astype(BF16)
    o['sgb'][...] = _sigmoid(proj(P_GB, P_GB + 1024)).astype(BF16)


def _inproj(x2d, norm_pre, w_packed, conv_w, conv_b, misc_bias, *, prompt, seq_len, tm):
    m, d = x2d.shape
    nt = m // tm
    rows = lambda w, dt: (jax.ShapeDtypeStruct((m, w), dt), pl.BlockSpec((tm, w), lambda i: (i, 0)))
    tiles = lambda n, r, c, dt: (jax.ShapeDtypeStruct((m // n, r, c), dt),
                                 pl.BlockSpec((tm // n, r, c), lambda i: (i, 0, 0)))
    kv_rows = (jax.ShapeDtypeStruct((B_KV * m, B_HD), F32), pl.BlockSpec((B_KV * tm, B_HD), lambda i: (i, 0)))
    outs = dict(q=rows(A_QK, BF16 if prompt else F32), k=rows(A_QK, BF16 if prompt else F32), av=rows(A_WIDTH, BF16),
                ga=rows(A_WIDTH, BF16), kf=kv_rows, vf=kv_rows, misc=rows(LANES, F32),
                zb=rows(B_WIDTH, BF16), sga=rows(d, BF16), sgb=rows(d, BF16))
    if prompt:
        outs.update(tail=tiles(tm, 8, 2 * A_QK, F32), bq=tiles(PAGE, B_WIDTH, PAGE, BF16),
                    iq=tiles(PAGE, IDX_HEADS * IDX_DIM, PAGE, BF16),
                    vt=tiles(2 * PAGE, B_KV * VT_ROWS, 2 * PAGE, BF16), kt=tiles(2 * PAGE, A_QK, 2 * PAGE, BF16),
                    kb=rows(B_KV * B_HD, BF16),
                    miscb=rows(LANES, BF16))
    else:
        outs.update(bq=rows(B_WIDTH, BF16), iq=rows(IDX_HEADS * IDX_DIM, BF16))
    names = tuple(outs)
    const = lambda shape: pl.BlockSpec(shape, lambda i: (0,) * len(shape))
    res = pl.pallas_call(
        functools.partial(_inproj_kernel, names=names, prompt=prompt, tiles_per_seq=max(seq_len // tm, 1)),
        grid=(nt,),
        in_specs=[pl.BlockSpec((tm, d), lambda i: (i, 0)), const((1, d))]
        + [pl.BlockSpec(seg.shape, lambda i: (0, 0), pipeline_mode=pl.Buffered(1)) for seg in w_packed]
        + [const((CONV_W, 2 * A_QK)), const((1, 2 * A_QK)), const((1, LANES))],
        out_specs=[outs[n][1] for n in names],
        out_shape=[outs[n][0] for n in names],
        scratch_shapes=[pltpu.VMEM((8 + tm, 2 * A_QK), F32)],
        compiler_params=pltpu.CompilerParams(dimension_semantics=("arbitrary",), vmem_limit_bytes=VMEM_LIMIT),
        name="inproj_prompt" if prompt else "inproj_decode",
    )(x2d, norm_pre, *w_packed, conv_w, conv_b, misc_bias)
    return dict(zip(names, res))


def _outproj_kernel(ya_ref, yb_ref, sga_ref, sgb_ref, x_ref, pe_ref, wa_ref, wb_ref, wo_ref, wg_ref, wp_ref,
                    g_ref, o_ref):
    merged = sga_ref[...] * _dot(ya_ref[...], wa_ref[...]) + sgb_ref[...] * _dot(yb_ref[...], wb_ref[...])
    z = _dot(merged.astype(BF16), wo_ref[...])
    ms = jnp.mean(z * z, axis=-1, keepdims=True)
    x1 = x_ref[...] + z * lax.rsqrt(ms + EPS) * g_ref[...]
    gate = _sigmoid(_dot(x1.astype(BF16), wg_ref[...]))
    o_ref[...] = x1 + _dot(pe_ref[...].astype(BF16), wp_ref[...]) * gate


def _outproj(ya, yb, sga, sgb, x2d, pe2d, wa, wb, wo, wg, wp, norm_post, *, tm):
    m, d = x2d.shape
    pd = pe2d.shape[1]
    row = lambda w: pl.BlockSpec((tm, w), lambda i: (i, 0))
    const = lambda shape: pl.BlockSpec(shape, lambda i: (0,) * len(shape))
    return pl.pallas_call(
        _outproj_kernel,
        grid=(m // tm,),
        in_specs=[row(A_WIDTH), row(B_WIDTH), row(d), row(d), row(d), row(pd),
                  const((A_WIDTH, d)), const((B_WIDTH, d)), const((d, d)), const((d, d)), const((pd, d)),
                  const((1, d))],
        out_specs=row(d),
        out_shape=jax.ShapeDtypeStruct((m, d), F32),
        compiler_params=pltpu.CompilerParams(dimension_semantics=("arbitrary",), vmem_limit_bytes=VMEM_LIMIT),
        name="outproj",
    )(ya, yb, sga, sgb, x2d, pe2d, wa, wb, wo, wg, wp, norm_post)


def _cumsum_rows(x):
    n = x.shape[0]
    row = lax.broadcasted_iota(I32, x.shape, 0)
    s = 1
    while s < n:
        x = x + jnp.where(row >= s, pltpu.roll(x, s, axis=0), 0.0)
        s *= 2
    return x


def _lane_col(x, lane_iota, idx):
    return jnp.sum(jnp.where(lane_iota == idx, x, 0.0), axis=1, keepdims=True)


def _mlstm_prompt_kernel(q_ref, k_ref, kt_ref, v_ref, misc_ref, ga_ref, na_ref, ya_ref, c_out_ref, nm_out_ref,
                         c_ref, n_ref, m_ref):
    c_idx = pl.program_id(1)
    nb, L = q_ref.shape[0], q_ref.shape[1]

    @pl.when(c_idx == 0)
    def _():
        c_ref[...] = jnp.zeros(c_ref.shape, F32)
        n_ref[...] = jnp.zeros(n_ref.shape, F32)
        m_ref[...] = jnp.zeros(m_ref.shape, F32)

    lane = lax.broadcasted_iota(I32, (L, LANES), 1)
    t_i = lax.broadcasted_iota(I32, (L, L), 0)
    s_i = lax.broadcasted_iota(I32, (L, L), 1)
    causal = s_i <= t_i

    for sq in range(nb):
        misc = misc_ref[sq]
        cs = _cumsum_rows(misc)
        b_al = pltpu.roll(cs, LANES - A_HEADS, axis=1)
        xa = jnp.where(lane < A_HEADS, misc - b_al, cs)
        xt = _transpose(xa)

        for h in range(A_HEADS):
            a_col = _lane_col(xa, lane, h)
            b_col = _lane_col(xa, lane, A_HEADS + h)
            a_row = xt[h:h + 1, :]
            m_prev = m_ref[sq, h:h + 1, 0:1]
            logit = jnp.where(causal, a_row, -jnp.inf)
            g_col = jnp.maximum(m_prev, jnp.max(logit, axis=1, keepdims=True))
            dmat = jnp.exp(logit - g_col)
            w_inter = jnp.exp(m_prev - g_col)
            qh = q_ref[sq, :, h * A_DQK:(h + 1) * A_DQK]
            kh = k_ref[sq, :, h * A_DQK:(h + 1) * A_DQK]
            vh = v_ref[sq, :, h * A_DV:(h + 1) * A_DV]
            c_h = c_ref[sq, h]
            n_h = n_ref[sq, h:h + 1, :]
            s = _dot_nt(qh, kh) * dmat
            num = _dot(s.astype(BF16), vh) + w_inter * _dot(qh, c_h.astype(BF16))
            den = (jnp.sum(s, axis=1, keepdims=True)
                   + w_inter * jnp.sum(qh.astype(F32) * n_h, axis=1, keepdims=True))
            m_t = b_col + g_col
            hh = num / jnp.maximum(jnp.abs(den), jnp.exp(-m_t))
            hn = hh * lax.rsqrt(jnp.mean(hh * hh, axis=1, keepdims=True) + EPS)
            sl = slice(h * A_DV, (h + 1) * A_DV)
            ya_ref[sq, :, sl] = (ga_ref[sq, :, sl] * (hn * na_ref[:, sl])).astype(ya_ref.dtype)

            g_last = g_col[L - 1:L, :]
            w_prev = jnp.exp(m_prev - g_last)
            w_s = jnp.exp(a_col - g_last)
            vw = vh.astype(F32) * w_s
            c_ref[sq, h] = w_prev * c_h + _dot(kt_ref[sq, h * A_DQK:(h + 1) * A_DQK, :], vw.astype(BF16))
            n_ref[sq, h:h + 1, :] = w_prev * n_h + jnp.sum(kh.astype(F32) * w_s, axis=0, keepdims=True)
            m_ref[sq, h:h + 1, :] = jnp.broadcast_to(b_col[L - 1:L, :] + g_last, (1, LANES))

    @pl.when(c_idx == pl.num_programs(1) - 1)
    def _():
        for sq in range(nb):
            for h in range(A_HEADS):
                c_out_ref[sq, h] = _transpose(c_ref[sq, h])
        nm_out_ref[:, 0:A_HEADS, :] = n_ref[:, 0:A_HEADS, :]
        nm_out_ref[:, A_HEADS:2 * A_HEADS, :] = m_ref[:, 0:A_HEADS, :]


def _mlstm_prompt(q, k, kt, v, misc, ga, norm_a, *, batch, seq_len, chunk):
    nc = seq_len // chunk
    nb = 1
    seq3 = lambda a: a.reshape(batch, seq_len, a.shape[-1])
    blk = lambda w: pl.BlockSpec((nb, chunk, w), lambda b, c: (b, c, 0))
    ya, c_out, nm_out = pl.pallas_call(
        _mlstm_prompt_kernel,
        grid=(batch // nb, nc),
        in_specs=[blk(A_QK), blk(A_QK), pl.BlockSpec((nb, A_QK, chunk), lambda b, c: (b * nc + c, 0, 0)),
                  blk(A_WIDTH), blk(LANES), blk(A_WIDTH), pl.BlockSpec((1, A_WIDTH), lambda b, c: (0, 0))],
        out_specs=[blk(A_WIDTH),
                   pl.BlockSpec((nb, A_HEADS, A_DV, A_DQK), lambda b, c: (b, 0, 0, 0)),
                   pl.BlockSpec((nb, 8, LANES), lambda b, c: (b, 0, 0))],
        out_shape=[jax.ShapeDtypeStruct((batch, seq_len, A_WIDTH), BF16),
                   jax.ShapeDtypeStruct((batch, A_HEADS, A_DV, A_DQK), F32),
                   jax.ShapeDtypeStruct((batch, 8, LANES), F32)],
        scratch_shapes=[pltpu.VMEM((nb, A_HEADS, A_DQK, A_DV), F32), pltpu.VMEM((nb, 8, LANES), F32),
                        pltpu.VMEM((nb, 8, LANES), F32)],
        compiler_params=pltpu.CompilerParams(dimension_semantics=("arbitrary", "arbitrary"),
                                             vmem_limit_bytes=VMEM_LIMIT),
        name="mlstm_prompt",
    )(seq3(q), seq3(k), kt, seq3(v), seq3(misc), seq3(ga), norm_a)
    return ya.reshape(batch * seq_len, A_WIDTH), c_out, nm_out


def _key_to_f32(key):
    return pltpu.bitcast(jnp.where(key < 0, key ^ 0x7FFFFFFF, key), F32)


def _radix_threshold(count_ge, shape, topk):
    def bit_body(i, carry):
        u, n_u = carry
        cand_u = u | jnp.left_shift(jnp.int32(1), 31 - i)
        cnt = count_ge(_key_to_f32(cand_u ^ INT_MIN))
        ok = cnt >= topk
        return jnp.where(ok, cand_u, u), jnp.where(ok, cnt, n_u)

    u, n_u = lax.fori_loop(0, 32, bit_body, (jnp.zeros(shape, I32), jnp.zeros(shape, I32)))
    key = jnp.maximum(u ^ INT_MIN, KEY_NEG_INF)
    return _key_to_f32(key), _key_to_f32(key + 1), n_u


def _refine_in_bin(count_ge, thr, thr_up, topk):
    lo, hi = thr, thr_up
    for _ in range(6):
        mid = lo + 0.5 * (hi - lo)
        ok = count_ge(mid) >= topk
        lo = jnp.where(ok, mid, lo)
        hi = jnp.where(ok, hi, mid)
    return lo


def _dsa_prompt_kernel(bq_ref, iq_ref, misc_ref, zb_ref, ikb_ref, kb_ref, vt_ref, yb_ref,
                       sc_ref, s_ref, s2_ref, p_ref, p2_ref, al_ref, al2_ref, acc_ref, m_ref, *, topk, nq):
    j = pl.program_id(1)
    T = PAGE
    KT = 2 * PAGE
    nkt = j // 2 + 1

    iq_t = iq_ref[0]
    zeros_pad = jnp.zeros((LANES - IDX_DIM, T), BF16)
    iq_pad = [jnp.concatenate([zeros_pad, iq_t[h * IDX_DIM:(h + 1) * IDX_DIM, :]], axis=0)
              for h in range(IDX_HEADS)]
    IDX_GROUP = 2
    iq_wide = [jnp.concatenate(iq_pad[h:h + IDX_GROUP], axis=1) for h in range(0, IDX_HEADS, IDX_GROUP)]
    misc_t = _transpose(misc_ref[...])
    q_t = bq_ref[0]
    q_grp = [jnp.concatenate([q_t[(g * B_GROUP + hh) * B_HD:(g * B_GROUP + hh + 1) * B_HD, :]
                              for hh in range(B_GROUP)], axis=1) for g in range(B_KV)]

    row_i = lax.broadcasted_iota(I32, (KT, T), 0)
    lane_i = lax.broadcasted_iota(I32, (KT, T), 1)

    last_pair = nq // 2 - 1

    def idx_body(c, carry):
        ikc = ikb_ref[0, c]
        acc = jnp.zeros((KT, T), F32)
        for h0 in range(0, IDX_HEADS, IDX_GROUP):
            prod = _dot(ikc, iq_wide[h0 // IDX_GROUP])
            for i in range(IDX_GROUP):
                r = jnp.maximum(prod[:, i * T:(i + 1) * T], 0.0)
                acc = acc + r * misc_t[MISC_IW + h0 + i:MISC_IW + h0 + i + 1, :]
        vis = (c * KT + row_i) <= (j * T + lane_i)
        sc_ref[c] = jnp.where(vis, acc, -jnp.inf)
        return carry

    lax.fori_loop(0, nkt, idx_body, 0)

    def count(pred):
        def body(c, cnt):
            m = pred(sc_ref[c], c).reshape(KT // 32, 4, 8, T)
            for v in range(KT // 32):
                cnt = jnp.where(m[v], cnt + 1, cnt)
            return cnt
        cnt4 = lax.fori_loop(0, nkt, body, jnp.zeros((4, 8, T), I32))
        return jnp.sum(jnp.sum(cnt4, axis=0), axis=0, keepdims=True)

    count_ge = lambda t: count(lambda sc, c: sc >= t)
    thr0, thr_up, n_ge = _radix_threshold(count_ge, (1, T), topk)

    def tie_search(_):
        thr_t = _refine_in_bin(count_ge, thr0, thr_up, topk)
        need = topk - count(lambda sc, c: sc > thr_t)
        nbits = (nq * T - 1).bit_length()
        def pbody(i, p):
            cand = p | jnp.left_shift(jnp.int32(1), nbits - 1 - i)
            cnt = count(lambda sc, c: (sc == thr_t) & ((c * KT + row_i) < cand))
            return jnp.where(cnt < need, cand, p)
        return thr_t, lax.fori_loop(0, nbits, pbody, jnp.zeros((1, T), I32))

    has_tie = jnp.max(jnp.where((n_ge > topk) & (thr0 > -jnp.inf), 1, 0)) > 0
    thr, p_lim = lax.cond(has_tie, tie_search, lambda _: (thr0, jnp.full((1, T), 2 ** 30, I32)), 0)

    acc_ref[...] = jnp.zeros(acc_ref.shape, F32)
    m_ref[...] = jnp.full(m_ref.shape, NEG_BIG, F32)

    def qk_scores(c, dst_ref):
        kc = kb_ref[0, jnp.minimum(c, last_pair)]
        for g in range(B_KV):
            dst_ref[g] = _dot(kc[:, g * B_HD:(g + 1) * B_HD], q_grp[g])

    def softmax(c, src_ref, p_dst, al_dst):
        sc = sc_ref[c]
        pos = c * KT + row_i
        sel = ((sc > thr) | ((sc == thr) & (pos <= p_lim))) & (sc > -jnp.inf)
        bias = jnp.where(sel, 0.0, NEG_BIG)
        for h in range(B_HEADS):
            g, hh = divmod(h, B_GROUP)
            s = src_ref[g, :, hh * T:(hh + 1) * T] + bias
            m_old = m_ref[h:h + 1, :]
            m_new = jnp.maximum(m_old, jnp.max(s, axis=0, keepdims=True))
            al_dst[h:h + 1, :] = jnp.exp2(m_old - m_new)
            p_dst[h // 2, :, (h % 2) * T:(h % 2 + 1) * T] = jnp.exp2(s - m_new).astype(BF16)
            m_ref[h:h + 1, :] = m_new

    def pv_update(c, p_src, al_src):
        vtc = vt_ref[0, c]
        for h0 in range(0, B_HEADS, 2):
            g = h0 // B_GROUP
            pv = _dot(vtc[g * VT_ROWS:(g + 1) * VT_ROWS, :], p_src[h0 // 2])
            for i, h in enumerate((h0, h0 + 1)):
                acc_ref[h] = acc_ref[h] * al_src[h:h + 1, :] + pv[:, i * T:(i + 1) * T]

    p2_ref[...] = jnp.zeros(p2_ref.shape, BF16)
    al2_ref[...] = jnp.ones(al2_ref.shape, F32)
    qk_scores(0, s_ref)

    def att_body(i, carry):
        c0 = 2 * i
        qk_scores(c0 + 1, s2_ref)
        softmax(c0, s_ref, p_ref, al_ref)
        pv_update(jnp.maximum(c0 - 1, 0), p2_ref, al2_ref)

        @pl.when(c0 + 1 < nkt)
        def _():
            qk_scores(c0 + 2, s_ref)
            softmax(c0 + 1, s2_ref, p2_ref, al2_ref)
            pv_update(c0, p_ref, al_ref)

        return carry

    lax.fori_loop(0, (nkt + 1) // 2, att_body, 0)

    @pl.when(nkt % 2 == 1)
    def _():
        pv_update(nkt - 1, p_ref, al_ref)

    @pl.when(nkt % 2 == 0)
    def _():
        pv_update(nkt - 1, p2_ref, al2_ref)

    for h in range(B_HEADS):
        o_t = acc_ref[h, 0:B_HD, :] / acc_ref[h, B_HD:B_HD + 1, :]
        sl = slice(h * B_HD, (h + 1) * B_HD)
        yb_ref[:, sl] = (_transpose(o_t) * zb_ref[:, sl]).astype(yb_ref.dtype)


def _dsa_prompt(bq, iq, misc, zb, ikb, kb, vt, *, batch, seq_len, topk):
    nq = seq_len // PAGE
    kt = 2 * PAGE
    scores = pltpu.VMEM((B_KV, kt, B_GROUP * PAGE), F32)
    weights = pltpu.VMEM((B_HEADS // 2, kt, 2 * PAGE), BF16)
    rescale = pltpu.VMEM((B_HEADS, PAGE), F32)
    row = lambda w: pl.BlockSpec((PAGE, w), lambda b, j: (b * nq + j, 0))
    tile_t = lambda r: pl.BlockSpec((1, r, PAGE), lambda b, j: (b * nq + j, 0, 0))
    per_b = lambda a: pl.BlockSpec((1,) + a.shape[1:], lambda b, j: (b, 0, 0, 0))
    return pl.pallas_call(
        functools.partial(_dsa_prompt_kernel, topk=topk, nq=nq),
        grid=(batch, nq),
        in_specs=[tile_t(B_WIDTH), tile_t(IDX_HEADS * IDX_DIM), row(LANES), row(B_WIDTH),
                  per_b(ikb), per_b(kb), per_b(vt)],
        out_specs=row(B_WIDTH),
        out_shape=jax.ShapeDtypeStruct((batch * seq_len, B_WIDTH), BF16),
        scratch_shapes=[pltpu.VMEM((nq // 2, kt, PAGE), F32), scores, scores, weights, weights, rescale, rescale,
                        pltpu.VMEM((B_HEADS, VT_ROWS, PAGE), F32),
                        pltpu.VMEM((B_HEADS, PAGE), F32)],
        compiler_params=pltpu.CompilerParams(dimension_semantics=("arbitrary", "arbitrary"),
                                             vmem_limit_bytes=VMEM_LIMIT),
        name="dsa_prompt",
    )(bq, iq, misc, zb, ikb, kb, vt)


def _prompt_path(x, pe, w, w_packed, misc_bias):
    batch, seq_len, d = x.shape
    m = batch * seq_len
    x2d = x.reshape(m, d)
    tm = min(256, seq_len)
    a = _inproj(x2d, w['norm_pre'], w_packed, w['conv_w'], w['conv_b'], misc_bias,
                prompt=True, seq_len=seq_len, tm=tm)
    chunk = 2 * PAGE
    ya, c_out, nm_out = _mlstm_prompt(a['q'], a['k'], a['kt'], a['av'], a['misc'], a['ga'], w['norm_a'],
                                      batch=batch, seq_len=seq_len, chunk=chunk)
    nk2 = seq_len // (2 * PAGE)
    ikb = a['miscb'].reshape(batch, nk2, 2 * PAGE, LANES)
    kb = a['kb'].reshape(batch, nk2, 2 * PAGE, B_KV * B_HD)
    vt = a['vt'].reshape(batch, nk2, B_KV * VT_ROWS, 2 * PAGE)
    topk = min(TOPK_MAX, seq_len // 4)
    yb = _dsa_prompt(a['bq'], a['iq'], a['misc'], a['zb'], ikb, kb, vt, batch=batch, seq_len=seq_len, topk=topk)
    y = _outproj(ya, yb, a['sga'], a['sgb'], x2d, pe.reshape(m, -1), w['wa'], w['wb'], w['wo'], w['wg'], w['wp'],
                 w['norm_post'], tm=min(256, m))
    tiles_per_seq = seq_len // tm
    tail = a['tail'].reshape(batch, tiles_per_seq, 8, 2 * A_QK)[:, -1, 8 - (CONV_W - 1):, :]
    return dict(
        y=y.reshape(batch, seq_len, d),
        k=a['kf'].reshape(1, batch, seq_len, B_KV, B_HD),
        v=a['vf'].reshape(1, batch, seq_len, B_KV, B_HD),
        ik=a['misc'][:, MISC_IK:].reshape(1, batch, seq_len, IDX_DIM),
        C=c_out[None],
        n=nm_out[None, :, 0:A_HEADS, :],
        m=nm_out[None, :, A_HEADS:2 * A_HEADS, 0],
        conv=tail[None],
    )


def _mlstm_sample_kernel(q_ref, k_ref, conv0_ref, cw_ref, cb_ref, v_ref, misc_ref, ga_ref, na_ref, n0_ref, m0_ref,
                         c0_ref, ya_ref, c_out_ref, n_out_ref, m_out_ref,
                         qs_ref, ks_ref, cq_ref, wprev_ref, vwt_ref, numi_ref, den_ref, wint_ref, enm_ref, *, db, ds):
    b = pl.program_id(0)
    ms = ds * db
    rows = lambda t: slice(t * db, (t + 1) * db)

    @pl.when(b == 0)
    def _():
        cw = cw_ref[...]
        u = [conv0_ref[j] for j in range(CONV_W - 1)]
        u += [jnp.concatenate([q_ref[rows(t), :], k_ref[rows(t), :]], axis=1) for t in range(ds)]
        for t in range(ds):
            y = cb_ref[...] + cw[0:1, :] * u[t]
            for jj in range(1, CONV_W):
                y = y + cw[jj:jj + 1, :] * u[t + jj]
            act = _silu(y)
            qs_ref[rows(t), :] = act[:, :A_QK] * Q_SCALE
            ks_ref[rows(t), :] = act[:, A_QK:]
        cq_ref[...] = jnp.zeros(cq_ref.shape, F32)

        lane = lax.broadcasted_iota(I32, (db, LANES), 1)
        for h in range(A_HEADS):
            i_t = [_lane_col(misc_ref[rows(t), :], lane, h) for t in range(ds)]
            lf_t = [_lane_col(misc_ref[rows(t), :], lane, A_HEADS + h) for t in range(ds)]
            b_t = [lf_t[0]]
            for t in range(1, ds):
                b_t.append(b_t[-1] + lf_t[t])
            a_t = [i_t[t] - b_t[t] for t in range(ds)]
            m_prev = jnp.max(m0_ref[h], axis=1, keepdims=True)
            g_t = [jnp.maximum(m_prev, a_t[0])]
            for t in range(1, ds):
                g_t.append(jnp.maximum(g_t[-1], a_t[t]))
            hq = slice(h * A_DQK, (h + 1) * A_DQK)
            hv = slice(h * A_DV, (h + 1) * A_DV)
            qh = [qs_ref[rows(t), hq] for t in range(ds)]
            kh = [ks_ref[rows(t), hq] for t in range(ds)]
            vh = [v_ref[rows(t), hv].astype(F32) for t in range(ds)]
            n0 = n0_ref[h]
            for t in range(ds):
                num = jnp.zeros((db, A_DV), F32)
                den = jnp.zeros((db, 1), F32)
                for s in range(t + 1):
                    w_ts = jnp.sum(qh[t] * kh[s], axis=1, keepdims=True) * jnp.exp(a_t[s] - g_t[t])
                    num = num + w_ts * vh[s]
                    den = den + w_ts
                w_inter = jnp.exp(m_prev - g_t[t])
                den = den + w_inter * jnp.sum(qh[t] * n0, axis=1, keepdims=True)
                numi_ref[rows(t), hv] = num
                den_ref[h, rows(t), :] = jnp.broadcast_to(den, (db, LANES))
                wint_ref[h, rows(t), :] = jnp.broadcast_to(w_inter, (db, LANES))
                enm_ref[h, rows(t), :] = jnp.broadcast_to(jnp.exp(-(b_t[t] + g_t[t])), (db, LANES))
            g_last = g_t[ds - 1]
            w_prev = jnp.exp(m_prev - g_last)
            wprev_ref[h] = jnp.broadcast_to(w_prev, (db, LANES))
            n_new = w_prev * n0
            vw = []
            for s in range(ds):
                w_s = jnp.exp(a_t[s] - g_last)
                n_new = n_new + w_s * kh[s]
                vw.append(vh[s] * w_s)
            n_out_ref[h] = n_new
            m_out_ref[h] = jnp.broadcast_to(b_t[ds - 1] + g_last, (db, LANES))
            vw_all = jnp.concatenate(vw, axis=0)
            if ms < LANES:
                vw_all = jnp.concatenate([vw_all, jnp.zeros((LANES - ms, A_DV), F32)], axis=0)
            vwt_ref[h] = _transpose(vw_all)

    mcols = vwt_ref.shape[2]
    row_i = lax.broadcasted_iota(I32, (ms, A_DQK), 0)
    col_i = lax.broadcasted_iota(I32, (A_DV, mcols), 1)
    row_mine = row_i == b
    col_mine = col_i == b
    for t in range(1, ds):
        row_mine = row_mine | (row_i == b + t * db)
        col_mine = col_mine | (col_i == b + t * db)
    for h in range(A_HEADS):
        hq = slice(h * A_DQK, (h + 1) * A_DQK)
        c0 = c0_ref[0, h]
        q_mine = jnp.where(row_mine, qs_ref[:, hq], 0.0).astype(BF16)
        cq_ref[h] = cq_ref[h] + _dot_nt(q_mine, c0.astype(BF16))
        vw_mine = jnp.where(col_mine, vwt_ref[h], 0.0).astype(BF16)
        k_all = ks_ref[:, hq]
        if mcols > ms:
            k_all = jnp.concatenate([k_all, jnp.zeros((mcols - ms, A_DQK), F32)], axis=0)
        c_out_ref[0, h] = wprev_ref[h, pl.ds(b, 1), :] * c0 + _dot(vw_mine, k_all.astype(BF16))

    @pl.when(b == pl.num_programs(0) - 1)
    def _():
        for h in range(A_HEADS):
            hv = slice(h * A_DV, (h + 1) * A_DV)
            num = numi_ref[:, hv] + wint_ref[h][:, 0:1] * cq_ref[h]
            hh = num / jnp.maximum(jnp.abs(den_ref[h][:, 0:1]), enm_ref[h][:, 0:1])
            hn = hh * lax.rsqrt(jnp.mean(hh * hh, axis=1, keepdims=True) + EPS)
            ya_ref[:, hv] = (ga_ref[:, hv] * (hn * na_ref[:, hv])).astype(ya_ref.dtype)


def _mlstm_sample(q_t, k_t, conv0_t, conv_w, conv_b, v_t, misc_t, ga_t, norm_a, n0_t, m0_bc, c0, *, db, ds):
    ms = db * ds
    mcols = max(ms, LANES)
    full = lambda a: pl.BlockSpec(a.shape, lambda b: (0,) * a.ndim)
    cblk = pl.BlockSpec((1, A_HEADS, A_DV, A_DQK), lambda b: (b, 0, 0, 0))
    hshape = jax.ShapeDtypeStruct((A_HEADS, db, LANES), F32)
    ins = (q_t, k_t, conv0_t, conv_w, conv_b, v_t, misc_t, ga_t, norm_a, n0_t, m0_bc)
    return pl.pallas_call(
        functools.partial(_mlstm_sample_kernel, db=db, ds=ds),
        grid=(db,),
        in_specs=[full(a) for a in ins] + [cblk],
        out_specs=[pl.BlockSpec((ms, A_WIDTH), lambda b: (0, 0)), cblk,
                   pl.BlockSpec((A_HEADS, db, LANES), lambda b: (0, 0, 0)),
                   pl.BlockSpec((A_HEADS, db, LANES), lambda b: (0, 0, 0))],
        out_shape=[jax.ShapeDtypeStruct((ms, A_WIDTH), BF16), jax.ShapeDtypeStruct(c0.shape, F32), hshape, hshape],
        scratch_shapes=[pltpu.VMEM((ms, A_QK), F32), pltpu.VMEM((ms, A_QK), F32),
                        pltpu.VMEM((A_HEADS, ms, A_DV), F32), pltpu.VMEM((A_HEADS, db, LANES), F32),
                        pltpu.VMEM((A_HEADS, A_DV, mcols), F32), pltpu.VMEM((ms, A_WIDTH), F32),
                        pltpu.VMEM((A_HEADS, ms, LANES), F32), pltpu.VMEM((A_HEADS, ms, LANES), F32),
                        pltpu.VMEM((A_HEADS, ms, LANES), F32)],
        compiler_params=pltpu.CompilerParams(dimension_semantics=("arbitrary",), vmem_limit_bytes=VMEM_LIMIT),
        name="mlstm_sample",
    )(*ins, c0)


def _page_ring(pt_ref, pages_per_step, streams):
    b, g = pl.program_id(0), pl.program_id(1)
    nb, ng = pl.num_programs(0), pl.num_programs(1)
    step = b * ng + g
    slot = lax.rem(step, 2)

    def copies(bb, gg, sl):
        return [pltpu.make_async_copy(hbm.at[pt_ref[bb, gg * pages_per_step + i]], buf.at[sl, i], sem.at[sl])
                for hbm, buf, sem in streams for i in range(pages_per_step)]

    def start_all(cps):
        for n, cp in enumerate(cps):
            cp.start(priority=n % 2)

    @pl.when(step == 0)
    def _():
        start_all(copies(0, 0, 0))

    @pl.when(step + 1 < nb * ng)
    def _():
        wrap = g + 1 == ng
        start_all(copies(jnp.where(wrap, b + 1, b), jnp.where(wrap, 0, g + 1), 1 - slot))

    for cp in copies(b, g, slot):
        cp.wait()
    return slot


def _dsa_sample_scores_kernel(pt_ref, iq_ref, iw_ref, iknew_ref, cache_ref, sc_ref, page_buf, sem,
                              *, pages_per_step, n_pages, ds):
    g = pl.program_id(1)
    slot = _page_ring(pt_ref, pages_per_step, [(cache_ref, page_buf, sem)])
    page_refs = [page_buf.at[slot, i] for i in range(pages_per_step)]
    iq = iq_ref[0]
    iw = iw_ref[0]

    def scores(keys_t, visible):
        n = keys_t.shape[1] // PAGE
        r = jnp.maximum(_dot(iq, keys_t), 0.0) * jnp.concatenate([iw] * n, axis=1)
        sc = jnp.sum(r.reshape(8, IDX_HEADS, n * PAGE), axis=1)
        return jnp.where(visible, sc, -jnp.inf)

    t_all = lax.broadcasted_iota(I32, (8, pages_per_step * PAGE), 0)
    sc_step = scores(jnp.concatenate([r[...] for r in page_refs], axis=1).astype(BF16), t_all < ds)
    for i in range(pages_per_step):
        sc_ref[0, g * pages_per_step + i] = sc_step[:, i * PAGE:(i + 1) * PAGE]

    @pl.when(g == pl.num_programs(1) - 1)
    def _():
        t_i = lax.broadcasted_iota(I32, (8, PAGE), 0)
        s_i = lax.broadcasted_iota(I32, (8, PAGE), 1)
        sc_ref[0, n_pages] = scores(iknew_ref[0], (t_i < ds) & (s_i <= t_i))


def _dsa_sample_scores(page_table, iq_s, iw_bc, ik_new, cache_ik, *, ds, pages_per_step):
    db, n_pages = page_table.shape
    npg = n_pages // pages_per_step
    per_b = lambda a: pl.BlockSpec((1,) + a.shape[1:], lambda b, g, pt: (b,) + (0,) * (a.ndim - 1))
    return pl.pallas_call(
        functools.partial(_dsa_sample_scores_kernel, pages_per_step=pages_per_step, n_pages=n_pages, ds=ds),
        grid_spec=pltpu.PrefetchScalarGridSpec(
            num_scalar_prefetch=1, grid=(db, npg),
            in_specs=[per_b(iq_s), per_b(iw_bc), per_b(ik_new), pl.BlockSpec(memory_space=pl.ANY)],
            out_specs=pl.BlockSpec((1, n_pages + 1, 8, PAGE), lambda b, g, pt: (b, 0, 0, 0)),
            scratch_shapes=[pltpu.VMEM((2, pages_per_step, IDX_DIM, PAGE), F32), pltpu.SemaphoreType.DMA((2,))]),
        out_shape=jax.ShapeDtypeStruct((db, n_pages + 1, 8, PAGE), F32),
        compiler_params=pltpu.CompilerParams(dimension_semantics=("arbitrary", "arbitrary"),
                                             vmem_limit_bytes=VMEM_LIMIT),
        name="dsa_sample_scores",
    )(page_table, iq_s, iw_bc, ik_new, cache_ik)


def _dsa_sample_search_kernel(sc_ref, bias_ref, *, topk):
    sc = sc_ref[...]
    nb, np1 = sc.shape[0], sc.shape[1]
    pos = lax.broadcasted_iota(I32, sc.shape, 1) * PAGE + lax.broadcasted_iota(I32, sc.shape, 3)

    def count(mask):
        per_lane = jnp.sum(mask.astype(I32), axis=1, keepdims=True)
        return jnp.sum(per_lane, axis=3, keepdims=True)

    count_ge = lambda t: count(sc_ref[...] >= t)
    thr0, thr_up, n_ge = _radix_threshold(count_ge, (nb, 1, 8, 1), topk)
    nbits = (np1 * PAGE - 1).bit_length()

    def tie_search(_):
        thr_t = _refine_in_bin(count_ge, thr0, thr_up, topk)
        need = topk - count(sc > thr_t)
        tied = sc == thr_t

        def pos_body(i, p):
            cand = p | jnp.left_shift(jnp.int32(1), nbits - 1 - i)
            cnt = count(tied & (pos < cand))
            return jnp.where(cnt < need, cand, p)
        return thr_t, lax.fori_loop(0, nbits, pos_body, jnp.zeros(thr0.shape, I32))

    has_tie = jnp.max(jnp.where((n_ge > topk) & (thr0 > -jnp.inf), 1, 0)) > 0
    thr, p_lim = lax.cond(has_tie, tie_search, lambda _: (thr0, jnp.full(thr0.shape, 2 ** 30, I32)), 0)
    sel = ((sc > thr) | ((sc == thr) & (pos <= p_lim))) & (sc > -jnp.inf)
    bias_ref[...] = jnp.where(sel, 0.0, NEG_BIG)


def _dsa_sample_search(scores, *, topk):
    db = scores.shape[0]
    nb = 8 if db % 8 == 0 else 1
    blk = pl.BlockSpec((nb,) + scores.shape[1:], lambda i: (i, 0, 0, 0))
    return pl.pallas_call(
        functools.partial(_dsa_sample_search_kernel, topk=topk),
        grid=(db // nb,),
        in_specs=[blk],
        out_specs=blk,
        out_shape=jax.ShapeDtypeStruct(scores.shape, F32),
        compiler_params=pltpu.CompilerParams(dimension_semantics=("arbitrary",), vmem_limit_bytes=VMEM_LIMIT),
        name="dsa_sample_search",
    )(scores)


def _dsa_sample_attend_kernel(pt_ref, q_ref, bias_ref, knew_ref, vnew_ref, zb_ref, ck_ref, cv_ref, o_ref,
                              k_buf, v_buf, k_sem, v_sem, acc_ref, m_ref, l_ref, *, pages_per_step, n_pages, ds):
    g = pl.program_id(1)
    slot = _page_ring(pt_ref, pages_per_step, [(ck_ref, k_buf, k_sem), (cv_ref, v_buf, v_sem)])
    k_refs = [k_buf.at[slot, i] for i in range(pages_per_step)]
    v_refs = [v_buf.at[slot, i] for i in range(pages_per_step)]
    rows_q = ds * B_HEADS
    q = q_ref[0]

    @pl.when(g == 0)
    def _():
        acc_ref[...] = jnp.zeros(acc_ref.shape, F32)
        m_ref[...] = jnp.full(m_ref.shape, NEG_BIG, F32)
        l_ref[...] = jnp.zeros(l_ref.shape, F32)

    def step(k_keys, v_keys, bias8):
        n = k_keys.shape[0]
        bias = jnp.concatenate([jnp.broadcast_to(bias8[t:t + 1, :], (B_HEADS, n)) for t in range(ds)], axis=0)
        s = _dot_nt(q, k_keys) + bias
        m_old = m_ref[...]
        m_new = jnp.maximum(m_old, jnp.max(s, axis=1, keepdims=True))
        alpha = jnp.exp2(m_old - m_new)
        p = jnp.exp2(s - m_new)
        l_ref[...] = alpha * l_ref[...] + jnp.sum(p, axis=1, keepdims=True)
        m_ref[...] = m_new
        acc_ref[...] = alpha * acc_ref[...] + _dot(p.astype(BF16), v_keys)

    def page_keys(ref):
        return jnp.concatenate([ref[pl.ds(kv, PAGE, stride=B_KV), :] for kv in range(B_KV)], axis=1).astype(BF16)

    step(jnp.concatenate([page_keys(r) for r in k_refs], axis=0),
         jnp.concatenate([page_keys(r) for r in v_refs], axis=0),
         jnp.concatenate([bias_ref[0, g * pages_per_step + i] for i in range(pages_per_step)], axis=1))

    @pl.when(g == pl.num_programs(1) - 1)
    def _():
        step(knew_ref[0], vnew_ref[0], bias_ref[0, n_pages])
        o = acc_ref[...] / l_ref[...]
        head = lax.broadcasted_iota(I32, (rows_q, B_HD), 0) & (B_HEADS - 1)
        o_sel = o[:, 0:B_HD]
        for kv in range(1, B_KV):
            o_sel = jnp.where(head >= kv * B_GROUP, o[:, kv * B_HD:(kv + 1) * B_HD], o_sel)
        o_ref[0] = (o_sel * zb_ref[0]).astype(o_ref.dtype)


def _dsa_sample_attend(page_table, q_bd, bias, k_new, v_new, zb_s, cache_k, cache_v, *, ds, pages_per_step):
    db, n_pages = page_table.shape
    npg = n_pages // pages_per_step
    rows_q = ds * B_HEADS
    kvw = B_KV * B_HD
    per_b = lambda a: pl.BlockSpec((1,) + a.shape[1:], lambda b, g, pt: (b,) + (0,) * (a.ndim - 1))
    any_space = pl.BlockSpec(memory_space=pl.ANY)
    page_buf = pltpu.VMEM((2, pages_per_step, PAGE * B_KV, B_HD), F32)
    return pl.pallas_call(
        functools.partial(_dsa_sample_attend_kernel, pages_per_step=pages_per_step, n_pages=n_pages, ds=ds),
        grid_spec=pltpu.PrefetchScalarGridSpec(
            num_scalar_prefetch=1, grid=(db, npg),
            in_specs=[per_b(q_bd), per_b(bias), per_b(k_new), per_b(v_new), per_b(zb_s), any_space, any_space],
            out_specs=pl.BlockSpec((1, rows_q, B_HD), lambda b, g, pt: (b, 0, 0)),
            scratch_shapes=[page_buf, page_buf, pltpu.SemaphoreType.DMA((2,)), pltpu.SemaphoreType.DMA((2,)),
                            pltpu.VMEM((rows_q, kvw), F32), pltpu.VMEM((rows_q, 1), F32),
                            pltpu.VMEM((rows_q, 1), F32)]),
        out_shape=jax.ShapeDtypeStruct((db, rows_q, B_HD), BF16),
        compiler_params=pltpu.CompilerParams(dimension_semantics=("arbitrary", "arbitrary"),
                                             vmem_limit_bytes=VMEM_LIMIT),
        name="dsa_sample_attend",
    )(page_table, q_bd, bias, k_new, v_new, zb_s, cache_k, cache_v)


def _sample_path(x, pe, cache_k, cache_v, cache_ik, page_table, c0, n0, m0, conv0, w, w_packed, misc_bias):
    db, ds, d = x.shape
    ms = db * ds
    assert ds >= CONV_W - 1 and ds <= 8
    n_pages = page_table.shape[1]
    a = _inproj(x.reshape(ms, d), w['norm_pre'], w_packed, w['conv_w'], w['conv_b'], misc_bias,
                prompt=False, seq_len=ms, tm=ms)

    tmaj = lambda z: jnp.swapaxes(z.reshape(db, ds, -1), 0, 1).reshape(ms, -1)
    ya_t, c_new, n_new, m_new = _mlstm_sample(
        tmaj(a['q']), tmaj(a['k']), jnp.swapaxes(conv0, 0, 1), w['conv_w'], w['conv_b'], tmaj(a['av']),
        tmaj(a['misc']), tmaj(a['ga']), w['norm_a'], jnp.swapaxes(n0, 0, 1),
        jnp.broadcast_to(jnp.swapaxes(m0, 0, 1)[:, :, None], (A_HEADS, db, LANES)), c0, db=db, ds=ds)
    ya = jnp.swapaxes(ya_t.reshape(ds, db, -1), 0, 1).reshape(ms, -1)

    topk = min(TOPK_MAX, (n_pages * PAGE + ds) // 4)
    pad_rows = lambda z: jnp.pad(z.reshape(db, ds, -1), ((0, 0), (0, PAGE - ds), (0, 0))).astype(BF16)
    pad_tok = lambda z: jnp.pad(z.reshape(db, ds, -1), ((0, 0), (0, 8 - ds), (0, 0)))
    iq_s = pad_tok(a['iq']).reshape(db, 8 * IDX_HEADS, IDX_DIM)
    iw_bc = jnp.broadcast_to(pad_tok(a['misc'][:, MISC_IW:MISC_IW + IDX_HEADS]).reshape(db, 8 * IDX_HEADS, 1),
                             (db, 8 * IDX_HEADS, LANES))
    largest_divisor = lambda cap: max(p for p in range(1, cap + 1) if n_pages % p == 0)
    pps_scores, pps_attend = largest_divisor(64), largest_divisor(32)
    scores = _dsa_sample_scores(page_table, iq_s, iw_bc, jnp.swapaxes(pad_rows(a['misc'][:, MISC_IK:]), 1, 2),
                                jnp.swapaxes(cache_ik, 1, 2), ds=ds, pages_per_step=pps_scores)
    bias = _dsa_sample_search(scores, topk=topk)
    q5 = a['bq'].reshape(db, ds * B_HEADS, 1, B_HD)
    kv_of_row = (jnp.arange(ds * B_HEADS) % B_HEADS) // B_GROUP
    q_bd = jnp.where((kv_of_row[None, :, None, None] == jnp.arange(B_KV)[None, None, :, None]), q5,
                     jnp.zeros((), BF16)).reshape(db, ds * B_HEADS, B_KV * B_HD)
    attn = _dsa_sample_attend(page_table, q_bd, bias, pad_rows(a['kf']), pad_rows(a['vf']),
                              a['zb'].reshape(db, ds * B_HEADS, B_HD),
                              cache_k.reshape(-1, PAGE * B_KV, B_HD), cache_v.reshape(-1, PAGE * B_KV, B_HD),
                              ds=ds, pages_per_step=pps_attend)
    yb = attn.reshape(ms, B_WIDTH)
    y = _outproj(ya, yb, a['sga'], a['sgb'], x.reshape(ms, d), pe.reshape(ms, -1), w['wa'], w['wb'], w['wo'],
                 w['wg'], w['wp'], w['norm_post'], tm=ms)
    qk_pre = jnp.concatenate([a['q'], a['k']], axis=1).reshape(db, ds, 2 * A_QK)
    return dict(
        y=y.reshape(db, ds, d),
        k=a['kf'].reshape(1, db, ds, B_KV, B_HD),
        v=a['vf'].reshape(1, db, ds, B_KV, B_HD),
        ik=a['misc'][:, MISC_IK:].reshape(1, db, ds, IDX_DIM),
        C=c_new[None],
        n=jnp.swapaxes(n_new, 0, 1)[None],
        m=jnp.swapaxes(m_new[:, :, 0], 0, 1)[None],
        conv=qk_pre[:, ds - (CONV_W - 1):, :][None],
    )


def _prep_weights(w_in, conv_w, conv_b, if_bias, norm_a, w_a_proj, w_b_proj, w_out, norm_pre, norm_post, w_ple,
                  w_ple_gate):
    d = w_in.shape[1]
    misc_bias = jnp.zeros((1, LANES), F32).at[0, :2 * A_HEADS].set(if_bias[0])
    w = dict(norm_pre=norm_pre[0][None], norm_post=norm_post[0][None], norm_a=norm_a[0][None],
             conv_w=conv_w[0], conv_b=conv_b[0][None],
             wa=w_a_proj[0].astype(BF16), wb=w_b_proj[0].astype(BF16), wo=w_out[0].astype(BF16),
             wg=w_ple_gate[0].astype(BF16), wp=w_ple[0].astype(BF16))
    return w, _pack_w_in(w_in[0], d), misc_bias


def kernel(x_prompt, x_sample, cache_k, cache_v, cache_idx_k, page_table, state_C, state_n, state_m, state_conv,
           p_prompt, p_sample, w_in, conv_w, conv_b, if_bias, norm_a, w_a_proj, w_b_proj, w_out, norm_pre,
           norm_post, w_ple, w_ple_gate):
    w, w_packed, misc_bias = _prep_weights(w_in, conv_w, conv_b, if_bias, norm_a, w_a_proj, w_b_proj, w_out,
                                           norm_pre, norm_post, w_ple, w_ple_gate)
    p = _prompt_path(x_prompt, p_prompt[0], w, w_packed, misc_bias)
    s = _sample_path(x_sample, p_sample[0], cache_k[0], cache_v[0], cache_idx_k[0], page_table, state_C[0],
                     state_n[0], state_m[0], state_conv[0], w, w_packed, misc_bias)
    names = ('k', 'v', 'ik', 'C', 'n', 'm', 'conv')
    return (p['y'], s['y']) + tuple(p[n] for n in names) + tuple(s[n] for n in names)
```

```python
import functools

import jax
import jax.numpy as jnp
from jax import lax
from jax.experimental import pallas as pl
from jax.experimental.pallas import tpu as pltpu

F32 = jnp.float32
BF16 = jnp.bfloat16
I32 = jnp.int32

EPS = 1e-6
A_HEADS = 4
A_DQK = 128
A_DV = 256
A_QK = A_HEADS * A_DQK
A_WIDTH = A_HEADS * A_DV
CONV_W = 4
B_HEADS = 8
B_KV = 2
B_HD = 128
B_GROUP = B_HEADS // B_KV
B_WIDTH = B_HEADS * B_HD
IDX_HEADS = 8
IDX_DIM = 64
TOPK_MAX = 256
PAGE = 128
LANES = 128
VMEM_LIMIT = 56 * 1024 * 1024

IN_PROJ_ROWS = 2 * PAGE
OUT_PROJ_ROWS = 4 * PAGE
IDX_GROUP = 2
SEARCH_SEQS = 8
SCORE_PAGES = 64
ATTEND_PAGES = 32

Q_SCALE = A_DQK ** -0.5
ATT_SCALE = B_HD ** -0.5
IDX_SCALE = IDX_DIM ** -0.5
IW_SCALE = IDX_HEADS ** -0.5
C_EXP = ATT_SCALE * 1.4426950408889634
VT_ROWS = B_HD + 16
NEG_BIG = -1e30
INT_MIN = -(2 ** 31)
KEY_NEG_INF = INT_MIN + 0x007FFFFF

_SPLITS = (('a_q', A_QK), ('a_k', A_QK), ('a_v', A_WIDTH), ('a_i', A_HEADS), ('a_f', A_HEADS),
           ('a_o', A_WIDTH), ('a_z', A_WIDTH), ('b_q', B_WIDTH), ('b_k', B_KV * B_HD), ('b_v', B_KV * B_HD),
           ('b_iq', IDX_HEADS * IDX_DIM), ('b_ik', IDX_DIM), ('b_iw', IDX_HEADS), ('b_z', B_WIDTH),
           ('g_a', 1024), ('g_b', 1024))

P_QK = 0
P_AV = 1024
P_AO = 2048
P_AZ = 3072
P_BQ = 4096
P_BK = 5120
P_BV = 5376
P_IQ = 5632
P_MISC = 6144
P_BZ = 6272
P_GA = 7296
P_GB = 8320
P_COLS = 9344
MISC_IW = 8
MISC_IK = 64


def _sigmoid(x):
    return 1.0 / (1.0 + jnp.exp(-x))


def _silu(x):
    return x * _sigmoid(x)


def _dot(a, b):
    return jnp.dot(a, b, preferred_element_type=F32)


def _dot_nt(a, b):
    return lax.dot_general(a, b, (((1,), (1,)), ((), ())), preferred_element_type=F32)


def _transpose(x):
    r, c = x.shape
    rows = []
    for j in range(c // LANES):
        rows.append(jnp.concatenate([x[i * LANES:(i + 1) * LANES, j * LANES:(j + 1) * LANES].T
                                     for i in range(r // LANES)], axis=1))
    return jnp.concatenate(rows, axis=0)


def _pack_w_in(w_in, d_model):
    offs = {}
    c = 0
    for name, n in _SPLITS:
        offs[name] = (c, c + n)
        c += n

    def col(name):
        lo, hi = offs[name]
        return w_in[:, lo:hi]

    z = lambda n: jnp.zeros((d_model, n), w_in.dtype)
    misc = jnp.concatenate([col('a_i'), col('a_f'), col('b_iw'), z(MISC_IK - 16), col('b_ik')], axis=1)
    segs = (w_in[:, offs['a_q'][0]:offs['a_v'][1]], w_in[:, offs['a_o'][0]:offs['b_iq'][1]], misc,
            w_in[:, offs['b_z'][0]:offs['g_b'][1]])
    assert [s.shape[1] for s in segs] == [P_AO - P_QK, P_MISC - P_AO, P_BZ - P_MISC, P_COLS - P_BZ]
    return tuple(s.astype(BF16) for s in segs)


def _inproj_kernel(x_ref, g_ref, w0_ref, w1_ref, w2_ref, w3_ref, cw_ref, cb_ref, mb_ref, *refs, names, prompt,
                   tiles_per_seq):
    o = dict(zip(names, refs))
    hist_ref = refs[len(names)]
    tm = x_ref.shape[0]
    x = x_ref[...]
    ms = jnp.mean(x * x, axis=-1, keepdims=True)
    xn = (x * lax.rsqrt(ms + EPS) * g_ref[...]).astype(BF16)

    segments = ((P_QK, w0_ref), (P_AO, w1_ref), (P_MISC, w2_ref), (P_BZ, w3_ref))

    def proj(lo, hi):
        start, ref = [s for s in segments if s[0] <= lo][-1]
        return _dot(xn, ref[:, lo - start:hi - start])

    qk = proj(P_QK, P_QK + 2 * A_QK)
    if prompt:
        o['tail'][0] = qk[tm - 8:tm, :]

        @pl.when(pl.program_id(0) % tiles_per_seq == 0)
        def _():
            hist_ref[0:8, :] = jnp.zeros((8, 2 * A_QK), F32)

        hist_ref[8:8 + tm, :] = qk
        cw = cw_ref[...]
        y = cb_ref[...] + cw[0:1, :] * hist_ref[5:5 + tm, :]
        y = y + cw[1:2, :] * hist_ref[6:6 + tm, :]
        y = y + cw[2:3, :] * hist_ref[7:7 + tm, :]
        y = y + cw[3:4, :] * qk
        act = _silu(y)
        o['q'][...] = (act[:, :A_QK] * Q_SCALE).astype(BF16)
        o['k'][...] = act[:, A_QK:].astype(BF16)
        for p in range(tm // (2 * PAGE)):
            o['kt'][p] = _transpose(act[p * 2 * PAGE:(p + 1) * 2 * PAGE, A_QK:]).astype(BF16)
        hist_ref[0:8, :] = qk[tm - 8:tm, :]
    else:
        o['q'][...] = qk[:, :A_QK]
        o['k'][...] = qk[:, A_QK:]

    o['av'][...] = proj(P_AV, P_AV + A_WIDTH).astype(BF16)
    o['ga'][...] = (_sigmoid(proj(P_AO, P_AO + A_WIDTH)) * _silu(proj(P_AZ, P_AZ + A_WIDTH))).astype(BF16)
    bq = proj(P_BQ, P_BQ + B_WIDTH) * C_EXP
    iq = proj(P_IQ, P_IQ + IDX_HEADS * IDX_DIM)
    kf = proj(P_BK, P_BK + B_KV * B_HD)
    vf = proj(P_BV, P_BV + B_KV * B_HD)
    for kv in range(B_KV):
        o['kf'][pl.ds(kv, tm, stride=B_KV), :] = kf[:, kv * B_HD:(kv + 1) * B_HD]
        o['vf'][pl.ds(kv, tm, stride=B_KV), :] = vf[:, kv * B_HD:(kv + 1) * B_HD]
    if prompt:
        for r in range(tm // PAGE):
            rows = slice(r * PAGE, (r + 1) * PAGE)
            o['bq'][r] = _transpose(bq[rows, :]).astype(BF16)
            o['iq'][r] = _transpose(iq[rows, :]).astype(BF16)
        ones = jnp.ones((VT_ROWS - B_HD, 2 * PAGE), BF16)
        for p in range(tm // (2 * PAGE)):
            rows = slice(p * 2 * PAGE, (p + 1) * 2 * PAGE)
            parts = []
            for g in range(B_KV):
                parts += [_transpose(vf[rows, g * B_HD:(g + 1) * B_HD]).astype(BF16), ones]
            o['vt'][p] = jnp.concatenate(parts, axis=0)
        o['kb'][...] = kf.astype(BF16)
    else:
        o['bq'][...] = bq.astype(BF16)
        o['iq'][...] = iq.astype(BF16)

    raw = proj(P_MISC, P_MISC + LANES) + mb_ref[...]
    lane = lax.broadcasted_iota(I32, raw.shape, 1)
    logsig = jnp.minimum(raw, 0.0) - jnp.log(1.0 + jnp.exp(-jnp.abs(raw)))
    o['misc'][...] = jnp.where((lane >= A_HEADS) & (lane < 2 * A_HEADS), logsig,
                               jnp.where((lane >= MISC_IW) & (lane < MISC_IW + IDX_HEADS),
                                         raw * (IW_SCALE * IDX_SCALE), raw))
    if prompt:
        o['miscb'][...] = jnp.where(lane >= MISC_IK, raw, 0.0).astype(BF16)

    o['zb'][...] = _silu(proj(P_BZ, P_BZ + B_WIDTH)).astype(BF16)
    o['sga'][...] = _sigmoid(proj(P_GA, P_GA + 1024)).astype(BF16)
    o['sgb'][...] = _sigmoid(proj(P_GB, P_GB + 1024)).astype(BF16)


def _inproj(x2d, norm_pre, w_packed, conv_w, conv_b, misc_bias, *, prompt, seq_len, tm):
    m, d = x2d.shape
    nt = m // tm
    rows = lambda w, dt: (jax.ShapeDtypeStruct((m, w), dt), pl.BlockSpec((tm, w), lambda i: (i, 0)))
    tiles = lambda n, r, c, dt: (jax.ShapeDtypeStruct((m // n, r, c), dt),
                                 pl.BlockSpec((tm // n, r, c), lambda i: (i, 0, 0)))
    kv_rows = (jax.ShapeDtypeStruct((B_KV * m, B_HD), F32), pl.BlockSpec((B_KV * tm, B_HD), lambda i: (i, 0)))
    outs = dict(q=rows(A_QK, BF16 if prompt else F32), k=rows(A_QK, BF16 if prompt else F32), av=rows(A_WIDTH, BF16),
                ga=rows(A_WIDTH, BF16), kf=kv_rows, vf=kv_rows, misc=rows(LANES, F32),
                zb=rows(B_WIDTH, BF16), sga=rows(d, BF16), sgb=rows(d, BF16))
    if prompt:
        outs.update(tail=tiles(tm, 8, 2 * A_QK, F32), bq=tiles(PAGE, B_WIDTH, PAGE, BF16),
                    iq=tiles(PAGE, IDX_HEADS * IDX_DIM, PAGE, BF16),
                    vt=tiles(2 * PAGE, B_KV * VT_ROWS, 2 * PAGE, BF16), kt=tiles(2 * PAGE, A_QK, 2 * PAGE, BF16),
                    kb=rows(B_KV * B_HD, BF16),
                    miscb=rows(LANES, BF16))
    else:
        outs.update(bq=rows(B_WIDTH, BF16), iq=rows(IDX_HEADS * IDX_DIM, BF16))
    names = tuple(outs)
    const = lambda shape: pl.BlockSpec(shape, lambda i: (0,) * len(shape))
    res = pl.pallas_call(
        functools.partial(_inproj_kernel, names=names, prompt=prompt, tiles_per_seq=max(seq_len // tm, 1)),
        grid=(nt,),
        in_specs=[pl.BlockSpec((tm, d), lambda i: (i, 0)), const((1, d))]
        + [pl.BlockSpec(seg.shape, lambda i: (0, 0), pipeline_mode=pl.Buffered(1)) for seg in w_packed]
        + [const((CONV_W, 2 * A_QK)), const((1, 2 * A_QK)), const((1, LANES))],
        out_specs=[outs[n][1] for n in names],
        out_shape=[outs[n][0] for n in names],
        scratch_shapes=[pltpu.VMEM((8 + tm, 2 * A_QK), F32)],
        compiler_params=pltpu.CompilerParams(dimension_semantics=("arbitrary",), vmem_limit_bytes=VMEM_LIMIT),
        name="inproj_prompt" if prompt else "inproj_decode",
    )(x2d, norm_pre, *w_packed, conv_w, conv_b, misc_bias)
    return dict(zip(names, res))


def _outproj_kernel(ya_ref, yb_ref, sga_ref, sgb_ref, x_ref, pe_ref, wa_ref, wb_ref, wo_ref, wg_ref, wp_ref,
                    g_ref, o_ref):
    merged = sga_ref[...] * _dot(ya_ref[...], wa_ref[...]) + sgb_ref[...] * _dot(yb_ref[...], wb_ref[...])
    z = _dot(merged.astype(BF16), wo_ref[...])
    ms = jnp.mean(z * z, axis=-1, keepdims=True)
    x1 = x_ref[...] + z * lax.rsqrt(ms + EPS) * g_ref[...]
    gate = _sigmoid(_dot(x1.astype(BF16), wg_ref[...]))
    o_ref[...] = x1 + _dot(pe_ref[...].astype(BF16), wp_ref[...]) * gate


def _outproj(ya, yb, sga, sgb, x2d, pe2d, wa, wb, wo, wg, wp, norm_post, *, tm):
    m, d = x2d.shape
    pd = pe2d.shape[1]
    row = lambda w: pl.BlockSpec((tm, w), lambda i: (i, 0))
    const = lambda shape: pl.BlockSpec(shape, lambda i: (0,) * len(shape))
    return pl.pallas_call(
        _outproj_kernel,
        grid=(m // tm,),
        in_specs=[row(A_WIDTH), row(B_WIDTH), row(d), row(d), row(d), row(pd),
                  const((A_WIDTH, d)), const((B_WIDTH, d)), const((d, d)), const((d, d)), const((pd, d)),
                  const((1, d))],
        out_specs=row(d),
        out_shape=jax.ShapeDtypeStruct((m, d), F32),
        compiler_params=pltpu.CompilerParams(dimension_semantics=("arbitrary",), vmem_limit_bytes=VMEM_LIMIT),
        name="outproj",
    )(ya, yb, sga, sgb, x2d, pe2d, wa, wb, wo, wg, wp, norm_post)


def _cumsum_rows(x):
    n = x.shape[0]
    row = lax.broadcasted_iota(I32, x.shape, 0)
    s = 1
    while s < n:
        x = x + jnp.where(row >= s, pltpu.roll(x, s, axis=0), 0.0)
        s *= 2
    return x


def _lane_col(x, lane_iota, idx):
    return jnp.sum(jnp.where(lane_iota == idx, x, 0.0), axis=1, keepdims=True)


def _mlstm_prompt_kernel(q_ref, k_ref, kt_ref, v_ref, misc_ref, ga_ref, na_ref, ya_ref, c_out_ref, nm_out_ref,
                         c_ref, n_ref, m_ref):
    c_idx = pl.program_id(1)
    nb, L = q_ref.shape[0], q_ref.shape[1]

    @pl.when(c_idx == 0)
    def _():
        c_ref[...] = jnp.zeros(c_ref.shape, F32)
        n_ref[...] = jnp.zeros(n_ref.shape, F32)
        m_ref[...] = jnp.zeros(m_ref.shape, F32)

    lane = lax.broadcasted_iota(I32, (L, LANES), 1)
    t_i = lax.broadcasted_iota(I32, (L, L), 0)
    s_i = lax.broadcasted_iota(I32, (L, L), 1)
    causal = s_i <= t_i

    for sq in range(nb):
        misc = misc_ref[sq]
        cs = _cumsum_rows(misc)
        b_al = pltpu.roll(cs, LANES - A_HEADS, axis=1)
        xa = jnp.where(lane < A_HEADS, misc - b_al, cs)
        xt = _transpose(xa)

        for h in range(A_HEADS):
            a_col = _lane_col(xa, lane, h)
            b_col = _lane_col(xa, lane, A_HEADS + h)
            a_row = xt[h:h + 1, :]
            m_prev = m_ref[sq, h:h + 1, 0:1]
            logit = jnp.where(causal, a_row, -jnp.inf)
            g_col = jnp.maximum(m_prev, jnp.max(logit, axis=1, keepdims=True))
            dmat = jnp.exp(logit - g_col)
            w_inter = jnp.exp(m_prev - g_col)
            qh = q_ref[sq, :, h * A_DQK:(h + 1) * A_DQK]
            kh = k_ref[sq, :, h * A_DQK:(h + 1) * A_DQK]
            vh = v_ref[sq, :, h * A_DV:(h + 1) * A_DV]
            c_h = c_ref[sq, h]
            n_h = n_ref[sq, h:h + 1, :]
            s = _dot_nt(qh, kh) * dmat
            num = _dot(s.astype(BF16), vh) + w_inter * _dot(qh, c_h.astype(BF16))
            den = (jnp.sum(s, axis=1, keepdims=True)
                   + w_inter * jnp.sum(qh.astype(F32) * n_h, axis=1, keepdims=True))
            m_t = b_col + g_col
            hh = num / jnp.maximum(jnp.abs(den), jnp.exp(-m_t))
            hn = hh * lax.rsqrt(jnp.mean(hh * hh, axis=1, keepdims=True) + EPS)
            sl = slice(h * A_DV, (h + 1) * A_DV)
            ya_ref[sq, :, sl] = (ga_ref[sq, :, sl] * (hn * na_ref[:, sl])).astype(ya_ref.dtype)

            g_last = g_col[L - 1:L, :]
            w_prev = jnp.exp(m_prev - g_last)
            w_s = jnp.exp(a_col - g_last)
            vw = vh.astype(F32) * w_s
            c_ref[sq, h] = w_prev * c_h + _dot(kt_ref[sq, h * A_DQK:(h + 1) * A_DQK, :], vw.astype(BF16))
            n_ref[sq, h:h + 1, :] = w_prev * n_h + jnp.sum(kh.astype(F32) * w_s, axis=0, keepdims=True)
            m_ref[sq, h:h + 1, :] = jnp.broadcast_to(b_col[L - 1:L, :] + g_last, (1, LANES))

    @pl.when(c_idx == pl.num_programs(1) - 1)
    def _():
        for sq in range(nb):
            for h in range(A_HEADS):
                c_out_ref[sq, h] = _transpose(c_ref[sq, h])
        nm_out_ref[:, 0:A_HEADS, :] = n_ref[:, 0:A_HEADS, :]
        nm_out_ref[:, A_HEADS:2 * A_HEADS, :] = m_ref[:, 0:A_HEADS, :]


def _mlstm_prompt(q, k, kt, v, misc, ga, norm_a, *, batch, seq_len, chunk):
    nc = seq_len // chunk
    nb = 1
    seq3 = lambda a: a.reshape(batch, seq_len, a.shape[-1])
    blk = lambda w: pl.BlockSpec((nb, chunk, w), lambda b, c: (b, c, 0))
    ya, c_out, nm_out = pl.pallas_call(
        _mlstm_prompt_kernel,
        grid=(batch // nb, nc),
        in_specs=[blk(A_QK), blk(A_QK), pl.BlockSpec((nb, A_QK, chunk), lambda b, c: (b * nc + c, 0, 0)),
                  blk(A_WIDTH), blk(LANES), blk(A_WIDTH), pl.BlockSpec((1, A_WIDTH), lambda b, c: (0, 0))],
        out_specs=[blk(A_WIDTH),
                   pl.BlockSpec((nb, A_HEADS, A_DV, A_DQK), lambda b, c: (b, 0, 0, 0)),
                   pl.BlockSpec((nb, 8, LANES), lambda b, c: (b, 0, 0))],
        out_shape=[jax.ShapeDtypeStruct((batch, seq_len, A_WIDTH), BF16),
                   jax.ShapeDtypeStruct((batch, A_HEADS, A_DV, A_DQK), F32),
                   jax.ShapeDtypeStruct((batch, 8, LANES), F32)],
        scratch_shapes=[pltpu.VMEM((nb, A_HEADS, A_DQK, A_DV), F32), pltpu.VMEM((nb, 8, LANES), F32),
                        pltpu.VMEM((nb, 8, LANES), F32)],
        compiler_params=pltpu.CompilerParams(dimension_semantics=("arbitrary", "arbitrary"),
                                             vmem_limit_bytes=VMEM_LIMIT),
        name="mlstm_prompt",
    )(seq3(q), seq3(k), kt, seq3(v), seq3(misc), seq3(ga), norm_a)
    return ya.reshape(batch * seq_len, A_WIDTH), c_out, nm_out


def _key_to_f32(key):
    return pltpu.bitcast(jnp.where(key < 0, key ^ 0x7FFFFFFF, key), F32)


def _radix_threshold(count_ge, shape, topk):
    def bit_body(i, carry):
        u, n_u = carry
        cand_u = u | jnp.left_shift(jnp.int32(1), 31 - i)
        cnt = count_ge(_key_to_f32(cand_u ^ INT_MIN))
        ok = cnt >= topk
        return jnp.where(ok, cand_u, u), jnp.where(ok, cnt, n_u)

    u, n_u = lax.fori_loop(0, 32, bit_body, (jnp.zeros(shape, I32), jnp.zeros(shape, I32)))
    key = jnp.maximum(u ^ INT_MIN, KEY_NEG_INF)
    return _key_to_f32(key), _key_to_f32(key + 1), n_u


def _refine_in_bin(count_ge, thr, thr_up, topk):
    lo, hi = thr, thr_up
    for _ in range(6):
        mid = lo + 0.5 * (hi - lo)
        ok = count_ge(mid) >= topk
        lo = jnp.where(ok, mid, lo)
        hi = jnp.where(ok, hi, mid)
    return lo


def _dsa_prompt_kernel(bq_ref, iq_ref, misc_ref, zb_ref, ikb_ref, kb_ref, vt_ref, yb_ref,
                       sc_ref, s_ref, s2_ref, p_ref, p2_ref, al_ref, al2_ref, acc_ref, m_ref, *, topk, nq):
    j = pl.program_id(1)
    T = PAGE
    KT = 2 * PAGE
    nkt = j // 2 + 1

    iq_t = iq_ref[0]
    zeros_pad = jnp.zeros((LANES - IDX_DIM, T), BF16)
    iq_pad = [jnp.concatenate([zeros_pad, iq_t[h * IDX_DIM:(h + 1) * IDX_DIM, :]], axis=0)
              for h in range(IDX_HEADS)]
    iq_wide = [jnp.concatenate(iq_pad[h:h + IDX_GROUP], axis=1) for h in range(0, IDX_HEADS, IDX_GROUP)]
    misc_t = _transpose(misc_ref[...])
    q_t = bq_ref[0]
    q_grp = [jnp.concatenate([q_t[(g * B_GROUP + hh) * B_HD:(g * B_GROUP + hh + 1) * B_HD, :]
                              for hh in range(B_GROUP)], axis=1) for g in range(B_KV)]

    row_i = lax.broadcasted_iota(I32, (KT, T), 0)
    lane_i = lax.broadcasted_iota(I32, (KT, T), 1)

    last_pair = nq // 2 - 1

    def idx_body(c, carry):
        ikc = ikb_ref[0, c]
        acc = jnp.zeros((KT, T), F32)
        for h0 in range(0, IDX_HEADS, IDX_GROUP):
            prod = _dot(ikc, iq_wide[h0 // IDX_GROUP])
            for i in range(IDX_GROUP):
                r = jnp.maximum(prod[:, i * T:(i + 1) * T], 0.0)
                acc = acc + r * misc_t[MISC_IW + h0 + i:MISC_IW + h0 + i + 1, :]
        vis = (c * KT + row_i) <= (j * T + lane_i)
        sc_ref[c] = jnp.where(vis, acc, -jnp.inf)
        return carry

    lax.fori_loop(0, nkt, idx_body, 0)

    def count(pred):
        def body(c, cnt):
            m = pred(sc_ref[c], c).reshape(KT // 32, 4, 8, T)
            for v in range(KT // 32):
                cnt = jnp.where(m[v], cnt + 1, cnt)
            return cnt
        cnt4 = lax.fori_loop(0, nkt, body, jnp.zeros((4, 8, T), I32))
        return jnp.sum(jnp.sum(cnt4, axis=0), axis=0, keepdims=True)

    count_ge = lambda t: count(lambda sc, c: sc >= t)
    thr0, thr_up, n_ge = _radix_threshold(count_ge, (1, T), topk)

    def tie_search(_):
        thr_t = _refine_in_bin(count_ge, thr0, thr_up, topk)
        need = topk - count(lambda sc, c: sc > thr_t)
        nbits = (nq * T - 1).bit_length()
        def pbody(i, p):
            cand = p | jnp.left_shift(jnp.int32(1), nbits - 1 - i)
            cnt = count(lambda sc, c: (sc == thr_t) & ((c * KT + row_i) < cand))
            return jnp.where(cnt < need, cand, p)
        return thr_t, lax.fori_loop(0, nbits, pbody, jnp.zeros((1, T), I32))

    has_tie = jnp.max(jnp.where((n_ge > topk) & (thr0 > -jnp.inf), 1, 0)) > 0
    thr, p_lim = lax.cond(has_tie, tie_search, lambda _: (thr0, jnp.full((1, T), 2 ** 30, I32)), 0)

    acc_ref[...] = jnp.zeros(acc_ref.shape, F32)
    m_ref[...] = jnp.full(m_ref.shape, NEG_BIG, F32)

    def qk_scores(c, dst_ref):
        kc = kb_ref[0, jnp.minimum(c, last_pair)]
        for g in range(B_KV):
            dst_ref[g] = _dot(kc[:, g * B_HD:(g + 1) * B_HD], q_grp[g])

    def softmax(c, src_ref, p_dst, al_dst):
        sc = sc_ref[c]
        pos = c * KT + row_i
        sel = ((sc > thr) | ((sc == thr) & (pos <= p_lim))) & (sc > -jnp.inf)
        bias = jnp.where(sel, 0.0, NEG_BIG)
        for h in range(B_HEADS):
            g, hh = divmod(h, B_GROUP)
            s = src_ref[g, :, hh * T:(hh + 1) * T] + bias
            m_old = m_ref[h:h + 1, :]
            m_new = jnp.maximum(m_old, jnp.max(s, axis=0, keepdims=True))
            al_dst[h:h + 1, :] = jnp.exp2(m_old - m_new)
            p_dst[h // 2, :, (h % 2) * T:(h % 2 + 1) * T] = jnp.exp2(s - m_new).astype(BF16)
            m_ref[h:h + 1, :] = m_new

    def pv_update(c, p_src, al_src):
        vtc = vt_ref[0, c]
        for h0 in range(0, B_HEADS, 2):
            g = h0 // B_GROUP
            pv = _dot(vtc[g * VT_ROWS:(g + 1) * VT_ROWS, :], p_src[h0 // 2])
            for i, h in enumerate((h0, h0 + 1)):
                acc_ref[h] = acc_ref[h] * al_src[h:h + 1, :] + pv[:, i * T:(i + 1) * T]

    p2_ref[...] = jnp.zeros(p2_ref.shape, BF16)
    al2_ref[...] = jnp.ones(al2_ref.shape, F32)
    qk_scores(0, s_ref)

    def att_body(i, carry):
        c0 = 2 * i
        qk_scores(c0 + 1, s2_ref)
        softmax(c0, s_ref, p_ref, al_ref)
        pv_update(jnp.maximum(c0 - 1, 0), p2_ref, al2_ref)

        @pl.when(c0 + 1 < nkt)
        def _():
            qk_scores(c0 + 2, s_ref)
            softmax(c0 + 1, s2_ref, p2_ref, al2_ref)
            pv_update(c0, p_ref, al_ref)

        return carry

    lax.fori_loop(0, (nkt + 1) // 2, att_body, 0)

    @pl.when(nkt % 2 == 1)
    def _():
        pv_update(nkt - 1, p_ref, al_ref)

    @pl.when(nkt % 2 == 0)
    def _():
        pv_update(nkt - 1, p2_ref, al2_ref)

    for h in range(B_HEADS):
        o_t = acc_ref[h, 0:B_HD, :] / acc_ref[h, B_HD:B_HD + 1, :]
        sl = slice(h * B_HD, (h + 1) * B_HD)
        yb_ref[:, sl] = (_transpose(o_t) * zb_ref[:, sl]).astype(yb_ref.dtype)


def _dsa_prompt(bq, iq, misc, zb, ikb, kb, vt, *, batch, seq_len, topk):
    nq = seq_len // PAGE
    kt = 2 * PAGE
    scores = pltpu.VMEM((B_KV, kt, B_GROUP * PAGE), F32)
    weights = pltpu.VMEM((B_HEADS // 2, kt, 2 * PAGE), BF16)
    rescale = pltpu.VMEM((B_HEADS, PAGE), F32)
    row = lambda w: pl.BlockSpec((PAGE, w), lambda b, j: (b * nq + j, 0))
    tile_t = lambda r: pl.BlockSpec((1, r, PAGE), lambda b, j: (b * nq + j, 0, 0))
    per_b = lambda a: pl.BlockSpec((1,) + a.shape[1:], lambda b, j: (b, 0, 0, 0))
    return pl.pallas_call(
        functools.partial(_dsa_prompt_kernel, topk=topk, nq=nq),
        grid=(batch, nq),
        in_specs=[tile_t(B_WIDTH), tile_t(IDX_HEADS * IDX_DIM), row(LANES), row(B_WIDTH),
                  per_b(ikb), per_b(kb), per_b(vt)],
        out_specs=row(B_WIDTH),
        out_shape=jax.ShapeDtypeStruct((batch * seq_len, B_WIDTH), BF16),
        scratch_shapes=[pltpu.VMEM((nq // 2, kt, PAGE), F32), scores, scores, weights, weights, rescale, rescale,
                        pltpu.VMEM((B_HEADS, VT_ROWS, PAGE), F32),
                        pltpu.VMEM((B_HEADS, PAGE), F32)],
        compiler_params=pltpu.CompilerParams(dimension_semantics=("arbitrary", "arbitrary"),
                                             vmem_limit_bytes=VMEM_LIMIT),
        name="dsa_prompt",
    )(bq, iq, misc, zb, ikb, kb, vt)


def _prompt_path(x, pe, w, w_packed, misc_bias):
    batch, seq_len, d = x.shape
    m = batch * seq_len
    x2d = x.reshape(m, d)
    tm = min(IN_PROJ_ROWS, seq_len)
    a = _inproj(x2d, w['norm_pre'], w_packed, w['conv_w'], w['conv_b'], misc_bias,
                prompt=True, seq_len=seq_len, tm=tm)
    chunk = IN_PROJ_ROWS
    ya, c_out, nm_out = _mlstm_prompt(a['q'], a['k'], a['kt'], a['av'], a['misc'], a['ga'], w['norm_a'],
                                      batch=batch, seq_len=seq_len, chunk=chunk)
    nk2 = seq_len // (2 * PAGE)
    ikb = a['miscb'].reshape(batch, nk2, 2 * PAGE, LANES)
    kb = a['kb'].reshape(batch, nk2, 2 * PAGE, B_KV * B_HD)
    vt = a['vt'].reshape(batch, nk2, B_KV * VT_ROWS, 2 * PAGE)
    topk = min(TOPK_MAX, seq_len // 4)
    yb = _dsa_prompt(a['bq'], a['iq'], a['misc'], a['zb'], ikb, kb, vt, batch=batch, seq_len=seq_len, topk=topk)
    y = _outproj(ya, yb, a['sga'], a['sgb'], x2d, pe.reshape(m, -1), w['wa'], w['wb'], w['wo'], w['wg'], w['wp'],
                 w['norm_post'], tm=min(OUT_PROJ_ROWS, m))
    tiles_per_seq = seq_len // tm
    tail = a['tail'].reshape(batch, tiles_per_seq, 8, 2 * A_QK)[:, -1, 8 - (CONV_W - 1):, :]
    return dict(
        y=y.reshape(batch, seq_len, d),
        k=a['kf'].reshape(1, batch, seq_len, B_KV, B_HD),
        v=a['vf'].reshape(1, batch, seq_len, B_KV, B_HD),
        ik=a['misc'][:, MISC_IK:].reshape(1, batch, seq_len, IDX_DIM),
        C=c_out[None],
        n=nm_out[None, :, 0:A_HEADS, :],
        m=nm_out[None, :, A_HEADS:2 * A_HEADS, 0],
        conv=tail[None],
    )


def _mlstm_sample_kernel(q_ref, k_ref, conv0_ref, cw_ref, cb_ref, v_ref, misc_ref, ga_ref, na_ref, n0_ref, m0_ref,
                         c0_ref, ya_ref, c_out_ref, n_out_ref, m_out_ref,
                         qs_ref, ks_ref, cq_ref, wprev_ref, vwt_ref, numi_ref, den_ref, wint_ref, enm_ref, *, db, ds):
    b = pl.program_id(0)
    ms = ds * db
    rows = lambda t: slice(t * db, (t + 1) * db)

    @pl.when(b == 0)
    def _():
        cw = cw_ref[...]
        u = [conv0_ref[j] for j in range(CONV_W - 1)]
        u += [jnp.concatenate([q_ref[rows(t), :], k_ref[rows(t), :]], axis=1) for t in range(ds)]
        for t in range(ds):
            y = cb_ref[...] + cw[0:1, :] * u[t]
            for jj in range(1, CONV_W):
                y = y + cw[jj:jj + 1, :] * u[t + jj]
            act = _silu(y)
            qs_ref[rows(t), :] = act[:, :A_QK] * Q_SCALE
            ks_ref[rows(t), :] = act[:, A_QK:]
        cq_ref[...] = jnp.zeros(cq_ref.shape, F32)

        lane = lax.broadcasted_iota(I32, (db, LANES), 1)
        for h in range(A_HEADS):
            i_t = [_lane_col(misc_ref[rows(t), :], lane, h) for t in range(ds)]
            lf_t = [_lane_col(misc_ref[rows(t), :], lane, A_HEADS + h) for t in range(ds)]
            b_t = [lf_t[0]]
            for t in range(1, ds):
                b_t.append(b_t[-1] + lf_t[t])
            a_t = [i_t[t] - b_t[t] for t in range(ds)]
            m_prev = jnp.max(m0_ref[h], axis=1, keepdims=True)
            g_t = [jnp.maximum(m_prev, a_t[0])]
            for t in range(1, ds):
                g_t.append(jnp.maximum(g_t[-1], a_t[t]))
            hq = slice(h * A_DQK, (h + 1) * A_DQK)
            hv = slice(h * A_DV, (h + 1) * A_DV)
            qh = [qs_ref[rows(t), hq] for t in range(ds)]
            kh = [ks_ref[rows(t), hq] for t in range(ds)]
            vh = [v_ref[rows(t), hv].astype(F32) for t in range(ds)]
            n0 = n0_ref[h]
            for t in range(ds):
                num = jnp.zeros((db, A_DV), F32)
                den = jnp.zeros((db, 1), F32)
                for s in range(t + 1):
                    w_ts = jnp.sum(qh[t] * kh[s], axis=1, keepdims=True) * jnp.exp(a_t[s] - g_t[t])
                    num = num + w_ts * vh[s]
                    den = den + w_ts
                w_inter = jnp.exp(m_prev - g_t[t])
                den = den + w_inter * jnp.sum(qh[t] * n0, axis=1, keepdims=True)
                numi_ref[rows(t), hv] = num
                den_ref[h, rows(t), :] = jnp.broadcast_to(den, (db, LANES))
                wint_ref[h, rows(t), :] = jnp.broadcast_to(w_inter, (db, LANES))
                enm_ref[h, rows(t), :] = jnp.broadcast_to(jnp.exp(-(b_t[t] + g_t[t])), (db, LANES))
            g_last = g_t[ds - 1]
            w_prev = jnp.exp(m_prev - g_last)
            wprev_ref[h] = jnp.broadcast_to(w_prev, (db, LANES))
            n_new = w_prev * n0
            vw = []
            for s in range(ds):
                w_s = jnp.exp(a_t[s] - g_last)
                n_new = n_new + w_s * kh[s]
                vw.append(vh[s] * w_s)
            n_out_ref[h] = n_new
            m_out_ref[h] = jnp.broadcast_to(b_t[ds - 1] + g_last, (db, LANES))
            vw_all = jnp.concatenate(vw, axis=0)
            if ms < LANES:
                vw_all = jnp.concatenate([vw_all, jnp.zeros((LANES - ms, A_DV), F32)], axis=0)
            vwt_ref[h] = _transpose(vw_all)

    mcols = vwt_ref.shape[2]
    row_i = lax.broadcasted_iota(I32, (ms, A_DQK), 0)
    col_i = lax.broadcasted_iota(I32, (A_DV, mcols), 1)
    row_mine = row_i == b
    col_mine = col_i == b
    for t in range(1, ds):
        row_mine = row_mine | (row_i == b + t * db)
        col_mine = col_mine | (col_i == b + t * db)
    for h in range(A_HEADS):
        hq = slice(h * A_DQK, (h + 1) * A_DQK)
        c0 = c0_ref[0, h]
        q_mine = jnp.where(row_mine, qs_ref[:, hq], 0.0).astype(BF16)
        cq_ref[h] = cq_ref[h] + _dot_nt(q_mine, c0.astype(BF16))
        vw_mine = jnp.where(col_mine, vwt_ref[h], 0.0).astype(BF16)
        k_all = ks_ref[:, hq]
        if mcols > ms:
            k_all = jnp.concatenate([k_all, jnp.zeros((mcols - ms, A_DQK), F32)], axis=0)
        c_out_ref[0, h] = wprev_ref[h, pl.ds(b, 1), :] * c0 + _dot(vw_mine, k_all.astype(BF16))

    @pl.when(b == pl.num_programs(0) - 1)
    def _():
        for h in range(A_HEADS):
            hv = slice(h * A_DV, (h + 1) * A_DV)
            num = numi_ref[:, hv] + wint_ref[h][:, 0:1] * cq_ref[h]
            hh = num / jnp.maximum(jnp.abs(den_ref[h][:, 0:1]), enm_ref[h][:, 0:1])
            hn = hh * lax.rsqrt(jnp.mean(hh * hh, axis=1, keepdims=True) + EPS)
            ya_ref[:, hv] = (ga_ref[:, hv] * (hn * na_ref[:, hv])).astype(ya_ref.dtype)


def _mlstm_sample(q_t, k_t, conv0_t, conv_w, conv_b, v_t, misc_t, ga_t, norm_a, n0_t, m0_bc, c0, *, db, ds):
    ms = db * ds
    mcols = max(ms, LANES)
    full = lambda a: pl.BlockSpec(a.shape, lambda b: (0,) * a.ndim)
    cblk = pl.BlockSpec((1, A_HEADS, A_DV, A_DQK), lambda b: (b, 0, 0, 0))
    hshape = jax.ShapeDtypeStruct((A_HEADS, db, LANES), F32)
    ins = (q_t, k_t, conv0_t, conv_w, conv_b, v_t, misc_t, ga_t, norm_a, n0_t, m0_bc)
    return pl.pallas_call(
        functools.partial(_mlstm_sample_kernel, db=db, ds=ds),
        grid=(db,),
        in_specs=[full(a) for a in ins] + [cblk],
        out_specs=[pl.BlockSpec((ms, A_WIDTH), lambda b: (0, 0)), cblk,
                   pl.BlockSpec((A_HEADS, db, LANES), lambda b: (0, 0, 0)),
                   pl.BlockSpec((A_HEADS, db, LANES), lambda b: (0, 0, 0))],
        out_shape=[jax.ShapeDtypeStruct((ms, A_WIDTH), BF16), jax.ShapeDtypeStruct(c0.shape, F32), hshape, hshape],
        scratch_shapes=[pltpu.VMEM((ms, A_QK), F32), pltpu.VMEM((ms, A_QK), F32),
                        pltpu.VMEM((A_HEADS, ms, A_DV), F32), pltpu.VMEM((A_HEADS, db, LANES), F32),
                        pltpu.VMEM((A_HEADS, A_DV, mcols), F32), pltpu.VMEM((ms, A_WIDTH), F32),
                        pltpu.VMEM((A_HEADS, ms, LANES), F32), pltpu.VMEM((A_HEADS, ms, LANES), F32),
                        pltpu.VMEM((A_HEADS, ms, LANES), F32)],
        compiler_params=pltpu.CompilerParams(dimension_semantics=("arbitrary",), vmem_limit_bytes=VMEM_LIMIT),
        name="mlstm_sample",
    )(*ins, c0)


def _page_ring(pt_ref, pages_per_step, streams):
    b, g = pl.program_id(0), pl.program_id(1)
    nb, ng = pl.num_programs(0), pl.num_programs(1)
    step = b * ng + g
    slot = lax.rem(step, 2)

    def copies(bb, gg, sl):
        return [pltpu.make_async_copy(hbm.at[pt_ref[bb, gg * pages_per_step + i]], buf.at[sl, i], sem.at[sl])
                for hbm, buf, sem in streams for i in range(pages_per_step)]

    def start_all(cps):
        for n, cp in enumerate(cps):
            cp.start(priority=n % 2)

    @pl.when(step == 0)
    def _():
        start_all(copies(0, 0, 0))

    @pl.when(step + 1 < nb * ng)
    def _():
        wrap = g + 1 == ng
        start_all(copies(jnp.where(wrap, b + 1, b), jnp.where(wrap, 0, g + 1), 1 - slot))

    for cp in copies(b, g, slot):
        cp.wait()
    return slot


def _dsa_sample_scores_kernel(pt_ref, iq_ref, iw_ref, iknew_ref, cache_ref, sc_ref, page_buf, sem,
                              *, pages_per_step, n_pages, ds):
    g = pl.program_id(1)
    slot = _page_ring(pt_ref, pages_per_step, [(cache_ref, page_buf, sem)])
    page_refs = [page_buf.at[slot, i] for i in range(pages_per_step)]
    iq = iq_ref[0]
    iw = iw_ref[0]

    def scores(keys_t, visible):
        n = keys_t.shape[1] // PAGE
        r = jnp.maximum(_dot(iq, keys_t), 0.0) * jnp.concatenate([iw] * n, axis=1)
        sc = jnp.sum(r.reshape(8, IDX_HEADS, n * PAGE), axis=1)
        return jnp.where(visible, sc, -jnp.inf)

    t_all = lax.broadcasted_iota(I32, (8, pages_per_step * PAGE), 0)
    sc_step = scores(jnp.concatenate([r[...] for r in page_refs], axis=1).astype(BF16), t_all < ds)
    for i in range(pages_per_step):
        sc_ref[0, g * pages_per_step + i] = sc_step[:, i * PAGE:(i + 1) * PAGE]

    @pl.when(g == pl.num_programs(1) - 1)
    def _():
        t_i = lax.broadcasted_iota(I32, (8, PAGE), 0)
        s_i = lax.broadcasted_iota(I32, (8, PAGE), 1)
        sc_ref[0, n_pages] = scores(iknew_ref[0], (t_i < ds) & (s_i <= t_i))


def _dsa_sample_scores(page_table, iq_s, iw_bc, ik_new, cache_ik, *, ds, pages_per_step):
    db, n_pages = page_table.shape
    npg = n_pages // pages_per_step
    per_b = lambda a: pl.BlockSpec((1,) + a.shape[1:], lambda b, g, pt: (b,) + (0,) * (a.ndim - 1))
    return pl.pallas_call(
        functools.partial(_dsa_sample_scores_kernel, pages_per_step=pages_per_step, n_pages=n_pages, ds=ds),
        grid_spec=pltpu.PrefetchScalarGridSpec(
            num_scalar_prefetch=1, grid=(db, npg),
            in_specs=[per_b(iq_s), per_b(iw_bc), per_b(ik_new), pl.BlockSpec(memory_space=pl.ANY)],
            out_specs=pl.BlockSpec((1, n_pages + 1, 8, PAGE), lambda b, g, pt: (b, 0, 0, 0)),
            scratch_shapes=[pltpu.VMEM((2, pages_per_step, IDX_DIM, PAGE), F32), pltpu.SemaphoreType.DMA((2,))]),
        out_shape=jax.ShapeDtypeStruct((db, n_pages + 1, 8, PAGE), F32),
        compiler_params=pltpu.CompilerParams(dimension_semantics=("arbitrary", "arbitrary"),
                                             vmem_limit_bytes=VMEM_LIMIT),
        name="dsa_sample_scores",
    )(page_table, iq_s, iw_bc, ik_new, cache_ik)


def _dsa_sample_search_kernel(sc_ref, bias_ref, *, topk):
    sc = sc_ref[...]
    nb, np1 = sc.shape[0], sc.shape[1]
    pos = lax.broadcasted_iota(I32, sc.shape, 1) * PAGE + lax.broadcasted_iota(I32, sc.shape, 3)

    def count(mask):
        per_lane = jnp.sum(mask.astype(I32), axis=1, keepdims=True)
        return jnp.sum(per_lane, axis=3, keepdims=True)

    count_ge = lambda t: count(sc_ref[...] >= t)
    thr0, thr_up, n_ge = _radix_threshold(count_ge, (nb, 1, 8, 1), topk)
    nbits = (np1 * PAGE - 1).bit_length()

    def tie_search(_):
        thr_t = _refine_in_bin(count_ge, thr0, thr_up, topk)
        need = topk - count(sc > thr_t)
        tied = sc == thr_t

        def pos_body(i, p):
            cand = p | jnp.left_shift(jnp.int32(1), nbits - 1 - i)
            cnt = count(tied & (pos < cand))
            return jnp.where(cnt < need, cand, p)
        return thr_t, lax.fori_loop(0, nbits, pos_body, jnp.zeros(thr0.shape, I32))

    has_tie = jnp.max(jnp.where((n_ge > topk) & (thr0 > -jnp.inf), 1, 0)) > 0
    thr, p_lim = lax.cond(has_tie, tie_search, lambda _: (thr0, jnp.full(thr0.shape, 2 ** 30, I32)), 0)
    sel = ((sc > thr) | ((sc == thr) & (pos <= p_lim))) & (sc > -jnp.inf)
    bias_ref[...] = jnp.where(sel, 0.0, NEG_BIG)


def _dsa_sample_search(scores, *, topk):
    db = scores.shape[0]
    nb = SEARCH_SEQS if db % SEARCH_SEQS == 0 else 1
    blk = pl.BlockSpec((nb,) + scores.shape[1:], lambda i: (i, 0, 0, 0))
    return pl.pallas_call(
        functools.partial(_dsa_sample_search_kernel, topk=topk),
        grid=(db // nb,),
        in_specs=[blk],
        out_specs=blk,
        out_shape=jax.ShapeDtypeStruct(scores.shape, F32),
        compiler_params=pltpu.CompilerParams(dimension_semantics=("arbitrary",), vmem_limit_bytes=VMEM_LIMIT),
        name="dsa_sample_search",
    )(scores)


def _dsa_sample_attend_kernel(pt_ref, q_ref, bias_ref, knew_ref, vnew_ref, zb_ref, ck_ref, cv_ref, o_ref,
                              k_buf, v_buf, k_sem, v_sem, acc_ref, m_ref, l_ref, *, pages_per_step, n_pages, ds):
    g = pl.program_id(1)
    slot = _page_ring(pt_ref, pages_per_step, [(ck_ref, k_buf, k_sem), (cv_ref, v_buf, v_sem)])
    k_refs = [k_buf.at[slot, i] for i in range(pages_per_step)]
    v_refs = [v_buf.at[slot, i] for i in range(pages_per_step)]
    rows_q = ds * B_HEADS
    q = q_ref[0]

    @pl.when(g == 0)
    def _():
        acc_ref[...] = jnp.zeros(acc_ref.shape, F32)
        m_ref[...] = jnp.full(m_ref.shape, NEG_BIG, F32)
        l_ref[...] = jnp.zeros(l_ref.shape, F32)

    def step(k_keys, v_keys, bias8):
        n = k_keys.shape[0]
        bias = jnp.concatenate([jnp.broadcast_to(bias8[t:t + 1, :], (B_HEADS, n)) for t in range(ds)], axis=0)
        s = _dot_nt(q, k_keys) + bias
        m_old = m_ref[...]
        m_new = jnp.maximum(m_old, jnp.max(s, axis=1, keepdims=True))
        alpha = jnp.exp2(m_old - m_new)
        p = jnp.exp2(s - m_new)
        l_ref[...] = alpha * l_ref[...] + jnp.sum(p, axis=1, keepdims=True)
        m_ref[...] = m_new
        acc_ref[...] = alpha * acc_ref[...] + _dot(p.astype(BF16), v_keys)

    def page_keys(ref):
        return jnp.concatenate([ref[pl.ds(kv, PAGE, stride=B_KV), :] for kv in range(B_KV)], axis=1).astype(BF16)

    step(jnp.concatenate([page_keys(r) for r in k_refs], axis=0),
         jnp.concatenate([page_keys(r) for r in v_refs], axis=0),
         jnp.concatenate([bias_ref[0, g * pages_per_step + i] for i in range(pages_per_step)], axis=1))

    @pl.when(g == pl.num_programs(1) - 1)
    def _():
        step(knew_ref[0], vnew_ref[0], bias_ref[0, n_pages])
        o = acc_ref[...] / l_ref[...]
        head = lax.broadcasted_iota(I32, (rows_q, B_HD), 0) & (B_HEADS - 1)
        o_sel = o[:, 0:B_HD]
        for kv in range(1, B_KV):
            o_sel = jnp.where(head >= kv * B_GROUP, o[:, kv * B_HD:(kv + 1) * B_HD], o_sel)
        o_ref[0] = (o_sel * zb_ref[0]).astype(o_ref.dtype)


def _dsa_sample_attend(page_table, q_bd, bias, k_new, v_new, zb_s, cache_k, cache_v, *, ds, pages_per_step):
    db, n_pages = page_table.shape
    npg = n_pages // pages_per_step
    rows_q = ds * B_HEADS
    kvw = B_KV * B_HD
    per_b = lambda a: pl.BlockSpec((1,) + a.shape[1:], lambda b, g, pt: (b,) + (0,) * (a.ndim - 1))
    any_space = pl.BlockSpec(memory_space=pl.ANY)
    page_buf = pltpu.VMEM((2, pages_per_step, PAGE * B_KV, B_HD), F32)
    return pl.pallas_call(
        functools.partial(_dsa_sample_attend_kernel, pages_per_step=pages_per_step, n_pages=n_pages, ds=ds),
        grid_spec=pltpu.PrefetchScalarGridSpec(
            num_scalar_prefetch=1, grid=(db, npg),
            in_specs=[per_b(q_bd), per_b(bias), per_b(k_new), per_b(v_new), per_b(zb_s), any_space, any_space],
            out_specs=pl.BlockSpec((1, rows_q, B_HD), lambda b, g, pt: (b, 0, 0)),
            scratch_shapes=[page_buf, page_buf, pltpu.SemaphoreType.DMA((2,)), pltpu.SemaphoreType.DMA((2,)),
                            pltpu.VMEM((rows_q, kvw), F32), pltpu.VMEM((rows_q, 1), F32),
                            pltpu.VMEM((rows_q, 1), F32)]),
        out_shape=jax.ShapeDtypeStruct((db, rows_q, B_HD), BF16),
        compiler_params=pltpu.CompilerParams(dimension_semantics=("arbitrary", "arbitrary"),
                                             vmem_limit_bytes=VMEM_LIMIT),
        name="dsa_sample_attend",
    )(page_table, q_bd, bias, k_new, v_new, zb_s, cache_k, cache_v)


def _sample_path(x, pe, cache_k, cache_v, cache_ik, page_table, c0, n0, m0, conv0, w, w_packed, misc_bias):
    db, ds, d = x.shape
    ms = db * ds
    assert ds >= CONV_W - 1 and ds <= 8
    n_pages = page_table.shape[1]
    a = _inproj(x.reshape(ms, d), w['norm_pre'], w_packed, w['conv_w'], w['conv_b'], misc_bias,
                prompt=False, seq_len=ms, tm=ms)

    tmaj = lambda z: jnp.swapaxes(z.reshape(db, ds, -1), 0, 1).reshape(ms, -1)
    ya_t, c_new, n_new, m_new = _mlstm_sample(
        tmaj(a['q']), tmaj(a['k']), jnp.swapaxes(conv0, 0, 1), w['conv_w'], w['conv_b'], tmaj(a['av']),
        tmaj(a['misc']), tmaj(a['ga']), w['norm_a'], jnp.swapaxes(n0, 0, 1),
        jnp.broadcast_to(jnp.swapaxes(m0, 0, 1)[:, :, None], (A_HEADS, db, LANES)), c0, db=db, ds=ds)
    ya = jnp.swapaxes(ya_t.reshape(ds, db, -1), 0, 1).reshape(ms, -1)

    topk = min(TOPK_MAX, (n_pages * PAGE + ds) // 4)
    pad_rows = lambda z: jnp.pad(z.reshape(db, ds, -1), ((0, 0), (0, PAGE - ds), (0, 0))).astype(BF16)
    pad_tok = lambda z: jnp.pad(z.reshape(db, ds, -1), ((0, 0), (0, 8 - ds), (0, 0)))
    iq_s = pad_tok(a['iq']).reshape(db, 8 * IDX_HEADS, IDX_DIM)
    iw_bc = jnp.broadcast_to(pad_tok(a['misc'][:, MISC_IW:MISC_IW + IDX_HEADS]).reshape(db, 8 * IDX_HEADS, 1),
                             (db, 8 * IDX_HEADS, LANES))
    largest_divisor = lambda cap: max(p for p in range(1, cap + 1) if n_pages % p == 0)
    pps_scores, pps_attend = largest_divisor(SCORE_PAGES), largest_divisor(ATTEND_PAGES)
    scores = _dsa_sample_scores(page_table, iq_s, iw_bc, jnp.swapaxes(pad_rows(a['misc'][:, MISC_IK:]), 1, 2),
                                jnp.swapaxes(cache_ik, 1, 2), ds=ds, pages_per_step=pps_scores)
    bias = _dsa_sample_search(scores, topk=topk)
    q5 = a['bq'].reshape(db, ds * B_HEADS, 1, B_HD)
    kv_of_row = (jnp.arange(ds * B_HEADS) % B_HEADS) // B_GROUP
    q_bd = jnp.where((kv_of_row[None, :, None, None] == jnp.arange(B_KV)[None, None, :, None]), q5,
                     jnp.zeros((), BF16)).reshape(db, ds * B_HEADS, B_KV * B_HD)
    attn = _dsa_sample_attend(page_table, q_bd, bias, pad_rows(a['kf']), pad_rows(a['vf']),
                              a['zb'].reshape(db, ds * B_HEADS, B_HD),
                              cache_k.reshape(-1, PAGE * B_KV, B_HD), cache_v.reshape(-1, PAGE * B_KV, B_HD),
                              ds=ds, pages_per_step=pps_attend)
    yb = attn.reshape(ms, B_WIDTH)
    y = _outproj(ya, yb, a['sga'], a['sgb'], x.reshape(ms, d), pe.reshape(ms, -1), w['wa'], w['wb'], w['wo'],
                 w['wg'], w['wp'], w['norm_post'], tm=ms)
    qk_pre = jnp.concatenate([a['q'], a['k']], axis=1).reshape(db, ds, 2 * A_QK)
    return dict(
        y=y.reshape(db, ds, d),
        k=a['kf'].reshape(1, db, ds, B_KV, B_HD),
        v=a['vf'].reshape(1, db, ds, B_KV, B_HD),
        ik=a['misc'][:, MISC_IK:].reshape(1, db, ds, IDX_DIM),
        C=c_new[None],
        n=jnp.swapaxes(n_new, 0, 1)[None],
        m=jnp.swapaxes(m_new[:, :, 0], 0, 1)[None],
        conv=qk_pre[:, ds - (CONV_W - 1):, :][None],
    )


def _prep_weights(w_in, conv_w, conv_b, if_bias, norm_a, w_a_proj, w_b_proj, w_out, norm_pre, norm_post, w_ple,
                  w_ple_gate):
    d = w_in.shape[1]
    misc_bias = jnp.zeros((1, LANES), F32).at[0, :2 * A_HEADS].set(if_bias[0])
    w = dict(norm_pre=norm_pre[0][None], norm_post=norm_post[0][None], norm_a=norm_a[0][None],
             conv_w=conv_w[0], conv_b=conv_b[0][None],
             wa=w_a_proj[0].astype(BF16), wb=w_b_proj[0].astype(BF16), wo=w_out[0].astype(BF16),
             wg=w_ple_gate[0].astype(BF16), wp=w_ple[0].astype(BF16))
    return w, _pack_w_in(w_in[0], d), misc_bias


def kernel(x_prompt, x_sample, cache_k, cache_v, cache_idx_k, page_table, state_C, state_n, state_m, state_conv,
           p_prompt, p_sample, w_in, conv_w, conv_b, if_bias, norm_a, w_a_proj, w_b_proj, w_out, norm_pre,
           norm_post, w_ple, w_ple_gate):
    w, w_packed, misc_bias = _prep_weights(w_in, conv_w, conv_b, if_bias, norm_a, w_a_proj, w_b_proj, w_out,
                                           norm_pre, norm_post, w_ple, w_ple_gate)
    p = _prompt_path(x_prompt, p_prompt[0], w, w_packed, misc_bias)
    s = _sample_path(x_sample, p_sample[0], cache_k[0], cache_v[0], cache_idx_k[0], page_table, state_C[0],
                     state_n[0], state_m[0], state_conv[0], w, w_packed, misc_bias)
    names = ('k', 'v', 'ik', 'C', 'n', 'm', 'conv')
    return (p['y'], s['y']) + tuple(p[n] for n in names) + tuple(s[n] for n in names)
```

```python
import functools

import jax
import jax.numpy as jnp
from jax import lax
from jax.experimental import pallas as pl
from jax.experimental.pallas import tpu as pltpu

F32 = jnp.float32
BF16 = jnp.bfloat16
I32 = jnp.int32

EPS = 1e-6
A_HEADS = 4
A_DQK = 128
A_DV = 256
A_QK = A_HEADS * A_DQK
A_WIDTH = A_HEADS * A_DV
CONV_W = 4
B_HEADS = 8
B_KV = 2
B_HD = 128
B_GROUP = B_HEADS // B_KV
B_WIDTH = B_HEADS * B_HD
IDX_HEADS = 8
IDX_DIM = 64
TOPK_MAX = 256
PAGE = 128
LANES = 128
VMEM_LIMIT = 56 * 1024 * 1024

IN_PROJ_ROWS = 2 * PAGE
OUT_PROJ_ROWS = 4 * PAGE
IDX_GROUP = 2
SEARCH_SEQS = 16
SCORE_PAGES = 128
ATTEND_PAGES = 32

Q_SCALE = A_DQK ** -0.5
ATT_SCALE = B_HD ** -0.5
IDX_SCALE = IDX_DIM ** -0.5
IW_SCALE = IDX_HEADS ** -0.5
C_EXP = ATT_SCALE * 1.4426950408889634
VT_ROWS = B_HD + 16
NEG_BIG = -1e30
INT_MIN = -(2 ** 31)
KEY_NEG_INF = INT_MIN + 0x007FFFFF

_SPLITS = (('a_q', A_QK), ('a_k', A_QK), ('a_v', A_WIDTH), ('a_i', A_HEADS), ('a_f', A_HEADS),
           ('a_o', A_WIDTH), ('a_z', A_WIDTH), ('b_q', B_WIDTH), ('b_k', B_KV * B_HD), ('b_v', B_KV * B_HD),
           ('b_iq', IDX_HEADS * IDX_DIM), ('b_ik', IDX_DIM), ('b_iw', IDX_HEADS), ('b_z', B_WIDTH),
           ('g_a', 1024), ('g_b', 1024))

P_QK = 0
P_AV = 1024
P_AO = 2048
P_AZ = 3072
P_BQ = 4096
P_BK = 5120
P_BV = 5376
P_IQ = 5632
P_MISC = 6144
P_BZ = 6272
P_GA = 7296
P_GB = 8320
P_COLS = 9344
MISC_IW = 8
MISC_IK = 64


def _sigmoid(x):
    return 1.0 / (1.0 + jnp.exp(-x))


def _silu(x):
    return x * _sigmoid(x)


def _dot(a, b):
    return jnp.dot(a, b, preferred_element_type=F32)


def _dot_nt(a, b):
    return lax.dot_general(a, b, (((1,), (1,)), ((), ())), preferred_element_type=F32)


def _transpose(x):
    r, c = x.shape
    rows = []
    for j in range(c // LANES):
        rows.append(jnp.concatenate([x[i * LANES:(i + 1) * LANES, j * LANES:(j + 1) * LANES].T
                                     for i in range(r // LANES)], axis=1))
    return jnp.concatenate(rows, axis=0)


def _pack_w_in(w_in, d_model):
    offs = {}
    c = 0
    for name, n in _SPLITS:
        offs[name] = (c, c + n)
        c += n

    def col(name):
        lo, hi = offs[name]
        return w_in[:, lo:hi]

    z = lambda n: jnp.zeros((d_model, n), w_in.dtype)
    misc = jnp.concatenate([col('a_i'), col('a_f'), col('b_iw'), z(MISC_IK - 16), col('b_ik')], axis=1)
    segs = (w_in[:, offs['a_q'][0]:offs['a_v'][1]], w_in[:, offs['a_o'][0]:offs['b_iq'][1]], misc,
            w_in[:, offs['b_z'][0]:offs['g_b'][1]])
    assert [s.shape[1] for s in segs] == [P_AO - P_QK, P_MISC - P_AO, P_BZ - P_MISC, P_COLS - P_BZ]
    return tuple(s.astype(BF16) for s in segs)


def _inproj_kernel(x_ref, g_ref, w0_ref, w1_ref, w2_ref, w3_ref, cw_ref, cb_ref, mb_ref, *refs, names, prompt,
                   tiles_per_seq):
    o = dict(zip(names, refs))
    hist_ref = refs[len(names)]
    tm = x_ref.shape[0]
    x = x_ref[...]
    ms = jnp.mean(x * x, axis=-1, keepdims=True)
    xn = (x * lax.rsqrt(ms + EPS) * g_ref[...]).astype(BF16)

    segments = ((P_QK, w0_ref), (P_AO, w1_ref), (P_MISC, w2_ref), (P_BZ, w3_ref))

    def proj(lo, hi):
        start, ref = [s for s in segments if s[0] <= lo][-1]
        return _dot(xn, ref[:, lo - start:hi - start])

    qk = proj(P_QK, P_QK + 2 * A_QK)
    if prompt:
        o['tail'][0] = qk[tm - 8:tm, :]

        @pl.when(pl.program_id(0) % tiles_per_seq == 0)
        def _():
            hist_ref[0:8, :] = jnp.zeros((8, 2 * A_QK), F32)

        hist_ref[8:8 + tm, :] = qk
        cw = cw_ref[...]
        y = cb_ref[...] + cw[0:1, :] * hist_ref[5:5 + tm, :]
        y = y + cw[1:2, :] * hist_ref[6:6 + tm, :]
        y = y + cw[2:3, :] * hist_ref[7:7 + tm, :]
        y = y + cw[3:4, :] * qk
        act = _silu(y)
        o['q'][...] = (act[:, :A_QK] * Q_SCALE).astype(BF16)
        o['k'][...] = act[:, A_QK:].astype(BF16)
        for p in range(tm // (2 * PAGE)):
            o['kt'][p] = _transpose(act[p * 2 * PAGE:(p + 1) * 2 * PAGE, A_QK:]).astype(BF16)
        hist_ref[0:8, :] = qk[tm - 8:tm, :]
    else:
        o['q'][...] = qk[:, :A_QK]
        o['k'][...] = qk[:, A_QK:]

    o['av'][...] = proj(P_AV, P_AV + A_WIDTH).astype(BF16)
    o['ga'][...] = (_sigmoid(proj(P_AO, P_AO + A_WIDTH)) * _silu(proj(P_AZ, P_AZ + A_WIDTH))).astype(BF16)
    bq = proj(P_BQ, P_BQ + B_WIDTH) * C_EXP
    iq = proj(P_IQ, P_IQ + IDX_HEADS * IDX_DIM)
    kf = proj(P_BK, P_BK + B_KV * B_HD)
    vf = proj(P_BV, P_BV + B_KV * B_HD)
    for kv in range(B_KV):
        o['kf'][pl.ds(kv, tm, stride=B_KV), :] = kf[:, kv * B_HD:(kv + 1) * B_HD]
        o['vf'][pl.ds(kv, tm, stride=B_KV), :] = vf[:, kv * B_HD:(kv + 1) * B_HD]
    if prompt:
        for r in range(tm // PAGE):
            rows = slice(r * PAGE, (r + 1) * PAGE)
            o['bq'][r] = _transpose(bq[rows, :]).astype(BF16)
            o['iq'][r] = _transpose(iq[rows, :]).astype(BF16)
        ones = jnp.ones((VT_ROWS - B_HD, 2 * PAGE), BF16)
        for p in range(tm // (2 * PAGE)):
            rows = slice(p * 2 * PAGE, (p + 1) * 2 * PAGE)
            parts = []
            for g in range(B_KV):
                parts += [_transpose(vf[rows, g * B_HD:(g + 1) * B_HD]).astype(BF16), ones]
            o['vt'][p] = jnp.concatenate(parts, axis=0)
        o['kb'][...] = kf.astype(BF16)
    else:
        o['bq'][...] = bq.astype(BF16)
        o['iq'][...] = iq.astype(BF16)

    raw = proj(P_MISC, P_MISC + LANES) + mb_ref[...]
    lane = lax.broadcasted_iota(I32, raw.shape, 1)
    logsig = jnp.minimum(raw, 0.0) - jnp.log(1.0 + jnp.exp(-jnp.abs(raw)))
    o['misc'][...] = jnp.where((lane >= A_HEADS) & (lane < 2 * A_HEADS), logsig,
                               jnp.where((lane >= MISC_IW) & (lane < MISC_IW + IDX_HEADS),
                                         raw * (IW_SCALE * IDX_SCALE), raw))
    if prompt:
        o['miscb'][...] = jnp.where(lane >= MISC_IK, raw, 0.0).astype(BF16)

    o['zb'][...] = _silu(proj(P_BZ, P_BZ + B_WIDTH)).astype(BF16)
    o['sga'][...] = _sigmoid(proj(P_GA, P_GA + 1024)).astype(BF16)
    o['sgb'][...] = _sigmoid(proj(P_GB, P_GB + 1024)).astype(BF16)


def _inproj(x2d, norm_pre, w_packed, conv_w, conv_b, misc_bias, *, prompt, seq_len, tm):
    m, d = x2d.shape
    nt = m // tm
    rows = lambda w, dt: (jax.ShapeDtypeStruct((m, w), dt), pl.BlockSpec((tm, w), lambda i: (i, 0)))
    tiles = lambda n, r, c, dt: (jax.ShapeDtypeStruct((m // n, r, c), dt),
                                 pl.BlockSpec((tm // n, r, c), lambda i: (i, 0, 0)))
    kv_rows = (jax.ShapeDtypeStruct((B_KV * m, B_HD), F32), pl.BlockSpec((B_KV * tm, B_HD), lambda i: (i, 0)))
    outs = dict(q=rows(A_QK, BF16 if prompt else F32), k=rows(A_QK, BF16 if prompt else F32), av=rows(A_WIDTH, BF16),
                ga=rows(A_WIDTH, BF16), kf=kv_rows, vf=kv_rows, misc=rows(LANES, F32),
                zb=rows(B_WIDTH, BF16), sga=rows(d, BF16), sgb=rows(d, BF16))
    if prompt:
        outs.update(tail=tiles(tm, 8, 2 * A_QK, F32), bq=tiles(PAGE, B_WIDTH, PAGE, BF16),
                    iq=tiles(PAGE, IDX_HEADS * IDX_DIM, PAGE, BF16),
                    vt=tiles(2 * PAGE, B_KV * VT_ROWS, 2 * PAGE, BF16), kt=tiles(2 * PAGE, A_QK, 2 * PAGE, BF16),
                    kb=rows(B_KV * B_HD, BF16),
                    miscb=rows(LANES, BF16))
    else:
        outs.update(bq=rows(B_WIDTH, BF16), iq=rows(IDX_HEADS * IDX_DIM, BF16))
    names = tuple(outs)
    const = lambda shape: pl.BlockSpec(shape, lambda i: (0,) * len(shape))
    res = pl.pallas_call(
        functools.partial(_inproj_kernel, names=names, prompt=prompt, tiles_per_seq=max(seq_len // tm, 1)),
        grid=(nt,),
        in_specs=[pl.BlockSpec((tm, d), lambda i: (i, 0)), const((1, d))]
        + [pl.BlockSpec(seg.shape, lambda i: (0, 0), pipeline_mode=pl.Buffered(1)) for seg in w_packed]
        + [const((CONV_W, 2 * A_QK)), const((1, 2 * A_QK)), const((1, LANES))],
        out_specs=[outs[n][1] for n in names],
        out_shape=[outs[n][0] for n in names],
        scratch_shapes=[pltpu.VMEM((8 + tm, 2 * A_QK), F32)],
        compiler_params=pltpu.CompilerParams(dimension_semantics=("arbitrary",), vmem_limit_bytes=VMEM_LIMIT),
        name="inproj_prompt" if prompt else "inproj_decode",
    )(x2d, norm_pre, *w_packed, conv_w, conv_b, misc_bias)
    return dict(zip(names, res))


def _outproj_kernel(ya_ref, yb_ref, sga_ref, sgb_ref, x_ref, pe_ref, wa_ref, wb_ref, wo_ref, wg_ref, wp_ref,
                    g_ref, o_ref):
    merged = sga_ref[...] * _dot(ya_ref[...], wa_ref[...]) + sgb_ref[...] * _dot(yb_ref[...], wb_ref[...])
    z = _dot(merged.astype(BF16), wo_ref[...])
    ms = jnp.mean(z * z, axis=-1, keepdims=True)
    x1 = x_ref[...] + z * lax.rsqrt(ms + EPS) * g_ref[...]
    gate = _sigmoid(_dot(x1.astype(BF16), wg_ref[...]))
    o_ref[...] = x1 + _dot(pe_ref[...].astype(BF16), wp_ref[...]) * gate


def _outproj(ya, yb, sga, sgb, x2d, pe2d, wa, wb, wo, wg, wp, norm_post, *, tm):
    m, d = x2d.shape
    pd = pe2d.shape[1]
    row = lambda w: pl.BlockSpec((tm, w), lambda i: (i, 0))
    const = lambda shape: pl.BlockSpec(shape, lambda i: (0,) * len(shape))
    return pl.pallas_call(
        _outproj_kernel,
        grid=(m // tm,),
        in_specs=[row(A_WIDTH), row(B_WIDTH), row(d), row(d), row(d), row(pd),
                  const((A_WIDTH, d)), const((B_WIDTH, d)), const((d, d)), const((d, d)), const((pd, d)),
                  const((1, d))],
        out_specs=row(d),
        out_shape=jax.ShapeDtypeStruct((m, d), F32),
        compiler_params=pltpu.CompilerParams(dimension_semantics=("arbitrary",), vmem_limit_bytes=VMEM_LIMIT),
        name="outproj",
    )(ya, yb, sga, sgb, x2d, pe2d, wa, wb, wo, wg, wp, norm_post)


def _cumsum_rows(x):
    n = x.shape[0]
    row = lax.broadcasted_iota(I32, x.shape, 0)
    s = 1
    while s < n:
        x = x + jnp.where(row >= s, pltpu.roll(x, s, axis=0), 0.0)
        s *= 2
    return x


def _lane_col(x, lane_iota, idx):
    return jnp.sum(jnp.where(lane_iota == idx, x, 0.0), axis=1, keepdims=True)


def _mlstm_prompt_kernel(q_ref, k_ref, kt_ref, v_ref, misc_ref, ga_ref, na_ref, ya_ref, c_out_ref, nm_out_ref,
                         c_ref, n_ref, m_ref):
    c_idx = pl.program_id(1)
    nb, L = q_ref.shape[0], q_ref.shape[1]

    @pl.when(c_idx == 0)
    def _():
        c_ref[...] = jnp.zeros(c_ref.shape, F32)
        n_ref[...] = jnp.zeros(n_ref.shape, F32)
        m_ref[...] = jnp.zeros(m_ref.shape, F32)

    lane = lax.broadcasted_iota(I32, (L, LANES), 1)
    t_i = lax.broadcasted_iota(I32, (L, L), 0)
    s_i = lax.broadcasted_iota(I32, (L, L), 1)
    causal = s_i <= t_i

    for sq in range(nb):
        misc = misc_ref[sq]
        cs = _cumsum_rows(misc)
        b_al = pltpu.roll(cs, LANES - A_HEADS, axis=1)
        xa = jnp.where(lane < A_HEADS, misc - b_al, cs)
        xt = _transpose(xa)

        for h in range(A_HEADS):
            a_col = _lane_col(xa, lane, h)
            b_col = _lane_col(xa, lane, A_HEADS + h)
            a_row = xt[h:h + 1, :]
            m_prev = m_ref[sq, h:h + 1, 0:1]
            logit = jnp.where(causal, a_row, -jnp.inf)
            g_col = jnp.maximum(m_prev, jnp.max(logit, axis=1, keepdims=True))
            dmat = jnp.exp(logit - g_col)
            w_inter = jnp.exp(m_prev - g_col)
            qh = q_ref[sq, :, h * A_DQK:(h + 1) * A_DQK]
            kh = k_ref[sq, :, h * A_DQK:(h + 1) * A_DQK]
            vh = v_ref[sq, :, h * A_DV:(h + 1) * A_DV]
            c_h = c_ref[sq, h]
            n_h = n_ref[sq, h:h + 1, :]
            s = _dot_nt(qh, kh) * dmat
            num = _dot(s.astype(BF16), vh) + w_inter * _dot(qh, c_h.astype(BF16))
            den = (jnp.sum(s, axis=1, keepdims=True)
                   + w_inter * jnp.sum(qh.astype(F32) * n_h, axis=1, keepdims=True))
            m_t = b_col + g_col
            hh = num / jnp.maximum(jnp.abs(den), jnp.exp(-m_t))
            hn = hh * lax.rsqrt(jnp.mean(hh * hh, axis=1, keepdims=True) + EPS)
            sl = slice(h * A_DV, (h + 1) * A_DV)
            ya_ref[sq, :, sl] = (ga_ref[sq, :, sl] * (hn * na_ref[:, sl])).astype(ya_ref.dtype)

            g_last = g_col[L - 1:L, :]
            w_prev = jnp.exp(m_prev - g_last)
            w_s = jnp.exp(a_col - g_last)
            vw = vh.astype(F32) * w_s
            c_ref[sq, h] = w_prev * c_h + _dot(kt_ref[sq, h * A_DQK:(h + 1) * A_DQK, :], vw.astype(BF16))
            n_ref[sq, h:h + 1, :] = w_prev * n_h + jnp.sum(kh.astype(F32) * w_s, axis=0, keepdims=True)
            m_ref[sq, h:h + 1, :] = jnp.broadcast_to(b_col[L - 1:L, :] + g_last, (1, LANES))

    @pl.when(c_idx == pl.num_programs(1) - 1)
    def _():
        for sq in range(nb):
            for h in range(A_HEADS):
                c_out_ref[sq, h] = _transpose(c_ref[sq, h])
        nm_out_ref[:, 0:A_HEADS, :] = n_ref[:, 0:A_HEADS, :]
        nm_out_ref[:, A_HEADS:2 * A_HEADS, :] = m_ref[:, 0:A_HEADS, :]


def _mlstm_prompt(q, k, kt, v, misc, ga, norm_a, *, batch, seq_len, chunk):
    nc = seq_len // chunk
    nb = 1
    seq3 = lambda a: a.reshape(batch, seq_len, a.shape[-1])
    blk = lambda w: pl.BlockSpec((nb, chunk, w), lambda b, c: (b, c, 0))
    ya, c_out, nm_out = pl.pallas_call(
        _mlstm_prompt_kernel,
        grid=(batch // nb, nc),
        in_specs=[blk(A_QK), blk(A_QK), pl.BlockSpec((nb, A_QK, chunk), lambda b, c: (b * nc + c, 0, 0)),
                  blk(A_WIDTH), blk(LANES), blk(A_WIDTH), pl.BlockSpec((1, A_WIDTH), lambda b, c: (0, 0))],
        out_specs=[blk(A_WIDTH),
                   pl.BlockSpec((nb, A_HEADS, A_DV, A_DQK), lambda b, c: (b, 0, 0, 0)),
                   pl.BlockSpec((nb, 8, LANES), lambda b, c: (b, 0, 0))],
        out_shape=[jax.ShapeDtypeStruct((batch, seq_len, A_WIDTH), BF16),
                   jax.ShapeDtypeStruct((batch, A_HEADS, A_DV, A_DQK), F32),
                   jax.ShapeDtypeStruct((batch, 8, LANES), F32)],
        scratch_shapes=[pltpu.VMEM((nb, A_HEADS, A_DQK, A_DV), F32), pltpu.VMEM((nb, 8, LANES), F32),
                        pltpu.VMEM((nb, 8, LANES), F32)],
        compiler_params=pltpu.CompilerParams(dimension_semantics=("arbitrary", "arbitrary"),
                                             vmem_limit_bytes=VMEM_LIMIT),
        name="mlstm_prompt",
    )(seq3(q), seq3(k), kt, seq3(v), seq3(misc), seq3(ga), norm_a)
    return ya.reshape(batch * seq_len, A_WIDTH), c_out, nm_out


def _key_to_f32(key):
    return pltpu.bitcast(jnp.where(key < 0, key ^ 0x7FFFFFFF, key), F32)


def _radix_threshold(count_ge, shape, topk):
    def bit_body(i, carry):
        u, n_u = carry
        cand_u = u | jnp.left_shift(jnp.int32(1), 31 - i)
        cnt = count_ge(_key_to_f32(cand_u ^ INT_MIN))
        ok = cnt >= topk
        return jnp.where(ok, cand_u, u), jnp.where(ok, cnt, n_u)

    u, n_u = lax.fori_loop(0, 32, bit_body, (jnp.zeros(shape, I32), jnp.zeros(shape, I32)))
    key = jnp.maximum(u ^ INT_MIN, KEY_NEG_INF)
    return _key_to_f32(key), _key_to_f32(key + 1), n_u


def _refine_in_bin(count_ge, thr, thr_up, topk):
    lo, hi = thr, thr_up
    for _ in range(6):
        mid = lo + 0.5 * (hi - lo)
        ok = count_ge(mid) >= topk
        lo = jnp.where(ok, mid, lo)
        hi = jnp.where(ok, hi, mid)
    return lo


def _dsa_prompt_kernel(bq_ref, iq_ref, misc_ref, zb_ref, ikb_ref, kb_ref, vt_ref, yb_ref,
                       sc_ref, s_ref, s2_ref, p_ref, p2_ref, al_ref, al2_ref, acc_ref, m_ref, *, topk, nq):
    j = pl.program_id(1)
    T = PAGE
    KT = 2 * PAGE
    nkt = j // 2 + 1

    iq_t = iq_ref[0]
    zeros_pad = jnp.zeros((LANES - IDX_DIM, T), BF16)
    iq_pad = [jnp.concatenate([zeros_pad, iq_t[h * IDX_DIM:(h + 1) * IDX_DIM, :]], axis=0)
              for h in range(IDX_HEADS)]
    iq_wide = [jnp.concatenate(iq_pad[h:h + IDX_GROUP], axis=1) for h in range(0, IDX_HEADS, IDX_GROUP)]
    misc_t = _transpose(misc_ref[...])
    q_t = bq_ref[0]
    q_grp = [jnp.concatenate([q_t[(g * B_GROUP + hh) * B_HD:(g * B_GROUP + hh + 1) * B_HD, :]
                              for hh in range(B_GROUP)], axis=1) for g in range(B_KV)]

    row_i = lax.broadcasted_iota(I32, (KT, T), 0)
    lane_i = lax.broadcasted_iota(I32, (KT, T), 1)

    last_pair = nq // 2 - 1

    def idx_body(c, carry):
        ikc = ikb_ref[0, c]
        acc = jnp.zeros((KT, T), F32)
        for h0 in range(0, IDX_HEADS, IDX_GROUP):
            prod = _dot(ikc, iq_wide[h0 // IDX_GROUP])
            for i in range(IDX_GROUP):
                r = jnp.maximum(prod[:, i * T:(i + 1) * T], 0.0)
                acc = acc + r * misc_t[MISC_IW + h0 + i:MISC_IW + h0 + i + 1, :]
        vis = (c * KT + row_i) <= (j * T + lane_i)
        sc_ref[c] = jnp.where(vis, acc, -jnp.inf)
        return carry

    lax.fori_loop(0, nkt, idx_body, 0)

    def count(pred):
        def body(c, cnt):
            m = pred(sc_ref[c], c).reshape(KT // 32, 4, 8, T)
            for v in range(KT // 32):
                cnt = jnp.where(m[v], cnt + 1, cnt)
            return cnt
        cnt4 = lax.fori_loop(0, nkt, body, jnp.zeros((4, 8, T), I32))
        return jnp.sum(jnp.sum(cnt4, axis=0), axis=0, keepdims=True)

    count_ge = lambda t: count(lambda sc, c: sc >= t)
    thr0, thr_up, n_ge = _radix_threshold(count_ge, (1, T), topk)

    def tie_search(_):
        thr_t = _refine_in_bin(count_ge, thr0, thr_up, topk)
        need = topk - count(lambda sc, c: sc > thr_t)
        nbits = (nq * T - 1).bit_length()
        def pbody(i, p):
            cand = p | jnp.left_shift(jnp.int32(1), nbits - 1 - i)
            cnt = count(lambda sc, c: (sc == thr_t) & ((c * KT + row_i) < cand))
            return jnp.where(cnt < need, cand, p)
        return thr_t, lax.fori_loop(0, nbits, pbody, jnp.zeros((1, T), I32))

    has_tie = jnp.max(jnp.where((n_ge > topk) & (thr0 > -jnp.inf), 1, 0)) > 0
    thr, p_lim = lax.cond(has_tie, tie_search, lambda _: (thr0, jnp.full((1, T), 2 ** 30, I32)), 0)

    acc_ref[...] = jnp.zeros(acc_ref.shape, F32)
    m_ref[...] = jnp.full(m_ref.shape, NEG_BIG, F32)

    def qk_scores(c, dst_ref):
        kc = kb_ref[0, jnp.minimum(c, last_pair)]
        for g in range(B_KV):
            dst_ref[g] = _dot(kc[:, g * B_HD:(g + 1) * B_HD], q_grp[g])

    def softmax(c, src_ref, p_dst, al_dst):
        sc = sc_ref[c]
        pos = c * KT + row_i
        sel = ((sc > thr) | ((sc == thr) & (pos <= p_lim))) & (sc > -jnp.inf)
        bias = jnp.where(sel, 0.0, NEG_BIG)
        for h in range(B_HEADS):
            g, hh = divmod(h, B_GROUP)
            s = src_ref[g, :, hh * T:(hh + 1) * T] + bias
            m_old = m_ref[h:h + 1, :]
            m_new = jnp.maximum(m_old, jnp.max(s, axis=0, keepdims=True))
            al_dst[h:h + 1, :] = jnp.exp2(m_old - m_new)
            p_dst[h // 2, :, (h % 2) * T:(h % 2 + 1) * T] = jnp.exp2(s - m_new).astype(BF16)
            m_ref[h:h + 1, :] = m_new

    def pv_update(c, p_src, al_src):
        vtc = vt_ref[0, c]
        for h0 in range(0, B_HEADS, 2):
            g = h0 // B_GROUP
            pv = _dot(vtc[g * VT_ROWS:(g + 1) * VT_ROWS, :], p_src[h0 // 2])
            for i, h in enumerate((h0, h0 + 1)):
                acc_ref[h] = acc_ref[h] * al_src[h:h + 1, :] + pv[:, i * T:(i + 1) * T]

    p2_ref[...] = jnp.zeros(p2_ref.shape, BF16)
    al2_ref[...] = jnp.ones(al2_ref.shape, F32)
    qk_scores(0, s_ref)

    def att_body(i, carry):
        c0 = 2 * i
        qk_scores(c0 + 1, s2_ref)
        softmax(c0, s_ref, p_ref, al_ref)
        pv_update(jnp.maximum(c0 - 1, 0), p2_ref, al2_ref)

        @pl.when(c0 + 1 < nkt)
        def _():
            qk_scores(c0 + 2, s_ref)
            softmax(c0 + 1, s2_ref, p2_ref, al2_ref)
            pv_update(c0, p_ref, al_ref)

        return carry

    lax.fori_loop(0, (nkt + 1) // 2, att_body, 0)

    @pl.when(nkt % 2 == 1)
    def _():
        pv_update(nkt - 1, p_ref, al_ref)

    @pl.when(nkt % 2 == 0)
    def _():
        pv_update(nkt - 1, p2_ref, al2_ref)

    for h in range(B_HEADS):
        o_t = acc_ref[h, 0:B_HD, :] / acc_ref[h, B_HD:B_HD + 1, :]
        sl = slice(h * B_HD, (h + 1) * B_HD)
        yb_ref[:, sl] = (_transpose(o_t) * zb_ref[:, sl]).astype(yb_ref.dtype)


def _dsa_prompt(bq, iq, misc, zb, ikb, kb, vt, *, batch, seq_len, topk):
    nq = seq_len // PAGE
    kt = 2 * PAGE
    scores = pltpu.VMEM((B_KV, kt, B_GROUP * PAGE), F32)
    weights = pltpu.VMEM((B_HEADS // 2, kt, 2 * PAGE), BF16)
    rescale = pltpu.VMEM((B_HEADS, PAGE), F32)
    row = lambda w: pl.BlockSpec((PAGE, w), lambda b, j: (b * nq + j, 0))
    tile_t = lambda r: pl.BlockSpec((1, r, PAGE), lambda b, j: (b * nq + j, 0, 0))
    per_b = lambda a: pl.BlockSpec((1,) + a.shape[1:], lambda b, j: (b, 0, 0, 0))
    return pl.pallas_call(
        functools.partial(_dsa_prompt_kernel, topk=topk, nq=nq),
        grid=(batch, nq),
        in_specs=[tile_t(B_WIDTH), tile_t(IDX_HEADS * IDX_DIM), row(LANES), row(B_WIDTH),
                  per_b(ikb), per_b(kb), per_b(vt)],
        out_specs=row(B_WIDTH),
        out_shape=jax.ShapeDtypeStruct((batch * seq_len, B_WIDTH), BF16),
        scratch_shapes=[pltpu.VMEM((nq // 2, kt, PAGE), F32), scores, scores, weights, weights, rescale, rescale,
                        pltpu.VMEM((B_HEADS, VT_ROWS, PAGE), F32),
                        pltpu.VMEM((B_HEADS, PAGE), F32)],
        compiler_params=pltpu.CompilerParams(dimension_semantics=("arbitrary", "arbitrary"),
                                             vmem_limit_bytes=VMEM_LIMIT),
        name="dsa_prompt",
    )(bq, iq, misc, zb, ikb, kb, vt)


def _prompt_path(x, pe, w, w_packed, misc_bias):
    batch, seq_len, d = x.shape
    m = batch * seq_len
    x2d = x.reshape(m, d)
    tm = min(IN_PROJ_ROWS, seq_len)
    a = _inproj(x2d, w['norm_pre'], w_packed, w['conv_w'], w['conv_b'], misc_bias,
                prompt=True, seq_len=seq_len, tm=tm)
    chunk = IN_PROJ_ROWS
    ya, c_out, nm_out = _mlstm_prompt(a['q'], a['k'], a['kt'], a['av'], a['misc'], a['ga'], w['norm_a'],
                                      batch=batch, seq_len=seq_len, chunk=chunk)
    nk2 = seq_len // (2 * PAGE)
    ikb = a['miscb'].reshape(batch, nk2, 2 * PAGE, LANES)
    kb = a['kb'].reshape(batch, nk2, 2 * PAGE, B_KV * B_HD)
    vt = a['vt'].reshape(batch, nk2, B_KV * VT_ROWS, 2 * PAGE)
    topk = min(TOPK_MAX, seq_len // 4)
    yb = _dsa_prompt(a['bq'], a['iq'], a['misc'], a['zb'], ikb, kb, vt, batch=batch, seq_len=seq_len, topk=topk)
    y = _outproj(ya, yb, a['sga'], a['sgb'], x2d, pe.reshape(m, -1), w['wa'], w['wb'], w['wo'], w['wg'], w['wp'],
                 w['norm_post'], tm=min(OUT_PROJ_ROWS, m))
    tiles_per_seq = seq_len // tm
    tail = a['tail'].reshape(batch, tiles_per_seq, 8, 2 * A_QK)[:, -1, 8 - (CONV_W - 1):, :]
    return dict(
        y=y.reshape(batch, seq_len, d),
        k=a['kf'].reshape(1, batch, seq_len, B_KV, B_HD),
        v=a['vf'].reshape(1, batch, seq_len, B_KV, B_HD),
        ik=a['misc'][:, MISC_IK:].reshape(1, batch, seq_len, IDX_DIM),
        C=c_out[None],
        n=nm_out[None, :, 0:A_HEADS, :],
        m=nm_out[None, :, A_HEADS:2 * A_HEADS, 0],
        conv=tail[None],
    )


def _mlstm_sample_kernel(q_ref, k_ref, conv0_ref, cw_ref, cb_ref, v_ref, misc_ref, ga_ref, na_ref, n0_ref, m0_ref,
                         c0_ref, ya_ref, c_out_ref, n_out_ref, m_out_ref,
                         qs_ref, ks_ref, cq_ref, wprev_ref, vwt_ref, numi_ref, den_ref, wint_ref, enm_ref, *, db, ds):
    b = pl.program_id(0)
    ms = ds * db
    rows = lambda t: slice(t * db, (t + 1) * db)

    @pl.when(b == 0)
    def _():
        cw = cw_ref[...]
        u = [conv0_ref[j] for j in range(CONV_W - 1)]
        u += [jnp.concatenate([q_ref[rows(t), :], k_ref[rows(t), :]], axis=1) for t in range(ds)]
        for t in range(ds):
            y = cb_ref[...] + cw[0:1, :] * u[t]
            for jj in range(1, CONV_W):
                y = y + cw[jj:jj + 1, :] * u[t + jj]
            act = _silu(y)
            qs_ref[rows(t), :] = act[:, :A_QK] * Q_SCALE
            ks_ref[rows(t), :] = act[:, A_QK:]
        cq_ref[...] = jnp.zeros(cq_ref.shape, F32)

        lane = lax.broadcasted_iota(I32, (db, LANES), 1)
        for h in range(A_HEADS):
            i_t = [_lane_col(misc_ref[rows(t), :], lane, h) for t in range(ds)]
            lf_t = [_lane_col(misc_ref[rows(t), :], lane, A_HEADS + h) for t in range(ds)]
            b_t = [lf_t[0]]
            for t in range(1, ds):
                b_t.append(b_t[-1] + lf_t[t])
            a_t = [i_t[t] - b_t[t] for t in range(ds)]
            m_prev = jnp.max(m0_ref[h], axis=1, keepdims=True)
            g_t = [jnp.maximum(m_prev, a_t[0])]
            for t in range(1, ds):
                g_t.append(jnp.maximum(g_t[-1], a_t[t]))
            hq = slice(h * A_DQK, (h + 1) * A_DQK)
            hv = slice(h * A_DV, (h + 1) * A_DV)
            qh = [qs_ref[rows(t), hq] for t in range(ds)]
            kh = [ks_ref[rows(t), hq] for t in range(ds)]
            vh = [v_ref[rows(t), hv].astype(F32) for t in range(ds)]
            n0 = n0_ref[h]
            for t in range(ds):
                num = jnp.zeros((db, A_DV), F32)
                den = jnp.zeros((db, 1), F32)
                for s in range(t + 1):
                    w_ts = jnp.sum(qh[t] * kh[s], axis=1, keepdims=True) * jnp.exp(a_t[s] - g_t[t])
                    num = num + w_ts * vh[s]
                    den = den + w_ts
                w_inter = jnp.exp(m_prev - g_t[t])
                den = den + w_inter * jnp.sum(qh[t] * n0, axis=1, keepdims=True)
                numi_ref[rows(t), hv] = num
                den_ref[h, rows(t), :] = jnp.broadcast_to(den, (db, LANES))
                wint_ref[h, rows(t), :] = jnp.broadcast_to(w_inter, (db, LANES))
                enm_ref[h, rows(t), :] = jnp.broadcast_to(jnp.exp(-(b_t[t] + g_t[t])), (db, LANES))
            g_last = g_t[ds - 1]
            w_prev = jnp.exp(m_prev - g_last)
            wprev_ref[h] = jnp.broadcast_to(w_prev, (db, LANES))
            n_new = w_prev * n0
            vw = []
            for s in range(ds):
                w_s = jnp.exp(a_t[s] - g_last)
                n_new = n_new + w_s * kh[s]
                vw.append(vh[s] * w_s)
            n_out_ref[h] = n_new
            m_out_ref[h] = jnp.broadcast_to(b_t[ds - 1] + g_last, (db, LANES))
            vw_all = jnp.concatenate(vw, axis=0)
            if ms < LANES:
                vw_all = jnp.concatenate([vw_all, jnp.zeros((LANES - ms, A_DV), F32)], axis=0)
            vwt_ref[h] = _transpose(vw_all)

    mcols = vwt_ref.shape[2]
    row_i = lax.broadcasted_iota(I32, (ms, A_DQK), 0)
    col_i = lax.broadcasted_iota(I32, (A_DV, mcols), 1)
    row_mine = row_i == b
    col_mine = col_i == b
    for t in range(1, ds):
        row_mine = row_mine | (row_i == b + t * db)
        col_mine = col_mine | (col_i == b + t * db)
    for h in range(A_HEADS):
        hq = slice(h * A_DQK, (h + 1) * A_DQK)
        c0 = c0_ref[0, h]
        q_mine = jnp.where(row_mine, qs_ref[:, hq], 0.0).astype(BF16)
        cq_ref[h] = cq_ref[h] + _dot_nt(q_mine, c0.astype(BF16))
        vw_mine = jnp.where(col_mine, vwt_ref[h], 0.0).astype(BF16)
        k_all = ks_ref[:, hq]
        if mcols > ms:
            k_all = jnp.concatenate([k_all, jnp.zeros((mcols - ms, A_DQK), F32)], axis=0)
        c_out_ref[0, h] = wprev_ref[h, pl.ds(b, 1), :] * c0 + _dot(vw_mine, k_all.astype(BF16))

    @pl.when(b == pl.num_programs(0) - 1)
    def _():
        for h in range(A_HEADS):
            hv = slice(h * A_DV, (h + 1) * A_DV)
            num = numi_ref[:, hv] + wint_ref[h][:, 0:1] * cq_ref[h]
            hh = num / jnp.maximum(jnp.abs(den_ref[h][:, 0:1]), enm_ref[h][:, 0:1])
            hn = hh * lax.rsqrt(jnp.mean(hh * hh, axis=1, keepdims=True) + EPS)
            ya_ref[:, hv] = (ga_ref[:, hv] * (hn * na_ref[:, hv])).astype(ya_ref.dtype)


def _mlstm_sample(q_t, k_t, conv0_t, conv_w, conv_b, v_t, misc_t, ga_t, norm_a, n0_t, m0_bc, c0, *, db, ds):
    ms = db * ds
    mcols = max(ms, LANES)
    full = lambda a: pl.BlockSpec(a.shape, lambda b: (0,) * a.ndim)
    cblk = pl.BlockSpec((1, A_HEADS, A_DV, A_DQK), lambda b: (b, 0, 0, 0))
    hshape = jax.ShapeDtypeStruct((A_HEADS, db, LANES), F32)
    ins = (q_t, k_t, conv0_t, conv_w, conv_b, v_t, misc_t, ga_t, norm_a, n0_t, m0_bc)
    return pl.pallas_call(
        functools.partial(_mlstm_sample_kernel, db=db, ds=ds),
        grid=(db,),
        in_specs=[full(a) for a in ins] + [cblk],
        out_specs=[pl.BlockSpec((ms, A_WIDTH), lambda b: (0, 0)), cblk,
                   pl.BlockSpec((A_HEADS, db, LANES), lambda b: (0, 0, 0)),
                   pl.BlockSpec((A_HEADS, db, LANES), lambda b: (0, 0, 0))],
        out_shape=[jax.ShapeDtypeStruct((ms, A_WIDTH), BF16), jax.ShapeDtypeStruct(c0.shape, F32), hshape, hshape],
        scratch_shapes=[pltpu.VMEM((ms, A_QK), F32), pltpu.VMEM((ms, A_QK), F32),
                        pltpu.VMEM((A_HEADS, ms, A_DV), F32), pltpu.VMEM((A_HEADS, db, LANES), F32),
                        pltpu.VMEM((A_HEADS, A_DV, mcols), F32), pltpu.VMEM((ms, A_WIDTH), F32),
                        pltpu.VMEM((A_HEADS, ms, LANES), F32), pltpu.VMEM((A_HEADS, ms, LANES), F32),
                        pltpu.VMEM((A_HEADS, ms, LANES), F32)],
        compiler_params=pltpu.CompilerParams(dimension_semantics=("arbitrary",), vmem_limit_bytes=VMEM_LIMIT),
        name="mlstm_sample",
    )(*ins, c0)


def _page_ring(pt_ref, pages_per_step, streams):
    b, g = pl.program_id(0), pl.program_id(1)
    nb, ng = pl.num_programs(0), pl.num_programs(1)
    step = b * ng + g
    slot = lax.rem(step, 2)

    def copies(bb, gg, sl):
        return [pltpu.make_async_copy(hbm.at[pt_ref[bb, gg * pages_per_step + i]], buf.at[sl, i], sem.at[sl])
                for hbm, buf, sem in streams for i in range(pages_per_step)]

    def start_all(cps):
        for n, cp in enumerate(cps):
            cp.start(priority=n % 2)

    @pl.when(step == 0)
    def _():
        start_all(copies(0, 0, 0))

    @pl.when(step + 1 < nb * ng)
    def _():
        wrap = g + 1 == ng
        start_all(copies(jnp.where(wrap, b + 1, b), jnp.where(wrap, 0, g + 1), 1 - slot))

    for cp in copies(b, g, slot):
        cp.wait()
    return slot


def _dsa_sample_scores_kernel(pt_ref, iq_ref, iw_ref, iknew_ref, cache_ref, sc_ref, page_buf, sem,
                              *, pages_per_step, n_pages, ds):
    g = pl.program_id(1)
    slot = _page_ring(pt_ref, pages_per_step, [(cache_ref, page_buf, sem)])
    page_refs = [page_buf.at[slot, i] for i in range(pages_per_step)]
    iq = iq_ref[0]
    iw = iw_ref[0]

    def scores(keys_t, visible):
        n = keys_t.shape[1] // PAGE
        r = jnp.maximum(_dot(iq, keys_t), 0.0) * jnp.concatenate([iw] * n, axis=1)
        sc = jnp.sum(r.reshape(8, IDX_HEADS, n * PAGE), axis=1)
        return jnp.where(visible, sc, -jnp.inf)

    t_all = lax.broadcasted_iota(I32, (8, pages_per_step * PAGE), 0)
    sc_step = scores(jnp.concatenate([r[...] for r in page_refs], axis=1).astype(BF16), t_all < ds)
    for i in range(pages_per_step):
        sc_ref[0, g * pages_per_step + i] = sc_step[:, i * PAGE:(i + 1) * PAGE]

    @pl.when(g == pl.num_programs(1) - 1)
    def _():
        t_i = lax.broadcasted_iota(I32, (8, PAGE), 0)
        s_i = lax.broadcasted_iota(I32, (8, PAGE), 1)
        sc_ref[0, n_pages] = scores(iknew_ref[0], (t_i < ds) & (s_i <= t_i))


def _dsa_sample_scores(page_table, iq_s, iw_bc, ik_new, cache_ik, *, ds, pages_per_step):
    db, n_pages = page_table.shape
    npg = n_pages // pages_per_step
    per_b = lambda a: pl.BlockSpec((1,) + a.shape[1:], lambda b, g, pt: (b,) + (0,) * (a.ndim - 1))
    return pl.pallas_call(
        functools.partial(_dsa_sample_scores_kernel, pages_per_step=pages_per_step, n_pages=n_pages, ds=ds),
        grid_spec=pltpu.PrefetchScalarGridSpec(
            num_scalar_prefetch=1, grid=(db, npg),
            in_specs=[per_b(iq_s), per_b(iw_bc), per_b(ik_new), pl.BlockSpec(memory_space=pl.ANY)],
            out_specs=pl.BlockSpec((1, n_pages + 1, 8, PAGE), lambda b, g, pt: (b, 0, 0, 0)),
            scratch_shapes=[pltpu.VMEM((2, pages_per_step, IDX_DIM, PAGE), F32), pltpu.SemaphoreType.DMA((2,))]),
        out_shape=jax.ShapeDtypeStruct((db, n_pages + 1, 8, PAGE), F32),
        compiler_params=pltpu.CompilerParams(dimension_semantics=("arbitrary", "arbitrary"),
                                             vmem_limit_bytes=VMEM_LIMIT),
        name="dsa_sample_scores",
    )(page_table, iq_s, iw_bc, ik_new, cache_ik)


def _dsa_sample_search_kernel(sc_ref, bias_ref, *, topk):
    sc = sc_ref[...]
    nb, np1 = sc.shape[0], sc.shape[1]
    pos = lax.broadcasted_iota(I32, sc.shape, 1) * PAGE + lax.broadcasted_iota(I32, sc.shape, 3)

    def count(mask):
        per_lane = jnp.sum(mask.astype(I32), axis=1, keepdims=True)
        return jnp.sum(per_lane, axis=3, keepdims=True)

    count_ge = lambda t: count(sc_ref[...] >= t)
    thr0, thr_up, n_ge = _radix_threshold(count_ge, (nb, 1, 8, 1), topk)
    nbits = (np1 * PAGE - 1).bit_length()

    def tie_search(_):
        thr_t = _refine_in_bin(count_ge, thr0, thr_up, topk)
        need = topk - count(sc > thr_t)
        tied = sc == thr_t

        def pos_body(i, p):
            cand = p | jnp.left_shift(jnp.int32(1), nbits - 1 - i)
            cnt = count(tied & (pos < cand))
            return jnp.where(cnt < need, cand, p)
        return thr_t, lax.fori_loop(0, nbits, pos_body, jnp.zeros(thr0.shape, I32))

    has_tie = jnp.max(jnp.where((n_ge > topk) & (thr0 > -jnp.inf), 1, 0)) > 0
    thr, p_lim = lax.cond(has_tie, tie_search, lambda _: (thr0, jnp.full(thr0.shape, 2 ** 30, I32)), 0)
    sel = ((sc > thr) | ((sc == thr) & (pos <= p_lim))) & (sc > -jnp.inf)
    bias_ref[...] = jnp.where(sel, 0.0, NEG_BIG)


def _dsa_sample_search(scores, *, topk):
    db = scores.shape[0]
    nb = SEARCH_SEQS if db % SEARCH_SEQS == 0 else 1
    blk = pl.BlockSpec((nb,) + scores.shape[1:], lambda i: (i, 0, 0, 0))
    return pl.pallas_call(
        functools.partial(_dsa_sample_search_kernel, topk=topk),
        grid=(db // nb,),
        in_specs=[blk],
        out_specs=blk,
        out_shape=jax.ShapeDtypeStruct(scores.shape, F32),
        compiler_params=pltpu.CompilerParams(dimension_semantics=("arbitrary",), vmem_limit_bytes=VMEM_LIMIT),
        name="dsa_sample_search",
    )(scores)


def _dsa_sample_attend_kernel(pt_ref, q_ref, bias_ref, knew_ref, vnew_ref, zb_ref, ck_ref, cv_ref, o_ref,
                              k_buf, v_buf, k_sem, v_sem, acc_ref, m_ref, l_ref, *, pages_per_step, n_pages, ds):
    g = pl.program_id(1)
    slot = _page_ring(pt_ref, pages_per_step, [(ck_ref, k_buf, k_sem), (cv_ref, v_buf, v_sem)])
    k_refs = [k_buf.at[slot, i] for i in range(pages_per_step)]
    v_refs = [v_buf.at[slot, i] for i in range(pages_per_step)]
    rows_q = ds * B_HEADS
    q = q_ref[0]

    @pl.when(g == 0)
    def _():
        acc_ref[...] = jnp.zeros(acc_ref.shape, F32)
        m_ref[...] = jnp.full(m_ref.shape, NEG_BIG, F32)
        l_ref[...] = jnp.zeros(l_ref.shape, F32)

    def step(k_keys, v_keys, bias8):
        n = k_keys.shape[0]
        bias = jnp.concatenate([jnp.broadcast_to(bias8[t:t + 1, :], (B_HEADS, n)) for t in range(ds)], axis=0)
        s = _dot_nt(q, k_keys) + bias
        m_old = m_ref[...]
        m_new = jnp.maximum(m_old, jnp.max(s, axis=1, keepdims=True))
        alpha = jnp.exp2(m_old - m_new)
        p = jnp.exp2(s - m_new)
        l_ref[...] = alpha * l_ref[...] + jnp.sum(p, axis=1, keepdims=True)
        m_ref[...] = m_new
        acc_ref[...] = alpha * acc_ref[...] + _dot(p.astype(BF16), v_keys)

    def page_keys(ref):
        return jnp.concatenate([ref[pl.ds(kv, PAGE, stride=B_KV), :] for kv in range(B_KV)], axis=1).astype(BF16)

    step(jnp.concatenate([page_keys(r) for r in k_refs], axis=0),
         jnp.concatenate([page_keys(r) for r in v_refs], axis=0),
         jnp.concatenate([bias_ref[0, g * pages_per_step + i] for i in range(pages_per_step)], axis=1))

    @pl.when(g == pl.num_programs(1) - 1)
    def _():
        step(knew_ref[0], vnew_ref[0], bias_ref[0, n_pages])
        o = acc_ref[...] / l_ref[...]
        head = lax.broadcasted_iota(I32, (rows_q, B_HD), 0) & (B_HEADS - 1)
        o_sel = o[:, 0:B_HD]
        for kv in range(1, B_KV):
            o_sel = jnp.where(head >= kv * B_GROUP, o[:, kv * B_HD:(kv + 1) * B_HD], o_sel)
        o_ref[0] = (o_sel * zb_ref[0]).astype(o_ref.dtype)


def _dsa_sample_attend(page_table, q_bd, bias, k_new, v_new, zb_s, cache_k, cache_v, *, ds, pages_per_step):
    db, n_pages = page_table.shape
    npg = n_pages // pages_per_step
    rows_q = ds * B_HEADS
    kvw = B_KV * B_HD
    per_b = lambda a: pl.BlockSpec((1,) + a.shape[1:], lambda b, g, pt: (b,) + (0,) * (a.ndim - 1))
    any_space = pl.BlockSpec(memory_space=pl.ANY)
    page_buf = pltpu.VMEM((2, pages_per_step, PAGE * B_KV, B_HD), F32)
    return pl.pallas_call(
        functools.partial(_dsa_sample_attend_kernel, pages_per_step=pages_per_step, n_pages=n_pages, ds=ds),
        grid_spec=pltpu.PrefetchScalarGridSpec(
            num_scalar_prefetch=1, grid=(db, npg),
            in_specs=[per_b(q_bd), per_b(bias), per_b(k_new), per_b(v_new), per_b(zb_s), any_space, any_space],
            out_specs=pl.BlockSpec((1, rows_q, B_HD), lambda b, g, pt: (b, 0, 0)),
            scratch_shapes=[page_buf, page_buf, pltpu.SemaphoreType.DMA((2,)), pltpu.SemaphoreType.DMA((2,)),
                            pltpu.VMEM((rows_q, kvw), F32), pltpu.VMEM((rows_q, 1), F32),
                            pltpu.VMEM((rows_q, 1), F32)]),
        out_shape=jax.ShapeDtypeStruct((db, rows_q, B_HD), BF16),
        compiler_params=pltpu.CompilerParams(dimension_semantics=("arbitrary", "arbitrary"),
                                             vmem_limit_bytes=VMEM_LIMIT),
        name="dsa_sample_attend",
    )(page_table, q_bd, bias, k_new, v_new, zb_s, cache_k, cache_v)


def _sample_path(x, pe, cache_k, cache_v, cache_ik, page_table, c0, n0, m0, conv0, w, w_packed, misc_bias):
    db, ds, d = x.shape
    ms = db * ds
    assert ds >= CONV_W - 1 and ds <= 8
    n_pages = page_table.shape[1]
    a = _inproj(x.reshape(ms, d), w['norm_pre'], w_packed, w['conv_w'], w['conv_b'], misc_bias,
                prompt=False, seq_len=ms, tm=ms)

    tmaj = lambda z: jnp.swapaxes(z.reshape(db, ds, -1), 0, 1).reshape(ms, -1)
    ya_t, c_new, n_new, m_new = _mlstm_sample(
        tmaj(a['q']), tmaj(a['k']), jnp.swapaxes(conv0, 0, 1), w['conv_w'], w['conv_b'], tmaj(a['av']),
        tmaj(a['misc']), tmaj(a['ga']), w['norm_a'], jnp.swapaxes(n0, 0, 1),
        jnp.broadcast_to(jnp.swapaxes(m0, 0, 1)[:, :, None], (A_HEADS, db, LANES)), c0, db=db, ds=ds)
    ya = jnp.swapaxes(ya_t.reshape(ds, db, -1), 0, 1).reshape(ms, -1)

    topk = min(TOPK_MAX, (n_pages * PAGE + ds) // 4)
    pad_rows = lambda z: jnp.pad(z.reshape(db, ds, -1), ((0, 0), (0, PAGE - ds), (0, 0))).astype(BF16)
    pad_tok = lambda z: jnp.pad(z.reshape(db, ds, -1), ((0, 0), (0, 8 - ds), (0, 0)))
    iq_s = pad_tok(a['iq']).reshape(db, 8 * IDX_HEADS, IDX_DIM)
    iw_bc = jnp.broadcast_to(pad_tok(a['misc'][:, MISC_IW:MISC_IW + IDX_HEADS]).reshape(db, 8 * IDX_HEADS, 1),
                             (db, 8 * IDX_HEADS, LANES))
    largest_divisor = lambda cap: max(p for p in range(1, cap + 1) if n_pages % p == 0)
    pps_scores, pps_attend = largest_divisor(SCORE_PAGES), largest_divisor(ATTEND_PAGES)
    scores = _dsa_sample_scores(page_table, iq_s, iw_bc, jnp.swapaxes(pad_rows(a['misc'][:, MISC_IK:]), 1, 2),
                                jnp.swapaxes(cache_ik, 1, 2), ds=ds, pages_per_step=pps_scores)
    bias = _dsa_sample_search(scores, topk=topk)
    q5 = a['bq'].reshape(db, ds * B_HEADS, 1, B_HD)
    kv_of_row = (jnp.arange(ds * B_HEADS) % B_HEADS) // B_GROUP
    q_bd = jnp.where((kv_of_row[None, :, None, None] == jnp.arange(B_KV)[None, None, :, None]), q5,
                     jnp.zeros((), BF16)).reshape(db, ds * B_HEADS, B_KV * B_HD)
    attn = _dsa_sample_attend(page_table, q_bd, bias, pad_rows(a['kf']), pad_rows(a['vf']),
                              a['zb'].reshape(db, ds * B_HEADS, B_HD),
                              cache_k.reshape(-1, PAGE * B_KV, B_HD), cache_v.reshape(-1, PAGE * B_KV, B_HD),
                              ds=ds, pages_per_step=pps_attend)
    yb = attn.reshape(ms, B_WIDTH)
    y = _outproj(ya, yb, a['sga'], a['sgb'], x.reshape(ms, d), pe.reshape(ms, -1), w['wa'], w['wb'], w['wo'],
                 w['wg'], w['wp'], w['norm_post'], tm=ms)
    qk_pre = jnp.concatenate([a['q'], a['k']], axis=1).reshape(db, ds, 2 * A_QK)
    return dict(
        y=y.reshape(db, ds, d),
        k=a['kf'].reshape(1, db, ds, B_KV, B_HD),
        v=a['vf'].reshape(1, db, ds, B_KV, B_HD),
        ik=a['misc'][:, MISC_IK:].reshape(1, db, ds, IDX_DIM),
        C=c_new[None],
        n=jnp.swapaxes(n_new, 0, 1)[None],
        m=jnp.swapaxes(m_new[:, :, 0], 0, 1)[None],
        conv=qk_pre[:, ds - (CONV_W - 1):, :][None],
    )


def _prep_weights(w_in, conv_w, conv_b, if_bias, norm_a, w_a_proj, w_b_proj, w_out, norm_pre, norm_post, w_ple,
                  w_ple_gate):
    d = w_in.shape[1]
    misc_bias = jnp.zeros((1, LANES), F32).at[0, :2 * A_HEADS].set(if_bias[0])
    w = dict(norm_pre=norm_pre[0][None], norm_post=norm_post[0][None], norm_a=norm_a[0][None],
             conv_w=conv_w[0], conv_b=conv_b[0][None],
             wa=w_a_proj[0].astype(BF16), wb=w_b_proj[0].astype(BF16), wo=w_out[0].astype(BF16),
             wg=w_ple_gate[0].astype(BF16), wp=w_ple[0].astype(BF16))
    return w, _pack_w_in(w_in[0], d), misc_bias


def kernel(x_prompt, x_sample, cache_k, cache_v, cache_idx_k, page_table, state_C, state_n, state_m, state_conv,
           p_prompt, p_sample, w_in, conv_w, conv_b, if_bias, norm_a, w_a_proj, w_b_proj, w_out, norm_pre,
           norm_post, w_ple, w_ple_gate):
    w, w_packed, misc_bias = _prep_weights(w_in, conv_w, conv_b, if_bias, norm_a, w_a_proj, w_b_proj, w_out,
                                           norm_pre, norm_post, w_ple, w_ple_gate)
    p = _prompt_path(x_prompt, p_prompt[0], w, w_packed, misc_bias)
    s = _sample_path(x_sample, p_sample[0], cache_k[0], cache_v[0], cache_idx_k[0], page_table, state_C[0],
                     state_n[0], state_m[0], state_conv[0], w, w_packed, misc_bias)
    names = ('k', 'v', 'ik', 'C', 'n', 'm', 'conv')
    return (p['y'], s['y']) + tuple(p[n] for n in names) + tuple(s[n] for n in names)
```

```python
import functools

import jax
import jax.numpy as jnp
from jax import lax
from jax.experimental import pallas as pl
from jax.experimental.pallas import tpu as pltpu

F32 = jnp.float32
BF16 = jnp.bfloat16
I32 = jnp.int32

EPS = 1e-6
A_HEADS = 4
A_DQK = 128
A_DV = 256
A_QK = A_HEADS * A_DQK
A_WIDTH = A_HEADS * A_DV
CONV_W = 4
B_HEADS = 8
B_KV = 2
B_HD = 128
B_GROUP = B_HEADS // B_KV
B_WIDTH = B_HEADS * B_HD
IDX_HEADS = 8
IDX_DIM = 64
TOPK_MAX = 256
PAGE = 128
LANES = 128
VMEM_LIMIT = 56 * 1024 * 1024

IN_PROJ_ROWS = 2 * PAGE
OUT_PROJ_ROWS = 4 * PAGE
IDX_GROUP = 2
SEARCH_SEQS = 16
SCORE_PAGES = 128
ATTEND_PAGES = 32

Q_SCALE = A_DQK ** -0.5
ATT_SCALE = B_HD ** -0.5
IDX_SCALE = IDX_DIM ** -0.5
IW_SCALE = IDX_HEADS ** -0.5
C_EXP = ATT_SCALE * 1.4426950408889634
VT_ROWS = B_HD + 16
NEG_BIG = -1e30
INT_MIN = -(2 ** 31)
KEY_NEG_INF = INT_MIN + 0x007FFFFF

_SPLITS = (('a_q', A_QK), ('a_k', A_QK), ('a_v', A_WIDTH), ('a_i', A_HEADS), ('a_f', A_HEADS),
           ('a_o', A_WIDTH), ('a_z', A_WIDTH), ('b_q', B_WIDTH), ('b_k', B_KV * B_HD), ('b_v', B_KV * B_HD),
           ('b_iq', IDX_HEADS * IDX_DIM), ('b_ik', IDX_DIM), ('b_iw', IDX_HEADS), ('b_z', B_WIDTH),
           ('g_a', 1024), ('g_b', 1024))

P_QK = 0
P_AV = 1024
P_AO = 2048
P_AZ = 3072
P_BQ = 4096
P_BK = 5120
P_BV = 5376
P_IQ = 5632
P_MISC = 6144
P_BZ = 6272
P_GA = 7296
P_GB = 8320
P_COLS = 9344
MISC_IW = 8
MISC_IK = 64


def _sigmoid(x):
    return 1.0 / (1.0 + jnp.exp(-x))


def _silu(x):
    return x * _sigmoid(x)


def _dot(a, b):
    return jnp.dot(a, b, preferred_element_type=F32)


def _dot_nt(a, b):
    return lax.dot_general(a, b, (((1,), (1,)), ((), ())), preferred_element_type=F32)


def _transpose(x):
    r, c = x.shape
    rows = []
    for j in range(c // LANES):
        rows.append(jnp.concatenate([x[i * LANES:(i + 1) * LANES, j * LANES:(j + 1) * LANES].T
                                     for i in range(r // LANES)], axis=1))
    return jnp.concatenate(rows, axis=0)


def _pack_w_in(w_in, d_model):
    offs = {}
    c = 0
    for name, n in _SPLITS:
        offs[name] = (c, c + n)
        c += n

    def col(name):
        lo, hi = offs[name]
        return w_in[:, lo:hi]

    z = lambda n: jnp.zeros((d_model, n), w_in.dtype)
    misc = jnp.concatenate([col('a_i'), col('a_f'), col('b_iw'), z(MISC_IK - 16), col('b_ik')], axis=1)
    segs = (w_in[:, offs['a_q'][0]:offs['a_v'][1]], w_in[:, offs['a_o'][0]:offs['b_iq'][1]], misc,
            w_in[:, offs['b_z'][0]:offs['g_b'][1]])
    assert [s.shape[1] for s in segs] == [P_AO - P_QK, P_MISC - P_AO, P_BZ - P_MISC, P_COLS - P_BZ]
    return tuple(s.astype(BF16) for s in segs)


def _inproj_kernel(x_ref, g_ref, w0_ref, w1_ref, w2_ref, w3_ref, cw_ref, cb_ref, mb_ref, *refs, names, prompt,
                   tiles_per_seq):
    o = dict(zip(names, refs))
    hist_ref = refs[len(names)]
    tm = x_ref.shape[0]
    x = x_ref[...]
    ms = jnp.mean(x * x, axis=-1, keepdims=True)
    xn = (x * lax.rsqrt(ms + EPS) * g_ref[...]).astype(BF16)

    segments = ((P_QK, w0_ref), (P_AO, w1_ref), (P_MISC, w2_ref), (P_BZ, w3_ref))

    def proj(lo, hi):
        start, ref = [s for s in segments if s[0] <= lo][-1]
        return _dot(xn, ref[:, lo - start:hi - start])

    qk = proj(P_QK, P_QK + 2 * A_QK)
    if prompt:
        o['tail'][0] = qk[tm - 8:tm, :]

        @pl.when(pl.program_id(0) % tiles_per_seq == 0)
        def _():
            hist_ref[0:8, :] = jnp.zeros((8, 2 * A_QK), F32)

        hist_ref[8:8 + tm, :] = qk
        cw = cw_ref[...]
        y = cb_ref[...] + cw[0:1, :] * hist_ref[5:5 + tm, :]
        y = y + cw[1:2, :] * hist_ref[6:6 + tm, :]
        y = y + cw[2:3, :] * hist_ref[7:7 + tm, :]
        y = y + cw[3:4, :] * qk
        act = _silu(y)
        o['q'][...] = (act[:, :A_QK] * Q_SCALE).astype(BF16)
        o['k'][...] = act[:, A_QK:].astype(BF16)
        for p in range(tm // (2 * PAGE)):
            o['kt'][p] = _transpose(act[p * 2 * PAGE:(p + 1) * 2 * PAGE, A_QK:]).astype(BF16)
        hist_ref[0:8, :] = qk[tm - 8:tm, :]
    else:
        o['q'][...] = qk[:, :A_QK]
        o['k'][...] = qk[:, A_QK:]

    o['av'][...] = proj(P_AV, P_AV + A_WIDTH).astype(BF16)
    o['ga'][...] = (_sigmoid(proj(P_AO, P_AO + A_WIDTH)) * _silu(proj(P_AZ, P_AZ + A_WIDTH))).astype(BF16)
    bq = proj(P_BQ, P_BQ + B_WIDTH) * C_EXP
    iq = proj(P_IQ, P_IQ + IDX_HEADS * IDX_DIM)
    kf = proj(P_BK, P_BK + B_KV * B_HD)
    vf = proj(P_BV, P_BV + B_KV * B_HD)
    for kv in range(B_KV):
        o['kf'][pl.ds(kv, tm, stride=B_KV), :] = kf[:, kv * B_HD:(kv + 1) * B_HD]
        o['vf'][pl.ds(kv, tm, stride=B_KV), :] = vf[:, kv * B_HD:(kv + 1) * B_HD]
    if prompt:
        for r in range(tm // PAGE):
            rows = slice(r * PAGE, (r + 1) * PAGE)
            o['bq'][r] = _transpose(bq[rows, :]).astype(BF16)
            o['iq'][r] = _transpose(iq[rows, :]).astype(BF16)
        ones = jnp.ones((VT_ROWS - B_HD, 2 * PAGE), BF16)
        for p in range(tm // (2 * PAGE)):
            rows = slice(p * 2 * PAGE, (p + 1) * 2 * PAGE)
            parts = []
            for g in range(B_KV):
                parts += [_transpose(vf[rows, g * B_HD:(g + 1) * B_HD]).astype(BF16), ones]
            o['vt'][p] = jnp.concatenate(parts, axis=0)
        o['kb'][...] = kf.astype(BF16)
    else:
        o['bq'][...] = bq.astype(BF16)
        o['iq'][...] = iq.astype(BF16)

    raw = proj(P_MISC, P_MISC + LANES) + mb_ref[...]
    lane = lax.broadcasted_iota(I32, raw.shape, 1)
    logsig = jnp.minimum(raw, 0.0) - jnp.log(1.0 + jnp.exp(-jnp.abs(raw)))
    o['misc'][...] = jnp.where((lane >= A_HEADS) & (lane < 2 * A_HEADS), logsig,
                               jnp.where((lane >= MISC_IW) & (lane < MISC_IW + IDX_HEADS),
                                         raw * (IW_SCALE * IDX_SCALE), raw))
    if prompt:
        o['miscb'][...] = jnp.where(lane >= MISC_IK, raw, 0.0).astype(BF16)

    o['zb'][...] = _silu(proj(P_BZ, P_BZ + B_WIDTH)).astype(BF16)
    o['sga'][...] = _sigmoid(proj(P_GA, P_GA + 1024)).astype(BF16)
    o['sgb'][...] = _sigmoid(proj(P_GB, P_GB + 1024)).astype(BF16)


def _inproj(x2d, norm_pre, w_packed, conv_w, conv_b, misc_bias, *, prompt, seq_len, tm):
    m, d = x2d.shape
    nt = m // tm
    rows = lambda w, dt: (jax.ShapeDtypeStruct((m, w), dt), pl.BlockSpec((tm, w), lambda i: (i, 0)))
    tiles = lambda n, r, c, dt: (jax.ShapeDtypeStruct((m // n, r, c), dt),
                                 pl.BlockSpec((tm // n, r, c), lambda i: (i, 0, 0)))
    kv_rows = (jax.ShapeDtypeStruct((B_KV * m, B_HD), F32), pl.BlockSpec((B_KV * tm, B_HD), lambda i: (i, 0)))
    outs = dict(q=rows(A_QK, BF16 if prompt else F32), k=rows(A_QK, BF16 if prompt else F32), av=rows(A_WIDTH, BF16),
                ga=rows(A_WIDTH, BF16), kf=kv_rows, vf=kv_rows, misc=rows(LANES, F32),
                zb=rows(B_WIDTH, BF16), sga=rows(d, BF16), sgb=rows(d, BF16))
    if prompt:
        outs.update(tail=tiles(tm, 8, 2 * A_QK, F32), bq=tiles(PAGE, B_WIDTH, PAGE, BF16),
                    iq=tiles(PAGE, IDX_HEADS * IDX_DIM, PAGE, BF16),
                    vt=tiles(2 * PAGE, B_KV * VT_ROWS, 2 * PAGE, BF16), kt=tiles(2 * PAGE, A_QK, 2 * PAGE, BF16),
                    kb=rows(B_KV * B_HD, BF16),
                    miscb=rows(LANES, BF16))
    else:
        outs.update(bq=rows(B_WIDTH, BF16), iq=rows(IDX_HEADS * IDX_DIM, BF16))
    names = tuple(outs)
    const = lambda shape: pl.BlockSpec(shape, lambda i: (0,) * len(shape))
    res = pl.pallas_call(
        functools.partial(_inproj_kernel, names=names, prompt=prompt, tiles_per_seq=max(seq_len // tm, 1)),
        grid=(nt,),
        in_specs=[pl.BlockSpec((tm, d), lambda i: (i, 0)), const((1, d))]
        + [pl.BlockSpec(seg.shape, lambda i: (0, 0), pipeline_mode=pl.Buffered(1)) for seg in w_packed]
        + [const((CONV_W, 2 * A_QK)), const((1, 2 * A_QK)), const((1, LANES))],
        out_specs=[outs[n][1] for n in names],
        out_shape=[outs[n][0] for n in names],
        scratch_shapes=[pltpu.VMEM((8 + tm, 2 * A_QK), F32)],
        compiler_params=pltpu.CompilerParams(dimension_semantics=("arbitrary",), vmem_limit_bytes=VMEM_LIMIT),
        name="inproj_prompt" if prompt else "inproj_decode",
    )(x2d, norm_pre, *w_packed, conv_w, conv_b, misc_bias)
    return dict(zip(names, res))


def _outproj_kernel(ya_ref, yb_ref, sga_ref, sgb_ref, x_ref, pe_ref, wa_ref, wb_ref, wo_ref, wg_ref, wp_ref,
                    g_ref, o_ref):
    merged = sga_ref[...] * _dot(ya_ref[...], wa_ref[...]) + sgb_ref[...] * _dot(yb_ref[...], wb_ref[...])
    z = _dot(merged.astype(BF16), wo_ref[...])
    ms = jnp.mean(z * z, axis=-1, keepdims=True)
    x1 = x_ref[...] + z * lax.rsqrt(ms + EPS) * g_ref[...]
    gate = _sigmoid(_dot(x1.astype(BF16), wg_ref[...]))
    o_ref[...] = x1 + _dot(pe_ref[...].astype(BF16), wp_ref[...]) * gate


def _outproj(ya, yb, sga, sgb, x2d, pe2d, wa, wb, wo, wg, wp, norm_post, *, tm):
    m, d = x2d.shape
    pd = pe2d.shape[1]
    row = lambda w: pl.BlockSpec((tm, w), lambda i: (i, 0))
    const = lambda shape: pl.BlockSpec(shape, lambda i: (0,) * len(shape))
    return pl.pallas_call(
        _outproj_kernel,
        grid=(m // tm,),
        in_specs=[row(A_WIDTH), row(B_WIDTH), row(d), row(d), row(d), row(pd),
                  const((A_WIDTH, d)), const((B_WIDTH, d)), const((d, d)), const((d, d)), const((pd, d)),
                  const((1, d))],
        out_specs=row(d),
        out_shape=jax.ShapeDtypeStruct((m, d), F32),
        compiler_params=pltpu.CompilerParams(dimension_semantics=("arbitrary",), vmem_limit_bytes=VMEM_LIMIT),
        name="outproj",
    )(ya, yb, sga, sgb, x2d, pe2d, wa, wb, wo, wg, wp, norm_post)


def _cumsum_rows(x):
    n = x.shape[0]
    row = lax.broadcasted_iota(I32, x.shape, 0)
    s = 1
    while s < n:
        x = x + jnp.where(row >= s, pltpu.roll(x, s, axis=0), 0.0)
        s *= 2
    return x


def _lane_col(x, lane_iota, idx):
    return jnp.sum(jnp.where(lane_iota == idx, x, 0.0), axis=1, keepdims=True)


def _mlstm_prompt_kernel(q_ref, k_ref, kt_ref, v_ref, misc_ref, ga_ref, na_ref, ya_ref, c_out_ref, nm_out_ref,
                         c_ref, n_ref, m_ref):
    c_idx = pl.program_id(1)
    nb, L = q_ref.shape[0], q_ref.shape[1]

    @pl.when(c_idx == 0)
    def _():
        c_ref[...] = jnp.zeros(c_ref.shape, F32)
        n_ref[...] = jnp.zeros(n_ref.shape, F32)
        m_ref[...] = jnp.zeros(m_ref.shape, F32)

    lane = lax.broadcasted_iota(I32, (L, LANES), 1)
    t_i = lax.broadcasted_iota(I32, (L, L), 0)
    s_i = lax.broadcasted_iota(I32, (L, L), 1)
    causal = s_i <= t_i

    for sq in range(nb):
        misc = misc_ref[sq]
        cs = _cumsum_rows(misc)
        b_al = pltpu.roll(cs, LANES - A_HEADS, axis=1)
        xa = jnp.where(lane < A_HEADS, misc - b_al, cs)
        xt = _transpose(xa)

        for h in range(A_HEADS):
            a_col = _lane_col(xa, lane, h)
            b_col = _lane_col(xa, lane, A_HEADS + h)
            a_row = xt[h:h + 1, :]
            m_prev = m_ref[sq, h:h + 1, 0:1]
            logit = jnp.where(causal, a_row, -jnp.inf)
            g_col = jnp.maximum(m_prev, jnp.max(logit, axis=1, keepdims=True))
            dmat = jnp.exp(logit - g_col)
            w_inter = jnp.exp(m_prev - g_col)
            qh = q_ref[sq, :, h * A_DQK:(h + 1) * A_DQK]
            kh = k_ref[sq, :, h * A_DQK:(h + 1) * A_DQK]
            vh = v_ref[sq, :, h * A_DV:(h + 1) * A_DV]
            c_h = c_ref[sq, h]
            n_h = n_ref[sq, h:h + 1, :]
            s = _dot_nt(qh, kh) * dmat
            num = _dot(s.astype(BF16), vh) + w_inter * _dot(qh, c_h.astype(BF16))
            den = (jnp.sum(s, axis=1, keepdims=True)
                   + w_inter * jnp.sum(qh.astype(F32) * n_h, axis=1, keepdims=True))
            m_t = b_col + g_col
            hh = num / jnp.maximum(jnp.abs(den), jnp.exp(-m_t))
            hn = hh * lax.rsqrt(jnp.mean(hh * hh, axis=1, keepdims=True) + EPS)
            sl = slice(h * A_DV, (h + 1) * A_DV)
            ya_ref[sq, :, sl] = (ga_ref[sq, :, sl] * (hn * na_ref[:, sl])).astype(ya_ref.dtype)

            g_last = g_col[L - 1:L, :]
            w_prev = jnp.exp(m_prev - g_last)
            w_s = jnp.exp(a_col - g_last)
            vw = vh.astype(F32) * w_s
            c_ref[sq, h] = w_prev * c_h + _dot(kt_ref[sq, h * A_DQK:(h + 1) * A_DQK, :], vw.astype(BF16))
            n_ref[sq, h:h + 1, :] = w_prev * n_h + jnp.sum(kh.astype(F32) * w_s, axis=0, keepdims=True)
            m_ref[sq, h:h + 1, :] = jnp.broadcast_to(b_col[L - 1:L, :] + g_last, (1, LANES))

    @pl.when(c_idx == pl.num_programs(1) - 1)
    def _():
        for sq in range(nb):
            for h in range(A_HEADS):
                c_out_ref[sq, h] = _transpose(c_ref[sq, h])
        nm_out_ref[:, 0:A_HEADS, :] = n_ref[:, 0:A_HEADS, :]
        nm_out_ref[:, A_HEADS:2 * A_HEADS, :] = m_ref[:, 0:A_HEADS, :]


def _mlstm_prompt(q, k, kt, v, misc, ga, norm_a, *, batch, seq_len, chunk):
    nc = seq_len // chunk
    nb = 1
    seq3 = lambda a: a.reshape(batch, seq_len, a.shape[-1])
    blk = lambda w: pl.BlockSpec((nb, chunk, w), lambda b, c: (b, c, 0))
    ya, c_out, nm_out = pl.pallas_call(
        _mlstm_prompt_kernel,
        grid=(batch // nb, nc),
        in_specs=[blk(A_QK), blk(A_QK), pl.BlockSpec((nb, A_QK, chunk), lambda b, c: (b * nc + c, 0, 0)),
                  blk(A_WIDTH), blk(LANES), blk(A_WIDTH), pl.BlockSpec((1, A_WIDTH), lambda b, c: (0, 0))],
        out_specs=[blk(A_WIDTH),
                   pl.BlockSpec((nb, A_HEADS, A_DV, A_DQK), lambda b, c: (b, 0, 0, 0)),
                   pl.BlockSpec((nb, 8, LANES), lambda b, c: (b, 0, 0))],
        out_shape=[jax.ShapeDtypeStruct((batch, seq_len, A_WIDTH), BF16),
                   jax.ShapeDtypeStruct((batch, A_HEADS, A_DV, A_DQK), F32),
                   jax.ShapeDtypeStruct((batch, 8, LANES), F32)],
        scratch_shapes=[pltpu.VMEM((nb, A_HEADS, A_DQK, A_DV), F32), pltpu.VMEM((nb, 8, LANES), F32),
                        pltpu.VMEM((nb, 8, LANES), F32)],
        compiler_params=pltpu.CompilerParams(dimension_semantics=("arbitrary", "arbitrary"),
                                             vmem_limit_bytes=VMEM_LIMIT),
        name="mlstm_prompt",
    )(seq3(q), seq3(k), kt, seq3(v), seq3(misc), seq3(ga), norm_a)
    return ya.reshape(batch * seq_len, A_WIDTH), c_out, nm_out


def _key_to_f32(key):
    return pltpu.bitcast(jnp.where(key < 0, key ^ 0x7FFFFFFF, key), F32)


def _radix_threshold(count_ge, shape, topk):
    def bit_body(i, carry):
        u, n_u = carry
        cand_u = u | jnp.left_shift(jnp.int32(1), 31 - i)
        cnt = count_ge(_key_to_f32(cand_u ^ INT_MIN))
        ok = cnt >= topk
        return jnp.where(ok, cand_u, u), jnp.where(ok, cnt, n_u)

    u, n_u = lax.fori_loop(0, 32, bit_body, (jnp.zeros(shape, I32), jnp.zeros(shape, I32)))
    key = jnp.maximum(u ^ INT_MIN, KEY_NEG_INF)
    return _key_to_f32(key), _key_to_f32(key + 1), n_u


def _refine_in_bin(count_ge, thr, thr_up, topk):
    lo, hi = thr, thr_up
    for _ in range(6):
        mid = lo + 0.5 * (hi - lo)
        ok = count_ge(mid) >= topk
        lo = jnp.where(ok, mid, lo)
        hi = jnp.where(ok, hi, mid)
    return lo


def _dsa_prompt_kernel(bq_ref, iq_ref, misc_ref, zb_ref, ikb_ref, kb_ref, vt_ref, yb_ref,
                       sc_ref, s_ref, s2_ref, p_ref, p2_ref, al_ref, al2_ref, acc_ref, m_ref, *, topk, nq):
    j = pl.program_id(1)
    T = PAGE
    KT = 2 * PAGE
    nkt = j // 2 + 1

    iq_t = iq_ref[0]
    zeros_pad = jnp.zeros((LANES - IDX_DIM, T), BF16)
    iq_pad = [jnp.concatenate([zeros_pad, iq_t[h * IDX_DIM:(h + 1) * IDX_DIM, :]], axis=0)
              for h in range(IDX_HEADS)]
    iq_wide = [jnp.concatenate(iq_pad[h:h + IDX_GROUP], axis=1) for h in range(0, IDX_HEADS, IDX_GROUP)]
    misc_t = _transpose(misc_ref[...])
    q_t = bq_ref[0]
    q_grp = [jnp.concatenate([q_t[(g * B_GROUP + hh) * B_HD:(g * B_GROUP + hh + 1) * B_HD, :]
                              for hh in range(B_GROUP)], axis=1) for g in range(B_KV)]

    row_i = lax.broadcasted_iota(I32, (KT, T), 0)
    lane_i = lax.broadcasted_iota(I32, (KT, T), 1)

    last_pair = nq // 2 - 1

    def idx_body(c, carry):
        ikc = ikb_ref[0, c]
        acc = jnp.zeros((KT, T), F32)
        for h0 in range(0, IDX_HEADS, IDX_GROUP):
            prod = _dot(ikc, iq_wide[h0 // IDX_GROUP])
            for i in range(IDX_GROUP):
                r = jnp.maximum(prod[:, i * T:(i + 1) * T], 0.0)
                acc = acc + r * misc_t[MISC_IW + h0 + i:MISC_IW + h0 + i + 1, :]
        vis = (c * KT + row_i) <= (j * T + lane_i)
        sc_ref[c] = jnp.where(vis, acc, -jnp.inf)
        return carry

    lax.fori_loop(0, nkt, idx_body, 0)

    def count(pred):
        def body(c, cnt):
            m = pred(sc_ref[c], c).reshape(KT // 32, 4, 8, T)
            for v in range(KT // 32):
                cnt = jnp.where(m[v], cnt + 1, cnt)
            return cnt
        cnt4 = lax.fori_loop(0, nkt, body, jnp.zeros((4, 8, T), I32))
        return jnp.sum(jnp.sum(cnt4, axis=0), axis=0, keepdims=True)

    count_ge = lambda t: count(lambda sc, c: sc >= t)
    thr0, thr_up, n_ge = _radix_threshold(count_ge, (1, T), topk)

    def tie_search(_):
        thr_t = _refine_in_bin(count_ge, thr0, thr_up, topk)
        need = topk - count(lambda sc, c: sc > thr_t)
        nbits = (nq * T - 1).bit_length()
        def pbody(i, p):
            cand = p | jnp.left_shift(jnp.int32(1), nbits - 1 - i)
            cnt = count(lambda sc, c: (sc == thr_t) & ((c * KT + row_i) < cand))
            return jnp.where(cnt < need, cand, p)
        return thr_t, lax.fori_loop(0, nbits, pbody, jnp.zeros((1, T), I32))

    has_tie = jnp.max(jnp.where((n_ge > topk) & (thr0 > -jnp.inf), 1, 0)) > 0
    thr, p_lim = lax.cond(has_tie, tie_search, lambda _: (thr0, jnp.full((1, T), 2 ** 30, I32)), 0)

    acc_ref[...] = jnp.zeros(acc_ref.shape, F32)
    m_ref[...] = jnp.full(m_ref.shape, NEG_BIG, F32)

    def qk_scores(c, dst_ref):
        kc = kb_ref[0, jnp.minimum(c, last_pair)]
        for g in range(B_KV):
            dst_ref[g] = _dot(kc[:, g * B_HD:(g + 1) * B_HD], q_grp[g])

    thr_fin = jnp.maximum(thr, jnp.finfo(F32).min)

    def softmax(c, src_ref, p_dst, al_dst, tied):
        sc = sc_ref[c]
        if tied:
            sel = (sc > thr_fin) | ((sc == thr_fin) & ((c * KT + row_i) <= p_lim))
        else:
            sel = sc >= thr_fin
        bias = jnp.where(sel, 0.0, NEG_BIG)
        for h in range(B_HEADS):
            g, hh = divmod(h, B_GROUP)
            s = src_ref[g, :, hh * T:(hh + 1) * T] + bias
            m_old = m_ref[h:h + 1, :]
            m_new = jnp.maximum(m_old, jnp.max(s, axis=0, keepdims=True))
            al_dst[h:h + 1, :] = jnp.exp2(m_old - m_new)
            p_dst[h // 2, :, (h % 2) * T:(h % 2 + 1) * T] = jnp.exp2(s - m_new).astype(BF16)
            m_ref[h:h + 1, :] = m_new

    def pv_update(c, p_src, al_src):
        vtc = vt_ref[0, c]
        for h0 in range(0, B_HEADS, 2):
            g = h0 // B_GROUP
            pv = _dot(vtc[g * VT_ROWS:(g + 1) * VT_ROWS, :], p_src[h0 // 2])
            for i, h in enumerate((h0, h0 + 1)):
                acc_ref[h] = acc_ref[h] * al_src[h:h + 1, :] + pv[:, i * T:(i + 1) * T]

    p2_ref[...] = jnp.zeros(p2_ref.shape, BF16)
    al2_ref[...] = jnp.ones(al2_ref.shape, F32)
    qk_scores(0, s_ref)

    def attend(tied):
        def att_body(i, carry):
            c0 = 2 * i
            qk_scores(c0 + 1, s2_ref)
            softmax(c0, s_ref, p_ref, al_ref, tied)
            pv_update(jnp.maximum(c0 - 1, 0), p2_ref, al2_ref)

            @pl.when(c0 + 1 < nkt)
            def _():
                qk_scores(c0 + 2, s_ref)
                softmax(c0 + 1, s2_ref, p2_ref, al2_ref, tied)
                pv_update(c0, p_ref, al_ref)

            return carry

        lax.fori_loop(0, (nkt + 1) // 2, att_body, 0)

    @pl.when(has_tie)
    def _():
        attend(True)

    @pl.when(jnp.logical_not(has_tie))
    def _():
        attend(False)

    @pl.when(nkt % 2 == 1)
    def _():
        pv_update(nkt - 1, p_ref, al_ref)

    @pl.when(nkt % 2 == 0)
    def _():
        pv_update(nkt - 1, p2_ref, al2_ref)

    for h in range(B_HEADS):
        o_t = acc_ref[h, 0:B_HD, :] / acc_ref[h, B_HD:B_HD + 1, :]
        sl = slice(h * B_HD, (h + 1) * B_HD)
        yb_ref[:, sl] = (_transpose(o_t) * zb_ref[:, sl]).astype(yb_ref.dtype)


def _dsa_prompt(bq, iq, misc, zb, ikb, kb, vt, *, batch, seq_len, topk):
    nq = seq_len // PAGE
    kt = 2 * PAGE
    scores = pltpu.VMEM((B_KV, kt, B_GROUP * PAGE), F32)
    weights = pltpu.VMEM((B_HEADS // 2, kt, 2 * PAGE), BF16)
    rescale = pltpu.VMEM((B_HEADS, PAGE), F32)
    row = lambda w: pl.BlockSpec((PAGE, w), lambda b, j: (b * nq + j, 0))
    tile_t = lambda r: pl.BlockSpec((1, r, PAGE), lambda b, j: (b * nq + j, 0, 0))
    per_b = lambda a: pl.BlockSpec((1,) + a.shape[1:], lambda b, j: (b, 0, 0, 0))
    return pl.pallas_call(
        functools.partial(_dsa_prompt_kernel, topk=topk, nq=nq),
        grid=(batch, nq),
        in_specs=[tile_t(B_WIDTH), tile_t(IDX_HEADS * IDX_DIM), row(LANES), row(B_WIDTH),
                  per_b(ikb), per_b(kb), per_b(vt)],
        out_specs=row(B_WIDTH),
        out_shape=jax.ShapeDtypeStruct((batch * seq_len, B_WIDTH), BF16),
        scratch_shapes=[pltpu.VMEM((nq // 2, kt, PAGE), F32), scores, scores, weights, weights, rescale, rescale,
                        pltpu.VMEM((B_HEADS, VT_ROWS, PAGE), F32),
                        pltpu.VMEM((B_HEADS, PAGE), F32)],
        compiler_params=pltpu.CompilerParams(dimension_semantics=("arbitrary", "arbitrary"),
                                             vmem_limit_bytes=VMEM_LIMIT),
        name="dsa_prompt",
    )(bq, iq, misc, zb, ikb, kb, vt)


def _prompt_path(x, pe, w, w_packed, misc_bias):
    batch, seq_len, d = x.shape
    m = batch * seq_len
    x2d = x.reshape(m, d)
    tm = min(IN_PROJ_ROWS, seq_len)
    a = _inproj(x2d, w['norm_pre'], w_packed, w['conv_w'], w['conv_b'], misc_bias,
                prompt=True, seq_len=seq_len, tm=tm)
    chunk = IN_PROJ_ROWS
    ya, c_out, nm_out = _mlstm_prompt(a['q'], a['k'], a['kt'], a['av'], a['misc'], a['ga'], w['norm_a'],
                                      batch=batch, seq_len=seq_len, chunk=chunk)
    nk2 = seq_len // (2 * PAGE)
    ikb = a['miscb'].reshape(batch, nk2, 2 * PAGE, LANES)
    kb = a['kb'].reshape(batch, nk2, 2 * PAGE, B_KV * B_HD)
    vt = a['vt'].reshape(batch, nk2, B_KV * VT_ROWS, 2 * PAGE)
    topk = min(TOPK_MAX, seq_len // 4)
    yb = _dsa_prompt(a['bq'], a['iq'], a['misc'], a['zb'], ikb, kb, vt, batch=batch, seq_len=seq_len, topk=topk)
    y = _outproj(ya, yb, a['sga'], a['sgb'], x2d, pe.reshape(m, -1), w['wa'], w['wb'], w['wo'], w['wg'], w['wp'],
                 w['norm_post'], tm=min(OUT_PROJ_ROWS, m))
    tiles_per_seq = seq_len // tm
    tail = a['tail'].reshape(batch, tiles_per_seq, 8, 2 * A_QK)[:, -1, 8 - (CONV_W - 1):, :]
    return dict(
        y=y.reshape(batch, seq_len, d),
        k=a['kf'].reshape(1, batch, seq_len, B_KV, B_HD),
        v=a['vf'].reshape(1, batch, seq_len, B_KV, B_HD),
        ik=a['misc'][:, MISC_IK:].reshape(1, batch, seq_len, IDX_DIM),
        C=c_out[None],
        n=nm_out[None, :, 0:A_HEADS, :],
        m=nm_out[None, :, A_HEADS:2 * A_HEADS, 0],
        conv=tail[None],
    )


def _mlstm_sample_kernel(q_ref, k_ref, conv0_ref, cw_ref, cb_ref, v_ref, misc_ref, ga_ref, na_ref, n0_ref, m0_ref,
                         c0_ref, ya_ref, c_out_ref, n_out_ref, m_out_ref,
                         qs_ref, ks_ref, cq_ref, wprev_ref, vwt_ref, numi_ref, den_ref, wint_ref, enm_ref, *, db, ds):
    b = pl.program_id(0)
    ms = ds * db
    rows = lambda t: slice(t * db, (t + 1) * db)

    @pl.when(b == 0)
    def _():
        cw = cw_ref[...]
        u = [conv0_ref[j] for j in range(CONV_W - 1)]
        u += [jnp.concatenate([q_ref[rows(t), :], k_ref[rows(t), :]], axis=1) for t in range(ds)]
        for t in range(ds):
            y = cb_ref[...] + cw[0:1, :] * u[t]
            for jj in range(1, CONV_W):
                y = y + cw[jj:jj + 1, :] * u[t + jj]
            act = _silu(y)
            qs_ref[rows(t), :] = act[:, :A_QK] * Q_SCALE
            ks_ref[rows(t), :] = act[:, A_QK:]
        cq_ref[...] = jnp.zeros(cq_ref.shape, F32)

        lane = lax.broadcasted_iota(I32, (db, LANES), 1)
        for h in range(A_HEADS):
            i_t = [_lane_col(misc_ref[rows(t), :], lane, h) for t in range(ds)]
            lf_t = [_lane_col(misc_ref[rows(t), :], lane, A_HEADS + h) for t in range(ds)]
            b_t = [lf_t[0]]
            for t in range(1, ds):
                b_t.append(b_t[-1] + lf_t[t])
            a_t = [i_t[t] - b_t[t] for t in range(ds)]
            m_prev = jnp.max(m0_ref[h], axis=1, keepdims=True)
            g_t = [jnp.maximum(m_prev, a_t[0])]
            for t in range(1, ds):
                g_t.append(jnp.maximum(g_t[-1], a_t[t]))
            hq = slice(h * A_DQK, (h + 1) * A_DQK)
            hv = slice(h * A_DV, (h + 1) * A_DV)
            qh = [qs_ref[rows(t), hq] for t in range(ds)]
            kh = [ks_ref[rows(t), hq] for t in range(ds)]
            vh = [v_ref[rows(t), hv].astype(F32) for t in range(ds)]
            n0 = n0_ref[h]
            for t in range(ds):
                num = jnp.zeros((db, A_DV), F32)
                den = jnp.zeros((db, 1), F32)
                for s in range(t + 1):
                    w_ts = jnp.sum(qh[t] * kh[s], axis=1, keepdims=True) * jnp.exp(a_t[s] - g_t[t])
                    num = num + w_ts * vh[s]
                    den = den + w_ts
                w_inter = jnp.exp(m_prev - g_t[t])
                den = den + w_inter * jnp.sum(qh[t] * n0, axis=1, keepdims=True)
                numi_ref[rows(t), hv] = num
                den_ref[h, rows(t), :] = jnp.broadcast_to(den, (db, LANES))
                wint_ref[h, rows(t), :] = jnp.broadcast_to(w_inter, (db, LANES))
                enm_ref[h, rows(t), :] = jnp.broadcast_to(jnp.exp(-(b_t[t] + g_t[t])), (db, LANES))
            g_last = g_t[ds - 1]
            w_prev = jnp.exp(m_prev - g_last)
            wprev_ref[h] = jnp.broadcast_to(w_prev, (db, LANES))
            n_new = w_prev * n0
            vw = []
            for s in range(ds):
                w_s = jnp.exp(a_t[s] - g_last)
                n_new = n_new + w_s * kh[s]
                vw.append(vh[s] * w_s)
            n_out_ref[h] = n_new
            m_out_ref[h] = jnp.broadcast_to(b_t[ds - 1] + g_last, (db, LANES))
            vw_all = jnp.concatenate(vw, axis=0)
            if ms < LANES:
                vw_all = jnp.concatenate([vw_all, jnp.zeros((LANES - ms, A_DV), F32)], axis=0)
            vwt_ref[h] = _transpose(vw_all)

    mcols = vwt_ref.shape[2]
    row_i = lax.broadcasted_iota(I32, (ms, A_DQK), 0)
    col_i = lax.broadcasted_iota(I32, (A_DV, mcols), 1)
    row_mine = row_i == b
    col_mine = col_i == b
    for t in range(1, ds):
        row_mine = row_mine | (row_i == b + t * db)
        col_mine = col_mine | (col_i == b + t * db)
    for h in range(A_HEADS):
        hq = slice(h * A_DQK, (h + 1) * A_DQK)
        c0 = c0_ref[0, h]
        q_mine = jnp.where(row_mine, qs_ref[:, hq], 0.0).astype(BF16)
        cq_ref[h] = cq_ref[h] + _dot_nt(q_mine, c0.astype(BF16))
        vw_mine = jnp.where(col_mine, vwt_ref[h], 0.0).astype(BF16)
        k_all = ks_ref[:, hq]
        if mcols > ms:
            k_all = jnp.concatenate([k_all, jnp.zeros((mcols - ms, A_DQK), F32)], axis=0)
        c_out_ref[0, h] = wprev_ref[h, pl.ds(b, 1), :] * c0 + _dot(vw_mine, k_all.astype(BF16))

    @pl.when(b == pl.num_programs(0) - 1)
    def _():
        for h in range(A_HEADS):
            hv = slice(h * A_DV, (h + 1) * A_DV)
            num = numi_ref[:, hv] + wint_ref[h][:, 0:1] * cq_ref[h]
            hh = num / jnp.maximum(jnp.abs(den_ref[h][:, 0:1]), enm_ref[h][:, 0:1])
            hn = hh * lax.rsqrt(jnp.mean(hh * hh, axis=1, keepdims=True) + EPS)
            ya_ref[:, hv] = (ga_ref[:, hv] * (hn * na_ref[:, hv])).astype(ya_ref.dtype)


def _mlstm_sample(q_t, k_t, conv0_t, conv_w, conv_b, v_t, misc_t, ga_t, norm_a, n0_t, m0_bc, c0, *, db, ds):
    ms = db * ds
    mcols = max(ms, LANES)
    full = lambda a: pl.BlockSpec(a.shape, lambda b: (0,) * a.ndim)
    cblk = pl.BlockSpec((1, A_HEADS, A_DV, A_DQK), lambda b: (b, 0, 0, 0))
    hshape = jax.ShapeDtypeStruct((A_HEADS, db, LANES), F32)
    ins = (q_t, k_t, conv0_t, conv_w, conv_b, v_t, misc_t, ga_t, norm_a, n0_t, m0_bc)
    return pl.pallas_call(
        functools.partial(_mlstm_sample_kernel, db=db, ds=ds),
        grid=(db,),
        in_specs=[full(a) for a in ins] + [cblk],
        out_specs=[pl.BlockSpec((ms, A_WIDTH), lambda b: (0, 0)), cblk,
                   pl.BlockSpec((A_HEADS, db, LANES), lambda b: (0, 0, 0)),
                   pl.BlockSpec((A_HEADS, db, LANES), lambda b: (0, 0, 0))],
        out_shape=[jax.ShapeDtypeStruct((ms, A_WIDTH), BF16), jax.ShapeDtypeStruct(c0.shape, F32), hshape, hshape],
        scratch_shapes=[pltpu.VMEM((ms, A_QK), F32), pltpu.VMEM((ms, A_QK), F32),
                        pltpu.VMEM((A_HEADS, ms, A_DV), F32), pltpu.VMEM((A_HEADS, db, LANES), F32),
                        pltpu.VMEM((A_HEADS, A_DV, mcols), F32), pltpu.VMEM((ms, A_WIDTH), F32),
                        pltpu.VMEM((A_HEADS, ms, LANES), F32), pltpu.VMEM((A_HEADS, ms, LANES), F32),
                        pltpu.VMEM((A_HEADS, ms, LANES), F32)],
        compiler_params=pltpu.CompilerParams(dimension_semantics=("arbitrary",), vmem_limit_bytes=VMEM_LIMIT),
        name="mlstm_sample",
    )(*ins, c0)


def _page_ring(pt_ref, pages_per_step, streams):
    b, g = pl.program_id(0), pl.program_id(1)
    nb, ng = pl.num_programs(0), pl.num_programs(1)
    step = b * ng + g
    slot = lax.rem(step, 2)

    def copies(bb, gg, sl):
        return [pltpu.make_async_copy(hbm.at[pt_ref[bb, gg * pages_per_step + i]], buf.at[sl, i], sem.at[sl])
                for hbm, buf, sem in streams for i in range(pages_per_step)]

    def start_all(cps):
        for n, cp in enumerate(cps):
            cp.start(priority=n % 2)

    @pl.when(step == 0)
    def _():
        start_all(copies(0, 0, 0))

    @pl.when(step + 1 < nb * ng)
    def _():
        wrap = g + 1 == ng
        start_all(copies(jnp.where(wrap, b + 1, b), jnp.where(wrap, 0, g + 1), 1 - slot))

    for cp in copies(b, g, slot):
        cp.wait()
    return slot


def _dsa_sample_scores_kernel(pt_ref, iq_ref, iw_ref, iknew_ref, cache_ref, sc_ref, page_buf, sem,
                              *, pages_per_step, n_pages, ds):
    g = pl.program_id(1)
    slot = _page_ring(pt_ref, pages_per_step, [(cache_ref, page_buf, sem)])
    page_refs = [page_buf.at[slot, i] for i in range(pages_per_step)]
    iq = iq_ref[0]
    iw = iw_ref[0]

    def scores(keys_t, visible):
        n = keys_t.shape[1] // PAGE
        r = jnp.maximum(_dot(iq, keys_t), 0.0) * jnp.concatenate([iw] * n, axis=1)
        sc = jnp.sum(r.reshape(8, IDX_HEADS, n * PAGE), axis=1)
        return jnp.where(visible, sc, -jnp.inf)

    t_all = lax.broadcasted_iota(I32, (8, pages_per_step * PAGE), 0)
    sc_step = scores(jnp.concatenate([r[...] for r in page_refs], axis=1).astype(BF16), t_all < ds)
    for i in range(pages_per_step):
        sc_ref[0, g * pages_per_step + i] = sc_step[:, i * PAGE:(i + 1) * PAGE]

    @pl.when(g == pl.num_programs(1) - 1)
    def _():
        t_i = lax.broadcasted_iota(I32, (8, PAGE), 0)
        s_i = lax.broadcasted_iota(I32, (8, PAGE), 1)
        sc_ref[0, n_pages] = scores(iknew_ref[0], (t_i < ds) & (s_i <= t_i))


def _dsa_sample_scores(page_table, iq_s, iw_bc, ik_new, cache_ik, *, ds, pages_per_step):
    db, n_pages = page_table.shape
    npg = n_pages // pages_per_step
    per_b = lambda a: pl.BlockSpec((1,) + a.shape[1:], lambda b, g, pt: (b,) + (0,) * (a.ndim - 1))
    return pl.pallas_call(
        functools.partial(_dsa_sample_scores_kernel, pages_per_step=pages_per_step, n_pages=n_pages, ds=ds),
        grid_spec=pltpu.PrefetchScalarGridSpec(
            num_scalar_prefetch=1, grid=(db, npg),
            in_specs=[per_b(iq_s), per_b(iw_bc), per_b(ik_new), pl.BlockSpec(memory_space=pl.ANY)],
            out_specs=pl.BlockSpec((1, n_pages + 1, 8, PAGE), lambda b, g, pt: (b, 0, 0, 0)),
            scratch_shapes=[pltpu.VMEM((2, pages_per_step, IDX_DIM, PAGE), F32), pltpu.SemaphoreType.DMA((2,))]),
        out_shape=jax.ShapeDtypeStruct((db, n_pages + 1, 8, PAGE), F32),
        compiler_params=pltpu.CompilerParams(dimension_semantics=("arbitrary", "arbitrary"),
                                             vmem_limit_bytes=VMEM_LIMIT),
        name="dsa_sample_scores",
    )(page_table, iq_s, iw_bc, ik_new, cache_ik)


def _dsa_sample_search_kernel(sc_ref, bias_ref, *, topk):
    sc = sc_ref[...]
    nb, np1 = sc.shape[0], sc.shape[1]
    pos = lax.broadcasted_iota(I32, sc.shape, 1) * PAGE + lax.broadcasted_iota(I32, sc.shape, 3)

    def count(mask):
        per_lane = jnp.sum(mask.astype(I32), axis=1, keepdims=True)
        return jnp.sum(per_lane, axis=3, keepdims=True)

    count_ge = lambda t: count(sc_ref[...] >= t)
    thr0, thr_up, n_ge = _radix_threshold(count_ge, (nb, 1, 8, 1), topk)
    nbits = (np1 * PAGE - 1).bit_length()

    def tie_search(_):
        thr_t = _refine_in_bin(count_ge, thr0, thr_up, topk)
        need = topk - count(sc > thr_t)
        tied = sc == thr_t

        def pos_body(i, p):
            cand = p | jnp.left_shift(jnp.int32(1), nbits - 1 - i)
            cnt = count(tied & (pos < cand))
            return jnp.where(cnt < need, cand, p)
        return thr_t, lax.fori_loop(0, nbits, pos_body, jnp.zeros(thr0.shape, I32))

    has_tie = jnp.max(jnp.where((n_ge > topk) & (thr0 > -jnp.inf), 1, 0)) > 0
    thr, p_lim = lax.cond(has_tie, tie_search, lambda _: (thr0, jnp.full(thr0.shape, 2 ** 30, I32)), 0)
    sel = ((sc > thr) | ((sc == thr) & (pos <= p_lim))) & (sc > -jnp.inf)
    bias_ref[...] = jnp.where(sel, 0.0, NEG_BIG)


def _dsa_sample_search(scores, *, topk):
    db = scores.shape[0]
    nb = SEARCH_SEQS if db % SEARCH_SEQS == 0 else 1
    blk = pl.BlockSpec((nb,) + scores.shape[1:], lambda i: (i, 0, 0, 0))
    return pl.pallas_call(
        functools.partial(_dsa_sample_search_kernel, topk=topk),
        grid=(db // nb,),
        in_specs=[blk],
        out_specs=blk,
        out_shape=jax.ShapeDtypeStruct(scores.shape, F32),
        compiler_params=pltpu.CompilerParams(dimension_semantics=("arbitrary",), vmem_limit_bytes=VMEM_LIMIT),
        name="dsa_sample_search",
    )(scores)


def _dsa_sample_attend_kernel(pt_ref, q_ref, bias_ref, knew_ref, vnew_ref, zb_ref, ck_ref, cv_ref, o_ref,
                              k_buf, v_buf, k_sem, v_sem, acc_ref, m_ref, l_ref, *, pages_per_step, n_pages, ds):
    g = pl.program_id(1)
    slot = _page_ring(pt_ref, pages_per_step, [(ck_ref, k_buf, k_sem), (cv_ref, v_buf, v_sem)])
    k_refs = [k_buf.at[slot, i] for i in range(pages_per_step)]
    v_refs = [v_buf.at[slot, i] for i in range(pages_per_step)]
    rows_q = ds * B_HEADS
    q = q_ref[0]

    @pl.when(g == 0)
    def _():
        acc_ref[...] = jnp.zeros(acc_ref.shape, F32)
        m_ref[...] = jnp.full(m_ref.shape, NEG_BIG, F32)
        l_ref[...] = jnp.zeros(l_ref.shape, F32)

    def step(k_keys, v_keys, bias8):
        n = k_keys.shape[0]
        bias = jnp.concatenate([jnp.broadcast_to(bias8[t:t + 1, :], (B_HEADS, n)) for t in range(ds)], axis=0)
        s = _dot_nt(q, k_keys) + bias
        m_old = m_ref[...]
        m_new = jnp.maximum(m_old, jnp.max(s, axis=1, keepdims=True))
        alpha = jnp.exp2(m_old - m_new)
        p = jnp.exp2(s - m_new)
        l_ref[...] = alpha * l_ref[...] + jnp.sum(p, axis=1, keepdims=True)
        m_ref[...] = m_new
        acc_ref[...] = alpha * acc_ref[...] + _dot(p.astype(BF16), v_keys)

    def page_keys(ref):
        return jnp.concatenate([ref[pl.ds(kv, PAGE, stride=B_KV), :] for kv in range(B_KV)], axis=1).astype(BF16)

    step(jnp.concatenate([page_keys(r) for r in k_refs], axis=0),
         jnp.concatenate([page_keys(r) for r in v_refs], axis=0),
         jnp.concatenate([bias_ref[0, g * pages_per_step + i] for i in range(pages_per_step)], axis=1))

    @pl.when(g == pl.num_programs(1) - 1)
    def _():
        step(knew_ref[0], vnew_ref[0], bias_ref[0, n_pages])
        o = acc_ref[...] / l_ref[...]
        head = lax.broadcasted_iota(I32, (rows_q, B_HD), 0) & (B_HEADS - 1)
        o_sel = o[:, 0:B_HD]
        for kv in range(1, B_KV):
            o_sel = jnp.where(head >= kv * B_GROUP, o[:, kv * B_HD:(kv + 1) * B_HD], o_sel)
        o_ref[0] = (o_sel * zb_ref[0]).astype(o_ref.dtype)


def _dsa_sample_attend(page_table, q_bd, bias, k_new, v_new, zb_s, cache_k, cache_v, *, ds, pages_per_step):
    db, n_pages = page_table.shape
    npg = n_pages // pages_per_step
    rows_q = ds * B_HEADS
    kvw = B_KV * B_HD
    per_b = lambda a: pl.BlockSpec((1,) + a.shape[1:], lambda b, g, pt: (b,) + (0,) * (a.ndim - 1))
    any_space = pl.BlockSpec(memory_space=pl.ANY)
    page_buf = pltpu.VMEM((2, pages_per_step, PAGE * B_KV, B_HD), F32)
    return pl.pallas_call(
        functools.partial(_dsa_sample_attend_kernel, pages_per_step=pages_per_step, n_pages=n_pages, ds=ds),
        grid_spec=pltpu.PrefetchScalarGridSpec(
            num_scalar_prefetch=1, grid=(db, npg),
            in_specs=[per_b(q_bd), per_b(bias), per_b(k_new), per_b(v_new), per_b(zb_s), any_space, any_space],
            out_specs=pl.BlockSpec((1, rows_q, B_HD), lambda b, g, pt: (b, 0, 0)),
            scratch_shapes=[page_buf, page_buf, pltpu.SemaphoreType.DMA((2,)), pltpu.SemaphoreType.DMA((2,)),
                            pltpu.VMEM((rows_q, kvw), F32), pltpu.VMEM((rows_q, 1), F32),
                            pltpu.VMEM((rows_q, 1), F32)]),
        out_shape=jax.ShapeDtypeStruct((db, rows_q, B_HD), BF16),
        compiler_params=pltpu.CompilerParams(dimension_semantics=("arbitrary", "arbitrary"),
                                             vmem_limit_bytes=VMEM_LIMIT),
        name="dsa_sample_attend",
    )(page_table, q_bd, bias, k_new, v_new, zb_s, cache_k, cache_v)


def _sample_path(x, pe, cache_k, cache_v, cache_ik, page_table, c0, n0, m0, conv0, w, w_packed, misc_bias):
    db, ds, d = x.shape
    ms = db * ds
    assert ds >= CONV_W - 1 and ds <= 8
    n_pages = page_table.shape[1]
    a = _inproj(x.reshape(ms, d), w['norm_pre'], w_packed, w['conv_w'], w['conv_b'], misc_bias,
                prompt=False, seq_len=ms, tm=ms)

    tmaj = lambda z: jnp.swapaxes(z.reshape(db, ds, -1), 0, 1).reshape(ms, -1)
    ya_t, c_new, n_new, m_new = _mlstm_sample(
        tmaj(a['q']), tmaj(a['k']), jnp.swapaxes(conv0, 0, 1), w['conv_w'], w['conv_b'], tmaj(a['av']),
        tmaj(a['misc']), tmaj(a['ga']), w['norm_a'], jnp.swapaxes(n0, 0, 1),
        jnp.broadcast_to(jnp.swapaxes(m0, 0, 1)[:, :, None], (A_HEADS, db, LANES)), c0, db=db, ds=ds)
    ya = jnp.swapaxes(ya_t.reshape(ds, db, -1), 0, 1).reshape(ms, -1)

    topk = min(TOPK_MAX, (n_pages * PAGE + ds) // 4)
    pad_rows = lambda z: jnp.pad(z.reshape(db, ds, -1), ((0, 0), (0, PAGE - ds), (0, 0))).astype(BF16)
    pad_tok = lambda z: jnp.pad(z.reshape(db, ds, -1), ((0, 0), (0, 8 - ds), (0, 0)))
    iq_s = pad_tok(a['iq']).reshape(db, 8 * IDX_HEADS, IDX_DIM)
    iw_bc = jnp.broadcast_to(pad_tok(a['misc'][:, MISC_IW:MISC_IW + IDX_HEADS]).reshape(db, 8 * IDX_HEADS, 1),
                             (db, 8 * IDX_HEADS, LANES))
    largest_divisor = lambda cap: max(p for p in range(1, cap + 1) if n_pages % p == 0)
    pps_scores, pps_attend = largest_divisor(SCORE_PAGES), largest_divisor(ATTEND_PAGES)
    scores = _dsa_sample_scores(page_table, iq_s, iw_bc, jnp.swapaxes(pad_rows(a['misc'][:, MISC_IK:]), 1, 2),
                                jnp.swapaxes(cache_ik, 1, 2), ds=ds, pages_per_step=pps_scores)
    bias = _dsa_sample_search(scores, topk=topk)
    q5 = a['bq'].reshape(db, ds * B_HEADS, 1, B_HD)
    kv_of_row = (jnp.arange(ds * B_HEADS) % B_HEADS) // B_GROUP
    q_bd = jnp.where((kv_of_row[None, :, None, None] == jnp.arange(B_KV)[None, None, :, None]), q5,
                     jnp.zeros((), BF16)).reshape(db, ds * B_HEADS, B_KV * B_HD)
    attn = _dsa_sample_attend(page_table, q_bd, bias, pad_rows(a['kf']), pad_rows(a['vf']),
                              a['zb'].reshape(db, ds * B_HEADS, B_HD),
                              cache_k.reshape(-1, PAGE * B_KV, B_HD), cache_v.reshape(-1, PAGE * B_KV, B_HD),
                              ds=ds, pages_per_step=pps_attend)
    yb = attn.reshape(ms, B_WIDTH)
    y = _outproj(ya, yb, a['sga'], a['sgb'], x.reshape(ms, d), pe.reshape(ms, -1), w['wa'], w['wb'], w['wo'],
                 w['wg'], w['wp'], w['norm_post'], tm=ms)
    qk_pre = jnp.concatenate([a['q'], a['k']], axis=1).reshape(db, ds, 2 * A_QK)
    return dict(
        y=y.reshape(db, ds, d),
        k=a['kf'].reshape(1, db, ds, B_KV, B_HD),
        v=a['vf'].reshape(1, db, ds, B_KV, B_HD),
        ik=a['misc'][:, MISC_IK:].reshape(1, db, ds, IDX_DIM),
        C=c_new[None],
        n=jnp.swapaxes(n_new, 0, 1)[None],
        m=jnp.swapaxes(m_new[:, :, 0], 0, 1)[None],
        conv=qk_pre[:, ds - (CONV_W - 1):, :][None],
    )


def _prep_weights(w_in, conv_w, conv_b, if_bias, norm_a, w_a_proj, w_b_proj, w_out, norm_pre, norm_post, w_ple,
                  w_ple_gate):
    d = w_in.shape[1]
    misc_bias = jnp.zeros((1, LANES), F32).at[0, :2 * A_HEADS].set(if_bias[0])
    w = dict(norm_pre=norm_pre[0][None], norm_post=norm_post[0][None], norm_a=norm_a[0][None],
             conv_w=conv_w[0], conv_b=conv_b[0][None],
             wa=w_a_proj[0].astype(BF16), wb=w_b_proj[0].astype(BF16), wo=w_out[0].astype(BF16),
             wg=w_ple_gate[0].astype(BF16), wp=w_ple[0].astype(BF16))
    return w, _pack_w_in(w_in[0], d), misc_bias


def kernel(x_prompt, x_sample, cache_k, cache_v, cache_idx_k, page_table, state_C, state_n, state_m, state_conv,
           p_prompt, p_sample, w_in, conv_w, conv_b, if_bias, norm_a, w_a_proj, w_b_proj, w_out, norm_pre,
           norm_post, w_ple, w_ple_gate):
    w, w_packed, misc_bias = _prep_weights(w_in, conv_w, conv_b, if_bias, norm_a, w_a_proj, w_b_proj, w_out,
                                           norm_pre, norm_post, w_ple, w_ple_gate)
    p = _prompt_path(x_prompt, p_prompt[0], w, w_packed, misc_bias)
    s = _sample_path(x_sample, p_sample[0], cache_k[0], cache_v[0], cache_idx_k[0], page_table, state_C[0],
                     state_n[0], state_m[0], state_conv[0], w, w_packed, misc_bias)
    names = ('k', 'v', 'ik', 'C', 'n', 'm', 'conv')
    return (p['y'], s['y']) + tuple(p[n] for n in names) + tuple(s[n] for n in names)
```
